```python
import jax, jax.numpy as jnp
from jax import lax
import numpy as np

D_MODEL = 1024
BATCH = 16
SEQ = 256
DEPTH = 2
DEC_BATCH = 2
DEC_SEQ = 1024
PAST_LEN = 256

GRID_W = 64
W_A = D_MODEL // 2
N_POOL_GROUPS = 4
POOL_GROUP = W_A // N_POOL_GROUPS
POOL_HALF = (1, 2, 4, 8)
N_HEADS_B = 8
HEAD_DIM_B = (D_MODEL // 2) // N_HEADS_B
W_B = N_HEADS_B * HEAD_DIM_B
WIN_H = 8
WIN_W = 16
W_C = D_MODEL // 2
W_D = D_MODEL // 2
CONV_C = 3
CONV_D = 31
N_EVEN = (DEPTH + 1) // 2
N_ODD = DEPTH // 2
IN_EVEN = 2 * W_A + 4 * W_B
IN_ODD = 4 * W_C + 3 * W_D
EPS = 1e-6

kernel_name = "hybrid_pool_natten_conv_dit_step"


def rmsnorm(x, g):
    xf = x.astype(jnp.float32)
    y = xf * lax.rsqrt(jnp.mean(xf * xf, axis=-1, keepdims=True) + EPS)
    return (y * g.astype(jnp.float32)).astype(x.dtype)


def layernorm(x, g, b):
    xf = x.astype(jnp.float32)
    mu = jnp.mean(xf, axis=-1, keepdims=True)
    var = jnp.mean(jnp.square(xf - mu), axis=-1, keepdims=True)
    y = (xf - mu) * lax.rsqrt(var + EPS)
    return (y * g.astype(jnp.float32) + b.astype(jnp.float32)).astype(x.dtype)


def modulated_norm(x, cond, norm_g, w_mod, b_mod):
    m = jax.nn.silu(cond) @ w_mod + b_mod
    shift, scale, gate = jnp.split(m, 3, axis=-1)
    return rmsnorm(x, norm_g) * (1 + scale) + shift, gate


def depthwise_conv(x, w, width):
    return lax.conv_general_dilated(x, w[:, None, :], window_strides=(1,),
                                    padding=[(width // 2, width // 2)],
                                    dimension_numbers=('NWC', 'WIO', 'NWC'),
                                    feature_group_count=x.shape[-1])


def pool_mixer(u, w_pool, pool_scale):
    bsz, t, _ = u.shape
    uf = u.astype(jnp.float32).reshape(bsz, t, N_POOL_GROUPS, POOL_GROUP)
    cs = jnp.concatenate([jnp.zeros((bsz, 1, N_POOL_GROUPS, POOL_GROUP), jnp.float32),
                          jnp.cumsum(uf, axis=1)], axis=1)
    pos = jnp.arange(t)[:, None]
    half = jnp.array(POOL_HALF, dtype=jnp.int32)[None, :]
    lo = jnp.clip(pos - half, 0, t)
    hi = jnp.clip(pos + half, 0, t)
    gidx = jnp.arange(N_POOL_GROUPS)[None, :]
    win_sum = cs[:, hi, gidx] - cs[:, lo, gidx]
    mean = win_sum / (hi - lo).astype(jnp.float32)[None, :, :, None]
    p = (mean - uf).astype(u.dtype)
    y = jnp.einsum('btgc,gcd->btgd', p, w_pool).reshape(bsz, t, W_A)
    return y * pool_scale


def context_attention(q, k, v):
    bsz, l = q.shape[:2]
    s = jnp.einsum('bqhd,bkhd->bhqk', q, k).astype(jnp.float32) * (HEAD_DIM_B ** -0.5)
    p = jax.nn.softmax(s, axis=-1).astype(v.dtype)
    return jnp.einsum('bhqk,bkhd->bqhd', p, v).reshape(bsz, l, W_B)


def neighbourhood_attention(q, k, v, kc, vc, rpb):
    bsz, t = q.shape[:2]
    rows = t // GRID_W
    kh = min(WIN_H, rows)
    kw = min(WIN_W, GRID_W)
    qg = q.reshape(bsz, rows, GRID_W, N_HEADS_B, HEAD_DIM_B)
    kg = k.reshape(bsz, rows, GRID_W, N_HEADS_B, HEAD_DIM_B)
    vg = v.reshape(bsz, rows, GRID_W, N_HEADS_B, HEAD_DIM_B)
    r = jnp.arange(rows)
    row_idx = jnp.clip(r - kh // 2, 0, rows - kh)[:, None] + jnp.arange(kh)[None, :]
    k_band = kg[:, row_idx]
    v_band = vg[:, row_idx]
    col = jnp.arange(GRID_W)
    col_start = jnp.clip(col - kw // 2, 0, GRID_W - kw)
    col_ok = (col[None, :] >= col_start[:, None]) & (col[None, :] < col_start[:, None] + kw)
    dr = row_idx - r[:, None] + (WIN_H - 1)
    dc = jnp.clip(col[None, :] - col[:, None], -(WIN_W - 1), WIN_W - 1) + (WIN_W - 1)
    bias = rpb[:, dr[:, None, :, None], dc[None, :, None, :]].astype(jnp.float32)
    scale = HEAD_DIM_B ** -0.5
    s_loc = jnp.einsum('brqhd,brkwhd->bhrqkw', qg, k_band).astype(jnp.float32) * scale + bias
    s_loc = jnp.where(col_ok[:, None, :], s_loc, -jnp.inf)
    s_ctx = jnp.einsum('brqhd,bhld->bhrql', qg, kc).astype(jnp.float32) * scale
    n_loc = kh * GRID_W
    s = jnp.concatenate([s_loc.reshape(bsz, N_HEADS_B, rows, GRID_W, n_loc), s_ctx], axis=-1)
    p = jax.nn.softmax(s, axis=-1).astype(v.dtype)
    p_loc = p[..., :n_loc].reshape(bsz, N_HEADS_B, rows, GRID_W, kh, GRID_W)
    p_ctx = p[..., n_loc:]
    o = (jnp.einsum('bhrqkw,brkwhd->brqhd', p_loc, v_band)
         + jnp.einsum('bhrql,bhld->brqhd', p_ctx, vc))
    return o.reshape(bsz, t, W_B)


def even_layer(x, cond, ctx_kv, norm_g, w_mod, b_mod, w_in, w_pool, pool_scale, rpb, w_out):
    bsz, t, _ = x.shape
    h, gate = modulated_norm(x, cond, norm_g, w_mod, b_mod)
    proj = h @ w_in
    u_a, g_a, q, k, v, g_b = jnp.split(
        proj, [W_A, 2 * W_A, 2 * W_A + W_B, 2 * W_A + 2 * W_B, 2 * W_A + 3 * W_B], axis=-1)
    q = q.reshape(bsz, t, N_HEADS_B, HEAD_DIM_B)
    k = k.reshape(bsz, t, N_HEADS_B, HEAD_DIM_B)
    v = v.reshape(bsz, t, N_HEADS_B, HEAD_DIM_B)
    a_out = pool_mixer(u_a, w_pool, pool_scale) * jax.nn.silu(g_a)
    if ctx_kv is None:
        att = context_attention(q, k, v)
        kv = (k.transpose(0, 2, 1, 3), v.transpose(0, 2, 1, 3))
    else:
        att = neighbourhood_attention(q, k, v, ctx_kv[0], ctx_kv[1], rpb)
        kv = None
    b_out = att * jax.nn.silu(g_b)
    y = jnp.concatenate([a_out, b_out], axis=-1) @ w_out
    return x + gate * y, kv


def odd_layer(x, cond, norm_g, w_mod, b_mod, w_in, conv_c, conv_d, conv_d_b, ln_g, ln_b, w_out):
    h, gate = modulated_norm(x, cond, norm_g, w_mod, b_mod)
    proj = h @ w_in
    b_c, c_c, x_c, g_c, a_d, b_d, g_d = jnp.split(
        proj, [W_C, 2 * W_C, 3 * W_C, 4 * W_C, 4 * W_C + W_D, 4 * W_C + 2 * W_D], axis=-1)
    c_out = b_c * depthwise_conv(c_c * x_c, conv_c, CONV_C) * jax.nn.silu(g_c)
    z = depthwise_conv(a_d * jax.nn.sigmoid(b_d), conv_d, CONV_D) + conv_d_b
    z = jax.nn.silu(layernorm(z, ln_g, ln_b))
    d_out = z * jax.nn.silu(g_d)
    y = jnp.concatenate([c_out, d_out], axis=-1) @ w_out
    return x + gate * y


def setup_inputs(seed: int = 0) -> dict:
    key = jax.random.key(seed)
    ks = jax.random.split(key, 24)
    nrm = lambda k, s: jax.random.normal(k, s, jnp.float32)
    D = D_MODEL
    return {
        "x_prompt": nrm(ks[0], (BATCH, SEQ, D)),
        "x_sample": nrm(ks[1], (DEC_BATCH, DEC_SEQ, D)),
        "cache_k": nrm(ks[2], (DEC_BATCH, N_EVEN, N_HEADS_B, PAST_LEN, HEAD_DIM_B)),
        "cache_v": nrm(ks[3], (DEC_BATCH, N_EVEN, N_HEADS_B, PAST_LEN, HEAD_DIM_B)),
        "c": nrm(ks[4], (DEC_BATCH, D)),
        "c_ctx": nrm(ks[5], (D,)),
        "norm_g": 1.0 + 0.02 * nrm(ks[6], (DEPTH, D)),
        "w_mod": 0.5 * D ** -0.5 * nrm(ks[7], (DEPTH, D, 3 * D)),
        "b_mod": 0.02 * nrm(ks[8], (DEPTH, 3 * D)),
        "w_in_even": D ** -0.5 * nrm(ks[9], (N_EVEN, D, IN_EVEN)),
        "w_pool": POOL_GROUP ** -0.5 * nrm(ks[10], (N_EVEN, N_POOL_GROUPS, POOL_GROUP, POOL_GROUP)),
        "pool_scale": 1.0 + 0.1 * nrm(ks[11], (N_EVEN, W_A)),
        "rpb": 0.1 * nrm(ks[12], (N_EVEN, N_HEADS_B, 2 * WIN_H - 1, 2 * WIN_W - 1)),
        "w_out_even": (W_A + W_B) ** -0.5 * nrm(ks[13], (N_EVEN, W_A + W_B, D)),
        "w_in_odd": D ** -0.5 * nrm(ks[14], (N_ODD, D, IN_ODD)),
        "conv_c": CONV_C ** -0.5 * nrm(ks[15], (N_ODD, CONV_C, W_C)),
        "conv_d": CONV_D ** -0.5 * nrm(ks[16], (N_ODD, CONV_D, W_D)),
        "conv_d_b": 0.02 * nrm(ks[17], (N_ODD, W_D)),
        "ln_g": 1.0 + 0.02 * nrm(ks[18], (N_ODD, W_D)),
        "ln_b": 0.02 * nrm(ks[19], (N_ODD, W_D)),
        "w_out_odd": (W_C + W_D) ** -0.5 * nrm(ks[20], (N_ODD, W_C + W_D, D)),
        "final_g": 1.0 + 0.02 * nrm(ks[21], (D,)),
    }


def reference(x_prompt, x_sample, cache_k, cache_v, c, c_ctx, norm_g, w_mod, b_mod,
              w_in_even, w_pool, pool_scale, rpb, w_out_even, w_in_odd, conv_c, conv_d,
              conv_d_b, ln_g, ln_b, w_out_odd, final_g):
    cond_ctx = c_ctx[None, None, :]
    cond_lat = c[:, None, :]
    xp, xs = x_prompt, x_sample
    new_k, new_v = [], []
    for layer in range(DEPTH):
        i = layer // 2
        common = (norm_g[layer], w_mod[layer], b_mod[layer])
        if layer % 2 == 0:
            ew = (w_in_even[i], w_pool[i], pool_scale[i], rpb[i], w_out_even[i])
            xp, kv = even_layer(xp, cond_ctx, None, *common, *ew)
            new_k.append(kv[0])
            new_v.append(kv[1])
            xs, _ = even_layer(xs, cond_lat, (cache_k[:, i], cache_v[:, i]), *common, *ew)
        else:
            ow = (w_in_odd[i], conv_c[i], conv_d[i], conv_d_b[i], ln_g[i], ln_b[i], w_out_odd[i])
            xp = odd_layer(xp, cond_ctx, *common, *ow)
            xs = odd_layer(xs, cond_lat, *common, *ow)
    y_prompt = rmsnorm(xp, final_g)
    y_sample = rmsnorm(xs, final_g)
    new_cache_k = jnp.stack(new_k, axis=1)
    new_cache_v = jnp.stack(new_v, axis=1)
    return (y_prompt, y_sample, new_cache_k, new_cache_v)
```

```python
import functools

import jax
import jax.numpy as jnp
from jax import lax
from jax.experimental import pallas as pl
from jax.experimental.pallas import tpu as pltpu

F32 = jnp.float32
BF16 = jnp.bfloat16

D_MODEL = 1024
W_HALF = 512
N_POOL_GROUPS = 4
POOL_HALF = (1, 2, 4, 8)
N_HEADS = 8
HEAD_DIM = 64
GRID_W = 64
WIN_H = 8
WIN_W = 16
CONV_C = 3
CONV_D = 31
EPS = 1e-6
MASKED = -1e30

LANES = 128
SUBLANES = 8
PAD = 16
ROW_CHUNK = 256
NORM_ROWS = 32
MIX_ROWS = 64
Q_ROWS = 128
MOD_COLS = 768
VMEM_LIMIT = 58 * 1024 * 1024

assert PAD >= CONV_D // 2 + 1 and PAD % SUBLANES == 0 and PAD >= 2 * SUBLANES


def _sigmoid(x):
    return 1.0 / (1.0 + jnp.exp(-x))


def _silu(x):
    return x * _sigmoid(x)


def _dot(a, b):
    return jnp.dot(a, b, preferred_element_type=F32)


def _dot_nt(a, b):
    return lax.dot_general(a, b, (((1,), (1,)), ((), ())), preferred_element_type=F32)


def _lanes(j):
    return slice(j * LANES, (j + 1) * LANES)


def _group(g):
    return slice(g * W_HALF, (g + 1) * W_HALF)


def _rows(start, size, align):
    return pl.ds(pl.multiple_of(start, align), size)


def _mod_body(c_ref, w_ref, b_ref, o_ref):
    s = _silu(c_ref[...]).astype(BF16)
    o_ref[0] = _dot(s, w_ref[0].astype(BF16)) + b_ref[0]


def _modulation(cond, w_mod, b_mod):
    depth, d, n = w_mod.shape
    rows = cond.shape[0]
    return pl.pallas_call(
        _mod_body,
        out_shape=jax.ShapeDtypeStruct((depth, rows, n), F32),
        grid=(depth, n // MOD_COLS),
        in_specs=[
            pl.BlockSpec((rows, d), lambda l, j: (0, 0)),
            pl.BlockSpec((1, d, MOD_COLS), lambda l, j: (l, 0, j)),
            pl.BlockSpec((1, 1, MOD_COLS), lambda l, j: (l, 0, j)),
        ],
        out_specs=pl.BlockSpec((1, rows, MOD_COLS), lambda l, j: (l, 0, j)),
        compiler_params=pltpu.CompilerParams(dimension_semantics=("arbitrary", "arbitrary")),
        name="mod",
    )(cond, w_mod, b_mod.reshape(depth, 1, n))


def _seq_of_chunk(c, t):
    per_seq = t // ROW_CHUNK
    if per_seq == 1:
        return c, 0
    s = c // per_seq
    return s, (c - s * per_seq) * ROW_CHUNK


def _pad_row(s, off, t):
    return s * (t + 2 * PAD) + PAD + off


def _modnorm_chunk(src_ref, h_ref, s, off, r0, gain, shift):
    def step(i, carry):
        x = src_ref[s, _rows(off + i * NORM_ROWS, NORM_ROWS, NORM_ROWS), :]
        ms = jnp.mean(x * x, axis=-1, keepdims=True)
        h_ref[_rows(r0 + i * NORM_ROWS, NORM_ROWS, NORM_ROWS), :] = (
            x * lax.rsqrt(ms + EPS) * gain + shift).astype(BF16)
        return carry
    lax.fori_loop(0, ROW_CHUNK // NORM_ROWS, step, 0)


def _zero_pads(pad_ref, nb, t):
    z = jnp.zeros((PAD, W_HALF), F32)
    for s in range(nb):
        pad_ref[_pad_row(s, 0, t) - PAD:_pad_row(s, 0, t), :] = z
        pad_ref[_pad_row(s, t, t):_pad_row(s, t, t) + PAD, :] = z


def _pool_phase(pad_ref, ga_ref, wp_ref, ps_ref, ab_ref, nb, t):
    per_seq = t // MIX_ROWS

    def step(i, carry):
        s = i // per_seq
        r0 = (i - s * per_seq) * MIX_ROWS
        prow = _pad_row(s, r0, t)
        rows = _rows(i * MIX_ROWS, MIX_ROWS, MIX_ROWS)
        pos = r0 + lax.broadcasted_iota(jnp.int32, (MIX_ROWS, LANES), 0)
        for g in range(N_POOL_GROUPS):
            hw = POOL_HALF[g]
            ln = _lanes(g)
            blk = pad_ref[_rows(prow - SUBLANES, MIX_ROWS + 2 * SUBLANES, SUBLANES), ln]
            win = None
            for d in range(-hw, hw):
                sl = blk[SUBLANES + d:SUBLANES + d + MIX_ROWS]
                win = sl if win is None else win + sl
            cnt = (jnp.minimum(pos + hw, t) - jnp.maximum(pos - hw, 0)).astype(F32)
            p = (win / cnt - blk[SUBLANES:SUBLANES + MIX_ROWS]).astype(BF16)
            y = _dot(p, wp_ref[g]) * ps_ref[:, ln] * ga_ref[rows, ln]
            ab_ref[rows, ln] = y.astype(BF16)
        return carry
    lax.fori_loop(0, nb * per_seq, step, 0)


def _out_proj_chunk(ab_ref, w_ref, x_ref, gate, dst_ref, s, off, r0):
    lhs = ab_ref[_rows(r0, ROW_CHUNK, ROW_CHUNK), :]
    rows = _rows(off, ROW_CHUNK, ROW_CHUNK)
    for g in range(D_MODEL // W_HALF):
        y = _dot(lhs, w_ref[:, _group(g)])
        dst_ref[s, rows, _group(g)] = x_ref[s, rows, _group(g)] + gate[:, _group(g)] * y


def _conv_phase(pad_c, pad_d, bc_ref, ga_ref, gb_ref, cc_ref, cdw_ref, cdb_ref, lng_ref, lnb_ref,
                ab_ref, nb, t):
    per_seq = t // MIX_ROWS

    def step(i, carry):
        s = i // per_seq
        r0 = (i - s * per_seq) * MIX_ROWS
        prow = _pad_row(s, r0, t)
        rows = _rows(i * MIX_ROWS, MIX_ROWS, MIX_ROWS)
        z = []
        for g in range(W_HALF // LANES):
            ln = _lanes(g)
            blk = pad_c[_rows(prow - SUBLANES, MIX_ROWS + 2 * SUBLANES, SUBLANES), ln]
            c3 = None
            for j in range(CONV_C):
                o = SUBLANES + j - CONV_C // 2
                term = blk[o:o + MIX_ROWS] * cc_ref[j:j + 1, ln]
                c3 = term if c3 is None else c3 + term
            ab_ref[rows, ln] = (bc_ref[rows, ln] * c3 * ga_ref[rows, ln]).astype(BF16)
            acc = None
            for sft in range(SUBLANES):
                part = None
                for a in range((CONV_D - sft + SUBLANES - 1) // SUBLANES):
                    j = SUBLANES * a + sft
                    src = pad_d[_rows(prow - 2 * SUBLANES + SUBLANES * a, MIX_ROWS + SUBLANES,
                                      SUBLANES), ln]
                    term = src * cdw_ref[j:j + 1, ln]
                    part = term if part is None else part + term
                o = SUBLANES + sft - (CONV_D // 2 - SUBLANES)
                part = part[o:o + MIX_ROWS]
                acc = part if acc is None else acc + part
            z.append(acc + cdb_ref[:, ln])
        z = jnp.concatenate(z, axis=-1)
        mu = jnp.mean(z, axis=-1, keepdims=True)
        zc = z - mu
        var = jnp.mean(zc * zc, axis=-1, keepdims=True)
        zn = zc * lax.rsqrt(var + EPS) * lng_ref[...] + lnb_ref[...]
        ab_ref[rows, W_HALF:] = (_silu(zn) * gb_ref[rows, :]).astype(BF16)
        return carry
    lax.fori_loop(0, nb * per_seq, step, 0)


def _final_norm_chunk(y_ref, fg, s, off):
    def step(i, carry):
        rows = _rows(off + i * NORM_ROWS, NORM_ROWS, NORM_ROWS)
        x = y_ref[s, rows, :]
        ms = jnp.mean(x * x, axis=-1, keepdims=True)
        y_ref[s, rows, :] = x * lax.rsqrt(ms + EPS) * fg
        return carry
    lax.fori_loop(0, ROW_CHUNK // NORM_ROWS, step, 0)


def _odd_layer(y_ref, m_row, g_row, fg, wio_ref, cc_ref, cdw_ref, cdb_ref, lng_ref, lnb_ref, woo_ref,
               h_ref, pad_c, pad_d, bc_ref, ga_ref, gb_ref, ab_ref, nb, t):
    shift = m_row[:, :D_MODEL]
    gain = g_row * (1.0 + m_row[:, D_MODEL:2 * D_MODEL])
    gate = m_row[:, 2 * D_MODEL:]
    n_chunks = nb * t // ROW_CHUNK

    def in_proj(c, carry):
        s, off = _seq_of_chunk(c, t)
        r0 = c * ROW_CHUNK
        _modnorm_chunk(y_ref, h_ref, s, off, r0, gain, shift)
        rows = _rows(r0, ROW_CHUNK, ROW_CHUNK)
        prow = _rows(_pad_row(s, off, t), ROW_CHUNK, SUBLANES)
        h = h_ref[rows, :]
        bc_ref[rows, :] = _dot(h, wio_ref[:, _group(0)])
        pad_c[prow, :] = _dot(h, wio_ref[:, _group(1)])
        pad_c[prow, :] = pad_c[prow, :] * _dot(h, wio_ref[:, _group(2)])
        ga_ref[rows, :] = _silu(_dot(h, wio_ref[:, _group(3)]))
        pad_d[prow, :] = _dot(h, wio_ref[:, _group(4)])
        pad_d[prow, :] = pad_d[prow, :] * _sigmoid(_dot(h, wio_ref[:, _group(5)]))
        gb_ref[rows, :] = _silu(_dot(h, wio_ref[:, _group(6)]))
        return carry
    lax.fori_loop(0, n_chunks, in_proj, 0)

    _conv_phase(pad_c, pad_d, bc_ref, ga_ref, gb_ref, cc_ref, cdw_ref, cdb_ref, lng_ref, lnb_ref,
                ab_ref, nb, t)

    def out_proj(c, carry):
        s, off = _seq_of_chunk(c, t)
        _out_proj_chunk(ab_ref, woo_ref, y_ref, gate, y_ref, s, off, c * ROW_CHUNK)
        _final_norm_chunk(y_ref, fg, s, off)
        return carry
    lax.fori_loop(0, n_chunks, out_proj, 0)


def _even_in_proj(x_ref, m_row, g_row, w_ref, h_ref, pad_a, ga_ref, gb_ref, q_ref, k_ref, v_ref,
                  kv_out, nb, t):
    shift = m_row[:, :D_MODEL]
    gain = g_row * (1.0 + m_row[:, D_MODEL:2 * D_MODEL])

    def in_proj(c, carry):
        s, off = _seq_of_chunk(c, t)
        r0 = c * ROW_CHUNK
        _modnorm_chunk(x_ref, h_ref, s, off, r0, gain, shift)
        rows = _rows(r0, ROW_CHUNK, ROW_CHUNK)
        prow = _rows(_pad_row(s, off, t), ROW_CHUNK, SUBLANES)
        h = h_ref[rows, :]
        pad_a[prow, :] = _dot(h, w_ref[:, _group(0)])
        ga_ref[rows, :] = _silu(_dot(h, w_ref[:, _group(1)]))
        q_ref[rows, :] = (_dot(h, w_ref[:, _group(2)]) * (HEAD_DIM ** -0.5)).astype(BF16)
        for dst, out, g in ((k_ref, 0, 3), (v_ref, 1, 4)):
            acc = _dot(h, w_ref[:, _group(g)])
            dst[rows, :] = acc.astype(BF16)
            if kv_out is not None:
                for hd in range(N_HEADS):
                    kv_out[out][s, 0, hd, :, :] = acc[:, hd * HEAD_DIM:(hd + 1) * HEAD_DIM]
        gb_ref[rows, :] = _silu(_dot(h, w_ref[:, _group(5)]))
        return carry
    lax.fori_loop(0, nb * t // ROW_CHUNK, in_proj, 0)


def _even_out_proj(x_ref, y_ref, m_row, w_ref, ab_ref, nb, t):
    gate = m_row[:, 2 * D_MODEL:]

    def out_proj(c, carry):
        s, off = _seq_of_chunk(c, t)
        _out_proj_chunk(ab_ref, w_ref, x_ref, gate, y_ref, s, off, c * ROW_CHUNK)
        return carry
    lax.fori_loop(0, nb * t // ROW_CHUNK, out_proj, 0)


def _split_heads(x):
    lane = lax.broadcasted_iota(jnp.int32, (1, LANES), 1)
    first = jnp.where(lane < HEAD_DIM, 1.0, 0.0).astype(x.dtype)
    return jnp.concatenate([x * first, x * (1 - first)], axis=0)


def _merge_heads(o):
    n = o.shape[0] // 2
    lane = lax.broadcasted_iota(jnp.int32, (n, LANES), 1)
    return jnp.where(lane < HEAD_DIM, o[:n], o[n:])


def _context_attention(q_ref, k_ref, v_ref, gb_ref, ab_ref, nb, t):
    assert t == ROW_CHUNK

    def per_seq(s, carry):
        seq = _rows(s * t, t, t)
        for j in range(N_HEADS // 2):
            ln = _lanes(j)
            kp = k_ref[seq, ln]
            vp = v_ref[seq, ln]
            for r0 in range(0, t, Q_ROWS):
                rows = _rows(s * t + r0, Q_ROWS, Q_ROWS)
                sc = _dot_nt(_split_heads(q_ref[rows, ln]), kp)
                p = jnp.exp(sc - jnp.max(sc, axis=-1, keepdims=True))
                o = _dot(p.astype(BF16), vp) / jnp.sum(p, axis=-1, keepdims=True)
                ab_ref[rows, W_HALF + j * LANES:W_HALF + (j + 1) * LANES] = (
                    _merge_heads(o) * gb_ref[rows, ln]).astype(BF16)
        return carry
    lax.fori_loop(0, nb, per_seq, 0)


def _prompt_body(x_ref, m_ref, ng_ref, fg_ref, wie_ref, wp_ref, ps_ref, woe_ref, wio_ref, cc_ref,
                 cdw_ref, cdb_ref, lng_ref, lnb_ref, woo_ref,
                 y_ref, ko_ref, vo_ref,
                 h_ref, pad_a, pad_b, ga_ref, gb_ref, bc_ref, q_ref, k_ref, v_ref, ab_ref,
                 *, nb, t):
    _zero_pads(pad_a, nb, t)
    _zero_pads(pad_b, nb, t)
    _even_in_proj(x_ref, m_ref[0, 0], ng_ref[0:1, :], wie_ref, h_ref, pad_a, ga_ref, gb_ref,
                  q_ref, k_ref, v_ref, (ko_ref, vo_ref), nb, t)
    _pool_phase(pad_a, ga_ref, wp_ref, ps_ref, ab_ref, nb, t)
    _context_attention(q_ref, k_ref, v_ref, gb_ref, ab_ref, nb, t)
    _even_out_proj(x_ref, y_ref, m_ref[0, 0], woe_ref, ab_ref, nb, t)
    _odd_layer(y_ref, m_ref[1, 0], ng_ref[1:2, :], fg_ref[...], wio_ref, cc_ref, cdw_ref, cdb_ref,
               lng_ref, lnb_ref, woo_ref, h_ref, pad_a, pad_b, bc_ref, ga_ref, gb_ref, ab_ref, nb, t)


def _rpb_rows(rpb_ref, e_ref):
    n = rpb_ref.shape[0]
    lane = lax.broadcasted_iota(jnp.int32, (n, LANES), 1)
    i = jnp.where(lane < GRID_W, lane, lane - LANES)
    idx = jnp.clip(i, -(WIN_W - 1), WIN_W - 1) + (WIN_W - 1)
    rp = rpb_ref[...]
    e = jnp.zeros((n, LANES), F32)
    for d in range(2 * WIN_W - 1):
        e = jnp.where(idx == d, rp[:, d:d + 1], e)
    e_ref[...] = e


def _bias_tables(e_ref, bias_ref, j):
    q = lax.broadcasted_iota(jnp.int32, (GRID_W, LANES), 0)
    lane = lax.broadcasted_iota(jnp.int32, (GRID_W, LANES), 1)
    kw = jnp.where(lane < GRID_W, lane, lane - GRID_W)
    start = jnp.clip(q - WIN_W // 2, 0, GRID_W - WIN_W)
    col_ok = (kw >= start) & (kw < start + WIN_W)
    n_dr = 2 * WIN_H - 1
    for o in range(WIN_H):
        for e in range(2):
            h = 2 * j + e
            for jp in range(WIN_H // 2):
                dr = (WIN_H - 1) - o + 2 * jp
                r_lo = h * n_dr + dr
                lo = jnp.broadcast_to(e_ref[r_lo:r_lo + 1, :], (GRID_W, LANES))
                hi = jnp.broadcast_to(e_ref[r_lo + 1:r_lo + 2, :], (GRID_W, LANES))
                lo = pltpu.roll(lo, 0, 1, stride=1, stride_axis=0)
                hi = pltpu.roll(hi, GRID_W, 1, stride=1, stride_axis=0)
                tile = jnp.where(lane < GRID_W, lo, hi)
                bias_ref[o, e * GRID_W:(e + 1) * GRID_W, _lanes(jp)] = jnp.where(col_ok, tile, MASKED)


def _neighbourhood_attention(q_ref, k_ref, v_ref, ck_ref, cv_ref, e_ref, bias_ref, kvc_ref, gb_ref,
                             ab_ref, t):
    grid_h = t // GRID_W
    band = WIN_H * GRID_W
    for j in range(N_HEADS // 2):
        ln = _lanes(j)
        _bias_tables(e_ref, bias_ref, j)
        for i, src in enumerate((ck_ref, cv_ref)):
            kvc_ref[i] = jnp.concatenate([src[0, 0, 2 * j], src[0, 0, 2 * j + 1]],
                                         axis=-1).astype(BF16)

        def per_row(r, carry, ln=ln, j=j):
            start = jnp.clip(r - WIN_H // 2, 0, grid_h - WIN_H)
            rows = _rows(r * GRID_W, GRID_W, GRID_W)
            keys = _rows(start * GRID_W, band, GRID_W)
            q2 = _split_heads(q_ref[rows, ln])
            s_loc = _dot_nt(q2, k_ref[keys, ln]) + bias_ref[r - start]
            s_ctx = _dot_nt(q2, kvc_ref[0])
            mx = jnp.maximum(jnp.max(s_loc, axis=-1, keepdims=True),
                             jnp.max(s_ctx, axis=-1, keepdims=True))
            p_loc = jnp.exp(s_loc - mx)
            p_ctx = jnp.exp(s_ctx - mx)
            den = jnp.sum(p_loc, axis=-1, keepdims=True) + jnp.sum(p_ctx, axis=-1, keepdims=True)
            o = (_dot(p_loc.astype(BF16), v_ref[keys, ln])
                 + _dot(p_ctx.astype(BF16), kvc_ref[1])) / den
            ab_ref[rows, W_HALF + j * LANES:W_HALF + (j + 1) * LANES] = (
                _merge_heads(o) * gb_ref[rows, ln]).astype(BF16)
            return carry
        lax.fori_loop(0, grid_h, per_row, 0)


def _sample_body(x_ref, m_ref, ng_ref, fg_ref, wie_ref, wp_ref, ps_ref, woe_ref, wio_ref, cc_ref,
                 cdw_ref, cdb_ref, lng_ref, lnb_ref, woo_ref, ck_ref, cv_ref, rpb_ref,
                 y_ref,
                 h_ref, pad_a, pad_b, ga_ref, gb_ref, bc_ref, q_ref, k_ref, v_ref, ab_ref,
                 e_ref, bias_ref, kvc_ref, *, t):
    _zero_pads(pad_a, 1, t)
    _zero_pads(pad_b, 1, t)
    _rpb_rows(rpb_ref, e_ref)
    _even_in_proj(x_ref, m_ref[0, 0], ng_ref[0:1, :], wie_ref, h_ref, pad_a, ga_ref, gb_ref,
                  q_ref, k_ref, v_ref, None, 1, t)
    _pool_phase(pad_a, ga_ref, wp_ref, ps_ref, ab_ref, 1, t)
    _neighbourhood_attention(q_ref, k_ref, v_ref, ck_ref, cv_ref, e_ref, bias_ref, kvc_ref, gb_ref,
                             ab_ref, t)
    _even_out_proj(x_ref, y_ref, m_ref[0, 0], woe_ref, ab_ref, 1, t)
    _odd_layer(y_ref, m_ref[1, 0], ng_ref[1:2, :], fg_ref[...], wio_ref, cc_ref, cdw_ref, cdb_ref,
               lng_ref, lnb_ref, woo_ref, h_ref, pad_a, pad_b, bc_ref, ga_ref, gb_ref, ab_ref, 1, t)


def _const_spec(shape):
    zeros = (0,) * len(shape)
    return pl.BlockSpec(shape, lambda i: zeros, pipeline_mode=pl.Buffered(1))


def _stream_scratch(nb, t):
    r = nb * t
    padded = nb * (t + 2 * PAD)
    return [
        pltpu.VMEM((r, D_MODEL), BF16),
        pltpu.VMEM((padded, W_HALF), F32),
        pltpu.VMEM((padded, W_HALF), F32),
        pltpu.VMEM((r, W_HALF), F32),
        pltpu.VMEM((r, W_HALF), F32),
        pltpu.VMEM((r, W_HALF), F32),
        pltpu.VMEM((r, W_HALF), BF16),
        pltpu.VMEM((r, W_HALF), BF16),
        pltpu.VMEM((r, W_HALF), BF16),
        pltpu.VMEM((r, D_MODEL), BF16),
    ]


def _weight_args(norm_g, final_g, w_in_even, w_pool, pool_scale, w_out_even, w_in_odd, conv_c,
                 conv_d, conv_d_b, ln_g, ln_b, w_out_odd):
    args = [
        norm_g, final_g.reshape(1, D_MODEL),
        w_in_even[0].astype(BF16), w_pool[0].astype(BF16), pool_scale, w_out_even[0].astype(BF16),
        w_in_odd[0].astype(BF16), conv_c[0], conv_d[0], conv_d_b, ln_g, ln_b,
        w_out_odd[0].astype(BF16),
    ]
    return args, [_const_spec(a.shape) for a in args]


def kernel(x_prompt, x_sample, cache_k, cache_v, c, c_ctx, norm_g, w_mod, b_mod, w_in_even, w_pool,
           pool_scale, rpb, w_out_even, w_in_odd, conv_c, conv_d, conv_d_b, ln_g, ln_b, w_out_odd,
           final_g):
    batch, seq, d = x_prompt.shape
    dec_batch, dec_seq, _ = x_sample.shape
    assert d == D_MODEL and w_mod.shape[0] == 2 and w_in_even.shape[0] == 1 and w_in_odd.shape[0] == 1
    assert seq == ROW_CHUNK and dec_seq % ROW_CHUNK == 0 and dec_seq // GRID_W >= WIN_H

    cond_rows = SUBLANES * ((1 + dec_batch + SUBLANES - 1) // SUBLANES)
    cond = jnp.concatenate(
        [c_ctx[None, :], c, jnp.zeros((cond_rows - 1 - dec_batch, d), F32)], axis=0)
    m = _modulation(cond, w_mod, b_mod).reshape(2, cond_rows, 1, 3 * d)

    w_args, w_specs = _weight_args(norm_g, final_g, w_in_even, w_pool, pool_scale, w_out_even,
                                   w_in_odd, conv_c, conv_d, conv_d_b, ln_g, ln_b, w_out_odd)

    nb = 2
    assert batch % nb == 0
    kv_shape = jax.ShapeDtypeStruct((batch, 1, N_HEADS, seq, HEAD_DIM), F32)
    kv_spec = pl.BlockSpec((nb, 1, N_HEADS, seq, HEAD_DIM), lambda i: (i, 0, 0, 0, 0))
    y_prompt, new_k, new_v = pl.pallas_call(
        functools.partial(_prompt_body, nb=nb, t=seq),
        out_shape=(jax.ShapeDtypeStruct(x_prompt.shape, F32), kv_shape, kv_shape),
        grid=(batch // nb,),
        in_specs=[pl.BlockSpec((nb, seq, d), lambda i: (i, 0, 0)),
                  pl.BlockSpec((2, 1, 1, 3 * d), lambda i: (0, 0, 0, 0))] + w_specs,
        out_specs=(pl.BlockSpec((nb, seq, d), lambda i: (i, 0, 0)), kv_spec, kv_spec),
        scratch_shapes=_stream_scratch(nb, seq),
        compiler_params=pltpu.CompilerParams(dimension_semantics=("arbitrary",),
                                             vmem_limit_bytes=VMEM_LIMIT),
        name="prompt",
    )(x_prompt, m, *w_args)

    past = cache_k.shape[3]
    cache_spec = pl.BlockSpec((1, 1, N_HEADS, past, HEAD_DIM), lambda i: (i, 0, 0, 0, 0))
    rpb2 = rpb[0].reshape(N_HEADS * (2 * WIN_H - 1), 2 * WIN_W - 1)
    y_sample = pl.pallas_call(
        functools.partial(_sample_body, t=dec_seq),
        out_shape=jax.ShapeDtypeStruct(x_sample.shape, F32),
        grid=(dec_batch,),
        in_specs=[pl.BlockSpec((1, dec_seq, d), lambda i: (i, 0, 0), pipeline_mode=pl.Buffered(1)),
                  pl.BlockSpec((2, 1, 1, 3 * d), lambda i: (0, i + 1, 0, 0))] + w_specs
                 + [cache_spec, cache_spec, _const_spec(rpb2.shape)],
        out_specs=pl.BlockSpec((1, dec_seq, d), lambda i: (i, 0, 0)),
        scratch_shapes=_stream_scratch(1, dec_seq) + [
            pltpu.VMEM(rpb2.shape[:1] + (LANES,), F32),
            pltpu.VMEM((WIN_H, 2 * GRID_W, WIN_H * GRID_W), F32),
            pltpu.VMEM((2, past, LANES), BF16),
        ],
        compiler_params=pltpu.CompilerParams(dimension_semantics=("arbitrary",),
                                             vmem_limit_bytes=VMEM_LIMIT),
        name="sample",
    )(x_sample, m, *w_args, cache_k, cache_v, rpb2)

    return (y_prompt, y_sample, new_k, new_v)
```

```python
import functools

import jax
import jax.numpy as jnp
from jax import lax
from jax.experimental import pallas as pl
from jax.experimental.pallas import tpu as pltpu

F32 = jnp.float32
BF16 = jnp.bfloat16

D_MODEL = 1024
W_HALF = 512
N_POOL_GROUPS = 4
POOL_HALF = (1, 2, 4, 8)
N_HEADS = 8
HEAD_DIM = 64
GRID_W = 64
WIN_H = 8
WIN_W = 16
CONV_C = 3
CONV_D = 31
EPS = 1e-6
MASKED = -1e30

LANES = 128
SUBLANES = 8
PAD = 16
ROW_CHUNK = 256
NORM_ROWS = 32
MIX_ROWS = 64
Q_ROWS = 128
NA_UNROLL = 4
MOD_COLS = 768
VMEM_LIMIT = 58 * 1024 * 1024

assert PAD >= CONV_D // 2 + 1 and PAD % SUBLANES == 0 and PAD >= 2 * SUBLANES


def _sigmoid(x):
    return 1.0 / (1.0 + jnp.exp(-x))


def _silu(x):
    return x * _sigmoid(x)


def _dot(a, b):
    return jnp.dot(a, b, preferred_element_type=F32)


def _dot_nt(a, b):
    return lax.dot_general(a, b, (((1,), (1,)), ((), ())), preferred_element_type=F32)


def _lanes(j):
    return slice(j * LANES, (j + 1) * LANES)


def _group(g):
    return slice(g * W_HALF, (g + 1) * W_HALF)


def _rows(start, size, align):
    return pl.ds(pl.multiple_of(start, align), size)


def _mod_body(c_ref, w_ref, b_ref, o_ref):
    s = _silu(c_ref[...]).astype(BF16)
    o_ref[0] = _dot(s, w_ref[0].astype(BF16)) + b_ref[0]


def _modulation(cond, w_mod, b_mod):
    depth, d, n = w_mod.shape
    rows = cond.shape[0]
    return pl.pallas_call(
        _mod_body,
        out_shape=jax.ShapeDtypeStruct((depth, rows, n), F32),
        grid=(depth, n // MOD_COLS),
        in_specs=[
            pl.BlockSpec((rows, d), lambda l, j: (0, 0)),
            pl.BlockSpec((1, d, MOD_COLS), lambda l, j: (l, 0, j)),
            pl.BlockSpec((1, 1, MOD_COLS), lambda l, j: (l, 0, j)),
        ],
        out_specs=pl.BlockSpec((1, rows, MOD_COLS), lambda l, j: (l, 0, j)),
        compiler_params=pltpu.CompilerParams(dimension_semantics=("arbitrary", "arbitrary")),
        name="mod",
    )(cond, w_mod, b_mod.reshape(depth, 1, n))


def _seq_of_chunk(c, t):
    per_seq = t // ROW_CHUNK
    if per_seq == 1:
        return c, 0
    s = c // per_seq
    return s, (c - s * per_seq) * ROW_CHUNK


def _pad_row(s, off, t):
    return s * (t + 2 * PAD) + PAD + off


def _modnorm_chunk(src_ref, h_ref, s, off, r0, gain, shift):
    def step(i, carry):
        x = src_ref[s, _rows(off + i * NORM_ROWS, NORM_ROWS, NORM_ROWS), :]
        ms = jnp.mean(x * x, axis=-1, keepdims=True)
        h_ref[_rows(r0 + i * NORM_ROWS, NORM_ROWS, NORM_ROWS), :] = (
            x * lax.rsqrt(ms + EPS) * gain + shift).astype(BF16)
        return carry
    lax.fori_loop(0, ROW_CHUNK // NORM_ROWS, step, 0, unroll=True)


def _zero_pads(pad_ref, nb, t):
    z = jnp.zeros((PAD, W_HALF), F32)
    for s in range(nb):
        pad_ref[_pad_row(s, 0, t) - PAD:_pad_row(s, 0, t), :] = z
        pad_ref[_pad_row(s, t, t):_pad_row(s, t, t) + PAD, :] = z


def _pool_phase(pad_ref, ga_ref, wp_ref, ps_ref, ab_ref, nb, t):
    per_seq = t // MIX_ROWS

    def step(i, carry):
        s = i // per_seq
        r0 = (i - s * per_seq) * MIX_ROWS
        prow = _pad_row(s, r0, t)
        rows = _rows(i * MIX_ROWS, MIX_ROWS, MIX_ROWS)
        pos = r0 + lax.broadcasted_iota(jnp.int32, (MIX_ROWS, LANES), 0)
        for g in range(N_POOL_GROUPS):
            hw = POOL_HALF[g]
            ln = _lanes(g)
            blk = pad_ref[_rows(prow - SUBLANES, MIX_ROWS + 2 * SUBLANES, SUBLANES), ln]
            win = None
            for d in range(-hw, hw):
                sl = blk[SUBLANES + d:SUBLANES + d + MIX_ROWS]
                win = sl if win is None else win + sl
            cnt = (jnp.minimum(pos + hw, t) - jnp.maximum(pos - hw, 0)).astype(F32)
            p = (win / cnt - blk[SUBLANES:SUBLANES + MIX_ROWS]).astype(BF16)
            y = _dot(p, wp_ref[g]) * ps_ref[:, ln] * ga_ref[rows, ln]
            ab_ref[rows, ln] = y.astype(BF16)
        return carry
    lax.fori_loop(0, nb * per_seq, step, 0)


def _out_proj_chunk(ab_ref, w_ref, x_ref, gate, dst_ref, s, off, r0):
    lhs = ab_ref[_rows(r0, ROW_CHUNK, ROW_CHUNK), :]
    rows = _rows(off, ROW_CHUNK, ROW_CHUNK)
    for g in range(D_MODEL // W_HALF):
        y = _dot(lhs, w_ref[:, _group(g)])
        dst_ref[s, rows, _group(g)] = x_ref[s, rows, _group(g)] + gate[:, _group(g)] * y


def _conv_phase(pad_c, pad_d, bc_ref, ga_ref, gb_ref, cc_ref, cdw_ref, cdb_ref, lng_ref, lnb_ref,
                ab_ref, nb, t):
    per_seq = t // MIX_ROWS

    def step(i, carry):
        s = i // per_seq
        r0 = (i - s * per_seq) * MIX_ROWS
        prow = _pad_row(s, r0, t)
        rows = _rows(i * MIX_ROWS, MIX_ROWS, MIX_ROWS)
        z = []
        for g in range(W_HALF // LANES):
            ln = _lanes(g)
            blk = pad_c[_rows(prow - SUBLANES, MIX_ROWS + 2 * SUBLANES, SUBLANES), ln]
            c3 = None
            for j in range(CONV_C):
                o = SUBLANES + j - CONV_C // 2
                term = blk[o:o + MIX_ROWS] * cc_ref[j:j + 1, ln]
                c3 = term if c3 is None else c3 + term
            ab_ref[rows, ln] = (bc_ref[rows, ln] * c3 * ga_ref[rows, ln]).astype(BF16)
            acc = None
            for sft in range(SUBLANES):
                part = None
                for a in range((CONV_D - sft + SUBLANES - 1) // SUBLANES):
                    j = SUBLANES * a + sft
                    src = pad_d[_rows(prow - 2 * SUBLANES + SUBLANES * a, MIX_ROWS + SUBLANES,
                                      SUBLANES), ln]
                    term = src * cdw_ref[j:j + 1, ln]
                    part = term if part is None else part + term
                o = SUBLANES + sft - (CONV_D // 2 - SUBLANES)
                part = part[o:o + MIX_ROWS]
                acc = part if acc is None else acc + part
            z.append(acc + cdb_ref[:, ln])
        z = jnp.concatenate(z, axis=-1)
        mu = jnp.mean(z, axis=-1, keepdims=True)
        zc = z - mu
        var = jnp.mean(zc * zc, axis=-1, keepdims=True)
        zn = zc * lax.rsqrt(var + EPS) * lng_ref[...] + lnb_ref[...]
        ab_ref[rows, W_HALF:] = (_silu(zn) * gb_ref[rows, :]).astype(BF16)
        return carry
    lax.fori_loop(0, nb * per_seq, step, 0)


def _final_norm_chunk(y_ref, fg, s, off):
    def step(i, carry):
        rows = _rows(off + i * NORM_ROWS, NORM_ROWS, NORM_ROWS)
        x = y_ref[s, rows, :]
        ms = jnp.mean(x * x, axis=-1, keepdims=True)
        y_ref[s, rows, :] = x * lax.rsqrt(ms + EPS) * fg
        return carry
    lax.fori_loop(0, ROW_CHUNK // NORM_ROWS, step, 0, unroll=True)


def _odd_layer(y_ref, m_row, g_row, fg, wio_ref, cc_ref, cdw_ref, cdb_ref, lng_ref, lnb_ref, woo_ref,
               h_ref, pad_c, pad_d, bc_ref, ga_ref, gb_ref, ab_ref, nb, t):
    shift = m_row[:, :D_MODEL]
    gain = g_row * (1.0 + m_row[:, D_MODEL:2 * D_MODEL])
    gate = m_row[:, 2 * D_MODEL:]
    n_chunks = nb * t // ROW_CHUNK

    def in_proj(c, carry):
        s, off = _seq_of_chunk(c, t)
        r0 = c * ROW_CHUNK
        _modnorm_chunk(y_ref, h_ref, s, off, r0, gain, shift)
        rows = _rows(r0, ROW_CHUNK, ROW_CHUNK)
        prow = _rows(_pad_row(s, off, t), ROW_CHUNK, SUBLANES)
        h = h_ref[rows, :]
        bc_ref[rows, :] = _dot(h, wio_ref[:, _group(0)])
        pad_c[prow, :] = _dot(h, wio_ref[:, _group(1)])
        pad_c[prow, :] = pad_c[prow, :] * _dot(h, wio_ref[:, _group(2)])
        ga_ref[rows, :] = _silu(_dot(h, wio_ref[:, _group(3)]))
        pad_d[prow, :] = _dot(h, wio_ref[:, _group(4)])
        pad_d[prow, :] = pad_d[prow, :] * _sigmoid(_dot(h, wio_ref[:, _group(5)]))
        gb_ref[rows, :] = _silu(_dot(h, wio_ref[:, _group(6)]))
        return carry
    lax.fori_loop(0, n_chunks, in_proj, 0)

    _conv_phase(pad_c, pad_d, bc_ref, ga_ref, gb_ref, cc_ref, cdw_ref, cdb_ref, lng_ref, lnb_ref,
                ab_ref, nb, t)

    def out_proj(c, carry):
        s, off = _seq_of_chunk(c, t)
        _out_proj_chunk(ab_ref, woo_ref, y_ref, gate, y_ref, s, off, c * ROW_CHUNK)
        _final_norm_chunk(y_ref, fg, s, off)
        return carry
    lax.fori_loop(0, n_chunks, out_proj, 0)


def _even_in_proj(x_ref, m_row, g_row, w_ref, h_ref, pad_a, ga_ref, gb_ref, q_ref, k_ref, v_ref,
                  kv_out, nb, t):
    shift = m_row[:, :D_MODEL]
    gain = g_row * (1.0 + m_row[:, D_MODEL:2 * D_MODEL])

    def in_proj(c, carry):
        s, off = _seq_of_chunk(c, t)
        r0 = c * ROW_CHUNK
        _modnorm_chunk(x_ref, h_ref, s, off, r0, gain, shift)
        rows = _rows(r0, ROW_CHUNK, ROW_CHUNK)
        prow = _rows(_pad_row(s, off, t), ROW_CHUNK, SUBLANES)
        h = h_ref[rows, :]
        pad_a[prow, :] = _dot(h, w_ref[:, _group(0)])
        ga_ref[rows, :] = _silu(_dot(h, w_ref[:, _group(1)]))
        q_ref[rows, :] = (_dot(h, w_ref[:, _group(2)]) * (HEAD_DIM ** -0.5)).astype(BF16)
        for dst, out, g in ((k_ref, 0, 3), (v_ref, 1, 4)):
            acc = _dot(h, w_ref[:, _group(g)])
            dst[rows, :] = acc.astype(BF16)
            if kv_out is not None:
                for hd in range(N_HEADS):
                    kv_out[out][s, 0, hd, :, :] = acc[:, hd * HEAD_DIM:(hd + 1) * HEAD_DIM]
        gb_ref[rows, :] = _silu(_dot(h, w_ref[:, _group(5)]))
        return carry
    lax.fori_loop(0, nb * t // ROW_CHUNK, in_proj, 0)


def _even_out_proj(x_ref, y_ref, m_row, w_ref, ab_ref, nb, t):
    gate = m_row[:, 2 * D_MODEL:]

    def out_proj(c, carry):
        s, off = _seq_of_chunk(c, t)
        _out_proj_chunk(ab_ref, w_ref, x_ref, gate, y_ref, s, off, c * ROW_CHUNK)
        return carry
    lax.fori_loop(0, nb * t // ROW_CHUNK, out_proj, 0)


def _split_heads(x):
    lane = lax.broadcasted_iota(jnp.int32, (1, LANES), 1)
    first = jnp.where(lane < HEAD_DIM, 1.0, 0.0).astype(x.dtype)
    return jnp.concatenate([x * first, x * (1 - first)], axis=0)


def _merge_heads(o):
    n = o.shape[0] // 2
    lane = lax.broadcasted_iota(jnp.int32, (n, LANES), 1)
    return jnp.where(lane < HEAD_DIM, o[:n], o[n:])


def _context_attention(q_ref, k_ref, v_ref, gb_ref, ab_ref, nb, t):
    assert t == ROW_CHUNK

    def per_seq(s, carry):
        seq = _rows(s * t, t, t)
        for j in range(N_HEADS // 2):
            ln = _lanes(j)
            kp = k_ref[seq, ln]
            vp = v_ref[seq, ln]
            for r0 in range(0, t, Q_ROWS):
                rows = _rows(s * t + r0, Q_ROWS, Q_ROWS)
                sc = _dot_nt(_split_heads(q_ref[rows, ln]), kp)
                p = jnp.exp(sc - jnp.max(sc, axis=-1, keepdims=True))
                o = _dot(p.astype(BF16), vp) / jnp.sum(p, axis=-1, keepdims=True)
                ab_ref[rows, W_HALF + j * LANES:W_HALF + (j + 1) * LANES] = (
                    _merge_heads(o) * gb_ref[rows, ln]).astype(BF16)
        return carry
    lax.fori_loop(0, nb, per_seq, 0)


def _prompt_body(x_ref, m_ref, ng_ref, fg_ref, wie_ref, wp_ref, ps_ref, woe_ref, wio_ref, cc_ref,
                 cdw_ref, cdb_ref, lng_ref, lnb_ref, woo_ref,
                 y_ref, ko_ref, vo_ref,
                 h_ref, pad_a, pad_b, ga_ref, gb_ref, bc_ref, q_ref, k_ref, v_ref, ab_ref,
                 *, nb, t):
    _zero_pads(pad_a, nb, t)
    _zero_pads(pad_b, nb, t)
    _even_in_proj(x_ref, m_ref[0, 0], ng_ref[0:1, :], wie_ref, h_ref, pad_a, ga_ref, gb_ref,
                  q_ref, k_ref, v_ref, (ko_ref, vo_ref), nb, t)
    _pool_phase(pad_a, ga_ref, wp_ref, ps_ref, ab_ref, nb, t)
    _context_attention(q_ref, k_ref, v_ref, gb_ref, ab_ref, nb, t)
    _even_out_proj(x_ref, y_ref, m_ref[0, 0], woe_ref, ab_ref, nb, t)
    _odd_layer(y_ref, m_ref[1, 0], ng_ref[1:2, :], fg_ref[...], wio_ref, cc_ref, cdw_ref, cdb_ref,
               lng_ref, lnb_ref, woo_ref, h_ref, pad_a, pad_b, bc_ref, ga_ref, gb_ref, ab_ref, nb, t)


def _rpb_rows(rpb_ref, e_ref):
    n = rpb_ref.shape[0]
    lane = lax.broadcasted_iota(jnp.int32, (n, LANES), 1)
    i = jnp.where(lane < GRID_W, lane, lane - LANES)
    idx = jnp.clip(i, -(WIN_W - 1), WIN_W - 1) + (WIN_W - 1)
    rp = rpb_ref[...]
    e = jnp.zeros((n, LANES), F32)
    for d in range(2 * WIN_W - 1):
        e = jnp.where(idx == d, rp[:, d:d + 1], e)
    e_ref[...] = e


def _bias_tables(e_ref, bias_ref, j):
    q = lax.broadcasted_iota(jnp.int32, (GRID_W, LANES), 0)
    lane = lax.broadcasted_iota(jnp.int32, (GRID_W, LANES), 1)
    kw = jnp.where(lane < GRID_W, lane, lane - GRID_W)
    start = jnp.clip(q - WIN_W // 2, 0, GRID_W - WIN_W)
    col_ok = (kw >= start) & (kw < start + WIN_W)
    n_dr = 2 * WIN_H - 1
    for o in range(WIN_H):
        for e in range(2):
            h = 2 * j + e
            for jp in range(WIN_H // 2):
                dr = (WIN_H - 1) - o + 2 * jp
                r_lo = h * n_dr + dr
                lo = jnp.broadcast_to(e_ref[r_lo:r_lo + 1, :], (GRID_W, LANES))
                hi = jnp.broadcast_to(e_ref[r_lo + 1:r_lo + 2, :], (GRID_W, LANES))
                lo = pltpu.roll(lo, 0, 1, stride=1, stride_axis=0)
                hi = pltpu.roll(hi, GRID_W, 1, stride=1, stride_axis=0)
                tile = jnp.where(lane < GRID_W, lo, hi)
                bias_ref[o, e * GRID_W:(e + 1) * GRID_W, _lanes(jp)] = jnp.where(col_ok, tile, MASKED)


def _neighbourhood_attention(q_ref, k_ref, v_ref, ck_ref, cv_ref, e_ref, bias_ref, kvc_ref, gb_ref,
                             ab_ref, t):
    grid_h = t // GRID_W
    band = WIN_H * GRID_W
    for j in range(N_HEADS // 2):
        ln = _lanes(j)
        _bias_tables(e_ref, bias_ref, j)
        for i, src in enumerate((ck_ref, cv_ref)):
            kvc_ref[i] = jnp.concatenate([src[0, 0, 2 * j], src[0, 0, 2 * j + 1]],
                                         axis=-1).astype(BF16)

        def per_row(r, carry, ln=ln, j=j):
            start = jnp.clip(r - WIN_H // 2, 0, grid_h - WIN_H)
            rows = _rows(r * GRID_W, GRID_W, GRID_W)
            keys = _rows(start * GRID_W, band, GRID_W)
            q2 = _split_heads(q_ref[rows, ln])
            s_loc = _dot_nt(q2, k_ref[keys, ln]) + bias_ref[r - start]
            s_ctx = _dot_nt(q2, kvc_ref[0])
            mx = jnp.maximum(jnp.max(s_loc, axis=-1, keepdims=True),
                             jnp.max(s_ctx, axis=-1, keepdims=True))
            p_loc = jnp.exp(s_loc - mx)
            p_ctx = jnp.exp(s_ctx - mx)
            den = jnp.sum(p_loc, axis=-1, keepdims=True) + jnp.sum(p_ctx, axis=-1, keepdims=True)
            o = (_dot(p_loc.astype(BF16), v_ref[keys, ln])
                 + _dot(p_ctx.astype(BF16), kvc_ref[1])) / den
            ab_ref[rows, W_HALF + j * LANES:W_HALF + (j + 1) * LANES] = (
                _merge_heads(o) * gb_ref[rows, ln]).astype(BF16)
            return carry
        lax.fori_loop(0, grid_h, per_row, 0, unroll=NA_UNROLL)


def _sample_body(x_ref, m_ref, ng_ref, fg_ref, wie_ref, wp_ref, ps_ref, woe_ref, wio_ref, cc_ref,
                 cdw_ref, cdb_ref, lng_ref, lnb_ref, woo_ref, ck_ref, cv_ref, rpb_ref,
                 y_ref,
                 h_ref, pad_a, pad_b, ga_ref, gb_ref, bc_ref, q_ref, k_ref, v_ref, ab_ref,
                 e_ref, bias_ref, kvc_ref, *, t):
    _zero_pads(pad_a, 1, t)
    _zero_pads(pad_b, 1, t)
    _rpb_rows(rpb_ref, e_ref)
    _even_in_proj(x_ref, m_ref[0, 0], ng_ref[0:1, :], wie_ref, h_ref, pad_a, ga_ref, gb_ref,
                  q_ref, k_ref, v_ref, None, 1, t)
    _pool_phase(pad_a, ga_ref, wp_ref, ps_ref, ab_ref, 1, t)
    _neighbourhood_attention(q_ref, k_ref, v_ref, ck_ref, cv_ref, e_ref, bias_ref, kvc_ref, gb_ref,
                             ab_ref, t)
    _even_out_proj(x_ref, y_ref, m_ref[0, 0], woe_ref, ab_ref, 1, t)
    _odd_layer(y_ref, m_ref[1, 0], ng_ref[1:2, :], fg_ref[...], wio_ref, cc_ref, cdw_ref, cdb_ref,
               lng_ref, lnb_ref, woo_ref, h_ref, pad_a, pad_b, bc_ref, ga_ref, gb_ref, ab_ref, 1, t)


def _const_spec(shape):
    zeros = (0,) * len(shape)
    return pl.BlockSpec(shape, lambda i: zeros, pipeline_mode=pl.Buffered(1))


def _stream_scratch(nb, t):
    r = nb * t
    padded = nb * (t + 2 * PAD)
    return [
        pltpu.VMEM((r, D_MODEL), BF16),
        pltpu.VMEM((padded, W_HALF), F32),
        pltpu.VMEM((padded, W_HALF), F32),
        pltpu.VMEM((r, W_HALF), F32),
        pltpu.VMEM((r, W_HALF), F32),
        pltpu.VMEM((r, W_HALF), F32),
        pltpu.VMEM((r, W_HALF), BF16),
        pltpu.VMEM((r, W_HALF), BF16),
        pltpu.VMEM((r, W_HALF), BF16),
        pltpu.VMEM((r, D_MODEL), BF16),
    ]


def _weight_args(norm_g, final_g, w_in_even, w_pool, pool_scale, w_out_even, w_in_odd, conv_c,
                 conv_d, conv_d_b, ln_g, ln_b, w_out_odd):
    args = [
        norm_g, final_g.reshape(1, D_MODEL),
        w_in_even[0].astype(BF16), w_pool[0].astype(BF16), pool_scale, w_out_even[0].astype(BF16),
        w_in_odd[0].astype(BF16), conv_c[0], conv_d[0], conv_d_b, ln_g, ln_b,
        w_out_odd[0].astype(BF16),
    ]
    return args, [_const_spec(a.shape) for a in args]


def kernel(x_prompt, x_sample, cache_k, cache_v, c, c_ctx, norm_g, w_mod, b_mod, w_in_even, w_pool,
           pool_scale, rpb, w_out_even, w_in_odd, conv_c, conv_d, conv_d_b, ln_g, ln_b, w_out_odd,
           final_g):
    batch, seq, d = x_prompt.shape
    dec_batch, dec_seq, _ = x_sample.shape
    assert d == D_MODEL and w_mod.shape[0] == 2 and w_in_even.shape[0] == 1 and w_in_odd.shape[0] == 1
    assert seq == ROW_CHUNK and dec_seq % ROW_CHUNK == 0 and dec_seq // GRID_W >= WIN_H

    cond_rows = SUBLANES * ((1 + dec_batch + SUBLANES - 1) // SUBLANES)
    cond = jnp.concatenate(
        [c_ctx[None, :], c, jnp.zeros((cond_rows - 1 - dec_batch, d), F32)], axis=0)
    m = _modulation(cond, w_mod, b_mod).reshape(2, cond_rows, 1, 3 * d)

    w_args, w_specs = _weight_args(norm_g, final_g, w_in_even, w_pool, pool_scale, w_out_even,
                                   w_in_odd, conv_c, conv_d, conv_d_b, ln_g, ln_b, w_out_odd)

    nb = 2
    assert batch % nb == 0
    kv_shape = jax.ShapeDtypeStruct((batch, 1, N_HEADS, seq, HEAD_DIM), F32)
    kv_spec = pl.BlockSpec((nb, 1, N_HEADS, seq, HEAD_DIM), lambda i: (i, 0, 0, 0, 0))
    y_prompt, new_k, new_v = pl.pallas_call(
        functools.partial(_prompt_body, nb=nb, t=seq),
        out_shape=(jax.ShapeDtypeStruct(x_prompt.shape, F32), kv_shape, kv_shape),
        grid=(batch // nb,),
        in_specs=[pl.BlockSpec((nb, seq, d), lambda i: (i, 0, 0)),
                  pl.BlockSpec((2, 1, 1, 3 * d), lambda i: (0, 0, 0, 0))] + w_specs,
        out_specs=(pl.BlockSpec((nb, seq, d), lambda i: (i, 0, 0)), kv_spec, kv_spec),
        scratch_shapes=_stream_scratch(nb, seq),
        compiler_params=pltpu.CompilerParams(dimension_semantics=("arbitrary",),
                                             vmem_limit_bytes=VMEM_LIMIT),
        name="prompt",
    )(x_prompt, m, *w_args)

    past = cache_k.shape[3]
    cache_spec = pl.BlockSpec((1, 1, N_HEADS, past, HEAD_DIM), lambda i: (i, 0, 0, 0, 0))
    rpb2 = rpb[0].reshape(N_HEADS * (2 * WIN_H - 1), 2 * WIN_W - 1)
    y_sample = pl.pallas_call(
        functools.partial(_sample_body, t=dec_seq),
        out_shape=jax.ShapeDtypeStruct(x_sample.shape, F32),
        grid=(dec_batch,),
        in_specs=[pl.BlockSpec((1, dec_seq, d), lambda i: (i, 0, 0), pipeline_mode=pl.Buffered(1)),
                  pl.BlockSpec((2, 1, 1, 3 * d), lambda i: (0, i + 1, 0, 0))] + w_specs
                 + [cache_spec, cache_spec, _const_spec(rpb2.shape)],
        out_specs=pl.BlockSpec((1, dec_seq, d), lambda i: (i, 0, 0)),
        scratch_shapes=_stream_scratch(1, dec_seq) + [
            pltpu.VMEM(rpb2.shape[:1] + (LANES,), F32),
            pltpu.VMEM((WIN_H, 2 * GRID_W, WIN_H * GRID_W), F32),
            pltpu.VMEM((2, past, LANES), BF16),
        ],
        compiler_params=pltpu.CompilerParams(dimension_semantics=("arbitrary",),
                                             vmem_limit_bytes=VMEM_LIMIT),
        name="sample",
    )(x_sample, m, *w_args, cache_k, cache_v, rpb2)

    return (y_prompt, y_sample, new_k, new_v)
```

```python
import functools

import jax
import jax.numpy as jnp
from jax import lax
from jax.experimental import pallas as pl
from jax.experimental.pallas import tpu as pltpu

F32 = jnp.float32
BF16 = jnp.bfloat16

D_MODEL = 1024
W_HALF = 512
N_POOL_GROUPS = 4
POOL_HALF = (1, 2, 4, 8)
N_HEADS = 8
HEAD_DIM = 64
GRID_W = 64
WIN_H = 8
WIN_W = 16
CONV_C = 3
CONV_D = 31
EPS = 1e-6
MASKED = -1e30

LANES = 128
SUBLANES = 8
PAD = 16
ROW_CHUNK = 512
NORM_ROWS = 32
MIX_ROWS = 64
Q_ROWS = 128
NB_PROMPT = 2
NA_UNROLL = 4
MOD_COLS = 1536
VMEM_LIMIT = 58 * 1024 * 1024

assert PAD >= CONV_D // 2 + 1 and PAD % SUBLANES == 0 and PAD >= 2 * SUBLANES
assert max(POOL_HALF) <= SUBLANES


def _sigmoid(x):
    return 1.0 / (1.0 + jnp.exp(-x))


def _silu(x):
    return x * _sigmoid(x)


def _dot(a, b):
    return jnp.dot(a, b, preferred_element_type=F32)


def _dot_nt(a, b):
    return lax.dot_general(a, b, (((1,), (1,)), ((), ())), preferred_element_type=F32)


def _lanes(j):
    return slice(j * LANES, (j + 1) * LANES)


def _group(g):
    return slice(g * W_HALF, (g + 1) * W_HALF)


def _rows(start, size, align):
    if isinstance(start, int):
        return slice(start, start + size)
    return pl.ds(pl.multiple_of(start, align), size)


def _mod_body(c_ref, w_ref, b_ref, o_ref):
    s = _silu(c_ref[...]).astype(BF16)
    o_ref[0] = _dot(s, w_ref[0].astype(BF16)) + b_ref[0]


def _modulation(cond, w_mod, b_mod):
    depth, d, n = w_mod.shape
    rows = cond.shape[0]
    return pl.pallas_call(
        _mod_body,
        out_shape=jax.ShapeDtypeStruct((depth, rows, n), F32),
        grid=(depth, n // MOD_COLS),
        in_specs=[
            pl.BlockSpec((rows, d), lambda l, j: (0, 0)),
            pl.BlockSpec((1, d, MOD_COLS), lambda l, j: (l, 0, j)),
            pl.BlockSpec((1, 1, MOD_COLS), lambda l, j: (l, 0, j)),
        ],
        out_specs=pl.BlockSpec((1, rows, MOD_COLS), lambda l, j: (l, 0, j)),
        compiler_params=pltpu.CompilerParams(dimension_semantics=("arbitrary", "arbitrary")),
        name="mod",
    )(cond, w_mod, b_mod.reshape(depth, 1, n))


def _pieces(c, nb, t):
    if t >= ROW_CHUNK:
        per_seq = t // ROW_CHUNK
        s = 0 if nb == 1 else c // per_seq
        return [(s, (c - s * per_seq) * ROW_CHUNK, ROW_CHUNK, 0)]
    per_chunk = ROW_CHUNK // t
    return [(c * per_chunk + i, 0, t, i * t) for i in range(per_chunk)]


def _for_chunks(n, body):
    if n == 1:
        body(0)
    else:
        lax.fori_loop(0, n, lambda c, carry: (body(c), carry)[1], 0)


def _pad_row(s, off, t):
    return s * (t + 2 * PAD) + PAD + off


def _store_padded(pad_ref, val, pieces, t):
    for s, off, n, o in pieces:
        pad_ref[_rows(_pad_row(s, off, t), n, SUBLANES), :] = val[o:o + n]


def _scale_padded(pad_ref, val, pieces, t):
    for s, off, n, o in pieces:
        rows = _rows(_pad_row(s, off, t), n, SUBLANES)
        pad_ref[rows, :] = pad_ref[rows, :] * val[o:o + n]


def _modnorm_chunk(src_ref, h_ref, c, nb, t, gain, shift):
    for s, off, n, o in _pieces(c, nb, t):
        for i in range(0, n, NORM_ROWS):
            x = src_ref[s, _rows(off + i, NORM_ROWS, NORM_ROWS), :]
            ms = jnp.mean(x * x, axis=-1, keepdims=True)
            h_ref[_rows(c * ROW_CHUNK + o + i, NORM_ROWS, NORM_ROWS), :] = (
                x * lax.rsqrt(ms + EPS) * gain + shift).astype(BF16)


def _zero_pads(pad_ref, nb, t):
    z = jnp.zeros((PAD, W_HALF), F32)
    for s in range(nb):
        pad_ref[_pad_row(s, 0, t) - PAD:_pad_row(s, 0, t), :] = z
        pad_ref[_pad_row(s, t, t):_pad_row(s, t, t) + PAD, :] = z


def _pool_phase(pad_ref, ga_ref, wp_ref, ps_ref, ab_ref, nb, t):
    per_seq = t // MIX_ROWS

    def step(i, carry):
        s = i // per_seq
        r0 = (i - s * per_seq) * MIX_ROWS
        prow = _pad_row(s, r0, t)
        rows = _rows(i * MIX_ROWS, MIX_ROWS, MIX_ROWS)
        pos = r0 + lax.broadcasted_iota(jnp.int32, (MIX_ROWS, LANES), 0)
        for g in range(N_POOL_GROUPS):
            hw = POOL_HALF[g]
            ln = _lanes(g)
            halo = MIX_ROWS + 2 * SUBLANES
            blk = pad_ref[_rows(prow - SUBLANES, halo, SUBLANES), ln]
            run, n = blk, 1
            while n < 2 * hw:
                run = run + pltpu.roll(run, halo - n, 0)
                n *= 2
            if hw < SUBLANES:
                run = pltpu.roll(run, halo - (SUBLANES - hw), 0)
            win = run[:MIX_ROWS]
            cnt = (jnp.minimum(pos + hw, t) - jnp.maximum(pos - hw, 0)).astype(F32)
            p = (win / cnt - blk[SUBLANES:SUBLANES + MIX_ROWS]).astype(BF16)
            y = _dot(p, wp_ref[g]) * ps_ref[:, ln] * ga_ref[rows, ln]
            ab_ref[rows, ln] = y.astype(BF16)
        return carry
    lax.fori_loop(0, nb * per_seq, step, 0)


def _out_proj_chunk(ab_ref, w_ref, x_ref, gate, dst_ref, c, nb, t):
    lhs = ab_ref[_rows(c * ROW_CHUNK, ROW_CHUNK, ROW_CHUNK), :]
    for g in range(D_MODEL // W_HALF):
        y = _dot(lhs, w_ref[:, _group(g)])
        for s, off, n, o in _pieces(c, nb, t):
            rows = _rows(off, n, n)
            dst_ref[s, rows, _group(g)] = x_ref[s, rows, _group(g)] + gate[:, _group(g)] * y[o:o + n]


def _conv_phase(pad_c, pad_d, bc_ref, ga_ref, gb_ref, cc_ref, cdw_ref, cdb_ref, lng_ref, lnb_ref,
                ab_ref, nb, t):
    per_seq = t // MIX_ROWS

    def step(i, carry):
        s = i // per_seq
        r0 = (i - s * per_seq) * MIX_ROWS
        prow = _pad_row(s, r0, t)
        rows = _rows(i * MIX_ROWS, MIX_ROWS, MIX_ROWS)
        z = []
        for g in range(W_HALF // LANES):
            ln = _lanes(g)
            blk = pad_c[_rows(prow - SUBLANES, MIX_ROWS + 2 * SUBLANES, SUBLANES), ln]
            c3 = None
            for j in range(CONV_C):
                o = SUBLANES + j - CONV_C // 2
                term = blk[o:o + MIX_ROWS] * cc_ref[j:j + 1, ln]
                c3 = term if c3 is None else c3 + term
            ab_ref[rows, ln] = (bc_ref[rows, ln] * c3 * ga_ref[rows, ln]).astype(BF16)
            acc = None
            for sft in range(SUBLANES):
                part = None
                for a in range((CONV_D - sft + SUBLANES - 1) // SUBLANES):
                    j = SUBLANES * a + sft
                    src = pad_d[_rows(prow - 2 * SUBLANES + SUBLANES * a, MIX_ROWS + SUBLANES,
                                      SUBLANES), ln]
                    term = src * cdw_ref[j:j + 1, ln]
                    part = term if part is None else part + term
                o = SUBLANES + sft - (CONV_D // 2 - SUBLANES)
                part = part[o:o + MIX_ROWS]
                acc = part if acc is None else acc + part
            z.append(acc + cdb_ref[:, ln])
        z = jnp.concatenate(z, axis=-1)
        mu = jnp.mean(z, axis=-1, keepdims=True)
        zc = z - mu
        var = jnp.mean(zc * zc, axis=-1, keepdims=True)
        zn = zc * lax.rsqrt(var + EPS) * lng_ref[...] + lnb_ref[...]
        ab_ref[rows, W_HALF:] = (_silu(zn) * gb_ref[rows, :]).astype(BF16)
        return carry
    lax.fori_loop(0, nb * per_seq, step, 0)


def _final_norm_chunk(y_ref, fg, c, nb, t):
    for s, off, n, _ in _pieces(c, nb, t):
        for i in range(0, n, NORM_ROWS):
            rows = _rows(off + i, NORM_ROWS, NORM_ROWS)
            x = y_ref[s, rows, :]
            ms = jnp.mean(x * x, axis=-1, keepdims=True)
            y_ref[s, rows, :] = x * lax.rsqrt(ms + EPS) * fg


def _odd_layer(y_ref, m_row, g_row, fg, wio_ref, cc_ref, cdw_ref, cdb_ref, lng_ref, lnb_ref, woo_ref,
               h_ref, pad_c, pad_d, bc_ref, ga_ref, gb_ref, ab_ref, nb, t):
    shift = m_row[:, :D_MODEL]
    gain = g_row * (1.0 + m_row[:, D_MODEL:2 * D_MODEL])
    gate = m_row[:, 2 * D_MODEL:]
    n_chunks = nb * t // ROW_CHUNK

    def in_proj(c):
        _modnorm_chunk(y_ref, h_ref, c, nb, t, gain, shift)
        rows = _rows(c * ROW_CHUNK, ROW_CHUNK, ROW_CHUNK)
        pieces = _pieces(c, nb, t)
        h = h_ref[rows, :]
        bc_ref[rows, :] = _dot(h, wio_ref[:, _group(0)])
        _store_padded(pad_c, _dot(h, wio_ref[:, _group(1)]), pieces, t)
        _scale_padded(pad_c, _dot(h, wio_ref[:, _group(2)]), pieces, t)
        ga_ref[rows, :] = _silu(_dot(h, wio_ref[:, _group(3)]))
        _store_padded(pad_d, _dot(h, wio_ref[:, _group(4)]), pieces, t)
        _scale_padded(pad_d, _sigmoid(_dot(h, wio_ref[:, _group(5)])), pieces, t)
        gb_ref[rows, :] = _silu(_dot(h, wio_ref[:, _group(6)]))
    _for_chunks(n_chunks, in_proj)

    _conv_phase(pad_c, pad_d, bc_ref, ga_ref, gb_ref, cc_ref, cdw_ref, cdb_ref, lng_ref, lnb_ref,
                ab_ref, nb, t)

    def out_proj(c):
        _out_proj_chunk(ab_ref, woo_ref, y_ref, gate, y_ref, c, nb, t)
        _final_norm_chunk(y_ref, fg, c, nb, t)
    _for_chunks(n_chunks, out_proj)


def _even_in_proj(x_ref, m_row, g_row, w_ref, h_ref, pad_a, ga_ref, gb_ref, q_ref, k_ref, v_ref,
                  kv_out, nb, t):
    shift = m_row[:, :D_MODEL]
    gain = g_row * (1.0 + m_row[:, D_MODEL:2 * D_MODEL])

    def in_proj(c):
        _modnorm_chunk(x_ref, h_ref, c, nb, t, gain, shift)
        rows = _rows(c * ROW_CHUNK, ROW_CHUNK, ROW_CHUNK)
        pieces = _pieces(c, nb, t)
        h = h_ref[rows, :]
        _store_padded(pad_a, _dot(h, w_ref[:, _group(0)]), pieces, t)
        ga_ref[rows, :] = _silu(_dot(h, w_ref[:, _group(1)]))
        q_ref[rows, :] = (_dot(h, w_ref[:, _group(2)]) * (HEAD_DIM ** -0.5)).astype(BF16)
        for dst, out, g in ((k_ref, 0, 3), (v_ref, 1, 4)):
            acc = _dot(h, w_ref[:, _group(g)])
            dst[rows, :] = acc.astype(BF16)
            if kv_out is not None:
                for s, off, n, o in pieces:
                    for hd in range(N_HEADS):
                        kv_out[out][s, 0, hd, _rows(off, n, n), :] = (
                            acc[o:o + n, hd * HEAD_DIM:(hd + 1) * HEAD_DIM])
        gb_ref[rows, :] = _silu(_dot(h, w_ref[:, _group(5)]))
    _for_chunks(nb * t // ROW_CHUNK, in_proj)


def _even_out_proj(x_ref, y_ref, m_row, w_ref, ab_ref, nb, t):
    gate = m_row[:, 2 * D_MODEL:]
    _for_chunks(nb * t // ROW_CHUNK,
                lambda c: _out_proj_chunk(ab_ref, w_ref, x_ref, gate, y_ref, c, nb, t))


def _split_heads(x):
    lane = lax.broadcasted_iota(jnp.int32, (1, LANES), 1)
    first = jnp.where(lane < HEAD_DIM, 1.0, 0.0).astype(x.dtype)
    return jnp.concatenate([x * first, x * (1 - first)], axis=0)


def _merge_heads(o):
    n = o.shape[0] // 2
    lane = lax.broadcasted_iota(jnp.int32, (n, LANES), 1)
    return jnp.where(lane < HEAD_DIM, o[:n], o[n:])


def _context_attention(q_ref, k_ref, v_ref, gb_ref, ab_ref, nb, t):

    def per_seq(s, carry):
        seq = _rows(s * t, t, t)
        for j in range(N_HEADS // 2):
            ln = _lanes(j)
            kp = k_ref[seq, ln]
            vp = v_ref[seq, ln]
            for r0 in range(0, t, Q_ROWS):
                rows = _rows(s * t + r0, Q_ROWS, Q_ROWS)
                sc = _dot_nt(_split_heads(q_ref[rows, ln]), kp)
                p = jnp.exp(sc - jnp.max(sc, axis=-1, keepdims=True))
                o = _dot(p.astype(BF16), vp) / jnp.sum(p, axis=-1, keepdims=True)
                ab_ref[rows, W_HALF + j * LANES:W_HALF + (j + 1) * LANES] = (
                    _merge_heads(o) * gb_ref[rows, ln]).astype(BF16)
        return carry
    lax.fori_loop(0, nb, per_seq, 0)


def _prompt_body(x_ref, m_ref, ng_ref, fg_ref, wie_ref, wp_ref, ps_ref, woe_ref, wio_ref, cc_ref,
                 cdw_ref, cdb_ref, lng_ref, lnb_ref, woo_ref,
                 y_ref, ko_ref, vo_ref,
                 h_ref, pad_a, pad_b, ga_ref, gb_ref, bc_ref, q_ref, k_ref, v_ref, ab_ref,
                 *, nb, t):
    _zero_pads(pad_a, nb, t)
    _zero_pads(pad_b, nb, t)
    _even_in_proj(x_ref, m_ref[0, 0], ng_ref[0:1, :], wie_ref, h_ref, pad_a, ga_ref, gb_ref,
                  q_ref, k_ref, v_ref, (ko_ref, vo_ref), nb, t)
    _pool_phase(pad_a, ga_ref, wp_ref, ps_ref, ab_ref, nb, t)
    _context_attention(q_ref, k_ref, v_ref, gb_ref, ab_ref, nb, t)
    _even_out_proj(x_ref, y_ref, m_ref[0, 0], woe_ref, ab_ref, nb, t)
    _odd_layer(y_ref, m_ref[1, 0], ng_ref[1:2, :], fg_ref[...], wio_ref, cc_ref, cdw_ref, cdb_ref,
               lng_ref, lnb_ref, woo_ref, h_ref, pad_a, pad_b, bc_ref, ga_ref, gb_ref, ab_ref, nb, t)


def _rpb_rows(rpb_ref, e_ref):
    n = rpb_ref.shape[0]
    lane = lax.broadcasted_iota(jnp.int32, (n, LANES), 1)
    i = jnp.where(lane < GRID_W, lane, lane - LANES)
    idx = jnp.clip(i, -(WIN_W - 1), WIN_W - 1) + (WIN_W - 1)
    rp = rpb_ref[...]
    e = jnp.zeros((n, LANES), F32)
    for d in range(2 * WIN_W - 1):
        e = jnp.where(idx == d, rp[:, d:d + 1], e)
    e_ref[...] = e


def _bias_tables(e_ref, bias_ref, j):
    q = lax.broadcasted_iota(jnp.int32, (GRID_W, LANES), 0)
    lane = lax.broadcasted_iota(jnp.int32, (GRID_W, LANES), 1)
    kw = jnp.where(lane < GRID_W, lane, lane - GRID_W)
    start = jnp.clip(q - WIN_W // 2, 0, GRID_W - WIN_W)
    col_ok = (kw >= start) & (kw < start + WIN_W)
    n_dr = 2 * WIN_H - 1
    for o in range(WIN_H):
        for e in range(2):
            h = 2 * j + e
            for jp in range(WIN_H // 2):
                dr = (WIN_H - 1) - o + 2 * jp
                r_lo = h * n_dr + dr
                lo = jnp.broadcast_to(e_ref[r_lo:r_lo + 1, :], (GRID_W, LANES))
                hi = jnp.broadcast_to(e_ref[r_lo + 1:r_lo + 2, :], (GRID_W, LANES))
                lo = pltpu.roll(lo, 0, 1, stride=1, stride_axis=0)
                hi = pltpu.roll(hi, GRID_W, 1, stride=1, stride_axis=0)
                tile = jnp.where(lane < GRID_W, lo, hi)
                bias_ref[o, e * GRID_W:(e + 1) * GRID_W, _lanes(jp)] = jnp.where(col_ok, tile, MASKED)


def _neighbourhood_attention(q_ref, k_ref, v_ref, ck_ref, cv_ref, e_ref, bias_ref, kvc_ref, gb_ref,
                             ab_ref, t):
    grid_h = t // GRID_W
    band = WIN_H * GRID_W
    for j in range(N_HEADS // 2):
        ln = _lanes(j)
        _bias_tables(e_ref, bias_ref, j)
        for i, src in enumerate((ck_ref, cv_ref)):
            kvc_ref[i] = jnp.concatenate([src[0, 0, 2 * j], src[0, 0, 2 * j + 1]],
                                         axis=-1).astype(BF16)

        def per_row(r, carry, ln=ln, j=j):
            start = jnp.clip(r - WIN_H // 2, 0, grid_h - WIN_H)
            rows = _rows(r * GRID_W, GRID_W, GRID_W)
            keys = _rows(start * GRID_W, band, GRID_W)
            q2 = _split_heads(q_ref[rows, ln])
            s_loc = _dot_nt(q2, k_ref[keys, ln]) + bias_ref[r - start]
            s_ctx = _dot_nt(q2, kvc_ref[0])
            mx = jnp.maximum(jnp.max(s_loc, axis=-1, keepdims=True),
                             jnp.max(s_ctx, axis=-1, keepdims=True))
            p_loc = jnp.exp(s_loc - mx)
            p_ctx = jnp.exp(s_ctx - mx)
            den = jnp.sum(p_loc, axis=-1, keepdims=True) + jnp.sum(p_ctx, axis=-1, keepdims=True)
            o = (_dot(p_loc.astype(BF16), v_ref[keys, ln])
                 + _dot(p_ctx.astype(BF16), kvc_ref[1])) / den
            ab_ref[rows, W_HALF + j * LANES:W_HALF + (j + 1) * LANES] = (
                _merge_heads(o) * gb_ref[rows, ln]).astype(BF16)
            return carry
        lax.fori_loop(0, grid_h, per_row, 0, unroll=NA_UNROLL)


def _sample_body(x_ref, m_ref, ng_ref, fg_ref, wie_ref, wp_ref, ps_ref, woe_ref, wio_ref, cc_ref,
                 cdw_ref, cdb_ref, lng_ref, lnb_ref, woo_ref, ck_ref, cv_ref, rpb_ref,
                 y_ref,
                 h_ref, pad_a, pad_b, ga_ref, gb_ref, bc_ref, q_ref, k_ref, v_ref, ab_ref,
                 e_ref, bias_ref, kvc_ref, *, t):
    _zero_pads(pad_a, 1, t)
    _zero_pads(pad_b, 1, t)
    _rpb_rows(rpb_ref, e_ref)
    _even_in_proj(x_ref, m_ref[0, 0], ng_ref[0:1, :], wie_ref, h_ref, pad_a, ga_ref, gb_ref,
                  q_ref, k_ref, v_ref, None, 1, t)
    _pool_phase(pad_a, ga_ref, wp_ref, ps_ref, ab_ref, 1, t)
    _neighbourhood_attention(q_ref, k_ref, v_ref, ck_ref, cv_ref, e_ref, bias_ref, kvc_ref, gb_ref,
                             ab_ref, t)
    _even_out_proj(x_ref, y_ref, m_ref[0, 0], woe_ref, ab_ref, 1, t)
    _odd_layer(y_ref, m_ref[1, 0], ng_ref[1:2, :], fg_ref[...], wio_ref, cc_ref, cdw_ref, cdb_ref,
               lng_ref, lnb_ref, woo_ref, h_ref, pad_a, pad_b, bc_ref, ga_ref, gb_ref, ab_ref, 1, t)


def _const_spec(shape):
    zeros = (0,) * len(shape)
    return pl.BlockSpec(shape, lambda i: zeros, pipeline_mode=pl.Buffered(1))


def _stream_scratch(nb, t):
    r = nb * t
    padded = nb * (t + 2 * PAD)
    return [
        pltpu.VMEM((r, D_MODEL), BF16),
        pltpu.VMEM((padded, W_HALF), F32),
        pltpu.VMEM((padded, W_HALF), F32),
        pltpu.VMEM((r, W_HALF), F32),
        pltpu.VMEM((r, W_HALF), F32),
        pltpu.VMEM((r, W_HALF), F32),
        pltpu.VMEM((r, W_HALF), BF16),
        pltpu.VMEM((r, W_HALF), BF16),
        pltpu.VMEM((r, W_HALF), BF16),
        pltpu.VMEM((r, D_MODEL), BF16),
    ]


def _weight_args(norm_g, final_g, w_in_even, w_pool, pool_scale, w_out_even, w_in_odd, conv_c,
                 conv_d, conv_d_b, ln_g, ln_b, w_out_odd):
    args = [
        norm_g, final_g.reshape(1, D_MODEL),
        w_in_even[0].astype(BF16), w_pool[0].astype(BF16), pool_scale, w_out_even[0].astype(BF16),
        w_in_odd[0].astype(BF16), conv_c[0], conv_d[0], conv_d_b, ln_g, ln_b,
        w_out_odd[0].astype(BF16),
    ]
    return args, [_const_spec(a.shape) for a in args]


def kernel(x_prompt, x_sample, cache_k, cache_v, c, c_ctx, norm_g, w_mod, b_mod, w_in_even, w_pool,
           pool_scale, rpb, w_out_even, w_in_odd, conv_c, conv_d, conv_d_b, ln_g, ln_b, w_out_odd,
           final_g):
    batch, seq, d = x_prompt.shape
    dec_batch, dec_seq, _ = x_sample.shape
    assert d == D_MODEL and w_mod.shape[0] == 2 and w_in_even.shape[0] == 1 and w_in_odd.shape[0] == 1
    assert (NB_PROMPT * seq) % ROW_CHUNK == 0 and ROW_CHUNK % seq == 0 and seq % Q_ROWS == 0
    assert dec_seq % ROW_CHUNK == 0 and dec_seq // GRID_W >= WIN_H

    cond_rows = SUBLANES * ((1 + dec_batch + SUBLANES - 1) // SUBLANES)
    cond = jnp.concatenate(
        [c_ctx[None, :], c, jnp.zeros((cond_rows - 1 - dec_batch, d), F32)], axis=0)
    m = _modulation(cond, w_mod, b_mod).reshape(2, cond_rows, 1, 3 * d)

    w_args, w_specs = _weight_args(norm_g, final_g, w_in_even, w_pool, pool_scale, w_out_even,
                                   w_in_odd, conv_c, conv_d, conv_d_b, ln_g, ln_b, w_out_odd)

    nb = NB_PROMPT
    assert batch % nb == 0
    kv_shape = jax.ShapeDtypeStruct((batch, 1, N_HEADS, seq, HEAD_DIM), F32)
    kv_spec = pl.BlockSpec((nb, 1, N_HEADS, seq, HEAD_DIM), lambda i: (i, 0, 0, 0, 0))
    y_prompt, new_k, new_v = pl.pallas_call(
        functools.partial(_prompt_body, nb=nb, t=seq),
        out_shape=(jax.ShapeDtypeStruct(x_prompt.shape, F32), kv_shape, kv_shape),
        grid=(batch // nb,),
        in_specs=[pl.BlockSpec((nb, seq, d), lambda i: (i, 0, 0)),
                  pl.BlockSpec((2, 1, 1, 3 * d), lambda i: (0, 0, 0, 0))] + w_specs,
        out_specs=(pl.BlockSpec((nb, seq, d), lambda i: (i, 0, 0)), kv_spec, kv_spec),
        scratch_shapes=_stream_scratch(nb, seq),
        compiler_params=pltpu.CompilerParams(dimension_semantics=("arbitrary",),
                                             vmem_limit_bytes=VMEM_LIMIT),
        name="prompt",
    )(x_prompt, m, *w_args)

    past = cache_k.shape[3]
    cache_spec = pl.BlockSpec((1, 1, N_HEADS, past, HEAD_DIM), lambda i: (i, 0, 0, 0, 0))
    rpb2 = rpb[0].reshape(N_HEADS * (2 * WIN_H - 1), 2 * WIN_W - 1)
    y_sample = pl.pallas_call(
        functools.partial(_sample_body, t=dec_seq),
        out_shape=jax.ShapeDtypeStruct(x_sample.shape, F32),
        grid=(dec_batch,),
        in_specs=[pl.BlockSpec((1, dec_seq, d), lambda i: (i, 0, 0), pipeline_mode=pl.Buffered(1)),
                  pl.BlockSpec((2, 1, 1, 3 * d), lambda i: (0, i + 1, 0, 0))] + w_specs
                 + [cache_spec, cache_spec, _const_spec(rpb2.shape)],
        out_specs=pl.BlockSpec((1, dec_seq, d), lambda i: (i, 0, 0)),
        scratch_shapes=_stream_scratch(1, dec_seq) + [
            pltpu.VMEM(rpb2.shape[:1] + (LANES,), F32),
            pltpu.VMEM((WIN_H, 2 * GRID_W, WIN_H * GRID_W), F32),
            pltpu.VMEM((2, past, LANES), BF16),
        ],
        compiler_params=pltpu.CompilerParams(dimension_semantics=("arbitrary",),
                                             vmem_limit_bytes=VMEM_LIMIT),
        name="sample",
    )(x_sample, m, *w_args, cache_k, cache_v, rpb2)

    return (y_prompt, y_sample, new_k, new_v)
```

```python
import functools

import jax
import jax.numpy as jnp
from jax import lax
from jax.experimental import pallas as pl
from jax.experimental.pallas import tpu as pltpu

F32 = jnp.float32
BF16 = jnp.bfloat16

D_MODEL = 1024
W_HALF = 512
N_POOL_GROUPS = 4
POOL_HALF = (1, 2, 4, 8)
N_HEADS = 8
HEAD_DIM = 64
GRID_W = 64
WIN_H = 8
WIN_W = 16
CONV_C = 3
CONV_D = 31
EPS = 1e-6
MASKED = -1e30

LANES = 128
SUBLANES = 8
PAD = 16
ROW_CHUNK = 512
NORM_ROWS = 32
MIX_ROWS = 64
Q_ROWS = 128
NB_PROMPT = 2
NA_UNROLL = 4
MOD_COLS = 1536
VMEM_LIMIT = 58 * 1024 * 1024

assert PAD >= CONV_D // 2 + 1 and PAD % SUBLANES == 0 and PAD >= 2 * SUBLANES
assert max(POOL_HALF) <= SUBLANES


def _sigmoid(x):
    return 1.0 / (1.0 + jnp.exp(-x))


def _silu(x):
    return x * _sigmoid(x)


def _dot(a, b):
    return jnp.dot(a, b, preferred_element_type=F32)


def _dot_nt(a, b):
    return lax.dot_general(a, b, (((1,), (1,)), ((), ())), preferred_element_type=F32)


def _lanes(j):
    return slice(j * LANES, (j + 1) * LANES)


def _group(g):
    return slice(g * W_HALF, (g + 1) * W_HALF)


def _rows(start, size, align):
    if isinstance(start, int):
        return slice(start, start + size)
    return pl.ds(pl.multiple_of(start, align), size)


def _mod_body(cctx_ref, c_ref, w_ref, b_ref, o_ref):
    rows, d = o_ref.shape[1], cctx_ref.shape[1]
    r = lax.broadcasted_iota(jnp.int32, (rows, d), 0)
    cond = jnp.where(r == 0, cctx_ref[...], 0.0)
    for i in range(c_ref.shape[0]):
        cond = jnp.where(r == i + 1, c_ref[i:i + 1, :], cond)
    bias = jnp.where(pl.program_id(0) == 0, b_ref[0:1, :], b_ref[1:2, :])
    o_ref[0] = _dot(_silu(cond).astype(BF16), w_ref[0].astype(BF16)) + bias


def _modulation(c_ctx, c, w_mod, b_mod, rows):
    depth, d, n = w_mod.shape
    assert depth == 2 and 1 + c.shape[0] <= rows
    return pl.pallas_call(
        _mod_body,
        out_shape=jax.ShapeDtypeStruct((depth, rows, n), F32),
        grid=(depth, n // MOD_COLS),
        in_specs=[
            pl.BlockSpec((1, d), lambda l, j: (0, 0)),
            pl.BlockSpec(c.shape, lambda l, j: (0, 0)),
            pl.BlockSpec((1, d, MOD_COLS), lambda l, j: (l, 0, j)),
            pl.BlockSpec((depth, MOD_COLS), lambda l, j: (0, j)),
        ],
        out_specs=pl.BlockSpec((1, rows, MOD_COLS), lambda l, j: (l, 0, j)),
        compiler_params=pltpu.CompilerParams(dimension_semantics=("arbitrary", "arbitrary")),
        name="mod",
    )(c_ctx.reshape(1, d), c, w_mod, b_mod)


def _cond_row(m_ref, layer, row):
    if isinstance(row, int):
        return m_ref[layer, row:row + 1, :]
    m = m_ref[layer]
    keep = lax.broadcasted_iota(jnp.int32, m.shape, 0) == row
    return jnp.sum(jnp.where(keep, m, 0.0), axis=0, keepdims=True)


def _pieces(c, nb, t):
    if t >= ROW_CHUNK:
        per_seq = t // ROW_CHUNK
        s = 0 if nb == 1 else c // per_seq
        return [(s, (c - s * per_seq) * ROW_CHUNK, ROW_CHUNK, 0)]
    per_chunk = ROW_CHUNK // t
    return [(c * per_chunk + i, 0, t, i * t) for i in range(per_chunk)]


def _for_chunks(n, body):
    if n == 1:
        body(0)
    else:
        lax.fori_loop(0, n, lambda c, carry: (body(c), carry)[1], 0)


def _pad_row(s, off, t):
    return s * (t + 2 * PAD) + PAD + off


def _store_padded(pad_ref, val, pieces, t):
    for s, off, n, o in pieces:
        pad_ref[_rows(_pad_row(s, off, t), n, SUBLANES), :] = val[o:o + n]


def _scale_padded(pad_ref, val, pieces, t):
    for s, off, n, o in pieces:
        rows = _rows(_pad_row(s, off, t), n, SUBLANES)
        pad_ref[rows, :] = pad_ref[rows, :] * val[o:o + n]


def _modnorm_chunk(src_ref, h_ref, c, nb, t, gain, shift):
    for s, off, n, o in _pieces(c, nb, t):
        for i in range(0, n, NORM_ROWS):
            x = src_ref[s, _rows(off + i, NORM_ROWS, NORM_ROWS), :]
            ms = jnp.mean(x * x, axis=-1, keepdims=True)
            h_ref[_rows(c * ROW_CHUNK + o + i, NORM_ROWS, NORM_ROWS), :] = (
                x * lax.rsqrt(ms + EPS) * gain + shift).astype(BF16)


def _zero_pads(pad_ref, nb, t):
    z = jnp.zeros((PAD, W_HALF), F32)
    for s in range(nb):
        pad_ref[_pad_row(s, 0, t) - PAD:_pad_row(s, 0, t), :] = z
        pad_ref[_pad_row(s, t, t):_pad_row(s, t, t) + PAD, :] = z


def _pool_phase(pad_ref, ga_ref, wp_ref, ps_ref, ab_ref, nb, t):
    per_seq = t // MIX_ROWS

    def step(i, carry):
        s = i // per_seq
        r0 = (i - s * per_seq) * MIX_ROWS
        prow = _pad_row(s, r0, t)
        rows = _rows(i * MIX_ROWS, MIX_ROWS, MIX_ROWS)
        pos = r0 + lax.broadcasted_iota(jnp.int32, (MIX_ROWS, LANES), 0)
        for g in range(N_POOL_GROUPS):
            hw = POOL_HALF[g]
            ln = _lanes(g)
            halo = MIX_ROWS + 2 * SUBLANES
            blk = pad_ref[_rows(prow - SUBLANES, halo, SUBLANES), ln]
            run, n = blk, 1
            while n < 2 * hw:
                run = run + pltpu.roll(run, halo - n, 0)
                n *= 2
            if hw < SUBLANES:
                run = pltpu.roll(run, halo - (SUBLANES - hw), 0)
            win = run[:MIX_ROWS]
            cnt = (jnp.minimum(pos + hw, t) - jnp.maximum(pos - hw, 0)).astype(F32)
            p = (win / cnt - blk[SUBLANES:SUBLANES + MIX_ROWS]).astype(BF16)
            y = _dot(p, wp_ref[g]) * ps_ref[:, ln] * ga_ref[rows, ln]
            ab_ref[rows, ln] = y.astype(BF16)
        return carry
    lax.fori_loop(0, nb * per_seq, step, 0)


def _out_proj_chunk(ab_ref, w_ref, x_ref, gate, dst_ref, c, nb, t):
    lhs = ab_ref[_rows(c * ROW_CHUNK, ROW_CHUNK, ROW_CHUNK), :]
    for g in range(D_MODEL // W_HALF):
        y = _dot(lhs, w_ref[:, _group(g)])
        for s, off, n, o in _pieces(c, nb, t):
            rows = _rows(off, n, n)
            dst_ref[s, rows, _group(g)] = x_ref[s, rows, _group(g)] + gate[:, _group(g)] * y[o:o + n]


def _conv_phase(pad_c, pad_d, bc_ref, ga_ref, gb_ref, cc_ref, cdw_ref, cdb_ref, lng_ref, lnb_ref,
                ab_ref, nb, t):
    per_seq = t // MIX_ROWS

    def step(i, carry):
        s = i // per_seq
        r0 = (i - s * per_seq) * MIX_ROWS
        prow = _pad_row(s, r0, t)
        rows = _rows(i * MIX_ROWS, MIX_ROWS, MIX_ROWS)
        z = []
        for g in range(W_HALF // LANES):
            ln = _lanes(g)
            blk = pad_c[_rows(prow - SUBLANES, MIX_ROWS + 2 * SUBLANES, SUBLANES), ln]
            c3 = None
            for j in range(CONV_C):
                o = SUBLANES + j - CONV_C // 2
                term = blk[o:o + MIX_ROWS] * cc_ref[j:j + 1, ln]
                c3 = term if c3 is None else c3 + term
            ab_ref[rows, ln] = (bc_ref[rows, ln] * c3 * ga_ref[rows, ln]).astype(BF16)
            acc = None
            for sft in range(SUBLANES):
                part = None
                for a in range((CONV_D - sft + SUBLANES - 1) // SUBLANES):
                    j = SUBLANES * a + sft
                    src = pad_d[_rows(prow - 2 * SUBLANES + SUBLANES * a, MIX_ROWS + SUBLANES,
                                      SUBLANES), ln]
                    term = src * cdw_ref[j:j + 1, ln]
                    part = term if part is None else part + term
                o = SUBLANES + sft - (CONV_D // 2 - SUBLANES)
                part = part[o:o + MIX_ROWS]
                acc = part if acc is None else acc + part
            z.append(acc + cdb_ref[:, ln])
        z = jnp.concatenate(z, axis=-1)
        mu = jnp.mean(z, axis=-1, keepdims=True)
        zc = z - mu
        var = jnp.mean(zc * zc, axis=-1, keepdims=True)
        zn = zc * lax.rsqrt(var + EPS) * lng_ref[...] + lnb_ref[...]
        ab_ref[rows, W_HALF:] = (_silu(zn) * gb_ref[rows, :]).astype(BF16)
        return carry
    lax.fori_loop(0, nb * per_seq, step, 0)


def _final_norm_chunk(y_ref, fg, c, nb, t):
    for s, off, n, _ in _pieces(c, nb, t):
        for i in range(0, n, NORM_ROWS):
            rows = _rows(off + i, NORM_ROWS, NORM_ROWS)
            x = y_ref[s, rows, :]
            ms = jnp.mean(x * x, axis=-1, keepdims=True)
            y_ref[s, rows, :] = x * lax.rsqrt(ms + EPS) * fg


def _odd_layer(y_ref, m_row, g_row, fg, wio_ref, cc_ref, cdw_ref, cdb_ref, lng_ref, lnb_ref, woo_ref,
               h_ref, pad_c, pad_d, bc_ref, ga_ref, gb_ref, ab_ref, nb, t):
    shift = m_row[:, :D_MODEL]
    gain = g_row * (1.0 + m_row[:, D_MODEL:2 * D_MODEL])
    gate = m_row[:, 2 * D_MODEL:]
    n_chunks = nb * t // ROW_CHUNK

    def in_proj(c):
        _modnorm_chunk(y_ref, h_ref, c, nb, t, gain, shift)
        rows = _rows(c * ROW_CHUNK, ROW_CHUNK, ROW_CHUNK)
        pieces = _pieces(c, nb, t)
        h = h_ref[rows, :]
        bc_ref[rows, :] = _dot(h, wio_ref[:, _group(0)])
        _store_padded(pad_c, _dot(h, wio_ref[:, _group(1)]), pieces, t)
        _scale_padded(pad_c, _dot(h, wio_ref[:, _group(2)]), pieces, t)
        ga_ref[rows, :] = _silu(_dot(h, wio_ref[:, _group(3)]))
        _store_padded(pad_d, _dot(h, wio_ref[:, _group(4)]), pieces, t)
        _scale_padded(pad_d, _sigmoid(_dot(h, wio_ref[:, _group(5)])), pieces, t)
        gb_ref[rows, :] = _silu(_dot(h, wio_ref[:, _group(6)]))
    _for_chunks(n_chunks, in_proj)

    _conv_phase(pad_c, pad_d, bc_ref, ga_ref, gb_ref, cc_ref, cdw_ref, cdb_ref, lng_ref, lnb_ref,
                ab_ref, nb, t)

    def out_proj(c):
        _out_proj_chunk(ab_ref, woo_ref, y_ref, gate, y_ref, c, nb, t)
        _final_norm_chunk(y_ref, fg, c, nb, t)
    _for_chunks(n_chunks, out_proj)


def _even_in_proj(x_ref, m_row, g_row, w_ref, h_ref, pad_a, ga_ref, gb_ref, q_ref, k_ref, v_ref,
                  kv_t, nb, t):
    shift = m_row[:, :D_MODEL]
    gain = g_row * (1.0 + m_row[:, D_MODEL:2 * D_MODEL])

    def in_proj(c):
        _modnorm_chunk(x_ref, h_ref, c, nb, t, gain, shift)
        rows = _rows(c * ROW_CHUNK, ROW_CHUNK, ROW_CHUNK)
        pieces = _pieces(c, nb, t)
        h = h_ref[rows, :]
        _store_padded(pad_a, _dot(h, w_ref[:, _group(0)]), pieces, t)
        ga_ref[rows, :] = _silu(_dot(h, w_ref[:, _group(1)]))
        q_ref[rows, :] = (_dot(h, w_ref[:, _group(2)]) * (HEAD_DIM ** -0.5)).astype(BF16)
        for i, (dst, g) in enumerate(((k_ref, 3), (v_ref, 4))):
            if kv_t is None:
                dst[rows, :] = _dot(h, w_ref[:, _group(g)]).astype(BF16)
                continue
            acc = _dot_nt(kv_t[2 + i][...], h)
            dst[:, rows] = acc.astype(BF16)
            for s, off, n, o in pieces:
                for hd in range(N_HEADS):
                    kv_t[i][s, 0, hd, :, _rows(off, n, n)] = (
                        acc[hd * HEAD_DIM:(hd + 1) * HEAD_DIM, o:o + n])
        gb_ref[rows, :] = _silu(_dot(h, w_ref[:, _group(5)]))
    _for_chunks(nb * t // ROW_CHUNK, in_proj)


def _even_out_proj(x_ref, y_ref, m_row, w_ref, ab_ref, nb, t):
    gate = m_row[:, 2 * D_MODEL:]
    _for_chunks(nb * t // ROW_CHUNK,
                lambda c: _out_proj_chunk(ab_ref, w_ref, x_ref, gate, y_ref, c, nb, t))


def _split_heads(x):
    lane = lax.broadcasted_iota(jnp.int32, (1, LANES), 1)
    first = jnp.where(lane < HEAD_DIM, 1.0, 0.0).astype(x.dtype)
    return jnp.concatenate([x * first, x * (1 - first)], axis=0)


def _merge_heads(o):
    n = o.shape[0] // 2
    lane = lax.broadcasted_iota(jnp.int32, (n, LANES), 1)
    return jnp.where(lane < HEAD_DIM, o[:n], o[n:])


def _context_attention(q_ref, kt_ref, vt_ref, gb_ref, ab_ref, nb, t):
    for s in range(nb):
        seq = slice(s * t, (s + 1) * t)
        for j in range(N_HEADS // 2):
            ln = _lanes(j)
            kp = kt_ref[ln, seq]
            vp = vt_ref[ln, seq]
            for r0 in range(0, t, Q_ROWS):
                rows = slice(s * t + r0, s * t + r0 + Q_ROWS)
                sc = _dot(_split_heads(q_ref[rows, ln]), kp)
                p = jnp.exp(sc - jnp.max(sc, axis=-1, keepdims=True))
                o = _dot_nt(p.astype(BF16), vp) / jnp.sum(p, axis=-1, keepdims=True)
                ab_ref[rows, W_HALF + j * LANES:W_HALF + (j + 1) * LANES] = (
                    _merge_heads(o) * gb_ref[rows, ln]).astype(BF16)


def _prompt_body(x_ref, m_ref, ng_ref, fg_ref, wie_ref, wp_ref, ps_ref, woe_ref, wio_ref, cc_ref,
                 cdw_ref, cdb_ref, lng_ref, lnb_ref, woo_ref, wkt_ref, wvt_ref,
                 y_ref, ko_ref, vo_ref,
                 h_ref, pad_a, pad_b, ga_ref, gb_ref, bc_ref, q_ref, kt_ref, vt_ref, ab_ref,
                 *, nb, t):
    _zero_pads(pad_a, nb, t)
    _zero_pads(pad_b, nb, t)
    m_even = _cond_row(m_ref, 0, 0)
    _even_in_proj(x_ref, m_even, ng_ref[0:1, :], wie_ref, h_ref, pad_a, ga_ref, gb_ref,
                  q_ref, kt_ref, vt_ref, (ko_ref, vo_ref, wkt_ref, wvt_ref), nb, t)
    _pool_phase(pad_a, ga_ref, wp_ref, ps_ref, ab_ref, nb, t)
    _context_attention(q_ref, kt_ref, vt_ref, gb_ref, ab_ref, nb, t)
    _even_out_proj(x_ref, y_ref, m_even, woe_ref, ab_ref, nb, t)
    _odd_layer(y_ref, _cond_row(m_ref, 1, 0), ng_ref[1:2, :], fg_ref[...], wio_ref, cc_ref, cdw_ref,
               cdb_ref, lng_ref, lnb_ref, woo_ref, h_ref, pad_a, pad_b, bc_ref, ga_ref, gb_ref, ab_ref,
               nb, t)


def _rpb_rows(rpb_ref, e_ref):
    n = rpb_ref.shape[0]
    lane = lax.broadcasted_iota(jnp.int32, (n, LANES), 1)
    i = jnp.where(lane < GRID_W, lane, lane - LANES)
    idx = jnp.clip(i, -(WIN_W - 1), WIN_W - 1) + (WIN_W - 1)
    rp = rpb_ref[...]
    e = jnp.zeros((n, LANES), F32)
    for d in range(2 * WIN_W - 1):
        e = jnp.where(idx == d, rp[:, d:d + 1], e)
    e_ref[...] = e


def _bias_tables(e_ref, bias_ref, j):
    q = lax.broadcasted_iota(jnp.int32, (GRID_W, LANES), 0)
    lane = lax.broadcasted_iota(jnp.int32, (GRID_W, LANES), 1)
    kw = jnp.where(lane < GRID_W, lane, lane - GRID_W)
    start = jnp.clip(q - WIN_W // 2, 0, GRID_W - WIN_W)
    col_ok = (kw >= start) & (kw < start + WIN_W)
    n_dr = 2 * WIN_H - 1
    for o in range(WIN_H):
        for e in range(2):
            h = 2 * j + e
            for jp in range(WIN_H // 2):
                dr = (WIN_H - 1) - o + 2 * jp
                r_lo = h * n_dr + dr
                lo = jnp.broadcast_to(e_ref[r_lo:r_lo + 1, :], (GRID_W, LANES))
                hi = jnp.broadcast_to(e_ref[r_lo + 1:r_lo + 2, :], (GRID_W, LANES))
                lo = pltpu.roll(lo, 0, 1, stride=1, stride_axis=0)
                hi = pltpu.roll(hi, GRID_W, 1, stride=1, stride_axis=0)
                tile = jnp.where(lane < GRID_W, lo, hi)
                bias_ref[o, e * GRID_W:(e + 1) * GRID_W, _lanes(jp)] = jnp.where(col_ok, tile, MASKED)


def _neighbourhood_attention(q_ref, k_ref, v_ref, ck_ref, cv_ref, e_ref, bias_ref, kvc_ref, gb_ref,
                             ab_ref, t):
    grid_h = t // GRID_W
    band = WIN_H * GRID_W
    for j in range(N_HEADS // 2):
        ln = _lanes(j)
        _bias_tables(e_ref, bias_ref, j)
        for i, src in enumerate((ck_ref, cv_ref)):
            kvc_ref[i] = jnp.concatenate([src[0, 0, 2 * j], src[0, 0, 2 * j + 1]],
                                         axis=0).astype(BF16)

        def per_row(r, carry, ln=ln, j=j):
            start = jnp.clip(r - WIN_H // 2, 0, grid_h - WIN_H)
            rows = _rows(r * GRID_W, GRID_W, GRID_W)
            keys = _rows(start * GRID_W, band, GRID_W)
            q2 = _split_heads(q_ref[rows, ln])
            s_loc = _dot_nt(q2, k_ref[keys, ln]) + bias_ref[r - start]
            s_ctx = _dot(q2, kvc_ref[0])
            mx = jnp.maximum(jnp.max(s_loc, axis=-1, keepdims=True),
                             jnp.max(s_ctx, axis=-1, keepdims=True))
            p_loc = jnp.exp(s_loc - mx)
            p_ctx = jnp.exp(s_ctx - mx)
            den = jnp.sum(p_loc, axis=-1, keepdims=True) + jnp.sum(p_ctx, axis=-1, keepdims=True)
            o = (_dot(p_loc.astype(BF16), v_ref[keys, ln])
                 + _dot_nt(p_ctx.astype(BF16), kvc_ref[1])) / den
            ab_ref[rows, W_HALF + j * LANES:W_HALF + (j + 1) * LANES] = (
                _merge_heads(o) * gb_ref[rows, ln]).astype(BF16)
            return carry
        lax.fori_loop(0, grid_h, per_row, 0, unroll=NA_UNROLL)


def _sample_body(x_ref, m_ref, ng_ref, fg_ref, wie_ref, wp_ref, ps_ref, woe_ref, wio_ref, cc_ref,
                 cdw_ref, cdb_ref, lng_ref, lnb_ref, woo_ref, ck_ref, cv_ref, rpb_ref,
                 y_ref,
                 h_ref, pad_a, pad_b, ga_ref, gb_ref, bc_ref, q_ref, k_ref, v_ref, ab_ref,
                 e_ref, bias_ref, kvc_ref, *, t):
    _zero_pads(pad_a, 1, t)
    _zero_pads(pad_b, 1, t)
    _rpb_rows(rpb_ref, e_ref)
    cond = pl.program_id(0) + 1
    m_even = _cond_row(m_ref, 0, cond)
    _even_in_proj(x_ref, m_even, ng_ref[0:1, :], wie_ref, h_ref, pad_a, ga_ref, gb_ref,
                  q_ref, k_ref, v_ref, None, 1, t)
    _pool_phase(pad_a, ga_ref, wp_ref, ps_ref, ab_ref, 1, t)
    _neighbourhood_attention(q_ref, k_ref, v_ref, ck_ref, cv_ref, e_ref, bias_ref, kvc_ref, gb_ref,
                             ab_ref, t)
    _even_out_proj(x_ref, y_ref, m_even, woe_ref, ab_ref, 1, t)
    _odd_layer(y_ref, _cond_row(m_ref, 1, cond), ng_ref[1:2, :], fg_ref[...], wio_ref, cc_ref, cdw_ref,
               cdb_ref, lng_ref, lnb_ref, woo_ref, h_ref, pad_a, pad_b, bc_ref, ga_ref, gb_ref, ab_ref,
               1, t)


def _const_spec(shape):
    zeros = (0,) * len(shape)
    return pl.BlockSpec(shape, lambda i: zeros, pipeline_mode=pl.Buffered(1))


def _stream_scratch(nb, t, kv_transposed):
    r = nb * t
    padded = nb * (t + 2 * PAD)
    kv = (W_HALF, r) if kv_transposed else (r, W_HALF)
    return [
        pltpu.VMEM((r, D_MODEL), BF16),
        pltpu.VMEM((padded, W_HALF), F32),
        pltpu.VMEM((padded, W_HALF), F32),
        pltpu.VMEM((r, W_HALF), F32),
        pltpu.VMEM((r, W_HALF), F32),
        pltpu.VMEM((r, W_HALF), F32),
        pltpu.VMEM((r, W_HALF), BF16),
        pltpu.VMEM(kv, BF16),
        pltpu.VMEM(kv, BF16),
        pltpu.VMEM((r, D_MODEL), BF16),
    ]


def _weight_args(norm_g, final_g, w_in_even, w_pool, pool_scale, w_out_even, w_in_odd, conv_c,
                 conv_d, conv_d_b, ln_g, ln_b, w_out_odd):
    args = [
        norm_g, final_g.reshape(1, D_MODEL),
        w_in_even[0].astype(BF16), w_pool[0].astype(BF16), pool_scale, w_out_even[0].astype(BF16),
        w_in_odd[0].astype(BF16), conv_c[0], conv_d[0], conv_d_b, ln_g, ln_b,
        w_out_odd[0].astype(BF16),
    ]
    return args, [_const_spec(a.shape) for a in args]


def kernel(x_prompt, x_sample, cache_k, cache_v, c, c_ctx, norm_g, w_mod, b_mod, w_in_even, w_pool,
           pool_scale, rpb, w_out_even, w_in_odd, conv_c, conv_d, conv_d_b, ln_g, ln_b, w_out_odd,
           final_g):
    batch, seq, d = x_prompt.shape
    dec_batch, dec_seq, _ = x_sample.shape
    assert d == D_MODEL and w_mod.shape[0] == 2 and w_in_even.shape[0] == 1 and w_in_odd.shape[0] == 1
    assert (NB_PROMPT * seq) % ROW_CHUNK == 0 and ROW_CHUNK % seq == 0 and seq % Q_ROWS == 0
    assert dec_seq % ROW_CHUNK == 0 and dec_seq // GRID_W >= WIN_H

    cond_rows = SUBLANES * ((1 + dec_batch + SUBLANES - 1) // SUBLANES)
    m = _modulation(c_ctx, c, w_mod, b_mod, cond_rows)
    m_spec = _const_spec(m.shape)

    w_args, w_specs = _weight_args(norm_g, final_g, w_in_even, w_pool, pool_scale, w_out_even,
                                   w_in_odd, conv_c, conv_d, conv_d_b, ln_g, ln_b, w_out_odd)

    nb = NB_PROMPT
    assert batch % nb == 0
    kv_shape = jax.ShapeDtypeStruct((batch, 1, N_HEADS, HEAD_DIM, seq), F32)
    kv_spec = pl.BlockSpec((nb, 1, N_HEADS, HEAD_DIM, seq), lambda i: (i, 0, 0, 0, 0))
    w_kv_t = [w_in_even[0][:, _group(g)].T.astype(BF16) for g in (3, 4)]
    y_prompt, new_kt, new_vt = pl.pallas_call(
        functools.partial(_prompt_body, nb=nb, t=seq),
        out_shape=(jax.ShapeDtypeStruct(x_prompt.shape, F32), kv_shape, kv_shape),
        grid=(batch // nb,),
        in_specs=[pl.BlockSpec((nb, seq, d), lambda i: (i, 0, 0)), m_spec] + w_specs
                 + [_const_spec(w.shape) for w in w_kv_t],
        out_specs=(pl.BlockSpec((nb, seq, d), lambda i: (i, 0, 0)), kv_spec, kv_spec),
        scratch_shapes=_stream_scratch(nb, seq, True),
        compiler_params=pltpu.CompilerParams(dimension_semantics=("arbitrary",),
                                             vmem_limit_bytes=VMEM_LIMIT),
        name="prompt",
    )(x_prompt, m, *w_args, *w_kv_t)

    past = cache_k.shape[3]
    cache_spec = pl.BlockSpec((1, 1, N_HEADS, HEAD_DIM, past), lambda i: (i, 0, 0, 0, 0))
    rpb2 = rpb[0].reshape(N_HEADS * (2 * WIN_H - 1), 2 * WIN_W - 1)
    y_sample = pl.pallas_call(
        functools.partial(_sample_body, t=dec_seq),
        out_shape=jax.ShapeDtypeStruct(x_sample.shape, F32),
        grid=(dec_batch,),
        in_specs=[pl.BlockSpec((1, dec_seq, d), lambda i: (i, 0, 0), pipeline_mode=pl.Buffered(1)),
                  m_spec] + w_specs
                 + [cache_spec, cache_spec, _const_spec(rpb2.shape)],
        out_specs=pl.BlockSpec((1, dec_seq, d), lambda i: (i, 0, 0)),
        scratch_shapes=_stream_scratch(1, dec_seq, False) + [
            pltpu.VMEM(rpb2.shape[:1] + (LANES,), F32),
            pltpu.VMEM((WIN_H, 2 * GRID_W, WIN_H * GRID_W), F32),
            pltpu.VMEM((2, LANES, past), BF16),
        ],
        compiler_params=pltpu.CompilerParams(dimension_semantics=("arbitrary",),
                                             vmem_limit_bytes=VMEM_LIMIT),
        name="sample",
    )(x_sample, m, *w_args, jnp.swapaxes(cache_k, 3, 4), jnp.swapaxes(cache_v, 3, 4), rpb2)

    return (y_prompt, y_sample, jnp.swapaxes(new_kt, 3, 4), jnp.swapaxes(new_vt, 3, 4))
```

```python
import functools

import jax
import jax.numpy as jnp
from jax import lax
from jax.experimental import pallas as pl
from jax.experimental.pallas import tpu as pltpu

F32 = jnp.float32
BF16 = jnp.bfloat16

D_MODEL = 1024
W_HALF = 512
N_POOL_GROUPS = 4
POOL_HALF = (1, 2, 4, 8)
N_HEADS = 8
HEAD_DIM = 64
GRID_W = 64
WIN_H = 8
WIN_W = 16
CONV_C = 3
CONV_D = 31
EPS = 1e-6
MASKED = -1e30
LOG2_E = 1.4426950408889634
Q_SCALE = HEAD_DIM ** -0.5 * LOG2_E

LANES = 128
SUBLANES = 8
PAD = 16
ROW_CHUNK = 512
NORM_ROWS = 32
POOL_ROWS = 256
CONV_ROWS = 128
Q_ROWS = 128
NB_PROMPT = 2
NA_UNROLL = 4
MOD_COLS = 1536
VMEM_LIMIT = 58 * 1024 * 1024

assert PAD >= CONV_D // 2 + 1 and PAD % SUBLANES == 0 and PAD >= 2 * SUBLANES
assert max(POOL_HALF) <= SUBLANES


def _sigmoid(x):
    return 1.0 / (1.0 + jnp.exp(-x))


def _silu(x):
    return x * _sigmoid(x)


def _dot(a, b):
    return jnp.dot(a, b, preferred_element_type=F32)


def _dot_nt(a, b):
    return lax.dot_general(a, b, (((1,), (1,)), ((), ())), preferred_element_type=F32)


def _lanes(j):
    return slice(j * LANES, (j + 1) * LANES)


def _group(g):
    return slice(g * W_HALF, (g + 1) * W_HALF)


def _rows(start, size, align):
    if isinstance(start, int):
        return slice(start, start + size)
    return pl.ds(pl.multiple_of(start, align), size)


def _mod_body(cctx_ref, c_ref, w_ref, b_ref, o_ref):
    rows, d = o_ref.shape[1], cctx_ref.shape[1]
    r = lax.broadcasted_iota(jnp.int32, (rows, d), 0)
    cond = jnp.where(r == 0, cctx_ref[...], 0.0)
    for i in range(c_ref.shape[0]):
        cond = jnp.where(r == i + 1, c_ref[i:i + 1, :], cond)
    bias = jnp.where(pl.program_id(0) == 0, b_ref[0:1, :], b_ref[1:2, :])
    o_ref[0] = _dot(_silu(cond).astype(BF16), w_ref[0].astype(BF16)) + bias


def _modulation(c_ctx, c, w_mod, b_mod, rows):
    depth, d, n = w_mod.shape
    assert depth == 2 and 1 + c.shape[0] <= rows
    return pl.pallas_call(
        _mod_body,
        out_shape=jax.ShapeDtypeStruct((depth, rows, n), F32),
        grid=(depth, n // MOD_COLS),
        in_specs=[
            pl.BlockSpec((1, d), lambda l, j: (0, 0)),
            pl.BlockSpec(c.shape, lambda l, j: (0, 0)),
            pl.BlockSpec((1, d, MOD_COLS), lambda l, j: (l, 0, j)),
            pl.BlockSpec((depth, MOD_COLS), lambda l, j: (0, j)),
        ],
        out_specs=pl.BlockSpec((1, rows, MOD_COLS), lambda l, j: (l, 0, j)),
        compiler_params=pltpu.CompilerParams(dimension_semantics=("arbitrary", "arbitrary")),
        name="mod",
    )(c_ctx.reshape(1, d), c, w_mod, b_mod)


def _cond_row(m_ref, layer, row):
    if isinstance(row, int):
        return m_ref[layer, row:row + 1, :]
    m = m_ref[layer]
    keep = lax.broadcasted_iota(jnp.int32, m.shape, 0) == row
    return jnp.sum(jnp.where(keep, m, 0.0), axis=0, keepdims=True)


def _pieces(c, nb, t):
    if t >= ROW_CHUNK:
        per_seq = t // ROW_CHUNK
        s = 0 if nb == 1 else c // per_seq
        return [(s, (c - s * per_seq) * ROW_CHUNK, ROW_CHUNK, 0)]
    per_chunk = ROW_CHUNK // t
    return [(c * per_chunk + i, 0, t, i * t) for i in range(per_chunk)]


def _for_chunks(n, body, unrolled=False):
    if unrolled or n == 1:
        for c in range(n):
            body(c)
    else:
        lax.fori_loop(0, n, lambda c, carry: (body(c), carry)[1], 0)


def _pad_row(s, off, t):
    return s * (t + 2 * PAD) + PAD + off


def _store_padded(pad_ref, val, pieces, t):
    for s, off, n, o in pieces:
        pad_ref[_rows(_pad_row(s, off, t), n, SUBLANES), :] = val[o:o + n]


def _scale_padded(pad_ref, val, pieces, t):
    for s, off, n, o in pieces:
        rows = _rows(_pad_row(s, off, t), n, SUBLANES)
        pad_ref[rows, :] = pad_ref[rows, :] * val[o:o + n]


def _modnorm_chunk(src_ref, h_ref, c, nb, t, gain, shift):
    for s, off, n, o in _pieces(c, nb, t):
        for i in range(0, n, NORM_ROWS):
            x = src_ref[s, _rows(off + i, NORM_ROWS, NORM_ROWS), :]
            ms = jnp.mean(x * x, axis=-1, keepdims=True)
            h_ref[_rows(c * ROW_CHUNK + o + i, NORM_ROWS, NORM_ROWS), :] = (
                x * lax.rsqrt(ms + EPS) * gain + shift).astype(BF16)


def _zero_pads(pad_ref, nb, t):
    z = jnp.zeros((PAD, W_HALF), F32)
    for s in range(nb):
        pad_ref[_pad_row(s, 0, t) - PAD:_pad_row(s, 0, t), :] = z
        pad_ref[_pad_row(s, t, t):_pad_row(s, t, t) + PAD, :] = z


def _pool_phase(pad_ref, ga_ref, wp_ref, ps_ref, ab_ref, nb, t):
    n_rows = POOL_ROWS
    per_seq = t // n_rows

    def step(i, carry):
        s = i // per_seq
        r0 = (i - s * per_seq) * n_rows
        prow = _pad_row(s, r0, t)
        rows = _rows(i * n_rows, n_rows, n_rows)
        pos = r0 + lax.broadcasted_iota(jnp.int32, (n_rows, LANES), 0)
        for g in range(N_POOL_GROUPS):
            hw = POOL_HALF[g]
            ln = _lanes(g)
            halo = n_rows + 2 * SUBLANES
            blk = pad_ref[_rows(prow - SUBLANES, halo, SUBLANES), ln]
            run, n = blk, 1
            while n < 2 * hw:
                run = run + pltpu.roll(run, halo - n, 0)
                n *= 2
            if hw < SUBLANES:
                run = pltpu.roll(run, halo - (SUBLANES - hw), 0)
            win = run[:n_rows]
            cnt = (jnp.minimum(pos + hw, t) - jnp.maximum(pos - hw, 0)).astype(F32)
            p = (win / cnt - blk[SUBLANES:SUBLANES + n_rows]).astype(BF16)
            y = _dot(p, wp_ref[g]) * ps_ref[:, ln] * ga_ref[rows, ln]
            ab_ref[rows, ln] = y.astype(BF16)
        return carry
    lax.fori_loop(0, nb * per_seq, step, 0)


def _out_proj_chunk(ab_ref, w_ref, x_ref, gate, dst_ref, c, nb, t):
    lhs = ab_ref[_rows(c * ROW_CHUNK, ROW_CHUNK, ROW_CHUNK), :]
    for g in range(D_MODEL // W_HALF):
        y = _dot(lhs, w_ref[:, _group(g)])
        for s, off, n, o in _pieces(c, nb, t):
            rows = _rows(off, n, n)
            dst_ref[s, rows, _group(g)] = x_ref[s, rows, _group(g)] + gate[:, _group(g)] * y[o:o + n]


def _conv_phase(pad_c, pad_d, bc_ref, ga_ref, gb_ref, cc_ref, cdw_ref, cdb_ref, lng_ref, lnb_ref,
                ab_ref, nb, t):
    n_rows = CONV_ROWS
    per_seq = t // n_rows

    def step(i, carry):
        s = i // per_seq
        r0 = (i - s * per_seq) * n_rows
        prow = _pad_row(s, r0, t)
        rows = _rows(i * n_rows, n_rows, n_rows)
        z = []
        for g in range(W_HALF // LANES):
            ln = _lanes(g)
            blk = pad_c[_rows(prow - SUBLANES, n_rows + 2 * SUBLANES, SUBLANES), ln]
            c3 = None
            for j in range(CONV_C):
                o = SUBLANES + j - CONV_C // 2
                term = blk[o:o + n_rows] * cc_ref[j:j + 1, ln]
                c3 = term if c3 is None else c3 + term
            ab_ref[rows, ln] = (bc_ref[rows, ln] * c3 * ga_ref[rows, ln]).astype(BF16)
            acc = None
            for sft in range(SUBLANES):
                part = None
                for a in range((CONV_D - sft + SUBLANES - 1) // SUBLANES):
                    j = SUBLANES * a + sft
                    src = pad_d[_rows(prow - 2 * SUBLANES + SUBLANES * a, n_rows + SUBLANES,
                                      SUBLANES), ln]
                    term = src * cdw_ref[j:j + 1, ln]
                    part = term if part is None else part + term
                o = SUBLANES + sft - (CONV_D // 2 - SUBLANES)
                part = part[o:o + n_rows]
                acc = part if acc is None else acc + part
            z.append(acc + cdb_ref[:, ln])
        z = jnp.concatenate(z, axis=-1)
        mu = jnp.mean(z, axis=-1, keepdims=True)
        zc = z - mu
        var = jnp.mean(zc * zc, axis=-1, keepdims=True)
        zn = zc * lax.rsqrt(var + EPS) * lng_ref[...] + lnb_ref[...]
        ab_ref[rows, W_HALF:] = (_silu(zn) * gb_ref[rows, :]).astype(BF16)
        return carry
    lax.fori_loop(0, nb * per_seq, step, 0)


def _final_norm_chunk(y_ref, fg, c, nb, t):
    for s, off, n, _ in _pieces(c, nb, t):
        for i in range(0, n, NORM_ROWS):
            rows = _rows(off + i, NORM_ROWS, NORM_ROWS)
            x = y_ref[s, rows, :]
            ms = jnp.mean(x * x, axis=-1, keepdims=True)
            y_ref[s, rows, :] = x * lax.rsqrt(ms + EPS) * fg


def _odd_layer(y_ref, m_row, g_row, fg, wio_ref, cc_ref, cdw_ref, cdb_ref, lng_ref, lnb_ref, woo_ref,
               h_ref, pad_c, pad_d, bc_ref, ga_ref, gb_ref, ab_ref, nb, t):
    shift = m_row[:, :D_MODEL]
    gain = g_row * (1.0 + m_row[:, D_MODEL:2 * D_MODEL])
    gate = m_row[:, 2 * D_MODEL:]
    n_chunks = nb * t // ROW_CHUNK

    def in_proj(c):
        _modnorm_chunk(y_ref, h_ref, c, nb, t, gain, shift)
        rows = _rows(c * ROW_CHUNK, ROW_CHUNK, ROW_CHUNK)
        pieces = _pieces(c, nb, t)
        h = h_ref[rows, :]
        bc_ref[rows, :] = _dot(h, wio_ref[:, _group(0)])
        _store_padded(pad_c, _dot(h, wio_ref[:, _group(1)]), pieces, t)
        _scale_padded(pad_c, _dot(h, wio_ref[:, _group(2)]), pieces, t)
        ga_ref[rows, :] = _silu(_dot(h, wio_ref[:, _group(3)]))
        _store_padded(pad_d, _dot(h, wio_ref[:, _group(4)]), pieces, t)
        _scale_padded(pad_d, _sigmoid(_dot(h, wio_ref[:, _group(5)])), pieces, t)
        gb_ref[rows, :] = _silu(_dot(h, wio_ref[:, _group(6)]))
    _for_chunks(n_chunks, in_proj)

    _conv_phase(pad_c, pad_d, bc_ref, ga_ref, gb_ref, cc_ref, cdw_ref, cdb_ref, lng_ref, lnb_ref,
                ab_ref, nb, t)

    def out_proj(c):
        _out_proj_chunk(ab_ref, woo_ref, y_ref, gate, y_ref, c, nb, t)
        _final_norm_chunk(y_ref, fg, c, nb, t)
    _for_chunks(n_chunks, out_proj, unrolled=True)


def _even_in_proj(x_ref, m_row, g_row, w_ref, h_ref, pad_a, ga_ref, gb_ref, q_ref, k_ref, v_ref,
                  kv_t, nb, t):
    shift = m_row[:, :D_MODEL]
    gain = g_row * (1.0 + m_row[:, D_MODEL:2 * D_MODEL])

    def in_proj(c):
        _modnorm_chunk(x_ref, h_ref, c, nb, t, gain, shift)
        rows = _rows(c * ROW_CHUNK, ROW_CHUNK, ROW_CHUNK)
        pieces = _pieces(c, nb, t)
        h = h_ref[rows, :]
        _store_padded(pad_a, _dot(h, w_ref[:, _group(0)]), pieces, t)
        ga_ref[rows, :] = _silu(_dot(h, w_ref[:, _group(1)]))
        q_ref[rows, :] = (_dot(h, w_ref[:, _group(2)]) * Q_SCALE).astype(BF16)
        for i, (dst, g) in enumerate(((k_ref, 3), (v_ref, 4))):
            if kv_t is None:
                dst[rows, :] = _dot(h, w_ref[:, _group(g)]).astype(BF16)
                continue
            acc = _dot_nt(kv_t[2 + i][...], h)
            dst[:, rows] = acc.astype(BF16)
            for s, off, n, o in pieces:
                for hd in range(N_HEADS):
                    kv_t[i][s, 0, hd, :, _rows(off, n, n)] = (
                        acc[hd * HEAD_DIM:(hd + 1) * HEAD_DIM, o:o + n])
        gb_ref[rows, :] = _silu(_dot(h, w_ref[:, _group(5)]))
    _for_chunks(nb * t // ROW_CHUNK, in_proj)


def _even_out_proj(x_ref, y_ref, m_row, w_ref, ab_ref, nb, t):
    gate = m_row[:, 2 * D_MODEL:]
    _for_chunks(nb * t // ROW_CHUNK,
                lambda c: _out_proj_chunk(ab_ref, w_ref, x_ref, gate, y_ref, c, nb, t))


def _split_heads(x):
    lane = lax.broadcasted_iota(jnp.int32, (1, LANES), 1)
    first = jnp.where(lane < HEAD_DIM, 1.0, 0.0).astype(x.dtype)
    return jnp.concatenate([x * first, x * (1 - first)], axis=0)


def _merge_heads(o):
    n = o.shape[0] // 2
    lane = lax.broadcasted_iota(jnp.int32, (n, LANES), 1)
    return jnp.where(lane < HEAD_DIM, o[:n], o[n:])


def _context_attention(q_ref, kt_ref, vt_ref, gb_ref, ab_ref, nb, t):
    for s in range(nb):
        seq = slice(s * t, (s + 1) * t)
        for j in range(N_HEADS // 2):
            ln = _lanes(j)
            kp = kt_ref[ln, seq]
            vp = vt_ref[ln, seq]
            for r0 in range(0, t, Q_ROWS):
                rows = slice(s * t + r0, s * t + r0 + Q_ROWS)
                sc = _dot(_split_heads(q_ref[rows, ln]), kp)
                p = jnp.exp2(sc - jnp.max(sc, axis=-1, keepdims=True))
                o = _dot_nt(p.astype(BF16), vp) / jnp.sum(p, axis=-1, keepdims=True)
                ab_ref[rows, W_HALF + j * LANES:W_HALF + (j + 1) * LANES] = (
                    _merge_heads(o) * gb_ref[rows, ln]).astype(BF16)


def _prompt_body(x_ref, m_ref, ng_ref, fg_ref, wie_ref, wp_ref, ps_ref, woe_ref, wio_ref, cc_ref,
                 cdw_ref, cdb_ref, lng_ref, lnb_ref, woo_ref, wkt_ref, wvt_ref,
                 y_ref, ko_ref, vo_ref,
                 h_ref, pad_a, pad_b, ga_ref, gb_ref, bc_ref, q_ref, kt_ref, vt_ref, ab_ref,
                 *, nb, t):
    _zero_pads(pad_a, nb, t)
    _zero_pads(pad_b, nb, t)
    m_even = _cond_row(m_ref, 0, 0)
    _even_in_proj(x_ref, m_even, ng_ref[0:1, :], wie_ref, h_ref, pad_a, ga_ref, gb_ref,
                  q_ref, kt_ref, vt_ref, (ko_ref, vo_ref, wkt_ref, wvt_ref), nb, t)
    _pool_phase(pad_a, ga_ref, wp_ref, ps_ref, ab_ref, nb, t)
    _context_attention(q_ref, kt_ref, vt_ref, gb_ref, ab_ref, nb, t)
    _even_out_proj(x_ref, y_ref, m_even, woe_ref, ab_ref, nb, t)
    _odd_layer(y_ref, _cond_row(m_ref, 1, 0), ng_ref[1:2, :], fg_ref[...], wio_ref, cc_ref, cdw_ref,
               cdb_ref, lng_ref, lnb_ref, woo_ref, h_ref, pad_a, pad_b, bc_ref, ga_ref, gb_ref, ab_ref,
               nb, t)


def _rpb_rows(rpb_ref, e_ref):
    n = rpb_ref.shape[0]
    lane = lax.broadcasted_iota(jnp.int32, (n, LANES), 1)
    i = jnp.where(lane < GRID_W, lane, lane - LANES)
    idx = jnp.clip(i, -(WIN_W - 1), WIN_W - 1) + (WIN_W - 1)
    rp = rpb_ref[...]
    e = jnp.zeros((n, LANES), F32)
    for d in range(2 * WIN_W - 1):
        e = jnp.where(idx == d, rp[:, d:d + 1], e)
    e_ref[...] = e


def _bias_tables(e_ref, bias_ref, j):
    q = lax.broadcasted_iota(jnp.int32, (GRID_W, LANES), 0)
    lane = lax.broadcasted_iota(jnp.int32, (GRID_W, LANES), 1)
    kw = jnp.where(lane < GRID_W, lane, lane - GRID_W)
    start = jnp.clip(q - WIN_W // 2, 0, GRID_W - WIN_W)
    col_ok = (kw >= start) & (kw < start + WIN_W)
    n_dr = 2 * WIN_H - 1
    for o in range(WIN_H):
        for e in range(2):
            h = 2 * j + e
            for jp in range(WIN_H // 2):
                dr = (WIN_H - 1) - o + 2 * jp
                r_lo = h * n_dr + dr
                lo = jnp.broadcast_to(e_ref[r_lo:r_lo + 1, :], (GRID_W, LANES))
                hi = jnp.broadcast_to(e_ref[r_lo + 1:r_lo + 2, :], (GRID_W, LANES))
                lo = pltpu.roll(lo, 0, 1, stride=1, stride_axis=0)
                hi = pltpu.roll(hi, GRID_W, 1, stride=1, stride_axis=0)
                tile = jnp.where(lane < GRID_W, lo, hi)
                bias_ref[o, e * GRID_W:(e + 1) * GRID_W, _lanes(jp)] = jnp.where(
                    col_ok, tile * LOG2_E, MASKED)


def _neighbourhood_attention(q_ref, k_ref, v_ref, ck_ref, cv_ref, e_ref, bias_ref, kvc_ref, gb_ref,
                             ab_ref, t):
    grid_h = t // GRID_W
    band = WIN_H * GRID_W
    for j in range(N_HEADS // 2):
        ln = _lanes(j)
        _bias_tables(e_ref, bias_ref, j)
        for i, src in enumerate((ck_ref, cv_ref)):
            kvc_ref[i] = jnp.concatenate([src[0, 0, 2 * j], src[0, 0, 2 * j + 1]],
                                         axis=0).astype(BF16)

        def per_row(r, carry, ln=ln, j=j):
            start = jnp.clip(r - WIN_H // 2, 0, grid_h - WIN_H)
            rows = _rows(r * GRID_W, GRID_W, GRID_W)
            keys = _rows(start * GRID_W, band, GRID_W)
            q2 = _split_heads(q_ref[rows, ln])
            s_loc = _dot_nt(q2, k_ref[keys, ln]) + bias_ref[r - start]
            s_ctx = _dot(q2, kvc_ref[0])
            mx = jnp.maximum(jnp.max(s_loc, axis=-1, keepdims=True),
                             jnp.max(s_ctx, axis=-1, keepdims=True))
            p_loc = jnp.exp2(s_loc - mx)
            p_ctx = jnp.exp2(s_ctx - mx)
            den = jnp.sum(p_loc, axis=-1, keepdims=True) + jnp.sum(p_ctx, axis=-1, keepdims=True)
            o = (_dot(p_loc.astype(BF16), v_ref[keys, ln])
                 + _dot_nt(p_ctx.astype(BF16), kvc_ref[1])) / den
            ab_ref[rows, W_HALF + j * LANES:W_HALF + (j + 1) * LANES] = (
                _merge_heads(o) * gb_ref[rows, ln]).astype(BF16)
            return carry
        lax.fori_loop(0, grid_h, per_row, 0, unroll=NA_UNROLL)


def _sample_body(x_ref, m_ref, ng_ref, fg_ref, wie_ref, wp_ref, ps_ref, woe_ref, wio_ref, cc_ref,
                 cdw_ref, cdb_ref, lng_ref, lnb_ref, woo_ref, ck_ref, cv_ref, rpb_ref,
                 y_ref,
                 h_ref, pad_a, pad_b, ga_ref, gb_ref, bc_ref, q_ref, k_ref, v_ref, ab_ref,
                 e_ref, bias_ref, kvc_ref, *, t):
    _zero_pads(pad_a, 1, t)
    _zero_pads(pad_b, 1, t)
    _rpb_rows(rpb_ref, e_ref)
    cond = pl.program_id(0) + 1
    m_even = _cond_row(m_ref, 0, cond)
    _even_in_proj(x_ref, m_even, ng_ref[0:1, :], wie_ref, h_ref, pad_a, ga_ref, gb_ref,
                  q_ref, k_ref, v_ref, None, 1, t)
    _pool_phase(pad_a, ga_ref, wp_ref, ps_ref, ab_ref, 1, t)
    _neighbourhood_attention(q_ref, k_ref, v_ref, ck_ref, cv_ref, e_ref, bias_ref, kvc_ref, gb_ref,
                             ab_ref, t)
    _even_out_proj(x_ref, y_ref, m_even, woe_ref, ab_ref, 1, t)
    _odd_layer(y_ref, _cond_row(m_ref, 1, cond), ng_ref[1:2, :], fg_ref[...], wio_ref, cc_ref, cdw_ref,
               cdb_ref, lng_ref, lnb_ref, woo_ref, h_ref, pad_a, pad_b, bc_ref, ga_ref, gb_ref, ab_ref,
               1, t)


def _const_spec(shape):
    zeros = (0,) * len(shape)
    return pl.BlockSpec(shape, lambda i: zeros, pipeline_mode=pl.Buffered(1))


def _stream_scratch(nb, t, kv_transposed):
    r = nb * t
    padded = nb * (t + 2 * PAD)
    kv = (W_HALF, r) if kv_transposed else (r, W_HALF)
    return [
        pltpu.VMEM((r, D_MODEL), BF16),
        pltpu.VMEM((padded, W_HALF), F32),
        pltpu.VMEM((padded, W_HALF), F32),
        pltpu.VMEM((r, W_HALF), F32),
        pltpu.VMEM((r, W_HALF), F32),
        pltpu.VMEM((r, W_HALF), F32),
        pltpu.VMEM((r, W_HALF), BF16),
        pltpu.VMEM(kv, BF16),
        pltpu.VMEM(kv, BF16),
        pltpu.VMEM((r, D_MODEL), BF16),
    ]


def _weight_args(norm_g, final_g, w_in_even, w_pool, pool_scale, w_out_even, w_in_odd, conv_c,
                 conv_d, conv_d_b, ln_g, ln_b, w_out_odd):
    args = [
        norm_g, final_g.reshape(1, D_MODEL),
        w_in_even[0].astype(BF16), w_pool[0].astype(BF16), pool_scale, w_out_even[0].astype(BF16),
        w_in_odd[0].astype(BF16), conv_c[0], conv_d[0], conv_d_b, ln_g, ln_b,
        w_out_odd[0].astype(BF16),
    ]
    return args, [_const_spec(a.shape) for a in args]


def kernel(x_prompt, x_sample, cache_k, cache_v, c, c_ctx, norm_g, w_mod, b_mod, w_in_even, w_pool,
           pool_scale, rpb, w_out_even, w_in_odd, conv_c, conv_d, conv_d_b, ln_g, ln_b, w_out_odd,
           final_g):
    batch, seq, d = x_prompt.shape
    dec_batch, dec_seq, _ = x_sample.shape
    assert d == D_MODEL and w_mod.shape[0] == 2 and w_in_even.shape[0] == 1 and w_in_odd.shape[0] == 1
    assert (NB_PROMPT * seq) % ROW_CHUNK == 0 and ROW_CHUNK % seq == 0 and seq % Q_ROWS == 0
    assert dec_seq % ROW_CHUNK == 0 and dec_seq // GRID_W >= WIN_H
    assert seq % POOL_ROWS == 0 and seq % CONV_ROWS == 0
    assert dec_seq % POOL_ROWS == 0 and dec_seq % CONV_ROWS == 0

    cond_rows = SUBLANES * ((1 + dec_batch + SUBLANES - 1) // SUBLANES)
    m = _modulation(c_ctx, c, w_mod, b_mod, cond_rows)
    m_spec = _const_spec(m.shape)

    w_args, w_specs = _weight_args(norm_g, final_g, w_in_even, w_pool, pool_scale, w_out_even,
                                   w_in_odd, conv_c, conv_d, conv_d_b, ln_g, ln_b, w_out_odd)

    nb = NB_PROMPT
    assert batch % nb == 0
    kv_shape = jax.ShapeDtypeStruct((batch, 1, N_HEADS, HEAD_DIM, seq), F32)
    kv_spec = pl.BlockSpec((nb, 1, N_HEADS, HEAD_DIM, seq), lambda i: (i, 0, 0, 0, 0))
    w_kv_t = [w_in_even[0][:, _group(g)].T.astype(BF16) for g in (3, 4)]
    y_prompt, new_kt, new_vt = pl.pallas_call(
        functools.partial(_prompt_body, nb=nb, t=seq),
        out_shape=(jax.ShapeDtypeStruct(x_prompt.shape, F32), kv_shape, kv_shape),
        grid=(batch // nb,),
        in_specs=[pl.BlockSpec((nb, seq, d), lambda i: (i, 0, 0)), m_spec] + w_specs
                 + [_const_spec(w.shape) for w in w_kv_t],
        out_specs=(pl.BlockSpec((nb, seq, d), lambda i: (i, 0, 0)), kv_spec, kv_spec),
        scratch_shapes=_stream_scratch(nb, seq, True),
        compiler_params=pltpu.CompilerParams(dimension_semantics=("arbitrary",),
                                             vmem_limit_bytes=VMEM_LIMIT),
        name="prompt",
    )(x_prompt, m, *w_args, *w_kv_t)

    past = cache_k.shape[3]
    cache_spec = pl.BlockSpec((1, 1, N_HEADS, HEAD_DIM, past), lambda i: (i, 0, 0, 0, 0))
    rpb2 = rpb[0].reshape(N_HEADS * (2 * WIN_H - 1), 2 * WIN_W - 1)
    y_sample = pl.pallas_call(
        functools.partial(_sample_body, t=dec_seq),
        out_shape=jax.ShapeDtypeStruct(x_sample.shape, F32),
        grid=(dec_batch,),
        in_specs=[pl.BlockSpec((1, dec_seq, d), lambda i: (i, 0, 0), pipeline_mode=pl.Buffered(1)),
                  m_spec] + w_specs
                 + [cache_spec, cache_spec, _const_spec(rpb2.shape)],
        out_specs=pl.BlockSpec((1, dec_seq, d), lambda i: (i, 0, 0)),
        scratch_shapes=_stream_scratch(1, dec_seq, False) + [
            pltpu.VMEM(rpb2.shape[:1] + (LANES,), F32),
            pltpu.VMEM((WIN_H, 2 * GRID_W, WIN_H * GRID_W), F32),
            pltpu.VMEM((2, LANES, past), BF16),
        ],
        compiler_params=pltpu.CompilerParams(dimension_semantics=("arbitrary",),
                                             vmem_limit_bytes=VMEM_LIMIT),
        name="sample",
    )(x_sample, m, *w_args, jnp.swapaxes(cache_k, 3, 4), jnp.swapaxes(cache_v, 3, 4), rpb2)

    return (y_prompt, y_sample, jnp.swapaxes(new_kt, 3, 4), jnp.swapaxes(new_vt, 3, 4))
```

```python
import functools

import jax
import jax.numpy as jnp
from jax import lax
from jax.experimental import pallas as pl
from jax.experimental.pallas import tpu as pltpu

F32 = jnp.float32
BF16 = jnp.bfloat16

D_MODEL = 1024
W_HALF = 512
N_POOL_GROUPS = 4
POOL_HALF = (1, 2, 4, 8)
N_HEADS = 8
HEAD_DIM = 64
GRID_W = 64
WIN_H = 8
WIN_W = 16
CONV_C = 3
CONV_D = 31
EPS = 1e-6
MASKED = -1e30
LOG2_E = 1.4426950408889634
Q_SCALE = HEAD_DIM ** -0.5 * LOG2_E

LANES = 128
SUBLANES = 8
PAD = 16
ROW_CHUNK = 512
NORM_ROWS = 32
POOL_ROWS = 256
CONV_ROWS = 128
Q_ROWS = 128
NB_PROMPT = 2
NA_UNROLL = 4
MOD_COLS = 1536
CAST_ROWS = 128
VMEM_LIMIT = 58 * 1024 * 1024

assert PAD >= CONV_D // 2 + 1 and PAD % SUBLANES == 0 and PAD >= 2 * SUBLANES
assert max(POOL_HALF) <= SUBLANES


def _sigmoid(x):
    return 1.0 / (1.0 + jnp.exp(-x))


def _silu(x):
    return x * _sigmoid(x)


def _dot(a, b):
    return jnp.dot(a, b, preferred_element_type=F32)


def _dot_nt(a, b):
    return lax.dot_general(a, b, (((1,), (1,)), ((), ())), preferred_element_type=F32)


def _lanes(j):
    return slice(j * LANES, (j + 1) * LANES)


def _group(g):
    return slice(g * W_HALF, (g + 1) * W_HALF)


def _rows(start, size, align):
    if isinstance(start, int):
        return slice(start, start + size)
    return pl.ds(pl.multiple_of(start, align), size)


def _mod_body(cctx_ref, c_ref, w_ref, b_ref, o_ref):
    rows, d = o_ref.shape[1], cctx_ref.shape[1]
    r = lax.broadcasted_iota(jnp.int32, (rows, d), 0)
    cond = jnp.where(r == 0, cctx_ref[...], 0.0)
    for i in range(c_ref.shape[0]):
        cond = jnp.where(r == i + 1, c_ref[i:i + 1, :], cond)
    bias = jnp.where(pl.program_id(0) == 0, b_ref[0:1, :], b_ref[1:2, :])
    o_ref[0] = _dot(_silu(cond).astype(BF16), w_ref[0].astype(BF16)) + bias


def _modulation(c_ctx, c, w_mod, b_mod, rows):
    depth, d, n = w_mod.shape
    assert depth == 2 and 1 + c.shape[0] <= rows
    return pl.pallas_call(
        _mod_body,
        out_shape=jax.ShapeDtypeStruct((depth, rows, n), F32),
        grid=(depth, n // MOD_COLS),
        in_specs=[
            pl.BlockSpec((1, d), lambda l, j: (0, 0)),
            pl.BlockSpec(c.shape, lambda l, j: (0, 0)),
            pl.BlockSpec((1, d, MOD_COLS), lambda l, j: (l, 0, j)),
            pl.BlockSpec((depth, MOD_COLS), lambda l, j: (0, j)),
        ],
        out_specs=pl.BlockSpec((1, rows, MOD_COLS), lambda l, j: (l, 0, j)),
        compiler_params=pltpu.CompilerParams(dimension_semantics=("arbitrary", "arbitrary")),
        name="mod",
    )(c_ctx.reshape(1, d), c, w_mod, b_mod)


def _cond_row(m_ref, layer, row):
    if isinstance(row, int):
        return m_ref[layer, row:row + 1, :]
    m = m_ref[layer]
    keep = lax.broadcasted_iota(jnp.int32, m.shape, 0) == row
    return jnp.sum(jnp.where(keep, m, 0.0), axis=0, keepdims=True)


def _pieces(c, nb, t):
    if t >= ROW_CHUNK:
        per_seq = t // ROW_CHUNK
        s = 0 if nb == 1 else c // per_seq
        return [(s, (c - s * per_seq) * ROW_CHUNK, ROW_CHUNK, 0)]
    per_chunk = ROW_CHUNK // t
    return [(c * per_chunk + i, 0, t, i * t) for i in range(per_chunk)]


def _for_chunks(n, body, unrolled=False):
    if unrolled or n == 1:
        for c in range(n):
            body(c)
    else:
        lax.fori_loop(0, n, lambda c, carry: (body(c), carry)[1], 0)


def _pad_row(s, off, t):
    return s * (t + 2 * PAD) + PAD + off


def _store_padded(pad_ref, val, pieces, t):
    for s, off, n, o in pieces:
        pad_ref[_rows(_pad_row(s, off, t), n, SUBLANES), :] = val[o:o + n]


def _scale_padded(pad_ref, val, pieces, t):
    for s, off, n, o in pieces:
        rows = _rows(_pad_row(s, off, t), n, SUBLANES)
        pad_ref[rows, :] = pad_ref[rows, :] * val[o:o + n]


def _modnorm_chunk(src_ref, h_ref, c, nb, t, gain, shift):
    for s, off, n, o in _pieces(c, nb, t):
        for i in range(0, n, NORM_ROWS):
            x = src_ref[s, _rows(off + i, NORM_ROWS, NORM_ROWS), :]
            ms = jnp.mean(x * x, axis=-1, keepdims=True)
            h_ref[_rows(c * ROW_CHUNK + o + i, NORM_ROWS, NORM_ROWS), :] = (
                x * lax.rsqrt(ms + EPS) * gain + shift).astype(BF16)


def _zero_pads(pad_ref, nb, t):
    z = jnp.zeros((PAD, W_HALF), F32)
    for s in range(nb):
        pad_ref[_pad_row(s, 0, t) - PAD:_pad_row(s, 0, t), :] = z
        pad_ref[_pad_row(s, t, t):_pad_row(s, t, t) + PAD, :] = z


def _pool_phase(pad_ref, ga_ref, wp_ref, ps_ref, ab_ref, nb, t):
    n_rows = POOL_ROWS
    per_seq = t // n_rows

    def step(i, carry):
        s = i // per_seq
        r0 = (i - s * per_seq) * n_rows
        prow = _pad_row(s, r0, t)
        rows = _rows(i * n_rows, n_rows, n_rows)
        pos = r0 + lax.broadcasted_iota(jnp.int32, (n_rows, LANES), 0)
        for g in range(N_POOL_GROUPS):
            hw = POOL_HALF[g]
            ln = _lanes(g)
            halo = n_rows + 2 * SUBLANES
            blk = pad_ref[_rows(prow - SUBLANES, halo, SUBLANES), ln]
            run, n = blk, 1
            while n < 2 * hw:
                run = run + pltpu.roll(run, halo - n, 0)
                n *= 2
            if hw < SUBLANES:
                run = pltpu.roll(run, halo - (SUBLANES - hw), 0)
            win = run[:n_rows]
            cnt = (jnp.minimum(pos + hw, t) - jnp.maximum(pos - hw, 0)).astype(F32)
            p = (win / cnt - blk[SUBLANES:SUBLANES + n_rows]).astype(BF16)
            y = _dot(p, wp_ref[g]) * ps_ref[:, ln] * ga_ref[rows, ln]
            ab_ref[rows, ln] = y.astype(BF16)
        return carry
    lax.fori_loop(0, nb * per_seq, step, 0)


def _out_proj_chunk(ab_ref, w_ref, x_ref, gate, dst_ref, c, nb, t):
    lhs = ab_ref[_rows(c * ROW_CHUNK, ROW_CHUNK, ROW_CHUNK), :]
    for g in range(D_MODEL // W_HALF):
        y = _dot(lhs, w_ref[:, _group(g)])
        for s, off, n, o in _pieces(c, nb, t):
            rows = _rows(off, n, n)
            dst_ref[s, rows, _group(g)] = x_ref[s, rows, _group(g)] + gate[:, _group(g)] * y[o:o + n]


def _conv_phase(pad_c, pad_d, bc_ref, ga_ref, gb_ref, cc_ref, cdw_ref, cdb_ref, lng_ref, lnb_ref,
                ab_ref, nb, t):
    n_rows = CONV_ROWS
    per_seq = t // n_rows

    def step(i, carry):
        s = i // per_seq
        r0 = (i - s * per_seq) * n_rows
        prow = _pad_row(s, r0, t)
        rows = _rows(i * n_rows, n_rows, n_rows)
        z = []
        for g in range(W_HALF // LANES):
            ln = _lanes(g)
            blk = pad_c[_rows(prow - SUBLANES, n_rows + 2 * SUBLANES, SUBLANES), ln]
            c3 = None
            for j in range(CONV_C):
                o = SUBLANES + j - CONV_C // 2
                term = blk[o:o + n_rows] * cc_ref[j:j + 1, ln]
                c3 = term if c3 is None else c3 + term
            ab_ref[rows, ln] = (bc_ref[rows, ln] * c3 * ga_ref[rows, ln]).astype(BF16)
            acc = None
            for sft in range(SUBLANES):
                part = None
                for a in range((CONV_D - sft + SUBLANES - 1) // SUBLANES):
                    j = SUBLANES * a + sft
                    src = pad_d[_rows(prow - 2 * SUBLANES + SUBLANES * a, n_rows + SUBLANES,
                                      SUBLANES), ln]
                    term = src * cdw_ref[j:j + 1, ln]
                    part = term if part is None else part + term
                o = SUBLANES + sft - (CONV_D // 2 - SUBLANES)
                part = part[o:o + n_rows]
                acc = part if acc is None else acc + part
            z.append(acc + cdb_ref[:, ln])
        z = jnp.concatenate(z, axis=-1)
        mu = jnp.mean(z, axis=-1, keepdims=True)
        zc = z - mu
        var = jnp.mean(zc * zc, axis=-1, keepdims=True)
        zn = zc * lax.rsqrt(var + EPS) * lng_ref[...] + lnb_ref[...]
        ab_ref[rows, W_HALF:] = (_silu(zn) * gb_ref[rows, :]).astype(BF16)
        return carry
    lax.fori_loop(0, nb * per_seq, step, 0)


def _final_norm_chunk(y_ref, fg, c, nb, t):
    for s, off, n, _ in _pieces(c, nb, t):
        for i in range(0, n, NORM_ROWS):
            rows = _rows(off + i, NORM_ROWS, NORM_ROWS)
            x = y_ref[s, rows, :]
            ms = jnp.mean(x * x, axis=-1, keepdims=True)
            y_ref[s, rows, :] = x * lax.rsqrt(ms + EPS) * fg


def _odd_layer(y_ref, m_row, g_row, fg, wio_ref, cc_ref, cdw_ref, cdb_ref, lng_ref, lnb_ref, woo_ref,
               h_ref, pad_c, pad_d, bc_ref, ga_ref, gb_ref, ab_ref, nb, t):
    shift = m_row[:, :D_MODEL]
    gain = g_row * (1.0 + m_row[:, D_MODEL:2 * D_MODEL])
    gate = m_row[:, 2 * D_MODEL:]
    n_chunks = nb * t // ROW_CHUNK

    def in_proj(c):
        _modnorm_chunk(y_ref, h_ref, c, nb, t, gain, shift)
        rows = _rows(c * ROW_CHUNK, ROW_CHUNK, ROW_CHUNK)
        pieces = _pieces(c, nb, t)
        h = h_ref[rows, :]
        bc_ref[rows, :] = _dot(h, wio_ref[:, _group(0)])
        _store_padded(pad_c, _dot(h, wio_ref[:, _group(1)]), pieces, t)
        _scale_padded(pad_c, _dot(h, wio_ref[:, _group(2)]), pieces, t)
        ga_ref[rows, :] = _silu(_dot(h, wio_ref[:, _group(3)]))
        _store_padded(pad_d, _dot(h, wio_ref[:, _group(4)]), pieces, t)
        _scale_padded(pad_d, _sigmoid(_dot(h, wio_ref[:, _group(5)])), pieces, t)
        gb_ref[rows, :] = _silu(_dot(h, wio_ref[:, _group(6)]))
    _for_chunks(n_chunks, in_proj)

    _conv_phase(pad_c, pad_d, bc_ref, ga_ref, gb_ref, cc_ref, cdw_ref, cdb_ref, lng_ref, lnb_ref,
                ab_ref, nb, t)

    def out_proj(c):
        _out_proj_chunk(ab_ref, woo_ref, y_ref, gate, y_ref, c, nb, t)
        _final_norm_chunk(y_ref, fg, c, nb, t)
    _for_chunks(n_chunks, out_proj, unrolled=True)


def _even_in_proj(x_ref, m_row, g_row, w_ref, h_ref, pad_a, ga_ref, gb_ref, q_ref, k_ref, v_ref,
                  kv_t, nb, t):
    shift = m_row[:, :D_MODEL]
    gain = g_row * (1.0 + m_row[:, D_MODEL:2 * D_MODEL])

    def in_proj(c):
        _modnorm_chunk(x_ref, h_ref, c, nb, t, gain, shift)
        rows = _rows(c * ROW_CHUNK, ROW_CHUNK, ROW_CHUNK)
        pieces = _pieces(c, nb, t)
        h = h_ref[rows, :]
        _store_padded(pad_a, _dot(h, w_ref[:, _group(0)]), pieces, t)
        ga_ref[rows, :] = _silu(_dot(h, w_ref[:, _group(1)]))
        q_ref[rows, :] = (_dot(h, w_ref[:, _group(2)]) * Q_SCALE).astype(BF16)
        for i, (dst, g) in enumerate(((k_ref, 3), (v_ref, 4))):
            if kv_t is None:
                dst[rows, :] = _dot(h, w_ref[:, _group(g)]).astype(BF16)
                continue
            acc = _dot_nt(kv_t[2 + i][...], h)
            dst[:, rows] = acc.astype(BF16)
            for s, off, n, o in pieces:
                for hd in range(N_HEADS):
                    kv_t[i][s, 0, hd, :, _rows(off, n, n)] = (
                        acc[hd * HEAD_DIM:(hd + 1) * HEAD_DIM, o:o + n])
        gb_ref[rows, :] = _silu(_dot(h, w_ref[:, _group(5)]))
    _for_chunks(nb * t // ROW_CHUNK, in_proj)


def _even_out_proj(x_ref, y_ref, m_row, w_ref, ab_ref, nb, t):
    gate = m_row[:, 2 * D_MODEL:]
    _for_chunks(nb * t // ROW_CHUNK,
                lambda c: _out_proj_chunk(ab_ref, w_ref, x_ref, gate, y_ref, c, nb, t))


def _split_heads(x):
    lane = lax.broadcasted_iota(jnp.int32, (1, LANES), 1)
    first = jnp.where(lane < HEAD_DIM, 1.0, 0.0).astype(x.dtype)
    return jnp.concatenate([x * first, x * (1 - first)], axis=0)


def _merge_heads(o):
    n = o.shape[0] // 2
    lane = lax.broadcasted_iota(jnp.int32, (n, LANES), 1)
    return jnp.where(lane < HEAD_DIM, o[:n], o[n:])


def _context_attention(q_ref, kt_ref, vt_ref, gb_ref, ab_ref, nb, t):
    for s in range(nb):
        seq = slice(s * t, (s + 1) * t)
        for j in range(N_HEADS // 2):
            ln = _lanes(j)
            kp = kt_ref[ln, seq]
            vp = vt_ref[ln, seq]
            for r0 in range(0, t, Q_ROWS):
                rows = slice(s * t + r0, s * t + r0 + Q_ROWS)
                sc = _dot(_split_heads(q_ref[rows, ln]), kp)
                p = jnp.exp2(sc - jnp.max(sc, axis=-1, keepdims=True))
                o = _dot_nt(p.astype(BF16), vp) / jnp.sum(p, axis=-1, keepdims=True)
                ab_ref[rows, W_HALF + j * LANES:W_HALF + (j + 1) * LANES] = (
                    _merge_heads(o) * gb_ref[rows, ln]).astype(BF16)


def _prompt_body(x_ref, m_ref, ng_ref, fg_ref, wie_ref, wp_ref, ps_ref, woe_ref, wio_ref, cc_ref,
                 cdw_ref, cdb_ref, lng_ref, lnb_ref, woo_ref, wkt_ref, wvt_ref,
                 y_ref, ko_ref, vo_ref,
                 h_ref, pad_a, pad_b, ga_ref, gb_ref, bc_ref, q_ref, kt_ref, vt_ref, ab_ref,
                 *, nb, t):
    _zero_pads(pad_a, nb, t)
    _zero_pads(pad_b, nb, t)
    m_even = _cond_row(m_ref, 0, 0)
    _even_in_proj(x_ref, m_even, ng_ref[0:1, :], wie_ref, h_ref, pad_a, ga_ref, gb_ref,
                  q_ref, kt_ref, vt_ref, (ko_ref, vo_ref, wkt_ref, wvt_ref), nb, t)
    _pool_phase(pad_a, ga_ref, wp_ref, ps_ref, ab_ref, nb, t)
    _context_attention(q_ref, kt_ref, vt_ref, gb_ref, ab_ref, nb, t)
    _even_out_proj(x_ref, y_ref, m_even, woe_ref, ab_ref, nb, t)
    _odd_layer(y_ref, _cond_row(m_ref, 1, 0), ng_ref[1:2, :], fg_ref[...], wio_ref, cc_ref, cdw_ref,
               cdb_ref, lng_ref, lnb_ref, woo_ref, h_ref, pad_a, pad_b, bc_ref, ga_ref, gb_ref, ab_ref,
               nb, t)


def _rpb_rows(rpb_ref, e_ref):
    n = rpb_ref.shape[0]
    lane = lax.broadcasted_iota(jnp.int32, (n, LANES), 1)
    i = jnp.where(lane < GRID_W, lane, lane - LANES)
    idx = jnp.clip(i, -(WIN_W - 1), WIN_W - 1) + (WIN_W - 1)
    rp = rpb_ref[...]
    e = jnp.zeros((n, LANES), F32)
    for d in range(2 * WIN_W - 1):
        e = jnp.where(idx == d, rp[:, d:d + 1], e)
    e_ref[...] = e


N_DR = 2 * WIN_H - 1
PAIR_TILES = N_DR // 2


def _bias_tile_index(j, dr_lo):
    if isinstance(dr_lo, int):
        parity, half = dr_lo % 2, dr_lo // 2
    else:
        parity, half = dr_lo & 1, lax.shift_right_logical(dr_lo, 1)
    return (2 * j + parity) * PAIR_TILES + half


def _bias_tables(e_ref, bias_ref):
    q = lax.broadcasted_iota(jnp.int32, (GRID_W, LANES), 0)
    lane = lax.broadcasted_iota(jnp.int32, (GRID_W, LANES), 1)
    kw = jnp.where(lane < GRID_W, lane, lane - GRID_W)
    start = jnp.clip(q - WIN_W // 2, 0, GRID_W - WIN_W)
    col_ok = (kw >= start) & (kw < start + WIN_W)
    for j in range(N_HEADS // 2):
        for dr in range(N_DR - 1):
            for e in range(2):
                r_lo = (2 * j + e) * N_DR + dr
                lo = jnp.broadcast_to(e_ref[r_lo:r_lo + 1, :], (GRID_W, LANES))
                hi = jnp.broadcast_to(e_ref[r_lo + 1:r_lo + 2, :], (GRID_W, LANES))
                lo = pltpu.roll(lo, 0, 1, stride=1, stride_axis=0)
                hi = pltpu.roll(hi, GRID_W, 1, stride=1, stride_axis=0)
                tile = jnp.where(lane < GRID_W, lo, hi)
                bias_ref[_bias_tile_index(j, dr), e * GRID_W:(e + 1) * GRID_W, :] = jnp.where(
                    col_ok, tile * LOG2_E, MASKED)


def _neighbourhood_attention(q_ref, k_ref, v_ref, ck_ref, cv_ref, bias_ref, kvc_ref, gb_ref, ab_ref, t):
    grid_h = t // GRID_W
    band = WIN_H * GRID_W
    for j in range(N_HEADS // 2):
        ln = _lanes(j)
        for i, src in enumerate((ck_ref, cv_ref)):
            kvc_ref[i] = jnp.concatenate([src[0, 0, 2 * j], src[0, 0, 2 * j + 1]],
                                         axis=0).astype(BF16)

        def per_row(r, carry, ln=ln, j=j):
            start = jnp.clip(r - WIN_H // 2, 0, grid_h - WIN_H)
            rows = _rows(r * GRID_W, GRID_W, GRID_W)
            keys = _rows(start * GRID_W, band, GRID_W)
            q2 = _split_heads(q_ref[rows, ln])
            dr0 = (WIN_H - 1) - (r - start)
            bias = jnp.concatenate([bias_ref[_bias_tile_index(j, dr0 + 2 * i)]
                                    for i in range(WIN_H // 2)], axis=-1)
            s_loc = _dot_nt(q2, k_ref[keys, ln]) + bias
            s_ctx = _dot(q2, kvc_ref[0])
            mx = jnp.maximum(jnp.max(s_loc, axis=-1, keepdims=True),
                             jnp.max(s_ctx, axis=-1, keepdims=True))
            p_loc = jnp.exp2(s_loc - mx)
            p_ctx = jnp.exp2(s_ctx - mx)
            den = jnp.sum(p_loc, axis=-1, keepdims=True) + jnp.sum(p_ctx, axis=-1, keepdims=True)
            o = (_dot(p_loc.astype(BF16), v_ref[keys, ln])
                 + _dot_nt(p_ctx.astype(BF16), kvc_ref[1])) / den
            ab_ref[rows, W_HALF + j * LANES:W_HALF + (j + 1) * LANES] = (
                _merge_heads(o) * gb_ref[rows, ln]).astype(BF16)
            return carry
        lax.fori_loop(0, grid_h, per_row, 0, unroll=NA_UNROLL)


def _sample_body(x_ref, m_ref, ng_ref, fg_ref, wie_ref, wp_ref, ps_ref, woe_ref, wio_ref, cc_ref,
                 cdw_ref, cdb_ref, lng_ref, lnb_ref, woo_ref, ck_ref, cv_ref, rpb_ref,
                 y_ref,
                 h_ref, pad_a, pad_b, ga_ref, gb_ref, bc_ref, q_ref, k_ref, v_ref, ab_ref,
                 e_ref, bias_ref, kvc_ref, *, t):
    _zero_pads(pad_a, 1, t)
    _zero_pads(pad_b, 1, t)

    @pl.when(pl.program_id(0) == 0)
    def _():
        _rpb_rows(rpb_ref, e_ref)
        _bias_tables(e_ref, bias_ref)

    cond = pl.program_id(0) + 1
    m_even = _cond_row(m_ref, 0, cond)
    _even_in_proj(x_ref, m_even, ng_ref[0:1, :], wie_ref, h_ref, pad_a, ga_ref, gb_ref,
                  q_ref, k_ref, v_ref, None, 1, t)
    _pool_phase(pad_a, ga_ref, wp_ref, ps_ref, ab_ref, 1, t)
    _neighbourhood_attention(q_ref, k_ref, v_ref, ck_ref, cv_ref, bias_ref, kvc_ref, gb_ref, ab_ref, t)
    _even_out_proj(x_ref, y_ref, m_even, woe_ref, ab_ref, 1, t)
    _odd_layer(y_ref, _cond_row(m_ref, 1, cond), ng_ref[1:2, :], fg_ref[...], wio_ref, cc_ref, cdw_ref,
               cdb_ref, lng_ref, lnb_ref, woo_ref, h_ref, pad_a, pad_b, bc_ref, ga_ref, gb_ref, ab_ref,
               1, t)


def _const_spec(shape):
    zeros = (0,) * len(shape)
    return pl.BlockSpec(shape, lambda i: zeros, pipeline_mode=pl.Buffered(1))


def _stream_scratch(nb, t, kv_transposed):
    r = nb * t
    padded = nb * (t + 2 * PAD)
    kv = (W_HALF, r) if kv_transposed else (r, W_HALF)
    return [
        pltpu.VMEM((r, D_MODEL), BF16),
        pltpu.VMEM((padded, W_HALF), F32),
        pltpu.VMEM((padded, W_HALF), F32),
        pltpu.VMEM((r, W_HALF), F32),
        pltpu.VMEM((r, W_HALF), F32),
        pltpu.VMEM((r, W_HALF), F32),
        pltpu.VMEM((r, W_HALF), BF16),
        pltpu.VMEM(kv, BF16),
        pltpu.VMEM(kv, BF16),
        pltpu.VMEM((r, D_MODEL), BF16),
    ]


def _cast_body(wie_ref, woe_ref, wio_ref, woo_ref, o_wie, o_woe, o_wio, o_woo, o_wkt, o_wvt):
    for src, dst in ((wie_ref, o_wie), (woe_ref, o_woe), (wio_ref, o_wio), (woo_ref, o_woo)):
        dst[...] = src[0].astype(BF16)
    o_wkt[...] = wie_ref[0, :, _group(3)].T.astype(BF16)
    o_wvt[...] = wie_ref[0, :, _group(4)].T.astype(BF16)


def _cast_weights(w_in_even, w_out_even, w_in_odd, w_out_odd):
    ws = (w_in_even, w_out_even, w_in_odd, w_out_odd)
    rows = w_in_even.shape[1]
    assert all(w.shape[0] == 1 and w.shape[1] == rows for w in ws) and rows % CAST_ROWS == 0
    t_shape = jax.ShapeDtypeStruct((W_HALF, rows), BF16)
    t_spec = pl.BlockSpec((W_HALF, CAST_ROWS), lambda i: (0, i))
    return pl.pallas_call(
        _cast_body,
        out_shape=[jax.ShapeDtypeStruct(w.shape[1:], BF16) for w in ws] + [t_shape, t_shape],
        grid=(rows // CAST_ROWS,),
        in_specs=[pl.BlockSpec((1, CAST_ROWS, w.shape[2]), lambda i: (0, i, 0)) for w in ws],
        out_specs=[pl.BlockSpec((CAST_ROWS, w.shape[2]), lambda i: (i, 0)) for w in ws]
                  + [t_spec, t_spec],
        compiler_params=pltpu.CompilerParams(dimension_semantics=("arbitrary",)),
        name="cast",
    )(*ws)


def _weight_args(norm_g, final_g, wie, w_pool, pool_scale, woe, wio, conv_c, conv_d, conv_d_b, ln_g,
                 ln_b, woo):
    args = [
        norm_g, final_g.reshape(1, D_MODEL), wie, w_pool[0].astype(BF16), pool_scale, woe,
        wio, conv_c[0], conv_d[0], conv_d_b, ln_g, ln_b, woo,
    ]
    return args, [_const_spec(a.shape) for a in args]


def kernel(x_prompt, x_sample, cache_k, cache_v, c, c_ctx, norm_g, w_mod, b_mod, w_in_even, w_pool,
           pool_scale, rpb, w_out_even, w_in_odd, conv_c, conv_d, conv_d_b, ln_g, ln_b, w_out_odd,
           final_g):
    batch, seq, d = x_prompt.shape
    dec_batch, dec_seq, _ = x_sample.shape
    assert d == D_MODEL and w_mod.shape[0] == 2 and w_in_even.shape[0] == 1 and w_in_odd.shape[0] == 1
    assert (NB_PROMPT * seq) % ROW_CHUNK == 0 and ROW_CHUNK % seq == 0 and seq % Q_ROWS == 0
    assert dec_seq % ROW_CHUNK == 0 and dec_seq // GRID_W >= WIN_H
    assert seq % POOL_ROWS == 0 and seq % CONV_ROWS == 0
    assert dec_seq % POOL_ROWS == 0 and dec_seq % CONV_ROWS == 0

    cond_rows = SUBLANES * ((1 + dec_batch + SUBLANES - 1) // SUBLANES)
    m = _modulation(c_ctx, c, w_mod, b_mod, cond_rows)
    m_spec = _const_spec(m.shape)

    wie, woe, wio, woo, wkt, wvt = _cast_weights(w_in_even, w_out_even, w_in_odd, w_out_odd)
    w_args, w_specs = _weight_args(norm_g, final_g, wie, w_pool, pool_scale, woe, wio, conv_c,
                                   conv_d, conv_d_b, ln_g, ln_b, woo)

    nb = NB_PROMPT
    assert batch % nb == 0
    kv_shape = jax.ShapeDtypeStruct((batch, 1, N_HEADS, HEAD_DIM, seq), F32)
    kv_spec = pl.BlockSpec((nb, 1, N_HEADS, HEAD_DIM, seq), lambda i: (i, 0, 0, 0, 0))
    w_kv_t = [wkt, wvt]
    y_prompt, new_kt, new_vt = pl.pallas_call(
        functools.partial(_prompt_body, nb=nb, t=seq),
        out_shape=(jax.ShapeDtypeStruct(x_prompt.shape, F32), kv_shape, kv_shape),
        grid=(batch // nb,),
        in_specs=[pl.BlockSpec((nb, seq, d), lambda i: (i, 0, 0)), m_spec] + w_specs
                 + [_const_spec(w.shape) for w in w_kv_t],
        out_specs=(pl.BlockSpec((nb, seq, d), lambda i: (i, 0, 0)), kv_spec, kv_spec),
        scratch_shapes=_stream_scratch(nb, seq, True),
        compiler_params=pltpu.CompilerParams(dimension_semantics=("arbitrary",),
                                             vmem_limit_bytes=VMEM_LIMIT),
        name="prompt",
    )(x_prompt, m, *w_args, *w_kv_t)

    past = cache_k.shape[3]
    cache_spec = pl.BlockSpec((1, 1, N_HEADS, HEAD_DIM, past), lambda i: (i, 0, 0, 0, 0))
    rpb2 = rpb[0].reshape(N_HEADS * (2 * WIN_H - 1), 2 * WIN_W - 1)
    y_sample = pl.pallas_call(
        functools.partial(_sample_body, t=dec_seq),
        out_shape=jax.ShapeDtypeStruct(x_sample.shape, F32),
        grid=(dec_batch,),
        in_specs=[pl.BlockSpec((1, dec_seq, d), lambda i: (i, 0, 0), pipeline_mode=pl.Buffered(1)),
                  m_spec] + w_specs
                 + [cache_spec, cache_spec, _const_spec(rpb2.shape)],
        out_specs=pl.BlockSpec((1, dec_seq, d), lambda i: (i, 0, 0)),
        scratch_shapes=_stream_scratch(1, dec_seq, False) + [
            pltpu.VMEM(rpb2.shape[:1] + (LANES,), F32),
            pltpu.VMEM((N_HEADS * PAIR_TILES, 2 * GRID_W, LANES), F32),
            pltpu.VMEM((2, LANES, past), BF16),
        ],
        compiler_params=pltpu.CompilerParams(dimension_semantics=("arbitrary",),
                                             vmem_limit_bytes=VMEM_LIMIT),
        name="sample",
    )(x_sample, m, *w_args, jnp.swapaxes(cache_k, 3, 4), jnp.swapaxes(cache_v, 3, 4), rpb2)

    return (y_prompt, y_sample, jnp.swapaxes(new_kt, 3, 4), jnp.swapaxes(new_vt, 3, 4))
```

```python
import functools

import jax
import jax.numpy as jnp
from jax import lax
from jax.experimental import pallas as pl
from jax.experimental.pallas import tpu as pltpu

F32 = jnp.float32
BF16 = jnp.bfloat16

D_MODEL = 1024
W_HALF = 512
N_POOL_GROUPS = 4
POOL_HALF = (1, 2, 4, 8)
N_HEADS = 8
HEAD_DIM = 64
GRID_W = 64
WIN_H = 8
WIN_W = 16
CONV_C = 3
CONV_D = 31
EPS = 1e-6
MASKED = -1e30
LOG2_E = 1.4426950408889634
Q_SCALE = HEAD_DIM ** -0.5 * LOG2_E

LANES = 128
SUBLANES = 8
PAD = 16
ROW_CHUNK = 512
NORM_ROWS = 32
POOL_ROWS = 256
CONV_ROWS = 128
Q_ROWS = 128
NB_PROMPT = 2
NA_GROUP = 8
MOD_COLS = 1536
CAST_ROWS = 128
VMEM_LIMIT = 58 * 1024 * 1024

assert PAD >= CONV_D // 2 + 1 and PAD % SUBLANES == 0 and PAD >= 2 * SUBLANES
assert max(POOL_HALF) <= SUBLANES


def _sigmoid(x):
    return 1.0 / (1.0 + jnp.exp(-x))


def _silu(x):
    return x * _sigmoid(x)


def _dot(a, b):
    return jnp.dot(a, b, preferred_element_type=F32)


def _dot_nt(a, b):
    return lax.dot_general(a, b, (((1,), (1,)), ((), ())), preferred_element_type=F32)


def _lanes(j):
    return slice(j * LANES, (j + 1) * LANES)


def _group(g):
    return slice(g * W_HALF, (g + 1) * W_HALF)


def _rows(start, size, align):
    if isinstance(start, int):
        return slice(start, start + size)
    return pl.ds(pl.multiple_of(start, align), size)


def _mod_body(cctx_ref, c_ref, w_ref, b_ref, o_ref):
    rows, d = o_ref.shape[1], cctx_ref.shape[1]
    r = lax.broadcasted_iota(jnp.int32, (rows, d), 0)
    cond = jnp.where(r == 0, cctx_ref[...], 0.0)
    for i in range(c_ref.shape[0]):
        cond = jnp.where(r == i + 1, c_ref[i:i + 1, :], cond)
    bias = jnp.where(pl.program_id(0) == 0, b_ref[0:1, :], b_ref[1:2, :])
    o_ref[0] = _dot(_silu(cond).astype(BF16), w_ref[0].astype(BF16)) + bias


def _modulation(c_ctx, c, w_mod, b_mod, rows):
    depth, d, n = w_mod.shape
    assert depth == 2 and 1 + c.shape[0] <= rows
    return pl.pallas_call(
        _mod_body,
        out_shape=jax.ShapeDtypeStruct((depth, rows, n), F32),
        grid=(depth, n // MOD_COLS),
        in_specs=[
            pl.BlockSpec((1, d), lambda l, j: (0, 0)),
            pl.BlockSpec(c.shape, lambda l, j: (0, 0)),
            pl.BlockSpec((1, d, MOD_COLS), lambda l, j: (l, 0, j)),
            pl.BlockSpec((depth, MOD_COLS), lambda l, j: (0, j)),
        ],
        out_specs=pl.BlockSpec((1, rows, MOD_COLS), lambda l, j: (l, 0, j)),
        compiler_params=pltpu.CompilerParams(dimension_semantics=("arbitrary", "arbitrary")),
        name="mod",
    )(c_ctx.reshape(1, d), c, w_mod, b_mod)


def _cond_row(m_ref, layer, row):
    if isinstance(row, int):
        return m_ref[layer, row:row + 1, :]
    m = m_ref[layer]
    keep = lax.broadcasted_iota(jnp.int32, m.shape, 0) == row
    return jnp.sum(jnp.where(keep, m, 0.0), axis=0, keepdims=True)


def _pieces(c, nb, t):
    if t >= ROW_CHUNK:
        per_seq = t // ROW_CHUNK
        s = 0 if nb == 1 else c // per_seq
        return [(s, (c - s * per_seq) * ROW_CHUNK, ROW_CHUNK, 0)]
    per_chunk = ROW_CHUNK // t
    return [(c * per_chunk + i, 0, t, i * t) for i in range(per_chunk)]


def _for_chunks(n, body, unrolled=False):
    if unrolled or n == 1:
        for c in range(n):
            body(c)
    else:
        lax.fori_loop(0, n, lambda c, carry: (body(c), carry)[1], 0)


def _pad_row(s, off, t):
    return s * (t + 2 * PAD) + PAD + off


def _store_padded(pad_ref, val, pieces, t):
    for s, off, n, o in pieces:
        pad_ref[_rows(_pad_row(s, off, t), n, SUBLANES), :] = val[o:o + n]


def _scale_padded(pad_ref, val, pieces, t):
    for s, off, n, o in pieces:
        rows = _rows(_pad_row(s, off, t), n, SUBLANES)
        pad_ref[rows, :] = pad_ref[rows, :] * val[o:o + n]


def _modnorm_chunk(src_ref, h_ref, c, nb, t, gain, shift):
    for s, off, n, o in _pieces(c, nb, t):
        for i in range(0, n, NORM_ROWS):
            x = src_ref[s, _rows(off + i, NORM_ROWS, NORM_ROWS), :]
            ms = jnp.mean(x * x, axis=-1, keepdims=True)
            h_ref[_rows(c * ROW_CHUNK + o + i, NORM_ROWS, NORM_ROWS), :] = (
                x * lax.rsqrt(ms + EPS) * gain + shift).astype(BF16)


def _zero_pads(pad_ref, nb, t):
    z = jnp.zeros((PAD, W_HALF), F32)
    for s in range(nb):
        pad_ref[_pad_row(s, 0, t) - PAD:_pad_row(s, 0, t), :] = z
        pad_ref[_pad_row(s, t, t):_pad_row(s, t, t) + PAD, :] = z


def _pool_phase(pad_ref, ga_ref, wp_ref, ps_ref, ab_ref, nb, t):
    n_rows = POOL_ROWS
    per_seq = t // n_rows

    def step(i, carry):
        s = i // per_seq
        r0 = (i - s * per_seq) * n_rows
        prow = _pad_row(s, r0, t)
        rows = _rows(i * n_rows, n_rows, n_rows)
        pos = r0 + lax.broadcasted_iota(jnp.int32, (n_rows, LANES), 0)
        for g in range(N_POOL_GROUPS):
            hw = POOL_HALF[g]
            ln = _lanes(g)
            halo = n_rows + 2 * SUBLANES
            blk = pad_ref[_rows(prow - SUBLANES, halo, SUBLANES), ln]
            run, n = blk, 1
            while n < 2 * hw:
                run = run + pltpu.roll(run, halo - n, 0)
                n *= 2
            if hw < SUBLANES:
                run = pltpu.roll(run, halo - (SUBLANES - hw), 0)
            win = run[:n_rows]
            cnt = (jnp.minimum(pos + hw, t) - jnp.maximum(pos - hw, 0)).astype(F32)
            p = (win / cnt - blk[SUBLANES:SUBLANES + n_rows]).astype(BF16)
            y = _dot(p, wp_ref[g]) * ps_ref[:, ln] * ga_ref[rows, ln]
            ab_ref[rows, ln] = y.astype(BF16)
        return carry
    lax.fori_loop(0, nb * per_seq, step, 0)


def _out_proj_chunk(ab_ref, w_ref, x_ref, gate, dst_ref, c, nb, t):
    lhs = ab_ref[_rows(c * ROW_CHUNK, ROW_CHUNK, ROW_CHUNK), :]
    for g in range(D_MODEL // W_HALF):
        y = _dot(lhs, w_ref[:, _group(g)])
        for s, off, n, o in _pieces(c, nb, t):
            rows = _rows(off, n, n)
            dst_ref[s, rows, _group(g)] = x_ref[s, rows, _group(g)] + gate[:, _group(g)] * y[o:o + n]


def _conv_phase(pad_c, pad_d, bc_ref, ga_ref, gb_ref, cc_ref, cdw_ref, cdb_ref, lng_ref, lnb_ref,
                ab_ref, nb, t):
    n_rows = CONV_ROWS
    per_seq = t // n_rows

    def step(i, carry):
        s = i // per_seq
        r0 = (i - s * per_seq) * n_rows
        prow = _pad_row(s, r0, t)
        rows = _rows(i * n_rows, n_rows, n_rows)
        z = []
        for g in range(W_HALF // LANES):
            ln = _lanes(g)
            blk = pad_c[_rows(prow - SUBLANES, n_rows + 2 * SUBLANES, SUBLANES), ln]
            c3 = None
            for j in range(CONV_C):
                o = SUBLANES + j - CONV_C // 2
                term = blk[o:o + n_rows] * cc_ref[j:j + 1, ln]
                c3 = term if c3 is None else c3 + term
            ab_ref[rows, ln] = (bc_ref[rows, ln] * c3 * ga_ref[rows, ln]).astype(BF16)
            acc = None
            for sft in range(SUBLANES):
                part = None
                for a in range((CONV_D - sft + SUBLANES - 1) // SUBLANES):
                    j = SUBLANES * a + sft
                    src = pad_d[_rows(prow - 2 * SUBLANES + SUBLANES * a, n_rows + SUBLANES,
                                      SUBLANES), ln]
                    term = src * cdw_ref[j:j + 1, ln]
                    part = term if part is None else part + term
                o = SUBLANES + sft - (CONV_D // 2 - SUBLANES)
                part = part[o:o + n_rows]
                acc = part if acc is None else acc + part
            z.append(acc + cdb_ref[:, ln])
        z = jnp.concatenate(z, axis=-1)
        mu = jnp.mean(z, axis=-1, keepdims=True)
        zc = z - mu
        var = jnp.mean(zc * zc, axis=-1, keepdims=True)
        zn = zc * lax.rsqrt(var + EPS) * lng_ref[...] + lnb_ref[...]
        ab_ref[rows, W_HALF:] = (_silu(zn) * gb_ref[rows, :]).astype(BF16)
        return carry
    lax.fori_loop(0, nb * per_seq, step, 0)


def _final_norm_chunk(y_ref, fg, c, nb, t):
    for s, off, n, _ in _pieces(c, nb, t):
        for i in range(0, n, NORM_ROWS):
            rows = _rows(off + i, NORM_ROWS, NORM_ROWS)
            x = y_ref[s, rows, :]
            ms = jnp.mean(x * x, axis=-1, keepdims=True)
            y_ref[s, rows, :] = x * lax.rsqrt(ms + EPS) * fg


def _odd_layer(y_ref, m_row, g_row, fg, wio_ref, cc_ref, cdw_ref, cdb_ref, lng_ref, lnb_ref, woo_ref,
               h_ref, pad_c, pad_d, bc_ref, ga_ref, gb_ref, ab_ref, nb, t):
    shift = m_row[:, :D_MODEL]
    gain = g_row * (1.0 + m_row[:, D_MODEL:2 * D_MODEL])
    gate = m_row[:, 2 * D_MODEL:]
    n_chunks = nb * t // ROW_CHUNK

    def in_proj(c):
        _modnorm_chunk(y_ref, h_ref, c, nb, t, gain, shift)
        rows = _rows(c * ROW_CHUNK, ROW_CHUNK, ROW_CHUNK)
        pieces = _pieces(c, nb, t)
        h = h_ref[rows, :]
        bc_ref[rows, :] = _dot(h, wio_ref[:, _group(0)])
        _store_padded(pad_c, _dot(h, wio_ref[:, _group(1)]), pieces, t)
        _scale_padded(pad_c, _dot(h, wio_ref[:, _group(2)]), pieces, t)
        ga_ref[rows, :] = _silu(_dot(h, wio_ref[:, _group(3)]))
        _store_padded(pad_d, _dot(h, wio_ref[:, _group(4)]), pieces, t)
        _scale_padded(pad_d, _sigmoid(_dot(h, wio_ref[:, _group(5)])), pieces, t)
        gb_ref[rows, :] = _silu(_dot(h, wio_ref[:, _group(6)]))
    _for_chunks(n_chunks, in_proj)

    _conv_phase(pad_c, pad_d, bc_ref, ga_ref, gb_ref, cc_ref, cdw_ref, cdb_ref, lng_ref, lnb_ref,
                ab_ref, nb, t)

    def out_proj(c):
        _out_proj_chunk(ab_ref, woo_ref, y_ref, gate, y_ref, c, nb, t)
        _final_norm_chunk(y_ref, fg, c, nb, t)
    _for_chunks(n_chunks, out_proj, unrolled=True)


def _even_in_proj(x_ref, m_row, g_row, w_ref, h_ref, pad_a, ga_ref, gb_ref, q_ref, k_ref, v_ref,
                  kv_t, nb, t):
    shift = m_row[:, :D_MODEL]
    gain = g_row * (1.0 + m_row[:, D_MODEL:2 * D_MODEL])

    def in_proj(c):
        _modnorm_chunk(x_ref, h_ref, c, nb, t, gain, shift)
        rows = _rows(c * ROW_CHUNK, ROW_CHUNK, ROW_CHUNK)
        pieces = _pieces(c, nb, t)
        h = h_ref[rows, :]
        _store_padded(pad_a, _dot(h, w_ref[:, _group(0)]), pieces, t)
        ga_ref[rows, :] = _silu(_dot(h, w_ref[:, _group(1)]))
        q_ref[rows, :] = (_dot(h, w_ref[:, _group(2)]) * Q_SCALE).astype(BF16)
        for i, (dst, g) in enumerate(((k_ref, 3), (v_ref, 4))):
            if kv_t is None:
                dst[rows, :] = _dot(h, w_ref[:, _group(g)]).astype(BF16)
                continue
            acc = _dot_nt(kv_t[2 + i][...], h)
            dst[:, rows] = acc.astype(BF16)
            for s, off, n, o in pieces:
                for hd in range(N_HEADS):
                    kv_t[i][s, 0, hd, :, _rows(off, n, n)] = (
                        acc[hd * HEAD_DIM:(hd + 1) * HEAD_DIM, o:o + n])
        gb_ref[rows, :] = _silu(_dot(h, w_ref[:, _group(5)]))
    _for_chunks(nb * t // ROW_CHUNK, in_proj)


def _even_out_proj(x_ref, y_ref, m_row, w_ref, ab_ref, nb, t):
    gate = m_row[:, 2 * D_MODEL:]
    _for_chunks(nb * t // ROW_CHUNK,
                lambda c: _out_proj_chunk(ab_ref, w_ref, x_ref, gate, y_ref, c, nb, t))


def _split_heads(x):
    lane = lax.broadcasted_iota(jnp.int32, (1, LANES), 1)
    first = jnp.where(lane < HEAD_DIM, 1.0, 0.0).astype(x.dtype)
    return jnp.concatenate([x * first, x * (1 - first)], axis=0)


def _merge_heads(o):
    n = o.shape[0] // 2
    lane = lax.broadcasted_iota(jnp.int32, (n, LANES), 1)
    return jnp.where(lane < HEAD_DIM, o[:n], o[n:])


def _context_attention(q_ref, kt_ref, vt_ref, gb_ref, ab_ref, nb, t):
    for s in range(nb):
        seq = slice(s * t, (s + 1) * t)
        for j in range(N_HEADS // 2):
            ln = _lanes(j)
            kp = kt_ref[ln, seq]
            vp = vt_ref[ln, seq]
            for r0 in range(0, t, Q_ROWS):
                rows = slice(s * t + r0, s * t + r0 + Q_ROWS)
                sc = _dot(_split_heads(q_ref[rows, ln]), kp)
                p = jnp.exp2(sc - jnp.max(sc, axis=-1, keepdims=True))
                o = _dot_nt(p.astype(BF16), vp) / jnp.sum(p, axis=-1, keepdims=True)
                ab_ref[rows, W_HALF + j * LANES:W_HALF + (j + 1) * LANES] = (
                    _merge_heads(o) * gb_ref[rows, ln]).astype(BF16)


def _prompt_body(x_ref, m_ref, ng_ref, fg_ref, wie_ref, wp_ref, ps_ref, woe_ref, wio_ref, cc_ref,
                 cdw_ref, cdb_ref, lng_ref, lnb_ref, woo_ref, wkt_ref, wvt_ref,
                 y_ref, ko_ref, vo_ref,
                 h_ref, pad_a, pad_b, ga_ref, gb_ref, bc_ref, q_ref, kt_ref, vt_ref, ab_ref,
                 *, nb, t):
    _zero_pads(pad_a, nb, t)
    _zero_pads(pad_b, nb, t)
    m_even = _cond_row(m_ref, 0, 0)
    _even_in_proj(x_ref, m_even, ng_ref[0:1, :], wie_ref, h_ref, pad_a, ga_ref, gb_ref,
                  q_ref, kt_ref, vt_ref, (ko_ref, vo_ref, wkt_ref, wvt_ref), nb, t)
    _pool_phase(pad_a, ga_ref, wp_ref, ps_ref, ab_ref, nb, t)
    _context_attention(q_ref, kt_ref, vt_ref, gb_ref, ab_ref, nb, t)
    _even_out_proj(x_ref, y_ref, m_even, woe_ref, ab_ref, nb, t)
    _odd_layer(y_ref, _cond_row(m_ref, 1, 0), ng_ref[1:2, :], fg_ref[...], wio_ref, cc_ref, cdw_ref,
               cdb_ref, lng_ref, lnb_ref, woo_ref, h_ref, pad_a, pad_b, bc_ref, ga_ref, gb_ref, ab_ref,
               nb, t)


def _rpb_rows(rpb_ref, e_ref):
    n = rpb_ref.shape[0]
    lane = lax.broadcasted_iota(jnp.int32, (n, LANES), 1)
    i = jnp.where(lane < GRID_W, lane, lane - LANES)
    idx = jnp.clip(i, -(WIN_W - 1), WIN_W - 1) + (WIN_W - 1)
    rp = rpb_ref[...]
    e = jnp.zeros((n, LANES), F32)
    for d in range(2 * WIN_W - 1):
        e = jnp.where(idx == d, rp[:, d:d + 1], e)
    e_ref[...] = e


N_DR = 2 * WIN_H - 1
PAIR_TILES = N_DR // 2


def _bias_tile_index(j, dr_lo):
    if isinstance(dr_lo, int):
        parity, half = dr_lo % 2, dr_lo // 2
    else:
        parity, half = dr_lo & 1, lax.shift_right_logical(dr_lo, 1)
    return (2 * j + parity) * PAIR_TILES + half


def _bias_tables(e_ref, bias_ref):
    q = lax.broadcasted_iota(jnp.int32, (GRID_W, LANES), 0)
    lane = lax.broadcasted_iota(jnp.int32, (GRID_W, LANES), 1)
    kw = jnp.where(lane < GRID_W, lane, lane - GRID_W)
    start = jnp.clip(q - WIN_W // 2, 0, GRID_W - WIN_W)
    col_ok = (kw >= start) & (kw < start + WIN_W)
    for j in range(N_HEADS // 2):
        for dr in range(N_DR - 1):
            for e in range(2):
                r_lo = (2 * j + e) * N_DR + dr
                lo = jnp.broadcast_to(e_ref[r_lo:r_lo + 1, :], (GRID_W, LANES))
                hi = jnp.broadcast_to(e_ref[r_lo + 1:r_lo + 2, :], (GRID_W, LANES))
                lo = pltpu.roll(lo, 0, 1, stride=1, stride_axis=0)
                hi = pltpu.roll(hi, GRID_W, 1, stride=1, stride_axis=0)
                tile = jnp.where(lane < GRID_W, lo, hi)
                bias_ref[_bias_tile_index(j, dr), e * GRID_W:(e + 1) * GRID_W, :] = jnp.where(
                    col_ok, tile * LOG2_E, MASKED)


def _neighbourhood_attention(q_ref, k_ref, v_ref, ck_ref, cv_ref, bias_ref, kvc_ref, gb_ref, ab_ref, t):
    grid_h = t // GRID_W
    band = WIN_H * GRID_W
    for j in range(N_HEADS // 2):
        ln = _lanes(j)
        for i, src in enumerate((ck_ref, cv_ref)):
            kvc_ref[i] = jnp.concatenate([src[0, 0, 2 * j], src[0, 0, 2 * j + 1]],
                                         axis=0).astype(BF16)

        def per_group(g, carry, ln=ln, j=j):
            scored = []
            for u in range(NA_GROUP):
                r = g * NA_GROUP + u
                start = jnp.clip(r - WIN_H // 2, 0, grid_h - WIN_H)
                rows = _rows(r * GRID_W, GRID_W, GRID_W)
                keys = _rows(start * GRID_W, band, GRID_W)
                q2 = _split_heads(q_ref[rows, ln])
                dr0 = (WIN_H - 1) - (r - start)
                bias = jnp.concatenate([bias_ref[_bias_tile_index(j, dr0 + 2 * i)]
                                        for i in range(WIN_H // 2)], axis=-1)
                scored.append((rows, keys, _dot_nt(q2, k_ref[keys, ln]) + bias, _dot(q2, kvc_ref[0])))
            weighted = []
            for rows, keys, s_loc, s_ctx in scored:
                mx = jnp.maximum(jnp.max(s_loc, axis=-1, keepdims=True),
                                 jnp.max(s_ctx, axis=-1, keepdims=True))
                p_loc = jnp.exp2(s_loc - mx)
                p_ctx = jnp.exp2(s_ctx - mx)
                den = (jnp.sum(p_loc, axis=-1, keepdims=True)
                       + jnp.sum(p_ctx, axis=-1, keepdims=True))
                weighted.append((rows, keys, p_loc.astype(BF16), p_ctx.astype(BF16), den))
            for rows, keys, p_loc, p_ctx, den in weighted:
                o = (_dot(p_loc, v_ref[keys, ln]) + _dot_nt(p_ctx, kvc_ref[1])) / den
                ab_ref[rows, W_HALF + j * LANES:W_HALF + (j + 1) * LANES] = (
                    _merge_heads(o) * gb_ref[rows, ln]).astype(BF16)
            return carry
        lax.fori_loop(0, grid_h // NA_GROUP, per_group, 0)


def _sample_body(x_ref, m_ref, ng_ref, fg_ref, wie_ref, wp_ref, ps_ref, woe_ref, wio_ref, cc_ref,
                 cdw_ref, cdb_ref, lng_ref, lnb_ref, woo_ref, ck_ref, cv_ref, rpb_ref,
                 y_ref,
                 h_ref, pad_a, pad_b, ga_ref, gb_ref, bc_ref, q_ref, k_ref, v_ref, ab_ref,
                 e_ref, bias_ref, kvc_ref, *, t):
    _zero_pads(pad_a, 1, t)
    _zero_pads(pad_b, 1, t)

    @pl.when(pl.program_id(0) == 0)
    def _():
        _rpb_rows(rpb_ref, e_ref)
        _bias_tables(e_ref, bias_ref)

    cond = pl.program_id(0) + 1
    m_even = _cond_row(m_ref, 0, cond)
    _even_in_proj(x_ref, m_even, ng_ref[0:1, :], wie_ref, h_ref, pad_a, ga_ref, gb_ref,
                  q_ref, k_ref, v_ref, None, 1, t)
    _pool_phase(pad_a, ga_ref, wp_ref, ps_ref, ab_ref, 1, t)
    _neighbourhood_attention(q_ref, k_ref, v_ref, ck_ref, cv_ref, bias_ref, kvc_ref, gb_ref, ab_ref, t)
    _even_out_proj(x_ref, y_ref, m_even, woe_ref, ab_ref, 1, t)
    _odd_layer(y_ref, _cond_row(m_ref, 1, cond), ng_ref[1:2, :], fg_ref[...], wio_ref, cc_ref, cdw_ref,
               cdb_ref, lng_ref, lnb_ref, woo_ref, h_ref, pad_a, pad_b, bc_ref, ga_ref, gb_ref, ab_ref,
               1, t)


def _const_spec(shape):
    zeros = (0,) * len(shape)
    return pl.BlockSpec(shape, lambda i: zeros, pipeline_mode=pl.Buffered(1))


def _stream_scratch(nb, t, kv_transposed):
    r = nb * t
    padded = nb * (t + 2 * PAD)
    kv = (W_HALF, r) if kv_transposed else (r, W_HALF)
    return [
        pltpu.VMEM((r, D_MODEL), BF16),
        pltpu.VMEM((padded, W_HALF), F32),
        pltpu.VMEM((padded, W_HALF), F32),
        pltpu.VMEM((r, W_HALF), F32),
        pltpu.VMEM((r, W_HALF), F32),
        pltpu.VMEM((r, W_HALF), F32),
        pltpu.VMEM((r, W_HALF), BF16),
        pltpu.VMEM(kv, BF16),
        pltpu.VMEM(kv, BF16),
        pltpu.VMEM((r, D_MODEL), BF16),
    ]


def _cast_body(wie_ref, woe_ref, wio_ref, woo_ref, o_wie, o_woe, o_wio, o_woo, o_wkt, o_wvt):
    for src, dst in ((wie_ref, o_wie), (woe_ref, o_woe), (wio_ref, o_wio), (woo_ref, o_woo)):
        dst[...] = src[0].astype(BF16)
    o_wkt[...] = wie_ref[0, :, _group(3)].T.astype(BF16)
    o_wvt[...] = wie_ref[0, :, _group(4)].T.astype(BF16)


def _cast_weights(w_in_even, w_out_even, w_in_odd, w_out_odd):
    ws = (w_in_even, w_out_even, w_in_odd, w_out_odd)
    rows = w_in_even.shape[1]
    assert all(w.shape[0] == 1 and w.shape[1] == rows for w in ws) and rows % CAST_ROWS == 0
    t_shape = jax.ShapeDtypeStruct((W_HALF, rows), BF16)
    t_spec = pl.BlockSpec((W_HALF, CAST_ROWS), lambda i: (0, i))
    return pl.pallas_call(
        _cast_body,
        out_shape=[jax.ShapeDtypeStruct(w.shape[1:], BF16) for w in ws] + [t_shape, t_shape],
        grid=(rows // CAST_ROWS,),
        in_specs=[pl.BlockSpec((1, CAST_ROWS, w.shape[2]), lambda i: (0, i, 0)) for w in ws],
        out_specs=[pl.BlockSpec((CAST_ROWS, w.shape[2]), lambda i: (i, 0)) for w in ws]
                  + [t_spec, t_spec],
        compiler_params=pltpu.CompilerParams(dimension_semantics=("arbitrary",)),
        name="cast",
    )(*ws)


def _weight_args(norm_g, final_g, wie, w_pool, pool_scale, woe, wio, conv_c, conv_d, conv_d_b, ln_g,
                 ln_b, woo):
    args = [
        norm_g, final_g.reshape(1, D_MODEL), wie, w_pool[0].astype(BF16), pool_scale, woe,
        wio, conv_c[0], conv_d[0], conv_d_b, ln_g, ln_b, woo,
    ]
    return args, [_const_spec(a.shape) for a in args]


def kernel(x_prompt, x_sample, cache_k, cache_v, c, c_ctx, norm_g, w_mod, b_mod, w_in_even, w_pool,
           pool_scale, rpb, w_out_even, w_in_odd, conv_c, conv_d, conv_d_b, ln_g, ln_b, w_out_odd,
           final_g):
    batch, seq, d = x_prompt.shape
    dec_batch, dec_seq, _ = x_sample.shape
    assert d == D_MODEL and w_mod.shape[0] == 2 and w_in_even.shape[0] == 1 and w_in_odd.shape[0] == 1
    assert (NB_PROMPT * seq) % ROW_CHUNK == 0 and ROW_CHUNK % seq == 0 and seq % Q_ROWS == 0
    assert dec_seq % ROW_CHUNK == 0 and dec_seq // GRID_W >= WIN_H
    assert seq % POOL_ROWS == 0 and seq % CONV_ROWS == 0
    assert dec_seq % POOL_ROWS == 0 and dec_seq % CONV_ROWS == 0
    assert (dec_seq // GRID_W) % NA_GROUP == 0

    cond_rows = SUBLANES * ((1 + dec_batch + SUBLANES - 1) // SUBLANES)
    m = _modulation(c_ctx, c, w_mod, b_mod, cond_rows)
    m_spec = _const_spec(m.shape)

    wie, woe, wio, woo, wkt, wvt = _cast_weights(w_in_even, w_out_even, w_in_odd, w_out_odd)
    w_args, w_specs = _weight_args(norm_g, final_g, wie, w_pool, pool_scale, woe, wio, conv_c,
                                   conv_d, conv_d_b, ln_g, ln_b, woo)

    nb = NB_PROMPT
    assert batch % nb == 0
    kv_shape = jax.ShapeDtypeStruct((batch, 1, N_HEADS, HEAD_DIM, seq), F32)
    kv_spec = pl.BlockSpec((nb, 1, N_HEADS, HEAD_DIM, seq), lambda i: (i, 0, 0, 0, 0))
    w_kv_t = [wkt, wvt]
    y_prompt, new_kt, new_vt = pl.pallas_call(
        functools.partial(_prompt_body, nb=nb, t=seq),
        out_shape=(jax.ShapeDtypeStruct(x_prompt.shape, F32), kv_shape, kv_shape),
        grid=(batch // nb,),
        in_specs=[pl.BlockSpec((nb, seq, d), lambda i: (i, 0, 0)), m_spec] + w_specs
                 + [_const_spec(w.shape) for w in w_kv_t],
        out_specs=(pl.BlockSpec((nb, seq, d), lambda i: (i, 0, 0)), kv_spec, kv_spec),
        scratch_shapes=_stream_scratch(nb, seq, True),
        compiler_params=pltpu.CompilerParams(dimension_semantics=("arbitrary",),
                                             vmem_limit_bytes=VMEM_LIMIT),
        name="prompt",
    )(x_prompt, m, *w_args, *w_kv_t)

    past = cache_k.shape[3]
    cache_spec = pl.BlockSpec((1, 1, N_HEADS, HEAD_DIM, past), lambda i: (i, 0, 0, 0, 0))
    rpb2 = rpb[0].reshape(N_HEADS * (2 * WIN_H - 1), 2 * WIN_W - 1)
    y_sample = pl.pallas_call(
        functools.partial(_sample_body, t=dec_seq),
        out_shape=jax.ShapeDtypeStruct(x_sample.shape, F32),
        grid=(dec_batch,),
        in_specs=[pl.BlockSpec((1, dec_seq, d), lambda i: (i, 0, 0), pipeline_mode=pl.Buffered(1)),
                  m_spec] + w_specs
                 + [cache_spec, cache_spec, _const_spec(rpb2.shape)],
        out_specs=pl.BlockSpec((1, dec_seq, d), lambda i: (i, 0, 0)),
        scratch_shapes=_stream_scratch(1, dec_seq, False) + [
            pltpu.VMEM(rpb2.shape[:1] + (LANES,), F32),
            pltpu.VMEM((N_HEADS * PAIR_TILES, 2 * GRID_W, LANES), F32),
            pltpu.VMEM((2, LANES, past), BF16),
        ],
        compiler_params=pltpu.CompilerParams(dimension_semantics=("arbitrary",),
                                             vmem_limit_bytes=VMEM_LIMIT),
        name="sample",
    )(x_sample, m, *w_args, jnp.swapaxes(cache_k, 3, 4), jnp.swapaxes(cache_v, 3, 4), rpb2)

    return (y_prompt, y_sample, jnp.swapaxes(new_kt, 3, 4), jnp.swapaxes(new_vt, 3, 4))
```

```python
import functools

import jax
import jax.numpy as jnp
from jax import lax
from jax.experimental import pallas as pl
from jax.experimental.pallas import tpu as pltpu

F32 = jnp.float32
BF16 = jnp.bfloat16

D_MODEL = 1024
W_HALF = 512
N_POOL_GROUPS = 4
POOL_HALF = (1, 2, 4, 8)
N_HEADS = 8
HEAD_DIM = 64
GRID_W = 64
WIN_H = 8
WIN_W = 16
CONV_C = 3
CONV_D = 31
EPS = 1e-6
MASKED = -1e30
LOG2_E = 1.4426950408889634
Q_SCALE = HEAD_DIM ** -0.5 * LOG2_E

LANES = 128
SUBLANES = 8
PAD = 16
ROW_CHUNK = 512
NORM_ROWS = 32
POOL_ROWS = 256
CONV_ROWS = 128
Q_ROWS = 128
NB_PROMPT = 2
NA_GROUP = 8
MOD_COLS = 1536
STAGE_ROWS = 256
VMEM_LIMIT = 58 * 1024 * 1024

assert PAD >= CONV_D // 2 + 1 and PAD % SUBLANES == 0 and PAD >= 2 * SUBLANES
assert max(POOL_HALF) <= SUBLANES


def _sigmoid(x):
    return 1.0 / (1.0 + jnp.exp(-x))


def _silu(x):
    return x * _sigmoid(x)


def _dot(a, b):
    return jnp.dot(a, b, preferred_element_type=F32)


def _dot_nt(a, b):
    return lax.dot_general(a, b, (((1,), (1,)), ((), ())), preferred_element_type=F32)


def _lanes(j):
    return slice(j * LANES, (j + 1) * LANES)


def _group(g):
    return slice(g * W_HALF, (g + 1) * W_HALF)


def _rows(start, size, align):
    if isinstance(start, int):
        return slice(start, start + size)
    return pl.ds(pl.multiple_of(start, align), size)


def _mod_body(cctx_ref, c_ref, w_ref, b_ref, o_ref):
    rows, d = o_ref.shape[1], cctx_ref.shape[1]
    r = lax.broadcasted_iota(jnp.int32, (rows, d), 0)
    cond = jnp.where(r == 0, cctx_ref[...], 0.0)
    for i in range(c_ref.shape[0]):
        cond = jnp.where(r == i + 1, c_ref[i:i + 1, :], cond)
    bias = jnp.where(pl.program_id(0) == 0, b_ref[0:1, :], b_ref[1:2, :])
    o_ref[0] = _dot(_silu(cond).astype(BF16), w_ref[0].astype(BF16)) + bias


def _modulation(c_ctx, c, w_mod, b_mod, rows):
    depth, d, n = w_mod.shape
    assert depth == 2 and 1 + c.shape[0] <= rows
    return pl.pallas_call(
        _mod_body,
        out_shape=jax.ShapeDtypeStruct((depth, rows, n), F32),
        grid=(depth, n // MOD_COLS),
        in_specs=[
            pl.BlockSpec((1, d), lambda l, j: (0, 0)),
            pl.BlockSpec(c.shape, lambda l, j: (0, 0)),
            pl.BlockSpec((1, d, MOD_COLS), lambda l, j: (l, 0, j)),
            pl.BlockSpec((depth, MOD_COLS), lambda l, j: (0, j)),
        ],
        out_specs=pl.BlockSpec((1, rows, MOD_COLS), lambda l, j: (l, 0, j)),
        compiler_params=pltpu.CompilerParams(dimension_semantics=("arbitrary", "arbitrary")),
        name="mod",
    )(c_ctx.reshape(1, d), c, w_mod, b_mod)


def _cond_row(m_ref, layer, row):
    if isinstance(row, int):
        return m_ref[layer, row:row + 1, :]
    m = m_ref[layer]
    keep = lax.broadcasted_iota(jnp.int32, m.shape, 0) == row
    return jnp.sum(jnp.where(keep, m, 0.0), axis=0, keepdims=True)


def _pieces(c, nb, t):
    if t >= ROW_CHUNK:
        per_seq = t // ROW_CHUNK
        s = 0 if nb == 1 else c // per_seq
        return [(s, (c - s * per_seq) * ROW_CHUNK, ROW_CHUNK, 0)]
    per_chunk = ROW_CHUNK // t
    return [(c * per_chunk + i, 0, t, i * t) for i in range(per_chunk)]


def _for_chunks(n, body, unrolled=False):
    if unrolled or n == 1:
        for c in range(n):
            body(c)
    else:
        lax.fori_loop(0, n, lambda c, carry: (body(c), carry)[1], 0)


def _pad_row(s, off, t):
    return s * (t + 2 * PAD) + PAD + off


def _store_padded(pad_ref, val, pieces, t):
    for s, off, n, o in pieces:
        pad_ref[_rows(_pad_row(s, off, t), n, SUBLANES), :] = val[o:o + n]


def _scale_padded(pad_ref, val, pieces, t):
    for s, off, n, o in pieces:
        rows = _rows(_pad_row(s, off, t), n, SUBLANES)
        pad_ref[rows, :] = pad_ref[rows, :] * val[o:o + n]


def _modnorm_chunk(src_ref, h_ref, c, nb, t, gain, shift):
    for s, off, n, o in _pieces(c, nb, t):
        for i in range(0, n, NORM_ROWS):
            x = src_ref[s, _rows(off + i, NORM_ROWS, NORM_ROWS), :]
            ms = jnp.mean(x * x, axis=-1, keepdims=True)
            h_ref[_rows(c * ROW_CHUNK + o + i, NORM_ROWS, NORM_ROWS), :] = (
                x * lax.rsqrt(ms + EPS) * gain + shift).astype(BF16)


def _zero_pads(pad_ref, nb, t):
    z = jnp.zeros((PAD, W_HALF), F32)
    for s in range(nb):
        pad_ref[_pad_row(s, 0, t) - PAD:_pad_row(s, 0, t), :] = z
        pad_ref[_pad_row(s, t, t):_pad_row(s, t, t) + PAD, :] = z


def _pool_phase(pad_ref, ga_ref, wp_ref, ps_ref, ab_ref, nb, t):
    n_rows = POOL_ROWS
    per_seq = t // n_rows

    def step(i, carry):
        s = i // per_seq
        r0 = (i - s * per_seq) * n_rows
        prow = _pad_row(s, r0, t)
        rows = _rows(i * n_rows, n_rows, n_rows)
        pos = r0 + lax.broadcasted_iota(jnp.int32, (n_rows, LANES), 0)
        for g in range(N_POOL_GROUPS):
            hw = POOL_HALF[g]
            ln = _lanes(g)
            halo = n_rows + 2 * SUBLANES
            blk = pad_ref[_rows(prow - SUBLANES, halo, SUBLANES), ln]
            run, n = blk, 1
            while n < 2 * hw:
                run = run + pltpu.roll(run, halo - n, 0)
                n *= 2
            if hw < SUBLANES:
                run = pltpu.roll(run, halo - (SUBLANES - hw), 0)
            win = run[:n_rows]
            cnt = (jnp.minimum(pos + hw, t) - jnp.maximum(pos - hw, 0)).astype(F32)
            p = (win / cnt - blk[SUBLANES:SUBLANES + n_rows]).astype(BF16)
            y = _dot(p, wp_ref[g]) * ps_ref[:, ln] * ga_ref[rows, ln]
            ab_ref[rows, ln] = y.astype(BF16)
        return carry
    lax.fori_loop(0, nb * per_seq, step, 0)


def _out_proj_chunk(ab_ref, w_ref, x_ref, gate, dst_ref, c, nb, t):
    lhs = ab_ref[_rows(c * ROW_CHUNK, ROW_CHUNK, ROW_CHUNK), :]
    for g in range(D_MODEL // W_HALF):
        y = _dot(lhs, w_ref[:, _group(g)])
        for s, off, n, o in _pieces(c, nb, t):
            rows = _rows(off, n, n)
            dst_ref[s, rows, _group(g)] = x_ref[s, rows, _group(g)] + gate[:, _group(g)] * y[o:o + n]


def _conv_phase(pad_c, pad_d, bc_ref, ga_ref, gb_ref, cc_ref, cdw_ref, cdb_ref, lng_ref, lnb_ref,
                ab_ref, nb, t):
    n_rows = CONV_ROWS
    per_seq = t // n_rows

    def step(i, carry):
        s = i // per_seq
        r0 = (i - s * per_seq) * n_rows
        prow = _pad_row(s, r0, t)
        rows = _rows(i * n_rows, n_rows, n_rows)
        z = []
        for g in range(W_HALF // LANES):
            ln = _lanes(g)
            blk = pad_c[_rows(prow - SUBLANES, n_rows + 2 * SUBLANES, SUBLANES), ln]
            c3 = None
            for j in range(CONV_C):
                o = SUBLANES + j - CONV_C // 2
                term = blk[o:o + n_rows] * cc_ref[j:j + 1, ln]
                c3 = term if c3 is None else c3 + term
            ab_ref[rows, ln] = (bc_ref[rows, ln] * c3 * ga_ref[rows, ln]).astype(BF16)
            acc = None
            for sft in range(SUBLANES):
                part = None
                for a in range((CONV_D - sft + SUBLANES - 1) // SUBLANES):
                    j = SUBLANES * a + sft
                    src = pad_d[_rows(prow - 2 * SUBLANES + SUBLANES * a, n_rows + SUBLANES,
                                      SUBLANES), ln]
                    term = src * cdw_ref[j:j + 1, ln]
                    part = term if part is None else part + term
                o = SUBLANES + sft - (CONV_D // 2 - SUBLANES)
                part = part[o:o + n_rows]
                acc = part if acc is None else acc + part
            z.append(acc + cdb_ref[:, ln])
        z = jnp.concatenate(z, axis=-1)
        mu = jnp.mean(z, axis=-1, keepdims=True)
        zc = z - mu
        var = jnp.mean(zc * zc, axis=-1, keepdims=True)
        zn = zc * lax.rsqrt(var + EPS) * lng_ref[...] + lnb_ref[...]
        ab_ref[rows, W_HALF:] = (_silu(zn) * gb_ref[rows, :]).astype(BF16)
        return carry
    lax.fori_loop(0, nb * per_seq, step, 0)


def _final_norm_chunk(y_ref, fg, c, nb, t):
    for s, off, n, _ in _pieces(c, nb, t):
        for i in range(0, n, NORM_ROWS):
            rows = _rows(off + i, NORM_ROWS, NORM_ROWS)
            x = y_ref[s, rows, :]
            ms = jnp.mean(x * x, axis=-1, keepdims=True)
            y_ref[s, rows, :] = x * lax.rsqrt(ms + EPS) * fg


def _odd_layer(y_ref, m_row, g_row, fg, wio_ref, cc_ref, cdw_ref, cdb_ref, lng_ref, lnb_ref, woo_ref,
               h_ref, pad_c, pad_d, bc_ref, ga_ref, gb_ref, ab_ref, nb, t):
    shift = m_row[:, :D_MODEL]
    gain = g_row * (1.0 + m_row[:, D_MODEL:2 * D_MODEL])
    gate = m_row[:, 2 * D_MODEL:]
    n_chunks = nb * t // ROW_CHUNK

    def in_proj(c):
        _modnorm_chunk(y_ref, h_ref, c, nb, t, gain, shift)
        rows = _rows(c * ROW_CHUNK, ROW_CHUNK, ROW_CHUNK)
        pieces = _pieces(c, nb, t)
        h = h_ref[rows, :]
        bc_ref[rows, :] = _dot(h, wio_ref[:, _group(0)])
        _store_padded(pad_c, _dot(h, wio_ref[:, _group(1)]), pieces, t)
        _scale_padded(pad_c, _dot(h, wio_ref[:, _group(2)]), pieces, t)
        ga_ref[rows, :] = _silu(_dot(h, wio_ref[:, _group(3)]))
        _store_padded(pad_d, _dot(h, wio_ref[:, _group(4)]), pieces, t)
        _scale_padded(pad_d, _sigmoid(_dot(h, wio_ref[:, _group(5)])), pieces, t)
        gb_ref[rows, :] = _silu(_dot(h, wio_ref[:, _group(6)]))
    _for_chunks(n_chunks, in_proj)

    _conv_phase(pad_c, pad_d, bc_ref, ga_ref, gb_ref, cc_ref, cdw_ref, cdb_ref, lng_ref, lnb_ref,
                ab_ref, nb, t)

    def out_proj(c):
        _out_proj_chunk(ab_ref, woo_ref, y_ref, gate, y_ref, c, nb, t)
        _final_norm_chunk(y_ref, fg, c, nb, t)
    _for_chunks(n_chunks, out_proj, unrolled=True)


def _even_in_proj(x_ref, m_row, g_row, w_ref, h_ref, pad_a, ga_ref, gb_ref, q_ref, k_ref, v_ref,
                  kv_t, nb, t):
    shift = m_row[:, :D_MODEL]
    gain = g_row * (1.0 + m_row[:, D_MODEL:2 * D_MODEL])

    def in_proj(c):
        _modnorm_chunk(x_ref, h_ref, c, nb, t, gain, shift)
        rows = _rows(c * ROW_CHUNK, ROW_CHUNK, ROW_CHUNK)
        pieces = _pieces(c, nb, t)
        h = h_ref[rows, :]
        _store_padded(pad_a, _dot(h, w_ref[:, _group(0)]), pieces, t)
        ga_ref[rows, :] = _silu(_dot(h, w_ref[:, _group(1)]))
        q_ref[rows, :] = (_dot(h, w_ref[:, _group(2)]) * Q_SCALE).astype(BF16)
        for i, (dst, g) in enumerate(((k_ref, 3), (v_ref, 4))):
            if kv_t is None:
                dst[rows, :] = _dot(h, w_ref[:, _group(g)]).astype(BF16)
                continue
            acc = _dot_nt(kv_t[2 + i][...], h)
            dst[:, rows] = acc.astype(BF16)
            for s, off, n, o in pieces:
                for hd in range(N_HEADS):
                    kv_t[i][s, 0, hd, :, _rows(off, n, n)] = (
                        acc[hd * HEAD_DIM:(hd + 1) * HEAD_DIM, o:o + n])
        gb_ref[rows, :] = _silu(_dot(h, w_ref[:, _group(5)]))
    _for_chunks(nb * t // ROW_CHUNK, in_proj)


def _even_out_proj(x_ref, y_ref, m_row, w_ref, ab_ref, nb, t):
    gate = m_row[:, 2 * D_MODEL:]
    _for_chunks(nb * t // ROW_CHUNK,
                lambda c: _out_proj_chunk(ab_ref, w_ref, x_ref, gate, y_ref, c, nb, t))


def _split_heads(x):
    lane = lax.broadcasted_iota(jnp.int32, (1, LANES), 1)
    first = jnp.where(lane < HEAD_DIM, 1.0, 0.0).astype(x.dtype)
    return jnp.concatenate([x * first, x * (1 - first)], axis=0)


def _merge_heads(o):
    n = o.shape[0] // 2
    lane = lax.broadcasted_iota(jnp.int32, (n, LANES), 1)
    return jnp.where(lane < HEAD_DIM, o[:n], o[n:])


def _context_attention(q_ref, kt_ref, vt_ref, gb_ref, ab_ref, nb, t):
    for s in range(nb):
        seq = slice(s * t, (s + 1) * t)
        for j in range(N_HEADS // 2):
            ln = _lanes(j)
            kp = kt_ref[ln, seq]
            vp = vt_ref[ln, seq]
            for r0 in range(0, t, Q_ROWS):
                rows = slice(s * t + r0, s * t + r0 + Q_ROWS)
                sc = _dot(_split_heads(q_ref[rows, ln]), kp)
                p = jnp.exp2(sc - jnp.max(sc, axis=-1, keepdims=True))
                o = _dot_nt(p.astype(BF16), vp) / jnp.sum(p, axis=-1, keepdims=True)
                ab_ref[rows, W_HALF + j * LANES:W_HALF + (j + 1) * LANES] = (
                    _merge_heads(o) * gb_ref[rows, ln]).astype(BF16)


def _stage_weights(w_hbm, w_bf, write_back, wkt_ref, wvt_ref, stage, sem_in):
    chunks = [(k, r0) for k in range(len(w_hbm)) for r0 in range(0, w_hbm[k].shape[1], STAGE_ROWS)]

    def fetch(i):
        k, r0 = chunks[i]
        cols = w_hbm[k].shape[2]
        return pltpu.make_async_copy(w_hbm[k].at[0, pl.ds(r0, STAGE_ROWS), :],
                                     stage.at[i % 2, :, pl.ds(0, cols)], sem_in.at[i % 2])

    fetch(0).start()
    for i, (k, r0) in enumerate(chunks):
        if i + 1 < len(chunks):
            fetch(i + 1).start()
        fetch(i).wait()
        cols = w_hbm[k].shape[2]
        rows = slice(r0, r0 + STAGE_ROWS)
        w_bf[k][rows, :] = stage[i % 2, :, 0:cols].astype(BF16)
        if k == 0:
            wkt_ref[:, rows] = stage[i % 2, :, _group(3)].T.astype(BF16)
            wvt_ref[:, rows] = stage[i % 2, :, _group(4)].T.astype(BF16)
        if i + 1 == len(chunks) or chunks[i + 1][0] != k:
            write_back(k).start()


def _prompt_body(x_ref, m_ref, ng_ref, fg_ref, wp_ref, ps_ref, cc_ref, cdw_ref, cdb_ref, lng_ref,
                 lnb_ref, wie_hbm, woe_hbm, wio_hbm, woo_hbm,
                 y_ref, ko_ref, vo_ref, wie_out, woe_out, wio_out, woo_out,
                 h_ref, pad_a, pad_b, ga_ref, gb_ref, bc_ref, q_ref, kt_ref, vt_ref, ab_ref,
                 wie_ref, woe_ref, wio_ref, woo_ref, wkt_ref, wvt_ref, stage, sem_in, sem_out,
                 *, nb, t):
    w_out = (wie_out, woe_out, wio_out, woo_out)
    w_bf = (wie_ref, woe_ref, wio_ref, woo_ref)

    def write_back(k):
        return pltpu.make_async_copy(w_bf[k], w_out[k], sem_out.at[k])

    @pl.when(pl.program_id(0) == 0)
    def _():
        _stage_weights((wie_hbm, woe_hbm, wio_hbm, woo_hbm), w_bf, write_back, wkt_ref, wvt_ref,
                       stage, sem_in)

    _zero_pads(pad_a, nb, t)
    _zero_pads(pad_b, nb, t)
    m_even = _cond_row(m_ref, 0, 0)
    _even_in_proj(x_ref, m_even, ng_ref[0:1, :], wie_ref, h_ref, pad_a, ga_ref, gb_ref,
                  q_ref, kt_ref, vt_ref, (ko_ref, vo_ref, wkt_ref, wvt_ref), nb, t)
    _pool_phase(pad_a, ga_ref, wp_ref, ps_ref, ab_ref, nb, t)
    _context_attention(q_ref, kt_ref, vt_ref, gb_ref, ab_ref, nb, t)
    _even_out_proj(x_ref, y_ref, m_even, woe_ref, ab_ref, nb, t)
    _odd_layer(y_ref, _cond_row(m_ref, 1, 0), ng_ref[1:2, :], fg_ref[...], wio_ref, cc_ref, cdw_ref,
               cdb_ref, lng_ref, lnb_ref, woo_ref, h_ref, pad_a, pad_b, bc_ref, ga_ref, gb_ref, ab_ref,
               nb, t)

    @pl.when(pl.program_id(0) == 0)
    def _():
        for k in range(len(w_bf)):
            write_back(k).wait()


def _rpb_rows(rpb_ref, e_ref):
    n = rpb_ref.shape[0]
    lane = lax.broadcasted_iota(jnp.int32, (n, LANES), 1)
    i = jnp.where(lane < GRID_W, lane, lane - LANES)
    idx = jnp.clip(i, -(WIN_W - 1), WIN_W - 1) + (WIN_W - 1)
    rp = rpb_ref[...]
    e = jnp.zeros((n, LANES), F32)
    for d in range(2 * WIN_W - 1):
        e = jnp.where(idx == d, rp[:, d:d + 1], e)
    e_ref[...] = e


N_DR = 2 * WIN_H - 1
PAIR_TILES = N_DR // 2


def _bias_tile_index(j, dr_lo):
    if isinstance(dr_lo, int):
        parity, half = dr_lo % 2, dr_lo // 2
    else:
        parity, half = dr_lo & 1, lax.shift_right_logical(dr_lo, 1)
    return (2 * j + parity) * PAIR_TILES + half


def _bias_tables(e_ref, bias_ref):
    q = lax.broadcasted_iota(jnp.int32, (GRID_W, LANES), 0)
    lane = lax.broadcasted_iota(jnp.int32, (GRID_W, LANES), 1)
    kw = jnp.where(lane < GRID_W, lane, lane - GRID_W)
    start = jnp.clip(q - WIN_W // 2, 0, GRID_W - WIN_W)
    col_ok = (kw >= start) & (kw < start + WIN_W)
    for j in range(N_HEADS // 2):
        for dr in range(N_DR - 1):
            for e in range(2):
                r_lo = (2 * j + e) * N_DR + dr
                lo = jnp.broadcast_to(e_ref[r_lo:r_lo + 1, :], (GRID_W, LANES))
                hi = jnp.broadcast_to(e_ref[r_lo + 1:r_lo + 2, :], (GRID_W, LANES))
                lo = pltpu.roll(lo, 0, 1, stride=1, stride_axis=0)
                hi = pltpu.roll(hi, GRID_W, 1, stride=1, stride_axis=0)
                tile = jnp.where(lane < GRID_W, lo, hi)
                bias_ref[_bias_tile_index(j, dr), e * GRID_W:(e + 1) * GRID_W, :] = jnp.where(
                    col_ok, tile * LOG2_E, MASKED)


def _neighbourhood_attention(q_ref, k_ref, v_ref, ck_ref, cv_ref, bias_ref, kvc_ref, gb_ref, ab_ref, t):
    grid_h = t // GRID_W
    band = WIN_H * GRID_W
    for j in range(N_HEADS // 2):
        ln = _lanes(j)
        for i, src in enumerate((ck_ref, cv_ref)):
            kvc_ref[i] = jnp.concatenate([src[0, 0, 2 * j], src[0, 0, 2 * j + 1]],
                                         axis=0).astype(BF16)

        def per_group(g, carry, ln=ln, j=j):
            scored = []
            for u in range(NA_GROUP):
                r = g * NA_GROUP + u
                start = jnp.clip(r - WIN_H // 2, 0, grid_h - WIN_H)
                rows = _rows(r * GRID_W, GRID_W, GRID_W)
                keys = _rows(start * GRID_W, band, GRID_W)
                q2 = _split_heads(q_ref[rows, ln])
                dr0 = (WIN_H - 1) - (r - start)
                bias = jnp.concatenate([bias_ref[_bias_tile_index(j, dr0 + 2 * i)]
                                        for i in range(WIN_H // 2)], axis=-1)
                scored.append((rows, keys, _dot_nt(q2, k_ref[keys, ln]) + bias, _dot(q2, kvc_ref[0])))
            weighted = []
            for rows, keys, s_loc, s_ctx in scored:
                mx = jnp.maximum(jnp.max(s_loc, axis=-1, keepdims=True),
                                 jnp.max(s_ctx, axis=-1, keepdims=True))
                p_loc = jnp.exp2(s_loc - mx)
                p_ctx = jnp.exp2(s_ctx - mx)
                den = (jnp.sum(p_loc, axis=-1, keepdims=True)
                       + jnp.sum(p_ctx, axis=-1, keepdims=True))
                weighted.append((rows, keys, p_loc.astype(BF16), p_ctx.astype(BF16), den))
            for rows, keys, p_loc, p_ctx, den in weighted:
                o = (_dot(p_loc, v_ref[keys, ln]) + _dot_nt(p_ctx, kvc_ref[1])) / den
                ab_ref[rows, W_HALF + j * LANES:W_HALF + (j + 1) * LANES] = (
                    _merge_heads(o) * gb_ref[rows, ln]).astype(BF16)
            return carry
        lax.fori_loop(0, grid_h // NA_GROUP, per_group, 0)


def _sample_body(x_ref, m_ref, ng_ref, fg_ref, wie_ref, wp_ref, ps_ref, woe_ref, wio_ref, cc_ref,
                 cdw_ref, cdb_ref, lng_ref, lnb_ref, woo_ref, ck_ref, cv_ref, rpb_ref,
                 y_ref,
                 h_ref, pad_a, pad_b, ga_ref, gb_ref, bc_ref, q_ref, k_ref, v_ref, ab_ref,
                 e_ref, bias_ref, kvc_ref, *, t):
    _zero_pads(pad_a, 1, t)
    _zero_pads(pad_b, 1, t)

    @pl.when(pl.program_id(0) == 0)
    def _():
        _rpb_rows(rpb_ref, e_ref)
        _bias_tables(e_ref, bias_ref)

    cond = pl.program_id(0) + 1
    m_even = _cond_row(m_ref, 0, cond)
    _even_in_proj(x_ref, m_even, ng_ref[0:1, :], wie_ref, h_ref, pad_a, ga_ref, gb_ref,
                  q_ref, k_ref, v_ref, None, 1, t)
    _pool_phase(pad_a, ga_ref, wp_ref, ps_ref, ab_ref, 1, t)
    _neighbourhood_attention(q_ref, k_ref, v_ref, ck_ref, cv_ref, bias_ref, kvc_ref, gb_ref, ab_ref, t)
    _even_out_proj(x_ref, y_ref, m_even, woe_ref, ab_ref, 1, t)
    _odd_layer(y_ref, _cond_row(m_ref, 1, cond), ng_ref[1:2, :], fg_ref[...], wio_ref, cc_ref, cdw_ref,
               cdb_ref, lng_ref, lnb_ref, woo_ref, h_ref, pad_a, pad_b, bc_ref, ga_ref, gb_ref, ab_ref,
               1, t)


def _const_spec(shape):
    zeros = (0,) * len(shape)
    return pl.BlockSpec(shape, lambda i: zeros, pipeline_mode=pl.Buffered(1))


def _stream_scratch(nb, t, kv_transposed):
    r = nb * t
    padded = nb * (t + 2 * PAD)
    kv = (W_HALF, r) if kv_transposed else (r, W_HALF)
    return [
        pltpu.VMEM((r, D_MODEL), BF16),
        pltpu.VMEM((padded, W_HALF), F32),
        pltpu.VMEM((padded, W_HALF), F32),
        pltpu.VMEM((r, W_HALF), F32),
        pltpu.VMEM((r, W_HALF), F32),
        pltpu.VMEM((r, W_HALF), F32),
        pltpu.VMEM((r, W_HALF), BF16),
        pltpu.VMEM(kv, BF16),
        pltpu.VMEM(kv, BF16),
        pltpu.VMEM((r, D_MODEL), BF16),
    ]


def _small_params(norm_g, final_g, w_pool, pool_scale, conv_c, conv_d, conv_d_b, ln_g, ln_b):
    return [norm_g, final_g.reshape(1, D_MODEL), w_pool[0].astype(BF16), pool_scale, conv_c[0],
            conv_d[0], conv_d_b, ln_g, ln_b]


def kernel(x_prompt, x_sample, cache_k, cache_v, c, c_ctx, norm_g, w_mod, b_mod, w_in_even, w_pool,
           pool_scale, rpb, w_out_even, w_in_odd, conv_c, conv_d, conv_d_b, ln_g, ln_b, w_out_odd,
           final_g):
    batch, seq, d = x_prompt.shape
    dec_batch, dec_seq, _ = x_sample.shape
    assert d == D_MODEL and w_mod.shape[0] == 2 and w_in_even.shape[0] == 1 and w_in_odd.shape[0] == 1
    assert (NB_PROMPT * seq) % ROW_CHUNK == 0 and ROW_CHUNK % seq == 0 and seq % Q_ROWS == 0
    assert dec_seq % ROW_CHUNK == 0 and dec_seq // GRID_W >= WIN_H
    assert seq % POOL_ROWS == 0 and seq % CONV_ROWS == 0
    assert dec_seq % POOL_ROWS == 0 and dec_seq % CONV_ROWS == 0
    assert (dec_seq // GRID_W) % NA_GROUP == 0

    cond_rows = SUBLANES * ((1 + dec_batch + SUBLANES - 1) // SUBLANES)
    m = _modulation(c_ctx, c, w_mod, b_mod, cond_rows)
    m_spec = _const_spec(m.shape)

    small = _small_params(norm_g, final_g, w_pool, pool_scale, conv_c, conv_d, conv_d_b, ln_g, ln_b)
    small_specs = [_const_spec(a.shape) for a in small]
    w_f32 = (w_in_even, w_out_even, w_in_odd, w_out_odd)
    assert all(w.shape[0] == 1 and w.shape[1] % STAGE_ROWS == 0 for w in w_f32)
    any_spec = pl.BlockSpec(memory_space=pl.ANY)

    nb = NB_PROMPT
    assert batch % nb == 0
    kv_shape = jax.ShapeDtypeStruct((batch, 1, N_HEADS, HEAD_DIM, seq), F32)
    kv_spec = pl.BlockSpec((nb, 1, N_HEADS, HEAD_DIM, seq), lambda i: (i, 0, 0, 0, 0))
    y_prompt, new_kt, new_vt, wie, woe, wio, woo = pl.pallas_call(
        functools.partial(_prompt_body, nb=nb, t=seq),
        out_shape=(jax.ShapeDtypeStruct(x_prompt.shape, F32), kv_shape, kv_shape)
                  + tuple(jax.ShapeDtypeStruct(w.shape[1:], BF16) for w in w_f32),
        grid=(batch // nb,),
        in_specs=[pl.BlockSpec((nb, seq, d), lambda i: (i, 0, 0)), m_spec] + small_specs
                 + [any_spec] * len(w_f32),
        out_specs=(pl.BlockSpec((nb, seq, d), lambda i: (i, 0, 0)), kv_spec, kv_spec)
                  + (any_spec,) * len(w_f32),
        scratch_shapes=_stream_scratch(nb, seq, True)
                       + [pltpu.VMEM(w.shape[1:], BF16) for w in w_f32] + [
            pltpu.VMEM((W_HALF, d), BF16),
            pltpu.VMEM((W_HALF, d), BF16),
            pltpu.VMEM((2, STAGE_ROWS, max(w.shape[2] for w in w_f32)), F32),
            pltpu.SemaphoreType.DMA((2,)),
            pltpu.SemaphoreType.DMA((len(w_f32),)),
        ],
        compiler_params=pltpu.CompilerParams(dimension_semantics=("arbitrary",),
                                             vmem_limit_bytes=VMEM_LIMIT),
        name="prompt",
    )(x_prompt, m, *small, *w_f32)
    ng, fg, wp, ps, cc, cdw, cdb, lng, lnb = small
    w_args = [ng, fg, wie, wp, ps, woe, wio, cc, cdw, cdb, lng, lnb, woo]
    w_specs = [_const_spec(a.shape) for a in w_args]

    past = cache_k.shape[3]
    cache_spec = pl.BlockSpec((1, 1, N_HEADS, HEAD_DIM, past), lambda i: (i, 0, 0, 0, 0))
    rpb2 = rpb[0].reshape(N_HEADS * (2 * WIN_H - 1), 2 * WIN_W - 1)
    y_sample = pl.pallas_call(
        functools.partial(_sample_body, t=dec_seq),
        out_shape=jax.ShapeDtypeStruct(x_sample.shape, F32),
        grid=(dec_batch,),
        in_specs=[pl.BlockSpec((1, dec_seq, d), lambda i: (i, 0, 0), pipeline_mode=pl.Buffered(1)),
                  m_spec] + w_specs
                 + [cache_spec, cache_spec, _const_spec(rpb2.shape)],
        out_specs=pl.BlockSpec((1, dec_seq, d), lambda i: (i, 0, 0)),
        scratch_shapes=_stream_scratch(1, dec_seq, False) + [
            pltpu.VMEM(rpb2.shape[:1] + (LANES,), F32),
            pltpu.VMEM((N_HEADS * PAIR_TILES, 2 * GRID_W, LANES), F32),
            pltpu.VMEM((2, LANES, past), BF16),
        ],
        compiler_params=pltpu.CompilerParams(dimension_semantics=("arbitrary",),
                                             vmem_limit_bytes=VMEM_LIMIT),
        name="sample",
    )(x_sample, m, *w_args, jnp.swapaxes(cache_k, 3, 4), jnp.swapaxes(cache_v, 3, 4), rpb2)

    return (y_prompt, y_sample, jnp.swapaxes(new_kt, 3, 4), jnp.swapaxes(new_vt, 3, 4))
```

```python
import functools

import jax
import jax.numpy as jnp
from jax import lax
from jax.experimental import pallas as pl
from jax.experimental.pallas import tpu as pltpu

F32 = jnp.float32
BF16 = jnp.bfloat16

D_MODEL = 1024
W_HALF = 512
N_POOL_GROUPS = 4
POOL_HALF = (1, 2, 4, 8)
N_HEADS = 8
HEAD_DIM = 64
GRID_W = 64
WIN_H = 8
WIN_W = 16
CONV_C = 3
CONV_D = 31
EPS = 1e-6
MASKED = -1e30
LOG2_E = 1.4426950408889634
Q_SCALE = HEAD_DIM ** -0.5 * LOG2_E

LANES = 128
SUBLANES = 8
PAD = 16
ROW_CHUNK = 512
NORM_ROWS = 32
POOL_ROWS = 256
CONV_ROWS = 128
Q_ROWS = 128
NB_PROMPT = 2
NA_GROUP = 8
MOD_COLS = 1536
STAGE_ROWS = 128
STAGE_SLOTS = 4
VMEM_LIMIT = 58 * 1024 * 1024

assert PAD >= CONV_D // 2 + 1 and PAD % SUBLANES == 0 and PAD >= 2 * SUBLANES
assert max(POOL_HALF) <= SUBLANES


def _sigmoid(x):
    return 1.0 / (1.0 + jnp.exp(-x))


def _silu(x):
    return x * _sigmoid(x)


def _dot(a, b):
    return jnp.dot(a, b, preferred_element_type=F32)


def _dot_nt(a, b):
    return lax.dot_general(a, b, (((1,), (1,)), ((), ())), preferred_element_type=F32)


def _lanes(j):
    return slice(j * LANES, (j + 1) * LANES)


def _group(g):
    return slice(g * W_HALF, (g + 1) * W_HALF)


def _rows(start, size, align):
    if isinstance(start, int):
        return slice(start, start + size)
    return pl.ds(pl.multiple_of(start, align), size)


def _mod_body(cctx_ref, c_ref, w_ref, b_ref, o_ref):
    rows, d = o_ref.shape[1], cctx_ref.shape[1]
    r = lax.broadcasted_iota(jnp.int32, (rows, d), 0)
    cond = jnp.where(r == 0, cctx_ref[...], 0.0)
    for i in range(c_ref.shape[0]):
        cond = jnp.where(r == i + 1, c_ref[i:i + 1, :], cond)
    bias = jnp.where(pl.program_id(0) == 0, b_ref[0:1, :], b_ref[1:2, :])
    o_ref[0] = _dot(_silu(cond).astype(BF16), w_ref[0].astype(BF16)) + bias


def _modulation(c_ctx, c, w_mod, b_mod, rows):
    depth, d, n = w_mod.shape
    assert depth == 2 and 1 + c.shape[0] <= rows
    return pl.pallas_call(
        _mod_body,
        out_shape=jax.ShapeDtypeStruct((depth, rows, n), F32),
        grid=(depth, n // MOD_COLS),
        in_specs=[
            pl.BlockSpec((1, d), lambda l, j: (0, 0)),
            pl.BlockSpec(c.shape, lambda l, j: (0, 0)),
            pl.BlockSpec((1, d, MOD_COLS), lambda l, j: (l, 0, j)),
            pl.BlockSpec((depth, MOD_COLS), lambda l, j: (0, j)),
        ],
        out_specs=pl.BlockSpec((1, rows, MOD_COLS), lambda l, j: (l, 0, j)),
        compiler_params=pltpu.CompilerParams(dimension_semantics=("arbitrary", "arbitrary")),
        name="mod",
    )(c_ctx.reshape(1, d), c, w_mod, b_mod)


def _cond_row(m_ref, layer, row):
    if isinstance(row, int):
        return m_ref[layer, row:row + 1, :]
    m = m_ref[layer]
    keep = lax.broadcasted_iota(jnp.int32, m.shape, 0) == row
    return jnp.sum(jnp.where(keep, m, 0.0), axis=0, keepdims=True)


def _pieces(c, nb, t):
    if t >= ROW_CHUNK:
        per_seq = t // ROW_CHUNK
        s = 0 if nb == 1 else c // per_seq
        return [(s, (c - s * per_seq) * ROW_CHUNK, ROW_CHUNK, 0)]
    per_chunk = ROW_CHUNK // t
    return [(c * per_chunk + i, 0, t, i * t) for i in range(per_chunk)]


def _for_chunks(n, body, unrolled=False):
    if unrolled or n == 1:
        for c in range(n):
            body(c)
    else:
        lax.fori_loop(0, n, lambda c, carry: (body(c), carry)[1], 0)


def _pad_row(s, off, t):
    return s * (t + 2 * PAD) + PAD + off


def _store_padded(pad_ref, val, pieces, t):
    for s, off, n, o in pieces:
        pad_ref[_rows(_pad_row(s, off, t), n, SUBLANES), :] = val[o:o + n]


def _scale_padded(pad_ref, val, pieces, t):
    for s, off, n, o in pieces:
        rows = _rows(_pad_row(s, off, t), n, SUBLANES)
        pad_ref[rows, :] = pad_ref[rows, :] * val[o:o + n]


def _modnorm_chunk(src_ref, h_ref, c, nb, t, gain, shift):
    for s, off, n, o in _pieces(c, nb, t):
        for i in range(0, n, NORM_ROWS):
            x = src_ref[s, _rows(off + i, NORM_ROWS, NORM_ROWS), :]
            ms = jnp.mean(x * x, axis=-1, keepdims=True)
            h_ref[_rows(c * ROW_CHUNK + o + i, NORM_ROWS, NORM_ROWS), :] = (
                x * lax.rsqrt(ms + EPS) * gain + shift).astype(BF16)


def _zero_pads(pad_ref, nb, t):
    z = jnp.zeros((PAD, W_HALF), F32)
    for s in range(nb):
        pad_ref[_pad_row(s, 0, t) - PAD:_pad_row(s, 0, t), :] = z
        pad_ref[_pad_row(s, t, t):_pad_row(s, t, t) + PAD, :] = z


def _pool_phase(pad_ref, ga_ref, wp_ref, ps_ref, ab_ref, nb, t):
    n_rows = POOL_ROWS
    per_seq = t // n_rows

    def step(i, carry):
        s = i // per_seq
        r0 = (i - s * per_seq) * n_rows
        prow = _pad_row(s, r0, t)
        rows = _rows(i * n_rows, n_rows, n_rows)
        pos = r0 + lax.broadcasted_iota(jnp.int32, (n_rows, LANES), 0)
        for g in range(N_POOL_GROUPS):
            hw = POOL_HALF[g]
            ln = _lanes(g)
            halo = n_rows + 2 * SUBLANES
            blk = pad_ref[_rows(prow - SUBLANES, halo, SUBLANES), ln]
            run, n = blk, 1
            while n < 2 * hw:
                run = run + pltpu.roll(run, halo - n, 0)
                n *= 2
            if hw < SUBLANES:
                run = pltpu.roll(run, halo - (SUBLANES - hw), 0)
            win = run[:n_rows]
            cnt = (jnp.minimum(pos + hw, t) - jnp.maximum(pos - hw, 0)).astype(F32)
            p = (win / cnt - blk[SUBLANES:SUBLANES + n_rows]).astype(BF16)
            y = _dot(p, wp_ref[g]) * ps_ref[:, ln] * ga_ref[rows, ln]
            ab_ref[rows, ln] = y.astype(BF16)
        return carry
    lax.fori_loop(0, nb * per_seq, step, 0)


def _out_proj_chunk(ab_ref, w_ref, x_ref, gate, dst_ref, c, nb, t):
    lhs = ab_ref[_rows(c * ROW_CHUNK, ROW_CHUNK, ROW_CHUNK), :]
    for g in range(D_MODEL // W_HALF):
        y = _dot(lhs, w_ref[:, _group(g)])
        for s, off, n, o in _pieces(c, nb, t):
            rows = _rows(off, n, n)
            dst_ref[s, rows, _group(g)] = x_ref[s, rows, _group(g)] + gate[:, _group(g)] * y[o:o + n]


def _conv_phase(pad_c, pad_d, bc_ref, ga_ref, gb_ref, cc_ref, cdw_ref, cdb_ref, lng_ref, lnb_ref,
                ab_ref, nb, t):
    n_rows = CONV_ROWS
    per_seq = t // n_rows

    def step(i, carry):
        s = i // per_seq
        r0 = (i - s * per_seq) * n_rows
        prow = _pad_row(s, r0, t)
        rows = _rows(i * n_rows, n_rows, n_rows)
        z = []
        for g in range(W_HALF // LANES):
            ln = _lanes(g)
            blk = pad_c[_rows(prow - SUBLANES, n_rows + 2 * SUBLANES, SUBLANES), ln]
            c3 = None
            for j in range(CONV_C):
                o = SUBLANES + j - CONV_C // 2
                term = blk[o:o + n_rows] * cc_ref[j:j + 1, ln]
                c3 = term if c3 is None else c3 + term
            ab_ref[rows, ln] = (bc_ref[rows, ln] * c3 * ga_ref[rows, ln]).astype(BF16)
            acc = None
            for sft in range(SUBLANES):
                part = None
                for a in range((CONV_D - sft + SUBLANES - 1) // SUBLANES):
                    j = SUBLANES * a + sft
                    src = pad_d[_rows(prow - 2 * SUBLANES + SUBLANES * a, n_rows + SUBLANES,
                                      SUBLANES), ln]
                    term = src * cdw_ref[j:j + 1, ln]
                    part = term if part is None else part + term
                o = SUBLANES + sft - (CONV_D // 2 - SUBLANES)
                part = part[o:o + n_rows]
                acc = part if acc is None else acc + part
            z.append(acc + cdb_ref[:, ln])
        z = jnp.concatenate(z, axis=-1)
        mu = jnp.mean(z, axis=-1, keepdims=True)
        zc = z - mu
        var = jnp.mean(zc * zc, axis=-1, keepdims=True)
        zn = zc * lax.rsqrt(var + EPS) * lng_ref[...] + lnb_ref[...]
        ab_ref[rows, W_HALF:] = (_silu(zn) * gb_ref[rows, :]).astype(BF16)
        return carry
    lax.fori_loop(0, nb * per_seq, step, 0)


def _final_norm_chunk(y_ref, fg, c, nb, t):
    for s, off, n, _ in _pieces(c, nb, t):
        for i in range(0, n, NORM_ROWS):
            rows = _rows(off + i, NORM_ROWS, NORM_ROWS)
            x = y_ref[s, rows, :]
            ms = jnp.mean(x * x, axis=-1, keepdims=True)
            y_ref[s, rows, :] = x * lax.rsqrt(ms + EPS) * fg


def _odd_layer(y_ref, m_row, g_row, fg, wio_ref, cc_ref, cdw_ref, cdb_ref, lng_ref, lnb_ref, woo_ref,
               h_ref, pad_c, pad_d, bc_ref, ga_ref, gb_ref, ab_ref, nb, t):
    shift = m_row[:, :D_MODEL]
    gain = g_row * (1.0 + m_row[:, D_MODEL:2 * D_MODEL])
    gate = m_row[:, 2 * D_MODEL:]
    n_chunks = nb * t // ROW_CHUNK

    def in_proj(c):
        _modnorm_chunk(y_ref, h_ref, c, nb, t, gain, shift)
        rows = _rows(c * ROW_CHUNK, ROW_CHUNK, ROW_CHUNK)
        pieces = _pieces(c, nb, t)
        h = h_ref[rows, :]
        bc_ref[rows, :] = _dot(h, wio_ref[:, _group(0)])
        _store_padded(pad_c, _dot(h, wio_ref[:, _group(1)]), pieces, t)
        _scale_padded(pad_c, _dot(h, wio_ref[:, _group(2)]), pieces, t)
        ga_ref[rows, :] = _silu(_dot(h, wio_ref[:, _group(3)]))
        _store_padded(pad_d, _dot(h, wio_ref[:, _group(4)]), pieces, t)
        _scale_padded(pad_d, _sigmoid(_dot(h, wio_ref[:, _group(5)])), pieces, t)
        gb_ref[rows, :] = _silu(_dot(h, wio_ref[:, _group(6)]))
    _for_chunks(n_chunks, in_proj)

    _conv_phase(pad_c, pad_d, bc_ref, ga_ref, gb_ref, cc_ref, cdw_ref, cdb_ref, lng_ref, lnb_ref,
                ab_ref, nb, t)

    def out_proj(c):
        _out_proj_chunk(ab_ref, woo_ref, y_ref, gate, y_ref, c, nb, t)
        _final_norm_chunk(y_ref, fg, c, nb, t)
    _for_chunks(n_chunks, out_proj, unrolled=True)


def _even_in_proj(x_ref, m_row, g_row, w_ref, h_ref, pad_a, ga_ref, gb_ref, q_ref, k_ref, v_ref,
                  kv_t, nb, t):
    shift = m_row[:, :D_MODEL]
    gain = g_row * (1.0 + m_row[:, D_MODEL:2 * D_MODEL])

    def in_proj(c):
        _modnorm_chunk(x_ref, h_ref, c, nb, t, gain, shift)
        rows = _rows(c * ROW_CHUNK, ROW_CHUNK, ROW_CHUNK)
        pieces = _pieces(c, nb, t)
        h = h_ref[rows, :]
        _store_padded(pad_a, _dot(h, w_ref[:, _group(0)]), pieces, t)
        ga_ref[rows, :] = _silu(_dot(h, w_ref[:, _group(1)]))
        q_ref[rows, :] = (_dot(h, w_ref[:, _group(2)]) * Q_SCALE).astype(BF16)
        for i, (dst, g) in enumerate(((k_ref, 3), (v_ref, 4))):
            if kv_t is None:
                dst[rows, :] = _dot(h, w_ref[:, _group(g)]).astype(BF16)
                continue
            acc = _dot_nt(kv_t[2 + i][...], h)
            dst[:, rows] = acc.astype(BF16)
            for s, off, n, o in pieces:
                for hd in range(N_HEADS):
                    kv_t[i][s, 0, hd, :, _rows(off, n, n)] = (
                        acc[hd * HEAD_DIM:(hd + 1) * HEAD_DIM, o:o + n])
        gb_ref[rows, :] = _silu(_dot(h, w_ref[:, _group(5)]))
    _for_chunks(nb * t // ROW_CHUNK, in_proj)


def _even_out_proj(x_ref, y_ref, m_row, w_ref, ab_ref, nb, t):
    gate = m_row[:, 2 * D_MODEL:]
    _for_chunks(nb * t // ROW_CHUNK,
                lambda c: _out_proj_chunk(ab_ref, w_ref, x_ref, gate, y_ref, c, nb, t))


def _split_heads(x):
    lane = lax.broadcasted_iota(jnp.int32, (1, LANES), 1)
    first = jnp.where(lane < HEAD_DIM, 1.0, 0.0).astype(x.dtype)
    return jnp.concatenate([x * first, x * (1 - first)], axis=0)


def _merge_heads(o):
    n = o.shape[0] // 2
    lane = lax.broadcasted_iota(jnp.int32, (n, LANES), 1)
    return jnp.where(lane < HEAD_DIM, o[:n], o[n:])


def _context_attention(q_ref, kt_ref, vt_ref, gb_ref, ab_ref, nb, t):
    for s in range(nb):
        seq = slice(s * t, (s + 1) * t)
        for j in range(N_HEADS // 2):
            ln = _lanes(j)
            kp = kt_ref[ln, seq]
            vp = vt_ref[ln, seq]
            for r0 in range(0, t, Q_ROWS):
                rows = slice(s * t + r0, s * t + r0 + Q_ROWS)
                sc = _dot(_split_heads(q_ref[rows, ln]), kp)
                p = jnp.exp2(sc - jnp.max(sc, axis=-1, keepdims=True))
                o = _dot_nt(p.astype(BF16), vp) / jnp.sum(p, axis=-1, keepdims=True)
                ab_ref[rows, W_HALF + j * LANES:W_HALF + (j + 1) * LANES] = (
                    _merge_heads(o) * gb_ref[rows, ln]).astype(BF16)


def _stage_weights(w_hbm, w_bf, write_back, wkt_ref, wvt_ref, stage, sem_in):
    chunks = [(k, r0) for k in range(len(w_hbm)) for r0 in range(0, w_hbm[k].shape[1], STAGE_ROWS)]

    def fetch(i):
        k, r0 = chunks[i]
        cols = w_hbm[k].shape[2]
        slot = i % STAGE_SLOTS
        return pltpu.make_async_copy(w_hbm[k].at[0, pl.ds(r0, STAGE_ROWS), :],
                                     stage.at[slot, :, pl.ds(0, cols)], sem_in.at[slot])

    for i in range(min(STAGE_SLOTS - 1, len(chunks))):
        fetch(i).start()
    for i, (k, r0) in enumerate(chunks):
        if i + STAGE_SLOTS - 1 < len(chunks):
            fetch(i + STAGE_SLOTS - 1).start()
        fetch(i).wait()
        cols = w_hbm[k].shape[2]
        rows = slice(r0, r0 + STAGE_ROWS)
        slot = i % STAGE_SLOTS
        w_bf[k][rows, :] = stage[slot, :, 0:cols].astype(BF16)
        if k == 0:
            wkt_ref[:, rows] = stage[slot, :, _group(3)].T.astype(BF16)
            wvt_ref[:, rows] = stage[slot, :, _group(4)].T.astype(BF16)
        if i + 1 == len(chunks) or chunks[i + 1][0] != k:
            write_back(k).start()


def _prompt_body(x_ref, m_ref, ng_ref, fg_ref, wp_ref, ps_ref, cc_ref, cdw_ref, cdb_ref, lng_ref,
                 lnb_ref, wie_hbm, woe_hbm, wio_hbm, woo_hbm,
                 y_ref, ko_ref, vo_ref, wie_out, woe_out, wio_out, woo_out,
                 h_ref, pad_a, pad_b, ga_ref, gb_ref, bc_ref, q_ref, kt_ref, vt_ref, ab_ref,
                 wie_ref, woe_ref, wio_ref, woo_ref, wkt_ref, wvt_ref, stage, sem_in, sem_out,
                 *, nb, t):
    w_out = (wie_out, woe_out, wio_out, woo_out)
    w_bf = (wie_ref, woe_ref, wio_ref, woo_ref)

    def write_back(k):
        return pltpu.make_async_copy(w_bf[k], w_out[k], sem_out.at[k])

    @pl.when(pl.program_id(0) == 0)
    def _():
        _stage_weights((wie_hbm, woe_hbm, wio_hbm, woo_hbm), w_bf, write_back, wkt_ref, wvt_ref,
                       stage, sem_in)

    _zero_pads(pad_a, nb, t)
    _zero_pads(pad_b, nb, t)
    m_even = _cond_row(m_ref, 0, 0)
    _even_in_proj(x_ref, m_even, ng_ref[0:1, :], wie_ref, h_ref, pad_a, ga_ref, gb_ref,
                  q_ref, kt_ref, vt_ref, (ko_ref, vo_ref, wkt_ref, wvt_ref), nb, t)
    _pool_phase(pad_a, ga_ref, wp_ref, ps_ref, ab_ref, nb, t)
    _context_attention(q_ref, kt_ref, vt_ref, gb_ref, ab_ref, nb, t)
    _even_out_proj(x_ref, y_ref, m_even, woe_ref, ab_ref, nb, t)
    _odd_layer(y_ref, _cond_row(m_ref, 1, 0), ng_ref[1:2, :], fg_ref[...], wio_ref, cc_ref, cdw_ref,
               cdb_ref, lng_ref, lnb_ref, woo_ref, h_ref, pad_a, pad_b, bc_ref, ga_ref, gb_ref, ab_ref,
               nb, t)

    @pl.when(pl.program_id(0) == 0)
    def _():
        for k in range(len(w_bf)):
            write_back(k).wait()


def _rpb_rows(rpb_ref, e_ref):
    n = rpb_ref.shape[0]
    lane = lax.broadcasted_iota(jnp.int32, (n, LANES), 1)
    i = jnp.where(lane < GRID_W, lane, lane - LANES)
    idx = jnp.clip(i, -(WIN_W - 1), WIN_W - 1) + (WIN_W - 1)
    rp = rpb_ref[...]
    e = jnp.zeros((n, LANES), F32)
    for d in range(2 * WIN_W - 1):
        e = jnp.where(idx == d, rp[:, d:d + 1], e)
    e_ref[...] = e


N_DR = 2 * WIN_H - 1
PAIR_TILES = N_DR // 2


def _bias_tile_index(j, dr_lo):
    if isinstance(dr_lo, int):
        parity, half = dr_lo % 2, dr_lo // 2
    else:
        parity, half = dr_lo & 1, lax.shift_right_logical(dr_lo, 1)
    return (2 * j + parity) * PAIR_TILES + half


def _bias_tables(e_ref, bias_ref):
    q = lax.broadcasted_iota(jnp.int32, (GRID_W, LANES), 0)
    lane = lax.broadcasted_iota(jnp.int32, (GRID_W, LANES), 1)
    kw = jnp.where(lane < GRID_W, lane, lane - GRID_W)
    start = jnp.clip(q - WIN_W // 2, 0, GRID_W - WIN_W)
    col_ok = (kw >= start) & (kw < start + WIN_W)
    for j in range(N_HEADS // 2):
        for dr in range(N_DR - 1):
            for e in range(2):
                r_lo = (2 * j + e) * N_DR + dr
                lo = jnp.broadcast_to(e_ref[r_lo:r_lo + 1, :], (GRID_W, LANES))
                hi = jnp.broadcast_to(e_ref[r_lo + 1:r_lo + 2, :], (GRID_W, LANES))
                lo = pltpu.roll(lo, 0, 1, stride=1, stride_axis=0)
                hi = pltpu.roll(hi, GRID_W, 1, stride=1, stride_axis=0)
                tile = jnp.where(lane < GRID_W, lo, hi)
                bias_ref[_bias_tile_index(j, dr), e * GRID_W:(e + 1) * GRID_W, :] = jnp.where(
                    col_ok, tile * LOG2_E, MASKED)


def _neighbourhood_attention(q_ref, k_ref, v_ref, ck_ref, cv_ref, bias_ref, kvc_ref, gb_ref, ab_ref, t):
    grid_h = t // GRID_W
    band = WIN_H * GRID_W
    for j in range(N_HEADS // 2):
        ln = _lanes(j)
        for i, src in enumerate((ck_ref, cv_ref)):
            kvc_ref[i] = jnp.concatenate([src[0, 0, 2 * j], src[0, 0, 2 * j + 1]],
                                         axis=0).astype(BF16)

        def per_group(g, carry, ln=ln, j=j):
            scored = []
            for u in range(NA_GROUP):
                r = g * NA_GROUP + u
                start = jnp.clip(r - WIN_H // 2, 0, grid_h - WIN_H)
                rows = _rows(r * GRID_W, GRID_W, GRID_W)
                keys = _rows(start * GRID_W, band, GRID_W)
                q2 = _split_heads(q_ref[rows, ln])
                dr0 = (WIN_H - 1) - (r - start)
                bias = jnp.concatenate([bias_ref[_bias_tile_index(j, dr0 + 2 * i)]
                                        for i in range(WIN_H // 2)], axis=-1)
                scored.append((rows, keys, _dot_nt(q2, k_ref[keys, ln]) + bias, _dot(q2, kvc_ref[0])))
            weighted = []
            for rows, keys, s_loc, s_ctx in scored:
                mx = jnp.maximum(jnp.max(s_loc, axis=-1, keepdims=True),
                                 jnp.max(s_ctx, axis=-1, keepdims=True))
                p_loc = jnp.exp2(s_loc - mx)
                p_ctx = jnp.exp2(s_ctx - mx)
                den = (jnp.sum(p_loc, axis=-1, keepdims=True)
                       + jnp.sum(p_ctx, axis=-1, keepdims=True))
                weighted.append((rows, keys, p_loc.astype(BF16), p_ctx.astype(BF16), den))
            for rows, keys, p_loc, p_ctx, den in weighted:
                o = (_dot(p_loc, v_ref[keys, ln]) + _dot_nt(p_ctx, kvc_ref[1])) / den
                ab_ref[rows, W_HALF + j * LANES:W_HALF + (j + 1) * LANES] = (
                    _merge_heads(o) * gb_ref[rows, ln]).astype(BF16)
            return carry
        lax.fori_loop(0, grid_h // NA_GROUP, per_group, 0)


def _sample_body(x_ref, m_ref, ng_ref, fg_ref, wie_ref, wp_ref, ps_ref, woe_ref, wio_ref, cc_ref,
                 cdw_ref, cdb_ref, lng_ref, lnb_ref, woo_ref, ck_ref, cv_ref, rpb_ref,
                 y_ref,
                 h_ref, pad_a, pad_b, ga_ref, gb_ref, bc_ref, q_ref, k_ref, v_ref, ab_ref,
                 e_ref, bias_ref, kvc_ref, *, t):
    _zero_pads(pad_a, 1, t)
    _zero_pads(pad_b, 1, t)

    @pl.when(pl.program_id(0) == 0)
    def _():
        _rpb_rows(rpb_ref, e_ref)
        _bias_tables(e_ref, bias_ref)

    cond = pl.program_id(0) + 1
    m_even = _cond_row(m_ref, 0, cond)
    _even_in_proj(x_ref, m_even, ng_ref[0:1, :], wie_ref, h_ref, pad_a, ga_ref, gb_ref,
                  q_ref, k_ref, v_ref, None, 1, t)
    _pool_phase(pad_a, ga_ref, wp_ref, ps_ref, ab_ref, 1, t)
    _neighbourhood_attention(q_ref, k_ref, v_ref, ck_ref, cv_ref, bias_ref, kvc_ref, gb_ref, ab_ref, t)
    _even_out_proj(x_ref, y_ref, m_even, woe_ref, ab_ref, 1, t)
    _odd_layer(y_ref, _cond_row(m_ref, 1, cond), ng_ref[1:2, :], fg_ref[...], wio_ref, cc_ref, cdw_ref,
               cdb_ref, lng_ref, lnb_ref, woo_ref, h_ref, pad_a, pad_b, bc_ref, ga_ref, gb_ref, ab_ref,
               1, t)


def _const_spec(shape):
    zeros = (0,) * len(shape)
    return pl.BlockSpec(shape, lambda i: zeros, pipeline_mode=pl.Buffered(1))


def _stream_scratch(nb, t, kv_transposed):
    r = nb * t
    padded = nb * (t + 2 * PAD)
    kv = (W_HALF, r) if kv_transposed else (r, W_HALF)
    return [
        pltpu.VMEM((r, D_MODEL), BF16),
        pltpu.VMEM((padded, W_HALF), F32),
        pltpu.VMEM((padded, W_HALF), F32),
        pltpu.VMEM((r, W_HALF), F32),
        pltpu.VMEM((r, W_HALF), F32),
        pltpu.VMEM((r, W_HALF), F32),
        pltpu.VMEM((r, W_HALF), BF16),
        pltpu.VMEM(kv, BF16),
        pltpu.VMEM(kv, BF16),
        pltpu.VMEM((r, D_MODEL), BF16),
    ]


def _small_params(norm_g, final_g, w_pool, pool_scale, conv_c, conv_d, conv_d_b, ln_g, ln_b):
    return [norm_g, final_g.reshape(1, D_MODEL), w_pool[0].astype(BF16), pool_scale, conv_c[0],
            conv_d[0], conv_d_b, ln_g, ln_b]


def kernel(x_prompt, x_sample, cache_k, cache_v, c, c_ctx, norm_g, w_mod, b_mod, w_in_even, w_pool,
           pool_scale, rpb, w_out_even, w_in_odd, conv_c, conv_d, conv_d_b, ln_g, ln_b, w_out_odd,
           final_g):
    batch, seq, d = x_prompt.shape
    dec_batch, dec_seq, _ = x_sample.shape
    assert d == D_MODEL and w_mod.shape[0] == 2 and w_in_even.shape[0] == 1 and w_in_odd.shape[0] == 1
    assert (NB_PROMPT * seq) % ROW_CHUNK == 0 and ROW_CHUNK % seq == 0 and seq % Q_ROWS == 0
    assert dec_seq % ROW_CHUNK == 0 and dec_seq // GRID_W >= WIN_H
    assert seq % POOL_ROWS == 0 and seq % CONV_ROWS == 0
    assert dec_seq % POOL_ROWS == 0 and dec_seq % CONV_ROWS == 0
    assert (dec_seq // GRID_W) % NA_GROUP == 0

    cond_rows = SUBLANES * ((1 + dec_batch + SUBLANES - 1) // SUBLANES)
    m = _modulation(c_ctx, c, w_mod, b_mod, cond_rows)
    m_spec = _const_spec(m.shape)

    small = _small_params(norm_g, final_g, w_pool, pool_scale, conv_c, conv_d, conv_d_b, ln_g, ln_b)
    small_specs = [_const_spec(a.shape) for a in small]
    w_f32 = (w_in_even, w_out_even, w_in_odd, w_out_odd)
    assert all(w.shape[0] == 1 and w.shape[1] % STAGE_ROWS == 0 for w in w_f32)
    any_spec = pl.BlockSpec(memory_space=pl.ANY)

    nb = NB_PROMPT
    assert batch % nb == 0
    kv_shape = jax.ShapeDtypeStruct((batch, 1, N_HEADS, HEAD_DIM, seq), F32)
    kv_spec = pl.BlockSpec((nb, 1, N_HEADS, HEAD_DIM, seq), lambda i: (i, 0, 0, 0, 0))
    y_prompt, new_kt, new_vt, wie, woe, wio, woo = pl.pallas_call(
        functools.partial(_prompt_body, nb=nb, t=seq),
        out_shape=(jax.ShapeDtypeStruct(x_prompt.shape, F32), kv_shape, kv_shape)
                  + tuple(jax.ShapeDtypeStruct(w.shape[1:], BF16) for w in w_f32),
        grid=(batch // nb,),
        in_specs=[pl.BlockSpec((nb, seq, d), lambda i: (i, 0, 0)), m_spec] + small_specs
                 + [any_spec] * len(w_f32),
        out_specs=(pl.BlockSpec((nb, seq, d), lambda i: (i, 0, 0)), kv_spec, kv_spec)
                  + (any_spec,) * len(w_f32),
        scratch_shapes=_stream_scratch(nb, seq, True)
                       + [pltpu.VMEM(w.shape[1:], BF16) for w in w_f32] + [
            pltpu.VMEM((W_HALF, d), BF16),
            pltpu.VMEM((W_HALF, d), BF16),
            pltpu.VMEM((STAGE_SLOTS, STAGE_ROWS, max(w.shape[2] for w in w_f32)), F32),
            pltpu.SemaphoreType.DMA((STAGE_SLOTS,)),
            pltpu.SemaphoreType.DMA((len(w_f32),)),
        ],
        compiler_params=pltpu.CompilerParams(dimension_semantics=("arbitrary",),
                                             vmem_limit_bytes=VMEM_LIMIT),
        name="prompt",
    )(x_prompt, m, *small, *w_f32)
    ng, fg, wp, ps, cc, cdw, cdb, lng, lnb = small
    w_args = [ng, fg, wie, wp, ps, woe, wio, cc, cdw, cdb, lng, lnb, woo]
    w_specs = [_const_spec(a.shape) for a in w_args]

    past = cache_k.shape[3]
    cache_spec = pl.BlockSpec((1, 1, N_HEADS, HEAD_DIM, past), lambda i: (i, 0, 0, 0, 0))
    rpb2 = rpb[0].reshape(N_HEADS * (2 * WIN_H - 1), 2 * WIN_W - 1)
    y_sample = pl.pallas_call(
        functools.partial(_sample_body, t=dec_seq),
        out_shape=jax.ShapeDtypeStruct(x_sample.shape, F32),
        grid=(dec_batch,),
        in_specs=[pl.BlockSpec((1, dec_seq, d), lambda i: (i, 0, 0), pipeline_mode=pl.Buffered(1)),
                  m_spec] + w_specs
                 + [cache_spec, cache_spec, _const_spec(rpb2.shape)],
        out_specs=pl.BlockSpec((1, dec_seq, d), lambda i: (i, 0, 0)),
        scratch_shapes=_stream_scratch(1, dec_seq, False) + [
            pltpu.VMEM(rpb2.shape[:1] + (LANES,), F32),
            pltpu.VMEM((N_HEADS * PAIR_TILES, 2 * GRID_W, LANES), F32),
            pltpu.VMEM((2, LANES, past), BF16),
        ],
        compiler_params=pltpu.CompilerParams(dimension_semantics=("arbitrary",),
                                             vmem_limit_bytes=VMEM_LIMIT),
        name="sample",
    )(x_sample, m, *w_args, jnp.swapaxes(cache_k, 3, 4), jnp.swapaxes(cache_v, 3, 4), rpb2)

    return (y_prompt, y_sample, jnp.swapaxes(new_kt, 3, 4), jnp.swapaxes(new_vt, 3, 4))
```

```python
import functools

import jax
import jax.numpy as jnp
from jax import lax
from jax.experimental import pallas as pl
from jax.experimental.pallas import tpu as pltpu

F32 = jnp.float32
BF16 = jnp.bfloat16

D_MODEL = 1024
W_HALF = 512
N_POOL_GROUPS = 4
POOL_HALF = (1, 2, 4, 8)
N_HEADS = 8
HEAD_DIM = 64
GRID_W = 64
WIN_H = 8
WIN_W = 16
CONV_C = 3
CONV_D = 31
EPS = 1e-6
MASKED = -1e30
LOG2_E = 1.4426950408889634
Q_SCALE = HEAD_DIM ** -0.5 * LOG2_E

LANES = 128
SUBLANES = 8
PAD = 16
ROW_CHUNK = 512
NORM_ROWS = 32
POOL_ROWS = 256
CONV_ROWS = 128
Q_ROWS = 128
NB_PROMPT = 2
NA_GROUP = 8
MOD_COLS = 1536
STAGE_ROWS = 128
STAGE_SLOTS = 4
VMEM_LIMIT = 58 * 1024 * 1024

assert PAD >= CONV_D // 2 + 1 and PAD % SUBLANES == 0 and PAD >= 2 * SUBLANES
assert max(POOL_HALF) <= SUBLANES


def _sigmoid(x):
    return 1.0 / (1.0 + jnp.exp(-x))


def _silu(x):
    return x * _sigmoid(x)


def _dot(a, b):
    return jnp.dot(a, b, preferred_element_type=F32)


def _dot_nt(a, b):
    return lax.dot_general(a, b, (((1,), (1,)), ((), ())), preferred_element_type=F32)


def _lanes(j):
    return slice(j * LANES, (j + 1) * LANES)


def _group(g):
    return slice(g * W_HALF, (g + 1) * W_HALF)


def _rows(start, size, align):
    if isinstance(start, int):
        return slice(start, start + size)
    return pl.ds(pl.multiple_of(start, align), size)


def _mod_body(cctx_ref, c_ref, w_ref, b_ref, o_ref):
    rows, d = o_ref.shape[1], cctx_ref.shape[1]
    r = lax.broadcasted_iota(jnp.int32, (rows, d), 0)
    cond = jnp.where(r == 0, cctx_ref[...], 0.0)
    for i in range(c_ref.shape[0]):
        cond = jnp.where(r == i + 1, c_ref[i:i + 1, :], cond)
    bias = jnp.where(pl.program_id(0) == 0, b_ref[0:1, :], b_ref[1:2, :])
    o_ref[0] = _dot(_silu(cond).astype(BF16), w_ref[0].astype(BF16)) + bias


def _modulation(c_ctx, c, w_mod, b_mod, rows):
    depth, d, n = w_mod.shape
    assert depth == 2 and 1 + c.shape[0] <= rows
    return pl.pallas_call(
        _mod_body,
        out_shape=jax.ShapeDtypeStruct((depth, rows, n), F32),
        grid=(depth, n // MOD_COLS),
        in_specs=[
            pl.BlockSpec((1, d), lambda l, j: (0, 0)),
            pl.BlockSpec(c.shape, lambda l, j: (0, 0)),
            pl.BlockSpec((1, d, MOD_COLS), lambda l, j: (l, 0, j)),
            pl.BlockSpec((depth, MOD_COLS), lambda l, j: (0, j)),
        ],
        out_specs=pl.BlockSpec((1, rows, MOD_COLS), lambda l, j: (l, 0, j)),
        compiler_params=pltpu.CompilerParams(dimension_semantics=("arbitrary", "arbitrary")),
        name="mod",
    )(c_ctx.reshape(1, d), c, w_mod, b_mod)


def _cond_row(m_ref, layer, row):
    if isinstance(row, int):
        return m_ref[layer, row:row + 1, :]
    m = m_ref[layer]
    keep = lax.broadcasted_iota(jnp.int32, m.shape, 0) == row
    return jnp.sum(jnp.where(keep, m, 0.0), axis=0, keepdims=True)


def _pieces(c, nb, t):
    if t >= ROW_CHUNK:
        per_seq = t // ROW_CHUNK
        s = 0 if nb == 1 else c // per_seq
        return [(s, (c - s * per_seq) * ROW_CHUNK, ROW_CHUNK, 0)]
    per_chunk = ROW_CHUNK // t
    return [(c * per_chunk + i, 0, t, i * t) for i in range(per_chunk)]


def _for_chunks(n, body, unrolled=False):
    if unrolled or n == 1:
        for c in range(n):
            body(c)
    else:
        lax.fori_loop(0, n, lambda c, carry: (body(c), carry)[1], 0)


def _pad_row(s, off, t):
    return s * (t + 2 * PAD) + PAD + off


def _store_padded(pad_ref, val, pieces, t):
    for s, off, n, o in pieces:
        pad_ref[_rows(_pad_row(s, off, t), n, SUBLANES), :] = val[o:o + n]


def _scale_padded(pad_ref, val, pieces, t):
    for s, off, n, o in pieces:
        rows = _rows(_pad_row(s, off, t), n, SUBLANES)
        pad_ref[rows, :] = pad_ref[rows, :] * val[o:o + n]


def _modnorm_chunk(src_ref, h_ref, c, nb, t, gain, shift):
    for s, off, n, o in _pieces(c, nb, t):
        for i in range(0, n, NORM_ROWS):
            x = src_ref[s, _rows(off + i, NORM_ROWS, NORM_ROWS), :]
            ms = jnp.mean(x * x, axis=-1, keepdims=True)
            h_ref[_rows(c * ROW_CHUNK + o + i, NORM_ROWS, NORM_ROWS), :] = (
                x * lax.rsqrt(ms + EPS) * gain + shift).astype(BF16)


def _zero_pads(pad_ref, nb, t):
    z = jnp.zeros((PAD, W_HALF), F32)
    for s in range(nb):
        pad_ref[_pad_row(s, 0, t) - PAD:_pad_row(s, 0, t), :] = z
        pad_ref[_pad_row(s, t, t):_pad_row(s, t, t) + PAD, :] = z


def _pool_phase(pad_ref, ga_ref, wp_ref, ps_ref, ab_ref, nb, t):
    n_rows = POOL_ROWS
    per_seq = t // n_rows

    def step(i, carry):
        s = i // per_seq
        r0 = (i - s * per_seq) * n_rows
        prow = _pad_row(s, r0, t)
        rows = _rows(i * n_rows, n_rows, n_rows)
        pos = r0 + lax.broadcasted_iota(jnp.int32, (n_rows, LANES), 0)
        for g in range(N_POOL_GROUPS):
            hw = POOL_HALF[g]
            ln = _lanes(g)
            halo = n_rows + 2 * SUBLANES
            blk = pad_ref[_rows(prow - SUBLANES, halo, SUBLANES), ln]
            run, n = blk, 1
            while n < 2 * hw:
                run = run + pltpu.roll(run, halo - n, 0)
                n *= 2
            if hw < SUBLANES:
                run = pltpu.roll(run, halo - (SUBLANES - hw), 0)
            win = run[:n_rows]
            cnt = (jnp.minimum(pos + hw, t) - jnp.maximum(pos - hw, 0)).astype(F32)
            p = (win / cnt - blk[SUBLANES:SUBLANES + n_rows]).astype(BF16)
            y = _dot(p, wp_ref[g]) * ps_ref[:, ln] * ga_ref[rows, ln]
            ab_ref[rows, ln] = y.astype(BF16)
        return carry
    lax.fori_loop(0, nb * per_seq, step, 0)


def _out_proj_chunk(ab_ref, w_ref, x_ref, gate, dst_ref, c, nb, t):
    lhs = ab_ref[_rows(c * ROW_CHUNK, ROW_CHUNK, ROW_CHUNK), :]
    for g in range(D_MODEL // W_HALF):
        y = _dot(lhs, w_ref[:, _group(g)])
        for s, off, n, o in _pieces(c, nb, t):
            rows = _rows(off, n, n)
            dst_ref[s, rows, _group(g)] = x_ref[s, rows, _group(g)] + gate[:, _group(g)] * y[o:o + n]


def _conv_phase(pad_c, pad_d, bc_ref, ga_ref, gb_ref, cc_ref, cdw_ref, cdb_ref, lng_ref, lnb_ref,
                ab_ref, nb, t):
    n_rows = CONV_ROWS
    per_seq = t // n_rows

    def step(i, carry):
        s = i // per_seq
        r0 = (i - s * per_seq) * n_rows
        prow = _pad_row(s, r0, t)
        rows = _rows(i * n_rows, n_rows, n_rows)
        z = []
        for g in range(W_HALF // LANES):
            ln = _lanes(g)
            blk = pad_c[_rows(prow - SUBLANES, n_rows + 2 * SUBLANES, SUBLANES), ln]
            c3 = None
            for j in range(CONV_C):
                o = SUBLANES + j - CONV_C // 2
                term = blk[o:o + n_rows] * cc_ref[j:j + 1, ln]
                c3 = term if c3 is None else c3 + term
            ab_ref[rows, ln] = (bc_ref[rows, ln] * c3 * ga_ref[rows, ln]).astype(BF16)
            acc = None
            for sft in range(SUBLANES):
                part = None
                for a in range((CONV_D - sft + SUBLANES - 1) // SUBLANES):
                    j = SUBLANES * a + sft
                    src = pad_d[_rows(prow - 2 * SUBLANES + SUBLANES * a, n_rows + SUBLANES,
                                      SUBLANES), ln]
                    term = src * cdw_ref[j:j + 1, ln]
                    part = term if part is None else part + term
                o = SUBLANES + sft - (CONV_D // 2 - SUBLANES)
                part = part[o:o + n_rows]
                acc = part if acc is None else acc + part
            z.append(acc + cdb_ref[:, ln])
        z = jnp.concatenate(z, axis=-1)
        mu = jnp.mean(z, axis=-1, keepdims=True)
        zc = z - mu
        var = jnp.mean(zc * zc, axis=-1, keepdims=True)
        zn = zc * lax.rsqrt(var + EPS) * lng_ref[...] + lnb_ref[...]
        ab_ref[rows, W_HALF:] = (_silu(zn) * gb_ref[rows, :]).astype(BF16)
        return carry
    lax.fori_loop(0, nb * per_seq, step, 0)


def _final_norm_chunk(y_ref, fg, c, nb, t):
    for s, off, n, _ in _pieces(c, nb, t):
        for i in range(0, n, NORM_ROWS):
            rows = _rows(off + i, NORM_ROWS, NORM_ROWS)
            x = y_ref[s, rows, :]
            ms = jnp.mean(x * x, axis=-1, keepdims=True)
            y_ref[s, rows, :] = x * lax.rsqrt(ms + EPS) * fg


def _odd_layer(y_ref, m_row, g_row, fg, wio_ref, cc_ref, cdw_ref, cdb_ref, lng_ref, lnb_ref, woo_ref,
               h_ref, pad_c, pad_d, bc_ref, ga_ref, gb_ref, ab_ref, nb, t):
    shift = m_row[:, :D_MODEL]
    gain = g_row * (1.0 + m_row[:, D_MODEL:2 * D_MODEL])
    gate = m_row[:, 2 * D_MODEL:]
    n_chunks = nb * t // ROW_CHUNK

    def in_proj(c):
        _modnorm_chunk(y_ref, h_ref, c, nb, t, gain, shift)
        rows = _rows(c * ROW_CHUNK, ROW_CHUNK, ROW_CHUNK)
        pieces = _pieces(c, nb, t)
        h = h_ref[rows, :]
        bc_ref[rows, :] = _dot(h, wio_ref[:, _group(0)])
        _store_padded(pad_c, _dot(h, wio_ref[:, _group(1)]), pieces, t)
        _scale_padded(pad_c, _dot(h, wio_ref[:, _group(2)]), pieces, t)
        ga_ref[rows, :] = _silu(_dot(h, wio_ref[:, _group(3)]))
        _store_padded(pad_d, _dot(h, wio_ref[:, _group(4)]), pieces, t)
        _scale_padded(pad_d, _sigmoid(_dot(h, wio_ref[:, _group(5)])), pieces, t)
        gb_ref[rows, :] = _silu(_dot(h, wio_ref[:, _group(6)]))
    _for_chunks(n_chunks, in_proj)

    _conv_phase(pad_c, pad_d, bc_ref, ga_ref, gb_ref, cc_ref, cdw_ref, cdb_ref, lng_ref, lnb_ref,
                ab_ref, nb, t)

    def out_proj(c):
        _out_proj_chunk(ab_ref, woo_ref, y_ref, gate, y_ref, c, nb, t)
        _final_norm_chunk(y_ref, fg, c, nb, t)
    _for_chunks(n_chunks, out_proj, unrolled=True)


def _even_in_proj(x_ref, m_row, g_row, w_ref, h_ref, pad_a, ga_ref, gb_ref, q_ref, k_ref, v_ref,
                  kv_t, nb, t):
    shift = m_row[:, :D_MODEL]
    gain = g_row * (1.0 + m_row[:, D_MODEL:2 * D_MODEL])

    def in_proj(c):
        _modnorm_chunk(x_ref, h_ref, c, nb, t, gain, shift)
        rows = _rows(c * ROW_CHUNK, ROW_CHUNK, ROW_CHUNK)
        pieces = _pieces(c, nb, t)
        h = h_ref[rows, :]
        _store_padded(pad_a, _dot(h, w_ref[:, _group(0)]), pieces, t)
        ga_ref[rows, :] = _silu(_dot(h, w_ref[:, _group(1)]))
        q_ref[rows, :] = (_dot(h, w_ref[:, _group(2)]) * Q_SCALE).astype(BF16)
        for i, (dst, g) in enumerate(((k_ref, 3), (v_ref, 4))):
            if kv_t is None:
                dst[rows, :] = _dot(h, w_ref[:, _group(g)]).astype(BF16)
                continue
            acc = _dot_nt(kv_t[2 + i][...], h)
            dst[:, rows] = acc.astype(BF16)
            for s, off, n, o in pieces:
                for hd in range(N_HEADS):
                    kv_t[i][s, 0, hd, :, _rows(off, n, n)] = (
                        acc[hd * HEAD_DIM:(hd + 1) * HEAD_DIM, o:o + n])
        gb_ref[rows, :] = _silu(_dot(h, w_ref[:, _group(5)]))
    _for_chunks(nb * t // ROW_CHUNK, in_proj)


def _even_out_proj(x_ref, y_ref, m_row, w_ref, ab_ref, nb, t):
    gate = m_row[:, 2 * D_MODEL:]
    _for_chunks(nb * t // ROW_CHUNK,
                lambda c: _out_proj_chunk(ab_ref, w_ref, x_ref, gate, y_ref, c, nb, t))


def _split_heads(x):
    lane = lax.broadcasted_iota(jnp.int32, (1, LANES), 1)
    first = jnp.where(lane < HEAD_DIM, 1.0, 0.0).astype(x.dtype)
    return jnp.concatenate([x * first, x * (1 - first)], axis=0)


def _merge_heads(o):
    n = o.shape[0] // 2
    lane = lax.broadcasted_iota(jnp.int32, (n, LANES), 1)
    return jnp.where(lane < HEAD_DIM, o[:n], o[n:])


def _context_attention(q_ref, kt_ref, vt_ref, gb_ref, ab_ref, nb, t):
    for s in range(nb):
        seq = slice(s * t, (s + 1) * t)
        for j in range(N_HEADS // 2):
            ln = _lanes(j)
            kp = kt_ref[ln, seq]
            vp = vt_ref[ln, seq]
            for r0 in range(0, t, Q_ROWS):
                rows = slice(s * t + r0, s * t + r0 + Q_ROWS)
                sc = _dot(_split_heads(q_ref[rows, ln]), kp)
                p = jnp.exp2(sc - jnp.max(sc, axis=-1, keepdims=True))
                o = _dot_nt(p.astype(BF16), vp) / jnp.sum(p, axis=-1, keepdims=True)
                ab_ref[rows, W_HALF + j * LANES:W_HALF + (j + 1) * LANES] = (
                    _merge_heads(o) * gb_ref[rows, ln]).astype(BF16)


def _stage_weights(w_hbm, w_bf, wkt_ref, wvt_ref, stage, sem_in):
    chunks = [(k, r0) for k in range(len(w_hbm)) for r0 in range(0, w_hbm[k].shape[1], STAGE_ROWS)]

    def fetch(i):
        k, r0 = chunks[i]
        cols = w_hbm[k].shape[2]
        slot = i % STAGE_SLOTS
        return pltpu.make_async_copy(w_hbm[k].at[0, pl.ds(r0, STAGE_ROWS), :],
                                     stage.at[slot, :, pl.ds(0, cols)], sem_in.at[slot])

    for i in range(min(STAGE_SLOTS - 1, len(chunks))):
        fetch(i).start()
    for i, (k, r0) in enumerate(chunks):
        if i + STAGE_SLOTS - 1 < len(chunks):
            fetch(i + STAGE_SLOTS - 1).start()
        fetch(i).wait()
        cols = w_hbm[k].shape[2]
        rows = slice(r0, r0 + STAGE_ROWS)
        slot = i % STAGE_SLOTS
        w_bf[k][rows, :] = stage[slot, :, 0:cols].astype(BF16)
        if k == 0:
            wkt_ref[:, rows] = stage[slot, :, _group(3)].T.astype(BF16)
            wvt_ref[:, rows] = stage[slot, :, _group(4)].T.astype(BF16)


def _prompt_body(x_ref, m_ref, ng_ref, fg_ref, wp_ref, ps_ref, cc_ref, cdw_ref, cdb_ref, lng_ref,
                 lnb_ref, wie_hbm, woe_hbm, wio_hbm, woo_hbm,
                 y_ref, ko_ref, vo_ref, wie_out, woe_out, wio_out, woo_out,
                 h_ref, pad_a, pad_b, ga_ref, gb_ref, bc_ref, q_ref, kt_ref, vt_ref, ab_ref,
                 wie_ref, woe_ref, wio_ref, woo_ref, wkt_ref, wvt_ref, stage, sem_in, sem_out,
                 *, nb, t):
    w_out = (wie_out, woe_out, wio_out, woo_out)
    w_bf = (wie_ref, woe_ref, wio_ref, woo_ref)

    def write_back(k):
        return pltpu.make_async_copy(w_bf[k], w_out[k], sem_out.at[k])

    @pl.when(pl.program_id(0) == 0)
    def _():
        _stage_weights((wie_hbm, woe_hbm, wio_hbm, woo_hbm), w_bf, wkt_ref, wvt_ref, stage, sem_in)
        for k in range(len(w_bf)):
            write_back(k).start()

    _zero_pads(pad_a, nb, t)
    _zero_pads(pad_b, nb, t)
    m_even = _cond_row(m_ref, 0, 0)
    _even_in_proj(x_ref, m_even, ng_ref[0:1, :], wie_ref, h_ref, pad_a, ga_ref, gb_ref,
                  q_ref, kt_ref, vt_ref, (ko_ref, vo_ref, wkt_ref, wvt_ref), nb, t)
    _pool_phase(pad_a, ga_ref, wp_ref, ps_ref, ab_ref, nb, t)
    _context_attention(q_ref, kt_ref, vt_ref, gb_ref, ab_ref, nb, t)
    _even_out_proj(x_ref, y_ref, m_even, woe_ref, ab_ref, nb, t)
    _odd_layer(y_ref, _cond_row(m_ref, 1, 0), ng_ref[1:2, :], fg_ref[...], wio_ref, cc_ref, cdw_ref,
               cdb_ref, lng_ref, lnb_ref, woo_ref, h_ref, pad_a, pad_b, bc_ref, ga_ref, gb_ref, ab_ref,
               nb, t)

    @pl.when(pl.program_id(0) == 0)
    def _():
        for k in range(len(w_bf)):
            write_back(k).wait()


def _rpb_rows(rpb_ref, e_ref):
    n = rpb_ref.shape[0]
    lane = lax.broadcasted_iota(jnp.int32, (n, LANES), 1)
    i = jnp.where(lane < GRID_W, lane, lane - LANES)
    idx = jnp.clip(i, -(WIN_W - 1), WIN_W - 1) + (WIN_W - 1)
    rp = rpb_ref[...]
    e = jnp.zeros((n, LANES), F32)
    for d in range(2 * WIN_W - 1):
        e = jnp.where(idx == d, rp[:, d:d + 1], e)
    e_ref[...] = e


N_DR = 2 * WIN_H - 1
PAIR_TILES = N_DR // 2


def _bias_tile_index(j, dr_lo):
    if isinstance(dr_lo, int):
        parity, half = dr_lo % 2, dr_lo // 2
    else:
        parity, half = dr_lo & 1, lax.shift_right_logical(dr_lo, 1)
    return (2 * j + parity) * PAIR_TILES + half


def _bias_tables(e_ref, bias_ref):
    q = lax.broadcasted_iota(jnp.int32, (GRID_W, LANES), 0)
    lane = lax.broadcasted_iota(jnp.int32, (GRID_W, LANES), 1)
    kw = jnp.where(lane < GRID_W, lane, lane - GRID_W)
    start = jnp.clip(q - WIN_W // 2, 0, GRID_W - WIN_W)
    col_ok = (kw >= start) & (kw < start + WIN_W)
    for j in range(N_HEADS // 2):
        for dr in range(N_DR - 1):
            for e in range(2):
                r_lo = (2 * j + e) * N_DR + dr
                lo = jnp.broadcast_to(e_ref[r_lo:r_lo + 1, :], (GRID_W, LANES))
                hi = jnp.broadcast_to(e_ref[r_lo + 1:r_lo + 2, :], (GRID_W, LANES))
                lo = pltpu.roll(lo, 0, 1, stride=1, stride_axis=0)
                hi = pltpu.roll(hi, GRID_W, 1, stride=1, stride_axis=0)
                tile = jnp.where(lane < GRID_W, lo, hi)
                bias_ref[_bias_tile_index(j, dr), e * GRID_W:(e + 1) * GRID_W, :] = jnp.where(
                    col_ok, tile * LOG2_E, MASKED)


def _neighbourhood_attention(q_ref, k_ref, v_ref, ck_ref, cv_ref, bias_ref, kvc_ref, gb_ref, ab_ref, t):
    grid_h = t // GRID_W
    band = WIN_H * GRID_W
    for j in range(N_HEADS // 2):
        ln = _lanes(j)
        for i, src in enumerate((ck_ref, cv_ref)):
            kvc_ref[i] = jnp.concatenate([src[0, 0, 2 * j], src[0, 0, 2 * j + 1]],
                                         axis=0).astype(BF16)

        def per_group(g, carry, ln=ln, j=j):
            scored = []
            for u in range(NA_GROUP):
                r = g * NA_GROUP + u
                start = jnp.clip(r - WIN_H // 2, 0, grid_h - WIN_H)
                rows = _rows(r * GRID_W, GRID_W, GRID_W)
                keys = _rows(start * GRID_W, band, GRID_W)
                q2 = _split_heads(q_ref[rows, ln])
                dr0 = (WIN_H - 1) - (r - start)
                bias = jnp.concatenate([bias_ref[_bias_tile_index(j, dr0 + 2 * i)]
                                        for i in range(WIN_H // 2)], axis=-1)
                scored.append((rows, keys, _dot_nt(q2, k_ref[keys, ln]) + bias, _dot(q2, kvc_ref[0])))
            weighted = []
            for rows, keys, s_loc, s_ctx in scored:
                mx = jnp.maximum(jnp.max(s_loc, axis=-1, keepdims=True),
                                 jnp.max(s_ctx, axis=-1, keepdims=True))
                p_loc = jnp.exp2(s_loc - mx)
                p_ctx = jnp.exp2(s_ctx - mx)
                den = (jnp.sum(p_loc, axis=-1, keepdims=True)
                       + jnp.sum(p_ctx, axis=-1, keepdims=True))
                weighted.append((rows, keys, p_loc.astype(BF16), p_ctx.astype(BF16), den))
            for rows, keys, p_loc, p_ctx, den in weighted:
                o = (_dot(p_loc, v_ref[keys, ln]) + _dot_nt(p_ctx, kvc_ref[1])) / den
                ab_ref[rows, W_HALF + j * LANES:W_HALF + (j + 1) * LANES] = (
                    _merge_heads(o) * gb_ref[rows, ln]).astype(BF16)
            return carry
        lax.fori_loop(0, grid_h // NA_GROUP, per_group, 0)


def _sample_body(x_ref, m_ref, ng_ref, fg_ref, wie_ref, wp_ref, ps_ref, woe_ref, wio_ref, cc_ref,
                 cdw_ref, cdb_ref, lng_ref, lnb_ref, woo_ref, ck_ref, cv_ref, rpb_ref,
                 y_ref,
                 h_ref, pad_a, pad_b, ga_ref, gb_ref, bc_ref, q_ref, k_ref, v_ref, ab_ref,
                 e_ref, bias_ref, kvc_ref, *, t):
    _zero_pads(pad_a, 1, t)
    _zero_pads(pad_b, 1, t)

    @pl.when(pl.program_id(0) == 0)
    def _():
        _rpb_rows(rpb_ref, e_ref)
        _bias_tables(e_ref, bias_ref)

    cond = pl.program_id(0) + 1
    m_even = _cond_row(m_ref, 0, cond)
    _even_in_proj(x_ref, m_even, ng_ref[0:1, :], wie_ref, h_ref, pad_a, ga_ref, gb_ref,
                  q_ref, k_ref, v_ref, None, 1, t)
    _pool_phase(pad_a, ga_ref, wp_ref, ps_ref, ab_ref, 1, t)
    _neighbourhood_attention(q_ref, k_ref, v_ref, ck_ref, cv_ref, bias_ref, kvc_ref, gb_ref, ab_ref, t)
    _even_out_proj(x_ref, y_ref, m_even, woe_ref, ab_ref, 1, t)
    _odd_layer(y_ref, _cond_row(m_ref, 1, cond), ng_ref[1:2, :], fg_ref[...], wio_ref, cc_ref, cdw_ref,
               cdb_ref, lng_ref, lnb_ref, woo_ref, h_ref, pad_a, pad_b, bc_ref, ga_ref, gb_ref, ab_ref,
               1, t)


def _const_spec(shape):
    zeros = (0,) * len(shape)
    return pl.BlockSpec(shape, lambda i: zeros, pipeline_mode=pl.Buffered(1))


def _stream_scratch(nb, t, kv_transposed):
    r = nb * t
    padded = nb * (t + 2 * PAD)
    kv = (W_HALF, r) if kv_transposed else (r, W_HALF)
    return [
        pltpu.VMEM((r, D_MODEL), BF16),
        pltpu.VMEM((padded, W_HALF), F32),
        pltpu.VMEM((padded, W_HALF), F32),
        pltpu.VMEM((r, W_HALF), F32),
        pltpu.VMEM((r, W_HALF), F32),
        pltpu.VMEM((r, W_HALF), F32),
        pltpu.VMEM((r, W_HALF), BF16),
        pltpu.VMEM(kv, BF16),
        pltpu.VMEM(kv, BF16),
        pltpu.VMEM((r, D_MODEL), BF16),
    ]


def _small_params(norm_g, final_g, w_pool, pool_scale, conv_c, conv_d, conv_d_b, ln_g, ln_b):
    return [norm_g, final_g.reshape(1, D_MODEL), w_pool[0].astype(BF16), pool_scale, conv_c[0],
            conv_d[0], conv_d_b, ln_g, ln_b]


def kernel(x_prompt, x_sample, cache_k, cache_v, c, c_ctx, norm_g, w_mod, b_mod, w_in_even, w_pool,
           pool_scale, rpb, w_out_even, w_in_odd, conv_c, conv_d, conv_d_b, ln_g, ln_b, w_out_odd,
           final_g):
    batch, seq, d = x_prompt.shape
    dec_batch, dec_seq, _ = x_sample.shape
    assert d == D_MODEL and w_mod.shape[0] == 2 and w_in_even.shape[0] == 1 and w_in_odd.shape[0] == 1
    assert (NB_PROMPT * seq) % ROW_CHUNK == 0 and ROW_CHUNK % seq == 0 and seq % Q_ROWS == 0
    assert dec_seq % ROW_CHUNK == 0 and dec_seq // GRID_W >= WIN_H
    assert seq % POOL_ROWS == 0 and seq % CONV_ROWS == 0
    assert dec_seq % POOL_ROWS == 0 and dec_seq % CONV_ROWS == 0
    assert (dec_seq // GRID_W) % NA_GROUP == 0

    cond_rows = SUBLANES * ((1 + dec_batch + SUBLANES - 1) // SUBLANES)
    m = _modulation(c_ctx, c, w_mod, b_mod, cond_rows)
    m_spec = _const_spec(m.shape)

    small = _small_params(norm_g, final_g, w_pool, pool_scale, conv_c, conv_d, conv_d_b, ln_g, ln_b)
    small_specs = [_const_spec(a.shape) for a in small]
    w_f32 = (w_in_even, w_out_even, w_in_odd, w_out_odd)
    assert all(w.shape[0] == 1 and w.shape[1] % STAGE_ROWS == 0 for w in w_f32)
    any_spec = pl.BlockSpec(memory_space=pl.ANY)

    nb = NB_PROMPT
    assert batch % nb == 0
    kv_shape = jax.ShapeDtypeStruct((batch, 1, N_HEADS, HEAD_DIM, seq), F32)
    kv_spec = pl.BlockSpec((nb, 1, N_HEADS, HEAD_DIM, seq), lambda i: (i, 0, 0, 0, 0))
    y_prompt, new_kt, new_vt, wie, woe, wio, woo = pl.pallas_call(
        functools.partial(_prompt_body, nb=nb, t=seq),
        out_shape=(jax.ShapeDtypeStruct(x_prompt.shape, F32), kv_shape, kv_shape)
                  + tuple(jax.ShapeDtypeStruct(w.shape[1:], BF16) for w in w_f32),
        grid=(batch // nb,),
        in_specs=[pl.BlockSpec((nb, seq, d), lambda i: (i, 0, 0)), m_spec] + small_specs
                 + [any_spec] * len(w_f32),
        out_specs=(pl.BlockSpec((nb, seq, d), lambda i: (i, 0, 0)), kv_spec, kv_spec)
                  + (any_spec,) * len(w_f32),
        scratch_shapes=_stream_scratch(nb, seq, True)
                       + [pltpu.VMEM(w.shape[1:], BF16) for w in w_f32] + [
            pltpu.VMEM((W_HALF, d), BF16),
            pltpu.VMEM((W_HALF, d), BF16),
            pltpu.VMEM((STAGE_SLOTS, STAGE_ROWS, max(w.shape[2] for w in w_f32)), F32),
            pltpu.SemaphoreType.DMA((STAGE_SLOTS,)),
            pltpu.SemaphoreType.DMA((len(w_f32),)),
        ],
        compiler_params=pltpu.CompilerParams(dimension_semantics=("arbitrary",),
                                             vmem_limit_bytes=VMEM_LIMIT),
        name="prompt",
    )(x_prompt, m, *small, *w_f32)
    ng, fg, wp, ps, cc, cdw, cdb, lng, lnb = small
    w_args = [ng, fg, wie, wp, ps, woe, wio, cc, cdw, cdb, lng, lnb, woo]
    w_specs = [_const_spec(a.shape) for a in w_args]

    past = cache_k.shape[3]
    cache_spec = pl.BlockSpec((1, 1, N_HEADS, HEAD_DIM, past), lambda i: (i, 0, 0, 0, 0))
    rpb2 = rpb[0].reshape(N_HEADS * (2 * WIN_H - 1), 2 * WIN_W - 1)
    y_sample = pl.pallas_call(
        functools.partial(_sample_body, t=dec_seq),
        out_shape=jax.ShapeDtypeStruct(x_sample.shape, F32),
        grid=(dec_batch,),
        in_specs=[pl.BlockSpec((1, dec_seq, d), lambda i: (i, 0, 0), pipeline_mode=pl.Buffered(1)),
                  m_spec] + w_specs
                 + [cache_spec, cache_spec, _const_spec(rpb2.shape)],
        out_specs=pl.BlockSpec((1, dec_seq, d), lambda i: (i, 0, 0)),
        scratch_shapes=_stream_scratch(1, dec_seq, False) + [
            pltpu.VMEM(rpb2.shape[:1] + (LANES,), F32),
            pltpu.VMEM((N_HEADS * PAIR_TILES, 2 * GRID_W, LANES), F32),
            pltpu.VMEM((2, LANES, past), BF16),
        ],
        compiler_params=pltpu.CompilerParams(dimension_semantics=("arbitrary",),
                                             vmem_limit_bytes=VMEM_LIMIT),
        name="sample",
    )(x_sample, m, *w_args, jnp.swapaxes(cache_k, 3, 4), jnp.swapaxes(cache_v, 3, 4), rpb2)

    return (y_prompt, y_sample, jnp.swapaxes(new_kt, 3, 4), jnp.swapaxes(new_vt, 3, 4))
```

```python
import functools

import jax
import jax.numpy as jnp
from jax import lax
from jax.experimental import pallas as pl
from jax.experimental.pallas import tpu as pltpu

F32 = jnp.float32
BF16 = jnp.bfloat16

D_MODEL = 1024
W_HALF = 512
N_POOL_GROUPS = 4
POOL_HALF = (1, 2, 4, 8)
N_HEADS = 8
HEAD_DIM = 64
GRID_W = 64
WIN_H = 8
WIN_W = 16
CONV_C = 3
CONV_D = 31
EPS = 1e-6
MASKED = -1e30
LOG2_E = 1.4426950408889634
Q_SCALE = HEAD_DIM ** -0.5 * LOG2_E

LANES = 128
SUBLANES = 8
PAD = 16
ROW_CHUNK = 512
NORM_ROWS = 32
POOL_ROWS = 256
CONV_ROWS = 128
Q_ROWS = 128
NB_PROMPT = 2
NA_GROUP = 8
MOD_COLS = 1536
STAGE_ROWS = 128
STAGE_SLOTS = 4
VMEM_LIMIT = 58 * 1024 * 1024

assert PAD >= CONV_D // 2 + 1 and PAD % SUBLANES == 0 and PAD >= 2 * SUBLANES
assert max(POOL_HALF) <= SUBLANES


def _sigmoid(x):
    return 1.0 / (1.0 + jnp.exp(-x))


def _silu(x):
    return x * _sigmoid(x)


def _dot(a, b):
    return jnp.dot(a, b, preferred_element_type=F32)


def _dot_nt(a, b):
    return lax.dot_general(a, b, (((1,), (1,)), ((), ())), preferred_element_type=F32)


def _lanes(j):
    return slice(j * LANES, (j + 1) * LANES)


def _group(g):
    return slice(g * W_HALF, (g + 1) * W_HALF)


def _rows(start, size, align):
    if isinstance(start, int):
        return slice(start, start + size)
    return pl.ds(pl.multiple_of(start, align), size)


def _mod_body(cctx_ref, c_ref, w_ref, b_ref, o_ref):
    rows, d = o_ref.shape[1], cctx_ref.shape[1]
    r = lax.broadcasted_iota(jnp.int32, (rows, d), 0)
    cond = jnp.where(r == 0, cctx_ref[...], 0.0)
    for i in range(c_ref.shape[0]):
        cond = jnp.where(r == i + 1, c_ref[i:i + 1, :], cond)
    bias = jnp.where(pl.program_id(0) == 0, b_ref[0:1, :], b_ref[1:2, :])
    o_ref[0] = _dot(_silu(cond).astype(BF16), w_ref[0].astype(BF16)) + bias


def _modulation(c_ctx, c, w_mod, b_mod, rows):
    depth, d, n = w_mod.shape
    assert depth == 2 and 1 + c.shape[0] <= rows
    return pl.pallas_call(
        _mod_body,
        out_shape=jax.ShapeDtypeStruct((depth, rows, n), F32),
        grid=(depth, n // MOD_COLS),
        in_specs=[
            pl.BlockSpec((1, d), lambda l, j: (0, 0)),
            pl.BlockSpec(c.shape, lambda l, j: (0, 0)),
            pl.BlockSpec((1, d, MOD_COLS), lambda l, j: (l, 0, j)),
            pl.BlockSpec((depth, MOD_COLS), lambda l, j: (0, j)),
        ],
        out_specs=pl.BlockSpec((1, rows, MOD_COLS), lambda l, j: (l, 0, j)),
        compiler_params=pltpu.CompilerParams(dimension_semantics=("arbitrary", "arbitrary")),
        name="mod",
    )(c_ctx.reshape(1, d), c, w_mod, b_mod)


def _cond_row(m_ref, layer, row):
    if isinstance(row, int):
        return m_ref[layer, row:row + 1, :]
    m = m_ref[layer]
    keep = lax.broadcasted_iota(jnp.int32, m.shape, 0) == row
    return jnp.sum(jnp.where(keep, m, 0.0), axis=0, keepdims=True)


def _pieces(c, nb, t):
    if t >= ROW_CHUNK:
        per_seq = t // ROW_CHUNK
        s = 0 if nb == 1 else c // per_seq
        return [(s, (c - s * per_seq) * ROW_CHUNK, ROW_CHUNK, 0)]
    per_chunk = ROW_CHUNK // t
    return [(c * per_chunk + i, 0, t, i * t) for i in range(per_chunk)]


def _for_chunks(n, body, unrolled=False):
    if unrolled or n == 1:
        for c in range(n):
            body(c)
    else:
        lax.fori_loop(0, n, lambda c, carry: (body(c), carry)[1], 0)


def _pad_row(s, off, t):
    return s * (t + 2 * PAD) + PAD + off


def _store_padded(pad_ref, val, pieces, t):
    for s, off, n, o in pieces:
        pad_ref[_rows(_pad_row(s, off, t), n, SUBLANES), :] = val[o:o + n]


def _scale_padded(pad_ref, val, pieces, t):
    for s, off, n, o in pieces:
        rows = _rows(_pad_row(s, off, t), n, SUBLANES)
        pad_ref[rows, :] = pad_ref[rows, :] * val[o:o + n]


def _modnorm_chunk(src_ref, h_ref, c, nb, t, gain, shift):
    for s, off, n, o in _pieces(c, nb, t):
        for i in range(0, n, NORM_ROWS):
            x = src_ref[s, _rows(off + i, NORM_ROWS, NORM_ROWS), :]
            ms = jnp.mean(x * x, axis=-1, keepdims=True)
            h_ref[_rows(c * ROW_CHUNK + o + i, NORM_ROWS, NORM_ROWS), :] = (
                x * lax.rsqrt(ms + EPS) * gain + shift).astype(BF16)


def _zero_pads(pad_ref, nb, t):
    z = jnp.zeros((PAD, W_HALF), F32)
    for s in range(nb):
        pad_ref[_pad_row(s, 0, t) - PAD:_pad_row(s, 0, t), :] = z
        pad_ref[_pad_row(s, t, t):_pad_row(s, t, t) + PAD, :] = z


def _pool_phase(pad_ref, ga_ref, wp_ref, ps_ref, ab_ref, nb, t):
    n_rows = POOL_ROWS
    per_seq = t // n_rows

    def step(i, carry):
        s = i // per_seq
        r0 = (i - s * per_seq) * n_rows
        prow = _pad_row(s, r0, t)
        rows = _rows(i * n_rows, n_rows, n_rows)
        pos = r0 + lax.broadcasted_iota(jnp.int32, (n_rows, LANES), 0)
        for g in range(N_POOL_GROUPS):
            hw = POOL_HALF[g]
            ln = _lanes(g)
            halo = n_rows + 2 * SUBLANES
            blk = pad_ref[_rows(prow - SUBLANES, halo, SUBLANES), ln]
            run, n = blk, 1
            while n < 2 * hw:
                run = run + pltpu.roll(run, halo - n, 0)
                n *= 2
            if hw < SUBLANES:
                run = pltpu.roll(run, halo - (SUBLANES - hw), 0)
            win = run[:n_rows]
            cnt = (jnp.minimum(pos + hw, t) - jnp.maximum(pos - hw, 0)).astype(F32)
            p = (win / cnt - blk[SUBLANES:SUBLANES + n_rows]).astype(BF16)
            y = _dot(p, wp_ref[g]) * ps_ref[:, ln] * ga_ref[rows, ln]
            ab_ref[rows, ln] = y.astype(BF16)
        return carry
    lax.fori_loop(0, nb * per_seq, step, 0)


def _out_proj_chunk(ab_ref, w_ref, x_ref, gate, dst_ref, c, nb, t):
    lhs = ab_ref[_rows(c * ROW_CHUNK, ROW_CHUNK, ROW_CHUNK), :]
    for g in range(D_MODEL // W_HALF):
        y = _dot(lhs, w_ref[:, _group(g)])
        for s, off, n, o in _pieces(c, nb, t):
            rows = _rows(off, n, n)
            dst_ref[s, rows, _group(g)] = x_ref[s, rows, _group(g)] + gate[:, _group(g)] * y[o:o + n]


def _conv_phase(pad_c, pad_d, bc_ref, ga_ref, gb_ref, cc_ref, cdw_ref, cdb_ref, lng_ref, lnb_ref,
                ab_ref, nb, t):
    n_rows = CONV_ROWS
    per_seq = t // n_rows

    def step(i, carry):
        s = i // per_seq
        r0 = (i - s * per_seq) * n_rows
        prow = _pad_row(s, r0, t)
        rows = _rows(i * n_rows, n_rows, n_rows)
        z = []
        for g in range(W_HALF // LANES):
            ln = _lanes(g)
            blk = pad_c[_rows(prow - SUBLANES, n_rows + 2 * SUBLANES, SUBLANES), ln]
            c3 = None
            for j in range(CONV_C):
                o = SUBLANES + j - CONV_C // 2
                term = blk[o:o + n_rows] * cc_ref[j:j + 1, ln]
                c3 = term if c3 is None else c3 + term
            ab_ref[rows, ln] = (bc_ref[rows, ln] * c3 * ga_ref[rows, ln]).astype(BF16)
            acc = None
            for sft in range(SUBLANES):
                part = None
                for a in range((CONV_D - sft + SUBLANES - 1) // SUBLANES):
                    j = SUBLANES * a + sft
                    src = pad_d[_rows(prow - 2 * SUBLANES + SUBLANES * a, n_rows + SUBLANES,
                                      SUBLANES), ln]
                    term = src * cdw_ref[j:j + 1, ln]
                    part = term if part is None else part + term
                o = SUBLANES + sft - (CONV_D // 2 - SUBLANES)
                part = part[o:o + n_rows]
                acc = part if acc is None else acc + part
            z.append(acc + cdb_ref[:, ln])
        z = jnp.concatenate(z, axis=-1)
        mu = jnp.mean(z, axis=-1, keepdims=True)
        zc = z - mu
        var = jnp.mean(zc * zc, axis=-1, keepdims=True)
        zn = zc * lax.rsqrt(var + EPS) * lng_ref[...] + lnb_ref[...]
        ab_ref[rows, W_HALF:] = (_silu(zn) * gb_ref[rows, :]).astype(BF16)
        return carry
    lax.fori_loop(0, nb * per_seq, step, 0)


def _final_norm_chunk(y_ref, fg, c, nb, t):
    for s, off, n, _ in _pieces(c, nb, t):
        for i in range(0, n, NORM_ROWS):
            rows = _rows(off + i, NORM_ROWS, NORM_ROWS)
            x = y_ref[s, rows, :]
            ms = jnp.mean(x * x, axis=-1, keepdims=True)
            y_ref[s, rows, :] = x * lax.rsqrt(ms + EPS) * fg


def _odd_layer(y_ref, m_row, g_row, fg, wio_ref, cc_ref, cdw_ref, cdb_ref, lng_ref, lnb_ref, woo_ref,
               h_ref, pad_c, pad_d, bc_ref, ga_ref, gb_ref, ab_ref, nb, t):
    shift = m_row[:, :D_MODEL]
    gain = g_row * (1.0 + m_row[:, D_MODEL:2 * D_MODEL])
    gate = m_row[:, 2 * D_MODEL:]
    n_chunks = nb * t // ROW_CHUNK

    def in_proj(c):
        _modnorm_chunk(y_ref, h_ref, c, nb, t, gain, shift)
        rows = _rows(c * ROW_CHUNK, ROW_CHUNK, ROW_CHUNK)
        pieces = _pieces(c, nb, t)
        h = h_ref[rows, :]
        bc_ref[rows, :] = _dot(h, wio_ref[:, _group(0)])
        _store_padded(pad_c, _dot(h, wio_ref[:, _group(1)]), pieces, t)
        _scale_padded(pad_c, _dot(h, wio_ref[:, _group(2)]), pieces, t)
        ga_ref[rows, :] = _silu(_dot(h, wio_ref[:, _group(3)]))
        _store_padded(pad_d, _dot(h, wio_ref[:, _group(4)]), pieces, t)
        _scale_padded(pad_d, _sigmoid(_dot(h, wio_ref[:, _group(5)])), pieces, t)
        gb_ref[rows, :] = _silu(_dot(h, wio_ref[:, _group(6)]))
    _for_chunks(n_chunks, in_proj)

    _conv_phase(pad_c, pad_d, bc_ref, ga_ref, gb_ref, cc_ref, cdw_ref, cdb_ref, lng_ref, lnb_ref,
                ab_ref, nb, t)

    def out_proj(c):
        _out_proj_chunk(ab_ref, woo_ref, y_ref, gate, y_ref, c, nb, t)
        _final_norm_chunk(y_ref, fg, c, nb, t)
    _for_chunks(n_chunks, out_proj, unrolled=True)


def _even_in_proj(x_ref, m_row, g_row, w_ref, h_ref, pad_a, ga_ref, gb_ref, q_ref, k_ref, v_ref,
                  kv_t, nb, t):
    shift = m_row[:, :D_MODEL]
    gain = g_row * (1.0 + m_row[:, D_MODEL:2 * D_MODEL])

    def in_proj(c):
        _modnorm_chunk(x_ref, h_ref, c, nb, t, gain, shift)
        rows = _rows(c * ROW_CHUNK, ROW_CHUNK, ROW_CHUNK)
        pieces = _pieces(c, nb, t)
        h = h_ref[rows, :]
        _store_padded(pad_a, _dot(h, w_ref[:, _group(0)]), pieces, t)
        ga_ref[rows, :] = _silu(_dot(h, w_ref[:, _group(1)]))
        q_ref[rows, :] = (_dot(h, w_ref[:, _group(2)]) * Q_SCALE).astype(BF16)
        for i, (dst, g) in enumerate(((k_ref, 3), (v_ref, 4))):
            if kv_t is None:
                dst[rows, :] = _dot(h, w_ref[:, _group(g)]).astype(BF16)
                continue
            acc = _dot_nt(kv_t[2 + i][...], h)
            dst[:, rows] = acc.astype(BF16)
            for s, off, n, o in pieces:
                for hd in range(N_HEADS):
                    kv_t[i][s, 0, hd, :, _rows(off, n, n)] = (
                        acc[hd * HEAD_DIM:(hd + 1) * HEAD_DIM, o:o + n])
        gb_ref[rows, :] = _silu(_dot(h, w_ref[:, _group(5)]))
    _for_chunks(nb * t // ROW_CHUNK, in_proj)


def _even_out_proj(x_ref, y_ref, m_row, w_ref, ab_ref, nb, t):
    gate = m_row[:, 2 * D_MODEL:]
    _for_chunks(nb * t // ROW_CHUNK,
                lambda c: _out_proj_chunk(ab_ref, w_ref, x_ref, gate, y_ref, c, nb, t))


def _split_heads(x):
    lane = lax.broadcasted_iota(jnp.int32, (1, LANES), 1)
    first = jnp.where(lane < HEAD_DIM, 1.0, 0.0).astype(x.dtype)
    return jnp.concatenate([x * first, x * (1 - first)], axis=0)


def _merge_heads(o):
    n = o.shape[0] // 2
    lane = lax.broadcasted_iota(jnp.int32, (n, LANES), 1)
    return jnp.where(lane < HEAD_DIM, o[:n], o[n:])


def _context_attention(q_ref, kt_ref, vt_ref, gb_ref, ab_ref, nb, t):
    for s in range(nb):
        seq = slice(s * t, (s + 1) * t)
        for j in range(N_HEADS // 2):
            ln = _lanes(j)
            kp = kt_ref[ln, seq]
            vp = vt_ref[ln, seq]
            for r0 in range(0, t, Q_ROWS):
                rows = slice(s * t + r0, s * t + r0 + Q_ROWS)
                sc = _dot(_split_heads(q_ref[rows, ln]), kp)
                p = jnp.exp2(sc - jnp.max(sc, axis=-1, keepdims=True))
                o = _dot_nt(p.astype(BF16), vp) / jnp.sum(p, axis=-1, keepdims=True)
                ab_ref[rows, W_HALF + j * LANES:W_HALF + (j + 1) * LANES] = (
                    _merge_heads(o) * gb_ref[rows, ln]).astype(BF16)


def _weight_stager(w_hbm, w_bf, wkt_ref, wvt_ref, stage, sem_in):
    chunks = [(k, r0) for k in range(len(w_hbm)) for r0 in range(0, w_hbm[k].shape[1], STAGE_ROWS)]
    bounds = [sum(1 for k, _ in chunks if k < j) for j in range(len(w_hbm) + 1)]

    def fetch(i):
        k, r0 = chunks[i]
        cols = w_hbm[k].shape[2]
        slot = i % STAGE_SLOTS
        return pltpu.make_async_copy(w_hbm[k].at[0, pl.ds(r0, STAGE_ROWS), :],
                                     stage.at[slot, :, pl.ds(0, cols)], sem_in.at[slot])

    def run(lo, hi):
        if lo == 0:
            for i in range(min(STAGE_SLOTS - 1, len(chunks))):
                fetch(i).start()
        for i in range(lo, hi):
            k, r0 = chunks[i]
            if i + STAGE_SLOTS - 1 < len(chunks):
                fetch(i + STAGE_SLOTS - 1).start()
            fetch(i).wait()
            cols = w_hbm[k].shape[2]
            rows = slice(r0, r0 + STAGE_ROWS)
            slot = i % STAGE_SLOTS
            w_bf[k][rows, :] = stage[slot, :, 0:cols].astype(BF16)
            if k == 0:
                wkt_ref[:, rows] = stage[slot, :, _group(3)].T.astype(BF16)
                wvt_ref[:, rows] = stage[slot, :, _group(4)].T.astype(BF16)

    return run, bounds


def _prompt_body(x_ref, m_ref, ng_ref, fg_ref, wp_ref, ps_ref, cc_ref, cdw_ref, cdb_ref, lng_ref,
                 lnb_ref, wie_hbm, woe_hbm, wio_hbm, woo_hbm,
                 y_ref, ko_ref, vo_ref, wie_out, woe_out, wio_out, woo_out,
                 h_ref, pad_a, pad_b, ga_ref, gb_ref, bc_ref, q_ref, kt_ref, vt_ref, ab_ref,
                 wie_ref, woe_ref, wio_ref, woo_ref, wkt_ref, wvt_ref, stage, sem_in, sem_out,
                 *, nb, t):
    w_out = (wie_out, woe_out, wio_out, woo_out)
    w_bf = (wie_ref, woe_ref, wio_ref, woo_ref)
    stage_chunks, first_chunk = _weight_stager((wie_hbm, woe_hbm, wio_hbm, woo_hbm), w_bf, wkt_ref,
                                               wvt_ref, stage, sem_in)
    mid_wio = (first_chunk[2] + first_chunk[3]) // 2

    def write_back(k):
        return pltpu.make_async_copy(w_bf[k], w_out[k], sem_out.at[k])

    def at_first_step(fn):
        pl.when(pl.program_id(0) == 0)(fn)

    at_first_step(lambda: stage_chunks(first_chunk[0], first_chunk[1]))
    _zero_pads(pad_a, nb, t)
    _zero_pads(pad_b, nb, t)
    m_even = _cond_row(m_ref, 0, 0)
    _even_in_proj(x_ref, m_even, ng_ref[0:1, :], wie_ref, h_ref, pad_a, ga_ref, gb_ref,
                  q_ref, kt_ref, vt_ref, (ko_ref, vo_ref, wkt_ref, wvt_ref), nb, t)
    at_first_step(lambda: stage_chunks(first_chunk[1], first_chunk[2]))
    _pool_phase(pad_a, ga_ref, wp_ref, ps_ref, ab_ref, nb, t)
    at_first_step(lambda: stage_chunks(first_chunk[2], mid_wio))
    _context_attention(q_ref, kt_ref, vt_ref, gb_ref, ab_ref, nb, t)
    at_first_step(lambda: stage_chunks(mid_wio, first_chunk[3]))
    _even_out_proj(x_ref, y_ref, m_even, woe_ref, ab_ref, nb, t)

    def finish_staging():
        stage_chunks(first_chunk[3], first_chunk[4])
        for k in range(len(w_bf)):
            write_back(k).start()
    at_first_step(finish_staging)

    _odd_layer(y_ref, _cond_row(m_ref, 1, 0), ng_ref[1:2, :], fg_ref[...], wio_ref, cc_ref, cdw_ref,
               cdb_ref, lng_ref, lnb_ref, woo_ref, h_ref, pad_a, pad_b, bc_ref, ga_ref, gb_ref, ab_ref,
               nb, t)

    def wait_write_backs():
        for k in range(len(w_bf)):
            write_back(k).wait()
    at_first_step(wait_write_backs)


def _rpb_rows(rpb_ref, e_ref):
    n = rpb_ref.shape[0]
    lane = lax.broadcasted_iota(jnp.int32, (n, LANES), 1)
    i = jnp.where(lane < GRID_W, lane, lane - LANES)
    idx = jnp.clip(i, -(WIN_W - 1), WIN_W - 1) + (WIN_W - 1)
    rp = rpb_ref[...]
    e = jnp.zeros((n, LANES), F32)
    for d in range(2 * WIN_W - 1):
        e = jnp.where(idx == d, rp[:, d:d + 1], e)
    e_ref[...] = e


N_DR = 2 * WIN_H - 1
PAIR_TILES = N_DR // 2


def _bias_tile_index(j, dr_lo):
    if isinstance(dr_lo, int):
        parity, half = dr_lo % 2, dr_lo // 2
    else:
        parity, half = dr_lo & 1, lax.shift_right_logical(dr_lo, 1)
    return (2 * j + parity) * PAIR_TILES + half


def _bias_tables(e_ref, bias_ref):
    q = lax.broadcasted_iota(jnp.int32, (GRID_W, LANES), 0)
    lane = lax.broadcasted_iota(jnp.int32, (GRID_W, LANES), 1)
    kw = jnp.where(lane < GRID_W, lane, lane - GRID_W)
    start = jnp.clip(q - WIN_W // 2, 0, GRID_W - WIN_W)
    col_ok = (kw >= start) & (kw < start + WIN_W)
    for j in range(N_HEADS // 2):
        for dr in range(N_DR - 1):
            for e in range(2):
                r_lo = (2 * j + e) * N_DR + dr
                lo = jnp.broadcast_to(e_ref[r_lo:r_lo + 1, :], (GRID_W, LANES))
                hi = jnp.broadcast_to(e_ref[r_lo + 1:r_lo + 2, :], (GRID_W, LANES))
                lo = pltpu.roll(lo, 0, 1, stride=1, stride_axis=0)
                hi = pltpu.roll(hi, GRID_W, 1, stride=1, stride_axis=0)
                tile = jnp.where(lane < GRID_W, lo, hi)
                bias_ref[_bias_tile_index(j, dr), e * GRID_W:(e + 1) * GRID_W, :] = jnp.where(
                    col_ok, tile * LOG2_E, MASKED)


def _neighbourhood_attention(q_ref, k_ref, v_ref, ck_ref, cv_ref, bias_ref, kvc_ref, gb_ref, ab_ref, t):
    grid_h = t // GRID_W
    band = WIN_H * GRID_W
    for j in range(N_HEADS // 2):
        ln = _lanes(j)
        for i, src in enumerate((ck_ref, cv_ref)):
            kvc_ref[i] = jnp.concatenate([src[0, 0, 2 * j], src[0, 0, 2 * j + 1]],
                                         axis=0).astype(BF16)

        def per_group(g, carry, ln=ln, j=j):
            scored = []
            for u in range(NA_GROUP):
                r = g * NA_GROUP + u
                start = jnp.clip(r - WIN_H // 2, 0, grid_h - WIN_H)
                rows = _rows(r * GRID_W, GRID_W, GRID_W)
                keys = _rows(start * GRID_W, band, GRID_W)
                q2 = _split_heads(q_ref[rows, ln])
                dr0 = (WIN_H - 1) - (r - start)
                bias = jnp.concatenate([bias_ref[_bias_tile_index(j, dr0 + 2 * i)]
                                        for i in range(WIN_H // 2)], axis=-1)
                scored.append((rows, keys, _dot_nt(q2, k_ref[keys, ln]) + bias, _dot(q2, kvc_ref[0])))
            weighted = []
            for rows, keys, s_loc, s_ctx in scored:
                mx = jnp.maximum(jnp.max(s_loc, axis=-1, keepdims=True),
                                 jnp.max(s_ctx, axis=-1, keepdims=True))
                p_loc = jnp.exp2(s_loc - mx)
                p_ctx = jnp.exp2(s_ctx - mx)
                den = (jnp.sum(p_loc, axis=-1, keepdims=True)
                       + jnp.sum(p_ctx, axis=-1, keepdims=True))
                weighted.append((rows, keys, p_loc.astype(BF16), p_ctx.astype(BF16), den))
            for rows, keys, p_loc, p_ctx, den in weighted:
                o = (_dot(p_loc, v_ref[keys, ln]) + _dot_nt(p_ctx, kvc_ref[1])) / den
                ab_ref[rows, W_HALF + j * LANES:W_HALF + (j + 1) * LANES] = (
                    _merge_heads(o) * gb_ref[rows, ln]).astype(BF16)
            return carry
        lax.fori_loop(0, grid_h // NA_GROUP, per_group, 0)


def _sample_body(x_ref, m_ref, ng_ref, fg_ref, wie_ref, wp_ref, ps_ref, woe_ref, wio_ref, cc_ref,
                 cdw_ref, cdb_ref, lng_ref, lnb_ref, woo_ref, ck_ref, cv_ref, rpb_ref,
                 y_ref,
                 h_ref, pad_a, pad_b, ga_ref, gb_ref, bc_ref, q_ref, k_ref, v_ref, ab_ref,
                 e_ref, bias_ref, kvc_ref, *, t):
    _zero_pads(pad_a, 1, t)
    _zero_pads(pad_b, 1, t)

    @pl.when(pl.program_id(0) == 0)
    def _():
        _rpb_rows(rpb_ref, e_ref)
        _bias_tables(e_ref, bias_ref)

    cond = pl.program_id(0) + 1
    m_even = _cond_row(m_ref, 0, cond)
    _even_in_proj(x_ref, m_even, ng_ref[0:1, :], wie_ref, h_ref, pad_a, ga_ref, gb_ref,
                  q_ref, k_ref, v_ref, None, 1, t)
    _pool_phase(pad_a, ga_ref, wp_ref, ps_ref, ab_ref, 1, t)
    _neighbourhood_attention(q_ref, k_ref, v_ref, ck_ref, cv_ref, bias_ref, kvc_ref, gb_ref, ab_ref, t)
    _even_out_proj(x_ref, y_ref, m_even, woe_ref, ab_ref, 1, t)
    _odd_layer(y_ref, _cond_row(m_ref, 1, cond), ng_ref[1:2, :], fg_ref[...], wio_ref, cc_ref, cdw_ref,
               cdb_ref, lng_ref, lnb_ref, woo_ref, h_ref, pad_a, pad_b, bc_ref, ga_ref, gb_ref, ab_ref,
               1, t)


def _const_spec(shape):
    zeros = (0,) * len(shape)
    return pl.BlockSpec(shape, lambda i: zeros, pipeline_mode=pl.Buffered(1))


def _stream_scratch(nb, t, kv_transposed):
    r = nb * t
    padded = nb * (t + 2 * PAD)
    kv = (W_HALF, r) if kv_transposed else (r, W_HALF)
    return [
        pltpu.VMEM((r, D_MODEL), BF16),
        pltpu.VMEM((padded, W_HALF), F32),
        pltpu.VMEM((padded, W_HALF), F32),
        pltpu.VMEM((r, W_HALF), F32),
        pltpu.VMEM((r, W_HALF), F32),
        pltpu.VMEM((r, W_HALF), F32),
        pltpu.VMEM((r, W_HALF), BF16),
        pltpu.VMEM(kv, BF16),
        pltpu.VMEM(kv, BF16),
        pltpu.VMEM((r, D_MODEL), BF16),
    ]


def _small_params(norm_g, final_g, w_pool, pool_scale, conv_c, conv_d, conv_d_b, ln_g, ln_b):
    return [norm_g, final_g.reshape(1, D_MODEL), w_pool[0].astype(BF16), pool_scale, conv_c[0],
            conv_d[0], conv_d_b, ln_g, ln_b]


def kernel(x_prompt, x_sample, cache_k, cache_v, c, c_ctx, norm_g, w_mod, b_mod, w_in_even, w_pool,
           pool_scale, rpb, w_out_even, w_in_odd, conv_c, conv_d, conv_d_b, ln_g, ln_b, w_out_odd,
           final_g):
    batch, seq, d = x_prompt.shape
    dec_batch, dec_seq, _ = x_sample.shape
    assert d == D_MODEL and w_mod.shape[0] == 2 and w_in_even.shape[0] == 1 and w_in_odd.shape[0] == 1
    assert (NB_PROMPT * seq) % ROW_CHUNK == 0 and ROW_CHUNK % seq == 0 and seq % Q_ROWS == 0
    assert dec_seq % ROW_CHUNK == 0 and dec_seq // GRID_W >= WIN_H
    assert seq % POOL_ROWS == 0 and seq % CONV_ROWS == 0
    assert dec_seq % POOL_ROWS == 0 and dec_seq % CONV_ROWS == 0
    assert (dec_seq // GRID_W) % NA_GROUP == 0

    cond_rows = SUBLANES * ((1 + dec_batch + SUBLANES - 1) // SUBLANES)
    m = _modulation(c_ctx, c, w_mod, b_mod, cond_rows)
    m_spec = _const_spec(m.shape)

    small = _small_params(norm_g, final_g, w_pool, pool_scale, conv_c, conv_d, conv_d_b, ln_g, ln_b)
    small_specs = [_const_spec(a.shape) for a in small]
    w_f32 = (w_in_even, w_out_even, w_in_odd, w_out_odd)
    assert all(w.shape[0] == 1 and w.shape[1] % STAGE_ROWS == 0 for w in w_f32)
    any_spec = pl.BlockSpec(memory_space=pl.ANY)

    nb = NB_PROMPT
    assert batch % nb == 0
    kv_shape = jax.ShapeDtypeStruct((batch, 1, N_HEADS, HEAD_DIM, seq), F32)
    kv_spec = pl.BlockSpec((nb, 1, N_HEADS, HEAD_DIM, seq), lambda i: (i, 0, 0, 0, 0))
    y_prompt, new_kt, new_vt, wie, woe, wio, woo = pl.pallas_call(
        functools.partial(_prompt_body, nb=nb, t=seq),
        out_shape=(jax.ShapeDtypeStruct(x_prompt.shape, F32), kv_shape, kv_shape)
                  + tuple(jax.ShapeDtypeStruct(w.shape[1:], BF16) for w in w_f32),
        grid=(batch // nb,),
        in_specs=[pl.BlockSpec((nb, seq, d), lambda i: (i, 0, 0)), m_spec] + small_specs
                 + [any_spec] * len(w_f32),
        out_specs=(pl.BlockSpec((nb, seq, d), lambda i: (i, 0, 0)), kv_spec, kv_spec)
                  + (any_spec,) * len(w_f32),
        scratch_shapes=_stream_scratch(nb, seq, True)
                       + [pltpu.VMEM(w.shape[1:], BF16) for w in w_f32] + [
            pltpu.VMEM((W_HALF, d), BF16),
            pltpu.VMEM((W_HALF, d), BF16),
            pltpu.VMEM((STAGE_SLOTS, STAGE_ROWS, max(w.shape[2] for w in w_f32)), F32),
            pltpu.SemaphoreType.DMA((STAGE_SLOTS,)),
            pltpu.SemaphoreType.DMA((len(w_f32),)),
        ],
        compiler_params=pltpu.CompilerParams(dimension_semantics=("arbitrary",),
                                             vmem_limit_bytes=VMEM_LIMIT),
        name="prompt",
    )(x_prompt, m, *small, *w_f32)
    ng, fg, wp, ps, cc, cdw, cdb, lng, lnb = small
    w_args = [ng, fg, wie, wp, ps, woe, wio, cc, cdw, cdb, lng, lnb, woo]
    w_specs = [_const_spec(a.shape) for a in w_args]

    past = cache_k.shape[3]
    cache_spec = pl.BlockSpec((1, 1, N_HEADS, HEAD_DIM, past), lambda i: (i, 0, 0, 0, 0))
    rpb2 = rpb[0].reshape(N_HEADS * (2 * WIN_H - 1), 2 * WIN_W - 1)
    y_sample = pl.pallas_call(
        functools.partial(_sample_body, t=dec_seq),
        out_shape=jax.ShapeDtypeStruct(x_sample.shape, F32),
        grid=(dec_batch,),
        in_specs=[pl.BlockSpec((1, dec_seq, d), lambda i: (i, 0, 0), pipeline_mode=pl.Buffered(1)),
                  m_spec] + w_specs
                 + [cache_spec, cache_spec, _const_spec(rpb2.shape)],
        out_specs=pl.BlockSpec((1, dec_seq, d), lambda i: (i, 0, 0)),
        scratch_shapes=_stream_scratch(1, dec_seq, False) + [
            pltpu.VMEM(rpb2.shape[:1] + (LANES,), F32),
            pltpu.VMEM((N_HEADS * PAIR_TILES, 2 * GRID_W, LANES), F32),
            pltpu.VMEM((2, LANES, past), BF16),
        ],
        compiler_params=pltpu.CompilerParams(dimension_semantics=("arbitrary",),
                                             vmem_limit_bytes=VMEM_LIMIT),
        name="sample",
    )(x_sample, m, *w_args, jnp.swapaxes(cache_k, 3, 4), jnp.swapaxes(cache_v, 3, 4), rpb2)

    return (y_prompt, y_sample, jnp.swapaxes(new_kt, 3, 4), jnp.swapaxes(new_vt, 3, 4))
```

```python
import functools

import jax
import jax.numpy as jnp
from jax import lax
from jax.experimental import pallas as pl
from jax.experimental.pallas import tpu as pltpu

F32 = jnp.float32
BF16 = jnp.bfloat16

D_MODEL = 1024
W_HALF = 512
N_POOL_GROUPS = 4
POOL_HALF = (1, 2, 4, 8)
N_HEADS = 8
HEAD_DIM = 64
GRID_W = 64
WIN_H = 8
WIN_W = 16
CONV_C = 3
CONV_D = 31
EPS = 1e-6
MASKED = -1e30
LOG2_E = 1.4426950408889634
Q_SCALE = HEAD_DIM ** -0.5 * LOG2_E

LANES = 128
SUBLANES = 8
PAD = 16
ROW_CHUNK = 512
NORM_ROWS = 32
POOL_ROWS = 256
CONV_ROWS = 128
Q_ROWS = 128
NB_PROMPT = 2
NA_GROUP = 8
MOD_COLS = 1536
STAGE_ROWS = 128
STAGE_SLOTS = 4
VMEM_LIMIT = 58 * 1024 * 1024

assert PAD >= CONV_D // 2 + 1 and PAD % SUBLANES == 0 and PAD >= 2 * SUBLANES
assert max(POOL_HALF) <= SUBLANES


def _sigmoid(x):
    return 1.0 / (1.0 + jnp.exp(-x))


def _silu(x):
    return x * _sigmoid(x)


def _dot(a, b):
    return jnp.dot(a, b, preferred_element_type=F32)


def _dot_nt(a, b):
    return lax.dot_general(a, b, (((1,), (1,)), ((), ())), preferred_element_type=F32)


def _lanes(j):
    return slice(j * LANES, (j + 1) * LANES)


def _group(g):
    return slice(g * W_HALF, (g + 1) * W_HALF)


def _rows(start, size, align):
    if isinstance(start, int):
        return slice(start, start + size)
    return pl.ds(pl.multiple_of(start, align), size)


def _mod_body(cctx_ref, c_ref, w_ref, b_ref, o_ref):
    rows, d = o_ref.shape[1], cctx_ref.shape[1]
    r = lax.broadcasted_iota(jnp.int32, (rows, d), 0)
    cond = jnp.where(r == 0, cctx_ref[...], 0.0)
    for i in range(c_ref.shape[0]):
        cond = jnp.where(r == i + 1, c_ref[i:i + 1, :], cond)
    bias = jnp.where(pl.program_id(0) == 0, b_ref[0:1, :], b_ref[1:2, :])
    o_ref[0] = _dot(_silu(cond).astype(BF16), w_ref[0].astype(BF16)) + bias


def _modulation(c_ctx, c, w_mod, b_mod, rows):
    depth, d, n = w_mod.shape
    assert depth == 2 and 1 + c.shape[0] <= rows
    return pl.pallas_call(
        _mod_body,
        out_shape=jax.ShapeDtypeStruct((depth, rows, n), F32),
        grid=(depth, n // MOD_COLS),
        in_specs=[
            pl.BlockSpec((1, d), lambda l, j: (0, 0)),
            pl.BlockSpec(c.shape, lambda l, j: (0, 0)),
            pl.BlockSpec((1, d, MOD_COLS), lambda l, j: (l, 0, j)),
            pl.BlockSpec((depth, MOD_COLS), lambda l, j: (0, j)),
        ],
        out_specs=pl.BlockSpec((1, rows, MOD_COLS), lambda l, j: (l, 0, j)),
        compiler_params=pltpu.CompilerParams(dimension_semantics=("arbitrary", "arbitrary")),
        name="mod",
    )(c_ctx.reshape(1, d), c, w_mod, b_mod)


def _cond_row(m_ref, layer, row):
    if isinstance(row, int):
        return m_ref[layer, row:row + 1, :]
    m = m_ref[layer]
    keep = lax.broadcasted_iota(jnp.int32, m.shape, 0) == row
    return jnp.sum(jnp.where(keep, m, 0.0), axis=0, keepdims=True)


def _pieces(c, nb, t):
    if t >= ROW_CHUNK:
        per_seq = t // ROW_CHUNK
        s = 0 if nb == 1 else c // per_seq
        return [(s, (c - s * per_seq) * ROW_CHUNK, ROW_CHUNK, 0)]
    per_chunk = ROW_CHUNK // t
    return [(c * per_chunk + i, 0, t, i * t) for i in range(per_chunk)]


def _for_chunks(n, body, unrolled=False):
    if unrolled or n == 1:
        for c in range(n):
            body(c)
    else:
        lax.fori_loop(0, n, lambda c, carry: (body(c), carry)[1], 0)


def _pad_row(s, off, t):
    return s * (t + 2 * PAD) + PAD + off


def _store_padded(pad_ref, val, pieces, t):
    for s, off, n, o in pieces:
        pad_ref[_rows(_pad_row(s, off, t), n, SUBLANES), :] = val[o:o + n]


def _scale_padded(pad_ref, val, pieces, t):
    for s, off, n, o in pieces:
        rows = _rows(_pad_row(s, off, t), n, SUBLANES)
        pad_ref[rows, :] = pad_ref[rows, :] * val[o:o + n]


def _modnorm_chunk(src_ref, h_ref, c, nb, t, gain, shift):
    for s, off, n, o in _pieces(c, nb, t):
        for i in range(0, n, NORM_ROWS):
            x = src_ref[s, _rows(off + i, NORM_ROWS, NORM_ROWS), :]
            ms = jnp.mean(x * x, axis=-1, keepdims=True)
            h_ref[_rows(c * ROW_CHUNK + o + i, NORM_ROWS, NORM_ROWS), :] = (
                x * lax.rsqrt(ms + EPS) * gain + shift).astype(BF16)


def _zero_pads(pad_ref, nb, t):
    z = jnp.zeros((PAD, W_HALF), F32)
    for s in range(nb):
        pad_ref[_pad_row(s, 0, t) - PAD:_pad_row(s, 0, t), :] = z
        pad_ref[_pad_row(s, t, t):_pad_row(s, t, t) + PAD, :] = z


def _pool_phase(pad_ref, ga_ref, wp_ref, ps_ref, ab_ref, nb, t):
    n_rows = POOL_ROWS
    per_seq = t // n_rows

    def step(i, carry):
        s = i // per_seq
        r0 = (i - s * per_seq) * n_rows
        prow = _pad_row(s, r0, t)
        rows = _rows(i * n_rows, n_rows, n_rows)
        pos = r0 + lax.broadcasted_iota(jnp.int32, (n_rows, LANES), 0)
        for g in range(N_POOL_GROUPS):
            hw = POOL_HALF[g]
            ln = _lanes(g)
            halo = n_rows + 2 * SUBLANES
            blk = pad_ref[_rows(prow - SUBLANES, halo, SUBLANES), ln]
            run, n = blk, 1
            while n < 2 * hw:
                run = run + pltpu.roll(run, halo - n, 0)
                n *= 2
            if hw < SUBLANES:
                run = pltpu.roll(run, halo - (SUBLANES - hw), 0)
            win = run[:n_rows]
            cnt = (jnp.minimum(pos + hw, t) - jnp.maximum(pos - hw, 0)).astype(F32)
            p = (win / cnt - blk[SUBLANES:SUBLANES + n_rows]).astype(BF16)
            y = _dot(p, wp_ref[0, g].astype(BF16)) * ps_ref[:, ln] * ga_ref[rows, ln]
            ab_ref[rows, ln] = y.astype(BF16)
        return carry
    lax.fori_loop(0, nb * per_seq, step, 0)


def _out_proj_chunk(ab_ref, w_ref, x_ref, gate, dst_ref, c, nb, t):
    lhs = ab_ref[_rows(c * ROW_CHUNK, ROW_CHUNK, ROW_CHUNK), :]
    for g in range(D_MODEL // W_HALF):
        y = _dot(lhs, w_ref[:, _group(g)])
        for s, off, n, o in _pieces(c, nb, t):
            rows = _rows(off, n, n)
            dst_ref[s, rows, _group(g)] = x_ref[s, rows, _group(g)] + gate[:, _group(g)] * y[o:o + n]


def _conv_phase(pad_c, pad_d, bc_ref, ga_ref, gb_ref, cc_ref, cdw_ref, cdb_ref, lng_ref, lnb_ref,
                ab_ref, nb, t):
    n_rows = CONV_ROWS
    per_seq = t // n_rows

    def step(i, carry):
        s = i // per_seq
        r0 = (i - s * per_seq) * n_rows
        prow = _pad_row(s, r0, t)
        rows = _rows(i * n_rows, n_rows, n_rows)
        z = []
        for g in range(W_HALF // LANES):
            ln = _lanes(g)
            blk = pad_c[_rows(prow - SUBLANES, n_rows + 2 * SUBLANES, SUBLANES), ln]
            c3 = None
            for j in range(CONV_C):
                o = SUBLANES + j - CONV_C // 2
                term = blk[o:o + n_rows] * cc_ref[0, j:j + 1, ln]
                c3 = term if c3 is None else c3 + term
            ab_ref[rows, ln] = (bc_ref[rows, ln] * c3 * ga_ref[rows, ln]).astype(BF16)
            acc = None
            for sft in range(SUBLANES):
                part = None
                for a in range((CONV_D - sft + SUBLANES - 1) // SUBLANES):
                    j = SUBLANES * a + sft
                    src = pad_d[_rows(prow - 2 * SUBLANES + SUBLANES * a, n_rows + SUBLANES,
                                      SUBLANES), ln]
                    term = src * cdw_ref[0, j:j + 1, ln]
                    part = term if part is None else part + term
                o = SUBLANES + sft - (CONV_D // 2 - SUBLANES)
                part = part[o:o + n_rows]
                acc = part if acc is None else acc + part
            z.append(acc + cdb_ref[:, ln])
        z = jnp.concatenate(z, axis=-1)
        mu = jnp.mean(z, axis=-1, keepdims=True)
        zc = z - mu
        var = jnp.mean(zc * zc, axis=-1, keepdims=True)
        zn = zc * lax.rsqrt(var + EPS) * lng_ref[...] + lnb_ref[...]
        ab_ref[rows, W_HALF:] = (_silu(zn) * gb_ref[rows, :]).astype(BF16)
        return carry
    lax.fori_loop(0, nb * per_seq, step, 0)


def _final_norm_chunk(y_ref, fg, c, nb, t):
    for s, off, n, _ in _pieces(c, nb, t):
        for i in range(0, n, NORM_ROWS):
            rows = _rows(off + i, NORM_ROWS, NORM_ROWS)
            x = y_ref[s, rows, :]
            ms = jnp.mean(x * x, axis=-1, keepdims=True)
            y_ref[s, rows, :] = x * lax.rsqrt(ms + EPS) * fg


def _odd_layer(y_ref, m_row, g_row, fg, wio_ref, cc_ref, cdw_ref, cdb_ref, lng_ref, lnb_ref, woo_ref,
               h_ref, pad_c, pad_d, bc_ref, ga_ref, gb_ref, ab_ref, nb, t):
    shift = m_row[:, :D_MODEL]
    gain = g_row * (1.0 + m_row[:, D_MODEL:2 * D_MODEL])
    gate = m_row[:, 2 * D_MODEL:]
    n_chunks = nb * t // ROW_CHUNK

    def in_proj(c):
        _modnorm_chunk(y_ref, h_ref, c, nb, t, gain, shift)
        rows = _rows(c * ROW_CHUNK, ROW_CHUNK, ROW_CHUNK)
        pieces = _pieces(c, nb, t)
        h = h_ref[rows, :]
        bc_ref[rows, :] = _dot(h, wio_ref[:, _group(0)])
        _store_padded(pad_c, _dot(h, wio_ref[:, _group(1)]), pieces, t)
        _scale_padded(pad_c, _dot(h, wio_ref[:, _group(2)]), pieces, t)
        ga_ref[rows, :] = _silu(_dot(h, wio_ref[:, _group(3)]))
        _store_padded(pad_d, _dot(h, wio_ref[:, _group(4)]), pieces, t)
        _scale_padded(pad_d, _sigmoid(_dot(h, wio_ref[:, _group(5)])), pieces, t)
        gb_ref[rows, :] = _silu(_dot(h, wio_ref[:, _group(6)]))
    _for_chunks(n_chunks, in_proj)

    _conv_phase(pad_c, pad_d, bc_ref, ga_ref, gb_ref, cc_ref, cdw_ref, cdb_ref, lng_ref, lnb_ref,
                ab_ref, nb, t)

    def out_proj(c):
        _out_proj_chunk(ab_ref, woo_ref, y_ref, gate, y_ref, c, nb, t)
        _final_norm_chunk(y_ref, fg, c, nb, t)
    _for_chunks(n_chunks, out_proj, unrolled=True)


def _even_in_proj(x_ref, m_row, g_row, w_ref, h_ref, pad_a, ga_ref, gb_ref, q_ref, k_ref, v_ref,
                  kv_t, nb, t):
    shift = m_row[:, :D_MODEL]
    gain = g_row * (1.0 + m_row[:, D_MODEL:2 * D_MODEL])

    def in_proj(c):
        _modnorm_chunk(x_ref, h_ref, c, nb, t, gain, shift)
        rows = _rows(c * ROW_CHUNK, ROW_CHUNK, ROW_CHUNK)
        pieces = _pieces(c, nb, t)
        h = h_ref[rows, :]
        _store_padded(pad_a, _dot(h, w_ref[:, _group(0)]), pieces, t)
        ga_ref[rows, :] = _silu(_dot(h, w_ref[:, _group(1)]))
        q_ref[rows, :] = (_dot(h, w_ref[:, _group(2)]) * Q_SCALE).astype(BF16)
        for i, (dst, g) in enumerate(((k_ref, 3), (v_ref, 4))):
            if kv_t is None:
                dst[rows, :] = _dot(h, w_ref[:, _group(g)]).astype(BF16)
                continue
            acc = _dot_nt(kv_t[2 + i][...], h)
            dst[:, rows] = acc.astype(BF16)
            for s, off, n, o in pieces:
                for hd in range(N_HEADS):
                    kv_t[i][s, 0, hd, :, _rows(off, n, n)] = (
                        acc[hd * HEAD_DIM:(hd + 1) * HEAD_DIM, o:o + n])
        gb_ref[rows, :] = _silu(_dot(h, w_ref[:, _group(5)]))
    _for_chunks(nb * t // ROW_CHUNK, in_proj)


def _even_out_proj(x_ref, y_ref, m_row, w_ref, ab_ref, nb, t):
    gate = m_row[:, 2 * D_MODEL:]
    _for_chunks(nb * t // ROW_CHUNK,
                lambda c: _out_proj_chunk(ab_ref, w_ref, x_ref, gate, y_ref, c, nb, t))


def _split_heads(x):
    lane = lax.broadcasted_iota(jnp.int32, (1, LANES), 1)
    first = jnp.where(lane < HEAD_DIM, 1.0, 0.0).astype(x.dtype)
    return jnp.concatenate([x * first, x * (1 - first)], axis=0)


def _merge_heads(o):
    n = o.shape[0] // 2
    lane = lax.broadcasted_iota(jnp.int32, (n, LANES), 1)
    return jnp.where(lane < HEAD_DIM, o[:n], o[n:])


def _context_attention(q_ref, kt_ref, vt_ref, gb_ref, ab_ref, nb, t):
    for s in range(nb):
        seq = slice(s * t, (s + 1) * t)
        for j in range(N_HEADS // 2):
            ln = _lanes(j)
            kp = kt_ref[ln, seq]
            vp = vt_ref[ln, seq]
            for r0 in range(0, t, Q_ROWS):
                rows = slice(s * t + r0, s * t + r0 + Q_ROWS)
                sc = _dot(_split_heads(q_ref[rows, ln]), kp)
                p = jnp.exp2(sc - jnp.max(sc, axis=-1, keepdims=True))
                o = _dot_nt(p.astype(BF16), vp) / jnp.sum(p, axis=-1, keepdims=True)
                ab_ref[rows, W_HALF + j * LANES:W_HALF + (j + 1) * LANES] = (
                    _merge_heads(o) * gb_ref[rows, ln]).astype(BF16)


def _stage_weights(w_hbm, w_bf, wkt_ref, wvt_ref, stage, sem_in):
    chunks = [(k, r0) for k in range(len(w_hbm)) for r0 in range(0, w_hbm[k].shape[1], STAGE_ROWS)]

    def fetch(i):
        k, r0 = chunks[i]
        cols = w_hbm[k].shape[2]
        slot = i % STAGE_SLOTS
        return pltpu.make_async_copy(w_hbm[k].at[0, pl.ds(r0, STAGE_ROWS), :],
                                     stage.at[slot, :, pl.ds(0, cols)], sem_in.at[slot])

    for i in range(min(STAGE_SLOTS - 1, len(chunks))):
        fetch(i).start()
    for i, (k, r0) in enumerate(chunks):
        if i + STAGE_SLOTS - 1 < len(chunks):
            fetch(i + STAGE_SLOTS - 1).start()
        fetch(i).wait()
        cols = w_hbm[k].shape[2]
        rows = slice(r0, r0 + STAGE_ROWS)
        slot = i % STAGE_SLOTS
        w_bf[k][rows, :] = stage[slot, :, 0:cols].astype(BF16)
        if k == 0:
            wkt_ref[:, rows] = stage[slot, :, _group(3)].T.astype(BF16)
            wvt_ref[:, rows] = stage[slot, :, _group(4)].T.astype(BF16)


def _prompt_body(x_ref, m_ref, ng_ref, fg_ref, wp_ref, ps_ref, cc_ref, cdw_ref, cdb_ref, lng_ref,
                 lnb_ref, wie_hbm, woe_hbm, wio_hbm, woo_hbm,
                 y_ref, ko_ref, vo_ref, wie_out, woe_out, wio_out, woo_out,
                 h_ref, pad_a, pad_b, ga_ref, gb_ref, bc_ref, q_ref, kt_ref, vt_ref, ab_ref,
                 wie_ref, woe_ref, wio_ref, woo_ref, wkt_ref, wvt_ref, stage, sem_in, sem_out,
                 *, nb, t):
    w_out = (wie_out, woe_out, wio_out, woo_out)
    w_bf = (wie_ref, woe_ref, wio_ref, woo_ref)

    def write_back(k):
        return pltpu.make_async_copy(w_bf[k], w_out[k], sem_out.at[k])

    @pl.when(pl.program_id(0) == 0)
    def _():
        _stage_weights((wie_hbm, woe_hbm, wio_hbm, woo_hbm), w_bf, wkt_ref, wvt_ref, stage, sem_in)
        for k in range(len(w_bf)):
            write_back(k).start()

    _zero_pads(pad_a, nb, t)
    _zero_pads(pad_b, nb, t)
    m_even = _cond_row(m_ref, 0, 0)
    _even_in_proj(x_ref, m_even, ng_ref[0:1, :], wie_ref, h_ref, pad_a, ga_ref, gb_ref,
                  q_ref, kt_ref, vt_ref, (ko_ref, vo_ref, wkt_ref, wvt_ref), nb, t)
    _pool_phase(pad_a, ga_ref, wp_ref, ps_ref, ab_ref, nb, t)
    _context_attention(q_ref, kt_ref, vt_ref, gb_ref, ab_ref, nb, t)
    _even_out_proj(x_ref, y_ref, m_even, woe_ref, ab_ref, nb, t)
    _odd_layer(y_ref, _cond_row(m_ref, 1, 0), ng_ref[1:2, :], fg_ref[...], wio_ref, cc_ref, cdw_ref,
               cdb_ref, lng_ref, lnb_ref, woo_ref, h_ref, pad_a, pad_b, bc_ref, ga_ref, gb_ref, ab_ref,
               nb, t)

    @pl.when(pl.program_id(0) == 0)
    def _():
        for k in range(len(w_bf)):
            write_back(k).wait()


def _rpb_rows(rpb_ref, e_ref):
    n = rpb_ref.shape[0]
    lane = lax.broadcasted_iota(jnp.int32, (n, LANES), 1)
    i = jnp.where(lane < GRID_W, lane, lane - LANES)
    idx = jnp.clip(i, -(WIN_W - 1), WIN_W - 1) + (WIN_W - 1)
    rp = rpb_ref[...]
    e = jnp.zeros((n, LANES), F32)
    for d in range(2 * WIN_W - 1):
        e = jnp.where(idx == d, rp[:, d:d + 1], e)
    e_ref[...] = e


N_DR = 2 * WIN_H - 1
PAIR_TILES = N_DR // 2


def _bias_tile_index(j, dr_lo):
    if isinstance(dr_lo, int):
        parity, half = dr_lo % 2, dr_lo // 2
    else:
        parity, half = dr_lo & 1, lax.shift_right_logical(dr_lo, 1)
    return (2 * j + parity) * PAIR_TILES + half


def _bias_tables(e_ref, bias_ref):
    q = lax.broadcasted_iota(jnp.int32, (GRID_W, LANES), 0)
    lane = lax.broadcasted_iota(jnp.int32, (GRID_W, LANES), 1)
    kw = jnp.where(lane < GRID_W, lane, lane - GRID_W)
    start = jnp.clip(q - WIN_W // 2, 0, GRID_W - WIN_W)
    col_ok = (kw >= start) & (kw < start + WIN_W)
    for j in range(N_HEADS // 2):
        for dr in range(N_DR - 1):
            for e in range(2):
                r_lo = (2 * j + e) * N_DR + dr
                lo = jnp.broadcast_to(e_ref[r_lo:r_lo + 1, :], (GRID_W, LANES))
                hi = jnp.broadcast_to(e_ref[r_lo + 1:r_lo + 2, :], (GRID_W, LANES))
                lo = pltpu.roll(lo, 0, 1, stride=1, stride_axis=0)
                hi = pltpu.roll(hi, GRID_W, 1, stride=1, stride_axis=0)
                tile = jnp.where(lane < GRID_W, lo, hi)
                bias_ref[_bias_tile_index(j, dr), e * GRID_W:(e + 1) * GRID_W, :] = jnp.where(
                    col_ok, tile * LOG2_E, MASKED)


def _neighbourhood_attention(q_ref, k_ref, v_ref, ck_ref, cv_ref, bias_ref, kvc_ref, gb_ref, ab_ref, t):
    grid_h = t // GRID_W
    band = WIN_H * GRID_W
    for j in range(N_HEADS // 2):
        ln = _lanes(j)
        for i, src in enumerate((ck_ref, cv_ref)):
            kvc_ref[i] = jnp.concatenate([src[0, 0, 2 * j], src[0, 0, 2 * j + 1]],
                                         axis=0).astype(BF16)

        def per_group(g, carry, ln=ln, j=j):
            scored = []
            for u in range(NA_GROUP):
                r = g * NA_GROUP + u
                start = jnp.clip(r - WIN_H // 2, 0, grid_h - WIN_H)
                rows = _rows(r * GRID_W, GRID_W, GRID_W)
                keys = _rows(start * GRID_W, band, GRID_W)
                q2 = _split_heads(q_ref[rows, ln])
                dr0 = (WIN_H - 1) - (r - start)
                bias = jnp.concatenate([bias_ref[_bias_tile_index(j, dr0 + 2 * i)]
                                        for i in range(WIN_H // 2)], axis=-1)
                scored.append((rows, keys, _dot_nt(q2, k_ref[keys, ln]) + bias, _dot(q2, kvc_ref[0])))
            weighted = []
            for rows, keys, s_loc, s_ctx in scored:
                mx = jnp.maximum(jnp.max(s_loc, axis=-1, keepdims=True),
                                 jnp.max(s_ctx, axis=-1, keepdims=True))
                p_loc = jnp.exp2(s_loc - mx)
                p_ctx = jnp.exp2(s_ctx - mx)
                den = (jnp.sum(p_loc, axis=-1, keepdims=True)
                       + jnp.sum(p_ctx, axis=-1, keepdims=True))
                weighted.append((rows, keys, p_loc.astype(BF16), p_ctx.astype(BF16), den))
            for rows, keys, p_loc, p_ctx, den in weighted:
                o = (_dot(p_loc, v_ref[keys, ln]) + _dot_nt(p_ctx, kvc_ref[1])) / den
                ab_ref[rows, W_HALF + j * LANES:W_HALF + (j + 1) * LANES] = (
                    _merge_heads(o) * gb_ref[rows, ln]).astype(BF16)
            return carry
        lax.fori_loop(0, grid_h // NA_GROUP, per_group, 0)


def _sample_body(x_ref, m_ref, ng_ref, fg_ref, wie_ref, wp_ref, ps_ref, woe_hbm, wio_hbm, cc_ref,
                 cdw_ref, cdb_ref, lng_ref, lnb_ref, woo_hbm, ck_ref, cv_ref, rpb_ref,
                 y_ref,
                 h_ref, pad_a, pad_b, ga_ref, gb_ref, bc_ref, q_ref, k_ref, v_ref, ab_ref,
                 e_ref, bias_ref, kvc_ref, woe_ref, wio_ref, woo_ref, sem_w, *, t):
    _zero_pads(pad_a, 1, t)
    _zero_pads(pad_b, 1, t)
    late = ((woe_hbm, woe_ref), (wio_hbm, wio_ref), (woo_hbm, woo_ref))

    def late_copy(i):
        return pltpu.make_async_copy(late[i][0], late[i][1], sem_w.at[i])

    first_step = pl.program_id(0) == 0

    @pl.when(first_step)
    def _():
        for i in range(len(late)):
            late_copy(i).start()
        _rpb_rows(rpb_ref, e_ref)
        _bias_tables(e_ref, bias_ref)

    cond = pl.program_id(0) + 1
    m_even = _cond_row(m_ref, 0, cond)
    _even_in_proj(x_ref, m_even, ng_ref[0:1, :], wie_ref, h_ref, pad_a, ga_ref, gb_ref,
                  q_ref, k_ref, v_ref, None, 1, t)
    _pool_phase(pad_a, ga_ref, wp_ref, ps_ref, ab_ref, 1, t)
    _neighbourhood_attention(q_ref, k_ref, v_ref, ck_ref, cv_ref, bias_ref, kvc_ref, gb_ref, ab_ref, t)
    pl.when(first_step)(lambda: late_copy(0).wait())
    _even_out_proj(x_ref, y_ref, m_even, woe_ref, ab_ref, 1, t)

    @pl.when(first_step)
    def _():
        late_copy(1).wait()
        late_copy(2).wait()

    _odd_layer(y_ref, _cond_row(m_ref, 1, cond), ng_ref[1:2, :], fg_ref[...], wio_ref, cc_ref, cdw_ref,
               cdb_ref, lng_ref, lnb_ref, woo_ref, h_ref, pad_a, pad_b, bc_ref, ga_ref, gb_ref, ab_ref,
               1, t)


def _const_spec(shape):
    zeros = (0,) * len(shape)
    return pl.BlockSpec(shape, lambda i: zeros, pipeline_mode=pl.Buffered(1))


def _stream_scratch(nb, t, kv_transposed):
    r = nb * t
    padded = nb * (t + 2 * PAD)
    kv = (W_HALF, r) if kv_transposed else (r, W_HALF)
    return [
        pltpu.VMEM((r, D_MODEL), BF16),
        pltpu.VMEM((padded, W_HALF), F32),
        pltpu.VMEM((padded, W_HALF), F32),
        pltpu.VMEM((r, W_HALF), F32),
        pltpu.VMEM((r, W_HALF), F32),
        pltpu.VMEM((r, W_HALF), F32),
        pltpu.VMEM((r, W_HALF), BF16),
        pltpu.VMEM(kv, BF16),
        pltpu.VMEM(kv, BF16),
        pltpu.VMEM((r, D_MODEL), BF16),
    ]


def _small_params(norm_g, final_g, w_pool, pool_scale, conv_c, conv_d, conv_d_b, ln_g, ln_b):
    return [norm_g, final_g.reshape(1, D_MODEL), w_pool, pool_scale, conv_c, conv_d, conv_d_b, ln_g,
            ln_b]


def kernel(x_prompt, x_sample, cache_k, cache_v, c, c_ctx, norm_g, w_mod, b_mod, w_in_even, w_pool,
           pool_scale, rpb, w_out_even, w_in_odd, conv_c, conv_d, conv_d_b, ln_g, ln_b, w_out_odd,
           final_g):
    batch, seq, d = x_prompt.shape
    dec_batch, dec_seq, _ = x_sample.shape
    assert d == D_MODEL and w_mod.shape[0] == 2 and w_in_even.shape[0] == 1 and w_in_odd.shape[0] == 1
    assert (NB_PROMPT * seq) % ROW_CHUNK == 0 and ROW_CHUNK % seq == 0 and seq % Q_ROWS == 0
    assert dec_seq % ROW_CHUNK == 0 and dec_seq // GRID_W >= WIN_H
    assert seq % POOL_ROWS == 0 and seq % CONV_ROWS == 0
    assert dec_seq % POOL_ROWS == 0 and dec_seq % CONV_ROWS == 0
    assert (dec_seq // GRID_W) % NA_GROUP == 0

    cond_rows = SUBLANES * ((1 + dec_batch + SUBLANES - 1) // SUBLANES)
    m = _modulation(c_ctx, c, w_mod, b_mod, cond_rows)
    m_spec = _const_spec(m.shape)

    small = _small_params(norm_g, final_g, w_pool, pool_scale, conv_c, conv_d, conv_d_b, ln_g, ln_b)
    small_specs = [_const_spec(a.shape) for a in small]
    w_f32 = (w_in_even, w_out_even, w_in_odd, w_out_odd)
    assert all(w.shape[0] == 1 and w.shape[1] % STAGE_ROWS == 0 for w in w_f32)
    any_spec = pl.BlockSpec(memory_space=pl.ANY)

    nb = NB_PROMPT
    assert batch % nb == 0
    kv_shape = jax.ShapeDtypeStruct((batch, 1, N_HEADS, HEAD_DIM, seq), F32)
    kv_spec = pl.BlockSpec((nb, 1, N_HEADS, HEAD_DIM, seq), lambda i: (i, 0, 0, 0, 0))
    y_prompt, new_kt, new_vt, wie, woe, wio, woo = pl.pallas_call(
        functools.partial(_prompt_body, nb=nb, t=seq),
        out_shape=(jax.ShapeDtypeStruct(x_prompt.shape, F32), kv_shape, kv_shape)
                  + tuple(jax.ShapeDtypeStruct(w.shape[1:], BF16) for w in w_f32),
        grid=(batch // nb,),
        in_specs=[pl.BlockSpec((nb, seq, d), lambda i: (i, 0, 0)), m_spec] + small_specs
                 + [any_spec] * len(w_f32),
        out_specs=(pl.BlockSpec((nb, seq, d), lambda i: (i, 0, 0)), kv_spec, kv_spec)
                  + (any_spec,) * len(w_f32),
        scratch_shapes=_stream_scratch(nb, seq, True)
                       + [pltpu.VMEM(w.shape[1:], BF16) for w in w_f32] + [
            pltpu.VMEM((W_HALF, d), BF16),
            pltpu.VMEM((W_HALF, d), BF16),
            pltpu.VMEM((STAGE_SLOTS, STAGE_ROWS, max(w.shape[2] for w in w_f32)), F32),
            pltpu.SemaphoreType.DMA((STAGE_SLOTS,)),
            pltpu.SemaphoreType.DMA((len(w_f32),)),
        ],
        compiler_params=pltpu.CompilerParams(dimension_semantics=("arbitrary",),
                                             vmem_limit_bytes=VMEM_LIMIT),
        name="prompt",
    )(x_prompt, m, *small, *w_f32)
    ng, fg, wp, ps, cc, cdw, cdb, lng, lnb = small
    w_args = [ng, fg, wie, wp, ps, woe, wio, cc, cdw, cdb, lng, lnb, woo]
    late_w = (woe, wio, woo)
    w_specs = [any_spec if any(a is w for w in late_w) else _const_spec(a.shape) for a in w_args]

    past = cache_k.shape[3]
    cache_spec = pl.BlockSpec((1, 1, N_HEADS, HEAD_DIM, past), lambda i: (i, 0, 0, 0, 0))
    rpb2 = rpb[0].reshape(N_HEADS * (2 * WIN_H - 1), 2 * WIN_W - 1)
    y_sample = pl.pallas_call(
        functools.partial(_sample_body, t=dec_seq),
        out_shape=jax.ShapeDtypeStruct(x_sample.shape, F32),
        grid=(dec_batch,),
        in_specs=[pl.BlockSpec((1, dec_seq, d), lambda i: (i, 0, 0), pipeline_mode=pl.Buffered(1)),
                  m_spec] + w_specs
                 + [cache_spec, cache_spec, _const_spec(rpb2.shape)],
        out_specs=pl.BlockSpec((1, dec_seq, d), lambda i: (i, 0, 0)),
        scratch_shapes=_stream_scratch(1, dec_seq, False) + [
            pltpu.VMEM(rpb2.shape[:1] + (LANES,), F32),
            pltpu.VMEM((N_HEADS * PAIR_TILES, 2 * GRID_W, LANES), F32),
            pltpu.VMEM((2, LANES, past), BF16),
        ] + [pltpu.VMEM(w.shape, BF16) for w in late_w] + [pltpu.SemaphoreType.DMA((len(late_w),))],
        compiler_params=pltpu.CompilerParams(dimension_semantics=("arbitrary",),
                                             vmem_limit_bytes=VMEM_LIMIT),
        name="sample",
    )(x_sample, m, *w_args, jnp.swapaxes(cache_k, 3, 4), jnp.swapaxes(cache_v, 3, 4), rpb2)

    return (y_prompt, y_sample, jnp.swapaxes(new_kt, 3, 4), jnp.swapaxes(new_vt, 3, 4))
```

```python
import functools

import jax
import jax.numpy as jnp
from jax import lax
from jax.experimental import pallas as pl
from jax.experimental.pallas import tpu as pltpu

F32 = jnp.float32
BF16 = jnp.bfloat16

D_MODEL = 1024
W_HALF = 512
N_POOL_GROUPS = 4
POOL_HALF = (1, 2, 4, 8)
N_HEADS = 8
HEAD_DIM = 64
GRID_W = 64
WIN_H = 8
WIN_W = 16
CONV_C = 3
CONV_D = 31
EPS = 1e-6
MASKED = -1e30
LOG2_E = 1.4426950408889634
Q_SCALE = HEAD_DIM ** -0.5 * LOG2_E

LANES = 128
SUBLANES = 8
PAD = 16
ROW_CHUNK = 512
NORM_ROWS = 32
POOL_ROWS = 256
CONV_ROWS = 128
Q_ROWS = 128
NB_PROMPT = 2
NA_GROUP = 8
MOD_COLS = 1536
STAGE_ROWS = 128
STAGE_SLOTS = 4
VMEM_LIMIT = 58 * 1024 * 1024

assert PAD >= CONV_D // 2 + 1 and PAD % SUBLANES == 0 and PAD >= 2 * SUBLANES
assert max(POOL_HALF) <= SUBLANES


def _sigmoid(x):
    return 1.0 / (1.0 + jnp.exp(-x))


def _silu(x):
    return x * _sigmoid(x)


def _dot(a, b):
    return jnp.dot(a, b, preferred_element_type=F32)


def _dot_nt(a, b):
    return lax.dot_general(a, b, (((1,), (1,)), ((), ())), preferred_element_type=F32)


def _lanes(j):
    return slice(j * LANES, (j + 1) * LANES)


def _group(g):
    return slice(g * W_HALF, (g + 1) * W_HALF)


def _rows(start, size, align):
    if isinstance(start, int):
        return slice(start, start + size)
    return pl.ds(pl.multiple_of(start, align), size)


def _mod_body(cctx_ref, c_ref, w_ref, b_ref, o_ref):
    rows, d = o_ref.shape[1], cctx_ref.shape[1]
    r = lax.broadcasted_iota(jnp.int32, (rows, d), 0)
    cond = jnp.where(r == 0, cctx_ref[...], 0.0)
    for i in range(c_ref.shape[0]):
        cond = jnp.where(r == i + 1, c_ref[i:i + 1, :], cond)
    bias = jnp.where(pl.program_id(0) == 0, b_ref[0:1, :], b_ref[1:2, :])
    o_ref[0] = _dot(_silu(cond).astype(BF16), w_ref[0].astype(BF16)) + bias


def _modulation(c_ctx, c, w_mod, b_mod, rows):
    depth, d, n = w_mod.shape
    assert depth == 2 and 1 + c.shape[0] <= rows
    return pl.pallas_call(
        _mod_body,
        out_shape=jax.ShapeDtypeStruct((depth, rows, n), F32),
        grid=(depth, n // MOD_COLS),
        in_specs=[
            pl.BlockSpec((1, d), lambda l, j: (0, 0)),
            pl.BlockSpec(c.shape, lambda l, j: (0, 0)),
            pl.BlockSpec((1, d, MOD_COLS), lambda l, j: (l, 0, j)),
            pl.BlockSpec((depth, MOD_COLS), lambda l, j: (0, j)),
        ],
        out_specs=pl.BlockSpec((1, rows, MOD_COLS), lambda l, j: (l, 0, j)),
        compiler_params=pltpu.CompilerParams(dimension_semantics=("arbitrary", "arbitrary")),
        name="mod",
    )(c_ctx.reshape(1, d), c, w_mod, b_mod)


def _cond_row(m_ref, layer, row):
    if isinstance(row, int):
        return m_ref[layer, row:row + 1, :]
    m = m_ref[layer]
    keep = lax.broadcasted_iota(jnp.int32, m.shape, 0) == row
    return jnp.sum(jnp.where(keep, m, 0.0), axis=0, keepdims=True)


def _pieces(c, nb, t):
    if t >= ROW_CHUNK:
        per_seq = t // ROW_CHUNK
        s = 0 if nb == 1 else c // per_seq
        return [(s, (c - s * per_seq) * ROW_CHUNK, ROW_CHUNK, 0)]
    per_chunk = ROW_CHUNK // t
    return [(c * per_chunk + i, 0, t, i * t) for i in range(per_chunk)]


def _for_chunks(n, body, unrolled=False):
    if unrolled or n == 1:
        for c in range(n):
            body(c)
    else:
        lax.fori_loop(0, n, lambda c, carry: (body(c), carry)[1], 0)


def _pad_row(s, off, t):
    return s * (t + 2 * PAD) + PAD + off


def _store_padded(pad_ref, val, pieces, t):
    for s, off, n, o in pieces:
        pad_ref[_rows(_pad_row(s, off, t), n, SUBLANES), :] = val[o:o + n]


def _scale_padded(pad_ref, val, pieces, t):
    for s, off, n, o in pieces:
        rows = _rows(_pad_row(s, off, t), n, SUBLANES)
        pad_ref[rows, :] = pad_ref[rows, :] * val[o:o + n]


def _modnorm_chunk(src_ref, h_ref, c, nb, t, gain, shift):
    for s, off, n, o in _pieces(c, nb, t):
        for i in range(0, n, NORM_ROWS):
            x = src_ref[s, _rows(off + i, NORM_ROWS, NORM_ROWS), :]
            ms = jnp.mean(x * x, axis=-1, keepdims=True)
            h_ref[_rows(c * ROW_CHUNK + o + i, NORM_ROWS, NORM_ROWS), :] = (
                x * lax.rsqrt(ms + EPS) * gain + shift).astype(BF16)


def _zero_pads(pad_ref, nb, t):
    z = jnp.zeros((PAD, W_HALF), F32)
    for s in range(nb):
        pad_ref[_pad_row(s, 0, t) - PAD:_pad_row(s, 0, t), :] = z
        pad_ref[_pad_row(s, t, t):_pad_row(s, t, t) + PAD, :] = z


def _pool_phase(pad_ref, ga_ref, wp_ref, ps_ref, ab_ref, nb, t):
    n_rows = POOL_ROWS
    per_seq = t // n_rows

    def step(i, carry):
        s = i // per_seq
        r0 = (i - s * per_seq) * n_rows
        prow = _pad_row(s, r0, t)
        rows = _rows(i * n_rows, n_rows, n_rows)
        pos = r0 + lax.broadcasted_iota(jnp.int32, (n_rows, LANES), 0)
        before = jnp.minimum(pos, SUBLANES)
        after = jnp.minimum(t - pos, SUBLANES)
        for g in range(N_POOL_GROUPS):
            hw = POOL_HALF[g]
            ln = _lanes(g)
            halo = n_rows + 2 * SUBLANES
            blk = pad_ref[_rows(prow - SUBLANES, halo, SUBLANES), ln]
            run, n = blk, 1
            while n < 2 * hw:
                run = run + pltpu.roll(run, halo - n, 0)
                n *= 2
            if hw < SUBLANES:
                run = pltpu.roll(run, halo - (SUBLANES - hw), 0)
            win = run[:n_rows]
            cnt = (jnp.minimum(before, hw) + jnp.minimum(after, hw)).astype(F32)
            p = (win / cnt - blk[SUBLANES:SUBLANES + n_rows]).astype(BF16)
            y = _dot(p, wp_ref[0, g].astype(BF16)) * ps_ref[:, ln] * ga_ref[rows, ln]
            ab_ref[rows, ln] = y.astype(BF16)
        return carry
    lax.fori_loop(0, nb * per_seq, step, 0)


def _out_proj_chunk(ab_ref, w_ref, x_ref, gate, dst_ref, c, nb, t):
    lhs = ab_ref[_rows(c * ROW_CHUNK, ROW_CHUNK, ROW_CHUNK), :]
    for g in range(D_MODEL // W_HALF):
        y = _dot(lhs, w_ref[:, _group(g)])
        for s, off, n, o in _pieces(c, nb, t):
            rows = _rows(off, n, n)
            dst_ref[s, rows, _group(g)] = x_ref[s, rows, _group(g)] + gate[:, _group(g)] * y[o:o + n]


def _shift_up(x, o, n):
    if o % SUBLANES == 0:
        return x[o:o + n]
    return pltpu.roll(x, x.shape[0] - o, 0)[:n]


def _conv_phase(pad_c, pad_d, bc_ref, ga_ref, gb_ref, cc_ref, cdw_ref, cdb_ref, lng_ref, lnb_ref,
                ab_ref, nb, t):
    n_rows = CONV_ROWS
    per_seq = t // n_rows

    def step(i, carry):
        s = i // per_seq
        r0 = (i - s * per_seq) * n_rows
        prow = _pad_row(s, r0, t)
        rows = _rows(i * n_rows, n_rows, n_rows)
        z = []
        for g in range(W_HALF // LANES):
            ln = _lanes(g)
            blk = pad_c[_rows(prow - SUBLANES, n_rows + 2 * SUBLANES, SUBLANES), ln]
            c3 = None
            for j in range(CONV_C):
                o = SUBLANES + j - CONV_C // 2
                term = _shift_up(blk, o, n_rows) * cc_ref[0, j:j + 1, ln]
                c3 = term if c3 is None else c3 + term
            ab_ref[rows, ln] = (bc_ref[rows, ln] * c3 * ga_ref[rows, ln]).astype(BF16)
            acc = None
            for sft in range(SUBLANES):
                part = None
                for a in range((CONV_D - sft + SUBLANES - 1) // SUBLANES):
                    j = SUBLANES * a + sft
                    src = pad_d[_rows(prow - 2 * SUBLANES + SUBLANES * a, n_rows + SUBLANES,
                                      SUBLANES), ln]
                    term = src * cdw_ref[0, j:j + 1, ln]
                    part = term if part is None else part + term
                o = SUBLANES + sft - (CONV_D // 2 - SUBLANES)
                part = _shift_up(part, o, n_rows)
                acc = part if acc is None else acc + part
            z.append(acc + cdb_ref[:, ln])
        z = jnp.concatenate(z, axis=-1)
        mu = jnp.mean(z, axis=-1, keepdims=True)
        zc = z - mu
        var = jnp.mean(zc * zc, axis=-1, keepdims=True)
        zn = zc * lax.rsqrt(var + EPS) * lng_ref[...] + lnb_ref[...]
        ab_ref[rows, W_HALF:] = (_silu(zn) * gb_ref[rows, :]).astype(BF16)
        return carry
    lax.fori_loop(0, nb * per_seq, step, 0)


def _final_norm_chunk(y_ref, fg, c, nb, t):
    for s, off, n, _ in _pieces(c, nb, t):
        for i in range(0, n, NORM_ROWS):
            rows = _rows(off + i, NORM_ROWS, NORM_ROWS)
            x = y_ref[s, rows, :]
            ms = jnp.mean(x * x, axis=-1, keepdims=True)
            y_ref[s, rows, :] = x * lax.rsqrt(ms + EPS) * fg


def _odd_layer(y_ref, m_row, g_row, fg, wio_ref, cc_ref, cdw_ref, cdb_ref, lng_ref, lnb_ref, woo_ref,
               h_ref, pad_c, pad_d, bc_ref, ga_ref, gb_ref, ab_ref, nb, t, between_phases=(None, None)):
    shift = m_row[:, :D_MODEL]
    gain = g_row * (1.0 + m_row[:, D_MODEL:2 * D_MODEL])
    gate = m_row[:, 2 * D_MODEL:]
    n_chunks = nb * t // ROW_CHUNK

    def in_proj(c):
        _modnorm_chunk(y_ref, h_ref, c, nb, t, gain, shift)
        rows = _rows(c * ROW_CHUNK, ROW_CHUNK, ROW_CHUNK)
        pieces = _pieces(c, nb, t)
        h = h_ref[rows, :]
        bc_ref[rows, :] = _dot(h, wio_ref[:, _group(0)])
        _store_padded(pad_c, _dot(h, wio_ref[:, _group(1)]), pieces, t)
        _scale_padded(pad_c, _dot(h, wio_ref[:, _group(2)]), pieces, t)
        ga_ref[rows, :] = _silu(_dot(h, wio_ref[:, _group(3)]))
        _store_padded(pad_d, _dot(h, wio_ref[:, _group(4)]), pieces, t)
        _scale_padded(pad_d, _sigmoid(_dot(h, wio_ref[:, _group(5)])), pieces, t)
        gb_ref[rows, :] = _silu(_dot(h, wio_ref[:, _group(6)]))
    _for_chunks(n_chunks, in_proj)
    if between_phases[0] is not None:
        between_phases[0]()

    _conv_phase(pad_c, pad_d, bc_ref, ga_ref, gb_ref, cc_ref, cdw_ref, cdb_ref, lng_ref, lnb_ref,
                ab_ref, nb, t)
    if between_phases[1] is not None:
        between_phases[1]()

    def out_proj(c):
        _out_proj_chunk(ab_ref, woo_ref, y_ref, gate, y_ref, c, nb, t)
        _final_norm_chunk(y_ref, fg, c, nb, t)
    _for_chunks(n_chunks, out_proj, unrolled=True)


def _even_in_proj(x_ref, m_row, g_row, w_ref, h_ref, pad_a, ga_ref, gb_ref, q_ref, k_ref, v_ref,
                  kv_t, nb, t):
    shift = m_row[:, :D_MODEL]
    gain = g_row * (1.0 + m_row[:, D_MODEL:2 * D_MODEL])

    def in_proj(c):
        _modnorm_chunk(x_ref, h_ref, c, nb, t, gain, shift)
        rows = _rows(c * ROW_CHUNK, ROW_CHUNK, ROW_CHUNK)
        pieces = _pieces(c, nb, t)
        h = h_ref[rows, :]
        _store_padded(pad_a, _dot(h, w_ref[:, _group(0)]), pieces, t)
        ga_ref[rows, :] = _silu(_dot(h, w_ref[:, _group(1)]))
        q_ref[rows, :] = (_dot(h, w_ref[:, _group(2)]) * Q_SCALE).astype(BF16)
        for i, (dst, g) in enumerate(((k_ref, 3), (v_ref, 4))):
            if kv_t is None:
                dst[rows, :] = _dot(h, w_ref[:, _group(g)]).astype(BF16)
                continue
            acc = _dot_nt(kv_t[2 + i][...], h)
            dst[:, rows] = acc.astype(BF16)
            for s, off, n, o in pieces:
                for hd in range(N_HEADS):
                    kv_t[i][s, 0, hd, :, _rows(off, n, n)] = (
                        acc[hd * HEAD_DIM:(hd + 1) * HEAD_DIM, o:o + n])
        gb_ref[rows, :] = _silu(_dot(h, w_ref[:, _group(5)]))
    _for_chunks(nb * t // ROW_CHUNK, in_proj)


def _even_out_proj(x_ref, y_ref, m_row, w_ref, ab_ref, nb, t):
    gate = m_row[:, 2 * D_MODEL:]
    _for_chunks(nb * t // ROW_CHUNK,
                lambda c: _out_proj_chunk(ab_ref, w_ref, x_ref, gate, y_ref, c, nb, t))


def _split_heads(x):
    lane = lax.broadcasted_iota(jnp.int32, (1, LANES), 1)
    first = jnp.where(lane < HEAD_DIM, 1.0, 0.0).astype(x.dtype)
    return jnp.concatenate([x * first, x * (1 - first)], axis=0)


def _merge_heads(o):
    n = o.shape[0] // 2
    lane = lax.broadcasted_iota(jnp.int32, (n, LANES), 1)
    return jnp.where(lane < HEAD_DIM, o[:n], o[n:])


def _context_attention(q_ref, kt_ref, vt_ref, gb_ref, ab_ref, nb, t):
    for s in range(nb):
        seq = slice(s * t, (s + 1) * t)
        for j in range(N_HEADS // 2):
            ln = _lanes(j)
            kp = kt_ref[ln, seq]
            vp = vt_ref[ln, seq]
            for r0 in range(0, t, Q_ROWS):
                rows = slice(s * t + r0, s * t + r0 + Q_ROWS)
                sc = _dot(_split_heads(q_ref[rows, ln]), kp)
                p = jnp.exp2(sc - jnp.max(sc, axis=-1, keepdims=True))
                o = _dot_nt(p.astype(BF16), vp) / jnp.sum(p, axis=-1, keepdims=True)
                ab_ref[rows, W_HALF + j * LANES:W_HALF + (j + 1) * LANES] = (
                    _merge_heads(o) * gb_ref[rows, ln]).astype(BF16)


def _weight_stager(w_hbm, w_bf, wkt_ref, wvt_ref, stage, sem_in):
    def chunks_of(k):
        return [(k, r0) for r0 in range(0, w_hbm[k].shape[1], STAGE_ROWS)]

    def fetch(k, r0, slot):
        cols = w_hbm[k].shape[2]
        return pltpu.make_async_copy(w_hbm[k].at[0, pl.ds(r0, STAGE_ROWS), :],
                                     stage.at[slot, :, pl.ds(0, cols)], sem_in.at[slot])

    def cast(k, r0, slot):
        cols = w_hbm[k].shape[2]
        rows = slice(r0, r0 + STAGE_ROWS)
        w_bf[k][rows, :] = stage[slot, :, 0:cols].astype(BF16)
        if k == 0:
            wkt_ref[:, rows] = stage[slot, :, _group(3)].T.astype(BF16)
            wvt_ref[:, rows] = stage[slot, :, _group(4)].T.astype(BF16)

    def stream(k):
        chunks = chunks_of(k)
        ahead = STAGE_SLOTS - 1
        for i in range(min(ahead, len(chunks))):
            fetch(*chunks[i], i % STAGE_SLOTS).start()
        for i, c in enumerate(chunks):
            if i + ahead < len(chunks):
                fetch(*chunks[i + ahead], (i + ahead) % STAGE_SLOTS).start()
            fetch(*c, i % STAGE_SLOTS).wait()
            cast(*c, i % STAGE_SLOTS)

    later = [c for k in range(1, len(w_hbm)) for c in chunks_of(k)]
    rounds = [later[i:i + STAGE_SLOTS] for i in range(0, len(later), STAGE_SLOTS)]

    def issue(r):
        for slot, c in enumerate(rounds[r]):
            fetch(*c, slot).start()

    def finish(r):
        for slot, c in enumerate(rounds[r]):
            fetch(*c, slot).wait()
            cast(*c, slot)
        last = {k: max(i for i, rnd in enumerate(rounds) if any(c[0] == k for c in rnd))
                for k in range(1, len(w_hbm))}
        return [k for k, i in last.items() if i == r]

    return stream, issue, finish, rounds


def _prompt_body(x_ref, m_ref, ng_ref, fg_ref, wp_ref, ps_ref, cc_ref, cdw_ref, cdb_ref, lng_ref,
                 lnb_ref, wie_hbm, woe_hbm, wio_hbm, woo_hbm,
                 y_ref, ko_ref, vo_ref, wie_out, woe_out, wio_out, woo_out,
                 h_ref, pad_a, pad_b, ga_ref, gb_ref, bc_ref, q_ref, kt_ref, vt_ref, ab_ref,
                 wie_ref, woe_ref, wio_ref, woo_ref, wkt_ref, wvt_ref, stage, sem_in, sem_out,
                 *, nb, t):
    w_out = (wie_out, woe_out, wio_out, woo_out)
    w_bf = (wie_ref, woe_ref, wio_ref, woo_ref)
    stream, issue, finish, rounds = _weight_stager((wie_hbm, woe_hbm, wio_hbm, woo_hbm), w_bf,
                                                   wkt_ref, wvt_ref, stage, sem_in)
    assert len(rounds) == 6
    assert all(k == 1 for k, _ in rounds[0] + rounds[1])
    assert all(k <= 2 for rnd in rounds[:4] for k, _ in rnd)
    assert all(k == 3 for k, _ in rounds[4] + rounds[5])

    def write_back(k):
        return pltpu.make_async_copy(w_bf[k], w_out[k], sem_out.at[k])

    def at_first_step(fn):
        pl.when(pl.program_id(0) == 0)(fn)

    def turn(r):
        def fn():
            for k in finish(r):
                write_back(k).start()
            if r + 1 < len(rounds):
                issue(r + 1)
        return lambda: at_first_step(fn)

    def first_weight():
        stream(0)
        write_back(0).start()
        issue(0)
    at_first_step(first_weight)

    _zero_pads(pad_a, nb, t)
    _zero_pads(pad_b, nb, t)
    m_even = _cond_row(m_ref, 0, 0)
    _even_in_proj(x_ref, m_even, ng_ref[0:1, :], wie_ref, h_ref, pad_a, ga_ref, gb_ref,
                  q_ref, kt_ref, vt_ref, (ko_ref, vo_ref, wkt_ref, wvt_ref), nb, t)
    turn(0)()
    _pool_phase(pad_a, ga_ref, wp_ref, ps_ref, ab_ref, nb, t)
    turn(1)()
    _context_attention(q_ref, kt_ref, vt_ref, gb_ref, ab_ref, nb, t)
    turn(2)()
    _even_out_proj(x_ref, y_ref, m_even, woe_ref, ab_ref, nb, t)
    turn(3)()
    _odd_layer(y_ref, _cond_row(m_ref, 1, 0), ng_ref[1:2, :], fg_ref[...], wio_ref, cc_ref, cdw_ref,
               cdb_ref, lng_ref, lnb_ref, woo_ref, h_ref, pad_a, pad_b, bc_ref, ga_ref, gb_ref, ab_ref,
               nb, t, between_phases=(turn(4), turn(5)))

    def wait_write_backs():
        for k in range(len(w_bf)):
            write_back(k).wait()
    at_first_step(wait_write_backs)


def _rpb_rows(rpb_ref, e_ref):
    n = rpb_ref.shape[0]
    lane = lax.broadcasted_iota(jnp.int32, (n, LANES), 1)
    i = jnp.where(lane < GRID_W, lane, lane - LANES)
    idx = jnp.clip(i, -(WIN_W - 1), WIN_W - 1) + (WIN_W - 1)
    rp = rpb_ref[...]
    e = jnp.zeros((n, LANES), F32)
    for d in range(2 * WIN_W - 1):
        e = jnp.where(idx == d, rp[:, d:d + 1], e)
    e_ref[...] = e


N_DR = 2 * WIN_H - 1
PAIR_TILES = N_DR // 2


def _bias_tile_index(j, dr_lo):
    if isinstance(dr_lo, int):
        parity, half = dr_lo % 2, dr_lo // 2
    else:
        parity, half = dr_lo & 1, lax.shift_right_logical(dr_lo, 1)
    return (2 * j + parity) * PAIR_TILES + half


def _bias_tables(e_ref, bias_ref):
    q = lax.broadcasted_iota(jnp.int32, (GRID_W, LANES), 0)
    lane = lax.broadcasted_iota(jnp.int32, (GRID_W, LANES), 1)
    kw = jnp.where(lane < GRID_W, lane, lane - GRID_W)
    start = jnp.clip(q - WIN_W // 2, 0, GRID_W - WIN_W)
    col_ok = (kw >= start) & (kw < start + WIN_W)
    for j in range(N_HEADS // 2):
        for dr in range(N_DR - 1):
            for e in range(2):
                r_lo = (2 * j + e) * N_DR + dr
                lo = jnp.broadcast_to(e_ref[r_lo:r_lo + 1, :], (GRID_W, LANES))
                hi = jnp.broadcast_to(e_ref[r_lo + 1:r_lo + 2, :], (GRID_W, LANES))
                lo = pltpu.roll(lo, 0, 1, stride=1, stride_axis=0)
                hi = pltpu.roll(hi, GRID_W, 1, stride=1, stride_axis=0)
                tile = jnp.where(lane < GRID_W, lo, hi)
                bias_ref[_bias_tile_index(j, dr), e * GRID_W:(e + 1) * GRID_W, :] = jnp.where(
                    col_ok, tile * LOG2_E, MASKED)


def _neighbourhood_attention(q_ref, k_ref, v_ref, ck_ref, cv_ref, bias_ref, kvc_ref, gb_ref, ab_ref, t):
    grid_h = t // GRID_W
    band = WIN_H * GRID_W
    for j in range(N_HEADS // 2):
        ln = _lanes(j)
        for i, src in enumerate((ck_ref, cv_ref)):
            kvc_ref[i] = jnp.concatenate([src[0, 0, 2 * j], src[0, 0, 2 * j + 1]],
                                         axis=0).astype(BF16)

        def per_group(g, carry, ln=ln, j=j):
            scored = []
            for u in range(NA_GROUP):
                r = g * NA_GROUP + u
                start = jnp.clip(r - WIN_H // 2, 0, grid_h - WIN_H)
                rows = _rows(r * GRID_W, GRID_W, GRID_W)
                keys = _rows(start * GRID_W, band, GRID_W)
                q2 = _split_heads(q_ref[rows, ln])
                dr0 = (WIN_H - 1) - (r - start)
                bias = jnp.concatenate([bias_ref[_bias_tile_index(j, dr0 + 2 * i)]
                                        for i in range(WIN_H // 2)], axis=-1)
                scored.append((rows, keys, _dot_nt(q2, k_ref[keys, ln]) + bias, _dot(q2, kvc_ref[0])))
            weighted = []
            for rows, keys, s_loc, s_ctx in scored:
                mx = jnp.maximum(jnp.max(s_loc, axis=-1, keepdims=True),
                                 jnp.max(s_ctx, axis=-1, keepdims=True))
                p_loc = jnp.exp2(s_loc - mx)
                p_ctx = jnp.exp2(s_ctx - mx)
                den = (jnp.sum(p_loc, axis=-1, keepdims=True)
                       + jnp.sum(p_ctx, axis=-1, keepdims=True))
                weighted.append((rows, keys, p_loc.astype(BF16), p_ctx.astype(BF16), den))
            for rows, keys, p_loc, p_ctx, den in weighted:
                o = (_dot(p_loc, v_ref[keys, ln]) + _dot_nt(p_ctx, kvc_ref[1])) / den
                ab_ref[rows, W_HALF + j * LANES:W_HALF + (j + 1) * LANES] = (
                    _merge_heads(o) * gb_ref[rows, ln]).astype(BF16)
            return carry
        lax.fori_loop(0, grid_h // NA_GROUP, per_group, 0)


def _sample_body(x_ref, m_ref, ng_ref, fg_ref, wie_ref, wp_ref, ps_ref, woe_hbm, wio_hbm, cc_ref,
                 cdw_ref, cdb_ref, lng_ref, lnb_ref, woo_hbm, ck_ref, cv_ref, rpb_ref,
                 y_ref,
                 h_ref, pad_a, pad_b, ga_ref, gb_ref, bc_ref, q_ref, k_ref, v_ref, ab_ref,
                 e_ref, bias_ref, kvc_ref, woe_ref, wio_ref, woo_ref, sem_w, *, t):
    _zero_pads(pad_a, 1, t)
    _zero_pads(pad_b, 1, t)
    late = ((woe_hbm, woe_ref), (wio_hbm, wio_ref), (woo_hbm, woo_ref))

    def late_copy(i):
        return pltpu.make_async_copy(late[i][0], late[i][1], sem_w.at[i])

    first_step = pl.program_id(0) == 0

    @pl.when(first_step)
    def _():
        for i in range(len(late)):
            late_copy(i).start()
        _rpb_rows(rpb_ref, e_ref)
        _bias_tables(e_ref, bias_ref)

    cond = pl.program_id(0) + 1
    m_even = _cond_row(m_ref, 0, cond)
    _even_in_proj(x_ref, m_even, ng_ref[0:1, :], wie_ref, h_ref, pad_a, ga_ref, gb_ref,
                  q_ref, k_ref, v_ref, None, 1, t)
    _pool_phase(pad_a, ga_ref, wp_ref, ps_ref, ab_ref, 1, t)
    _neighbourhood_attention(q_ref, k_ref, v_ref, ck_ref, cv_ref, bias_ref, kvc_ref, gb_ref, ab_ref, t)
    pl.when(first_step)(lambda: late_copy(0).wait())
    _even_out_proj(x_ref, y_ref, m_even, woe_ref, ab_ref, 1, t)

    @pl.when(first_step)
    def _():
        late_copy(1).wait()
        late_copy(2).wait()

    _odd_layer(y_ref, _cond_row(m_ref, 1, cond), ng_ref[1:2, :], fg_ref[...], wio_ref, cc_ref, cdw_ref,
               cdb_ref, lng_ref, lnb_ref, woo_ref, h_ref, pad_a, pad_b, bc_ref, ga_ref, gb_ref, ab_ref,
               1, t)


def _const_spec(shape):
    zeros = (0,) * len(shape)
    return pl.BlockSpec(shape, lambda i: zeros, pipeline_mode=pl.Buffered(1))


def _stream_scratch(nb, t, kv_transposed):
    r = nb * t
    padded = nb * (t + 2 * PAD)
    kv = (W_HALF, r) if kv_transposed else (r, W_HALF)
    return [
        pltpu.VMEM((r, D_MODEL), BF16),
        pltpu.VMEM((padded, W_HALF), F32),
        pltpu.VMEM((padded, W_HALF), F32),
        pltpu.VMEM((r, W_HALF), F32),
        pltpu.VMEM((r, W_HALF), F32),
        pltpu.VMEM((r, W_HALF), F32),
        pltpu.VMEM((r, W_HALF), BF16),
        pltpu.VMEM(kv, BF16),
        pltpu.VMEM(kv, BF16),
        pltpu.VMEM((r, D_MODEL), BF16),
    ]


def _small_params(norm_g, final_g, w_pool, pool_scale, conv_c, conv_d, conv_d_b, ln_g, ln_b):
    return [norm_g, final_g.reshape(1, D_MODEL), w_pool, pool_scale, conv_c, conv_d, conv_d_b, ln_g,
            ln_b]


def kernel(x_prompt, x_sample, cache_k, cache_v, c, c_ctx, norm_g, w_mod, b_mod, w_in_even, w_pool,
           pool_scale, rpb, w_out_even, w_in_odd, conv_c, conv_d, conv_d_b, ln_g, ln_b, w_out_odd,
           final_g):
    batch, seq, d = x_prompt.shape
    dec_batch, dec_seq, _ = x_sample.shape
    assert d == D_MODEL and w_mod.shape[0] == 2 and w_in_even.shape[0] == 1 and w_in_odd.shape[0] == 1
    assert (NB_PROMPT * seq) % ROW_CHUNK == 0 and ROW_CHUNK % seq == 0 and seq % Q_ROWS == 0
    assert dec_seq % ROW_CHUNK == 0 and dec_seq // GRID_W >= WIN_H
    assert seq % POOL_ROWS == 0 and seq % CONV_ROWS == 0
    assert dec_seq % POOL_ROWS == 0 and dec_seq % CONV_ROWS == 0
    assert (dec_seq // GRID_W) % NA_GROUP == 0

    cond_rows = SUBLANES * ((1 + dec_batch + SUBLANES - 1) // SUBLANES)
    m = _modulation(c_ctx, c, w_mod, b_mod, cond_rows)
    m_spec = _const_spec(m.shape)

    small = _small_params(norm_g, final_g, w_pool, pool_scale, conv_c, conv_d, conv_d_b, ln_g, ln_b)
    small_specs = [_const_spec(a.shape) for a in small]
    w_f32 = (w_in_even, w_out_even, w_in_odd, w_out_odd)
    assert all(w.shape[0] == 1 and w.shape[1] % STAGE_ROWS == 0 for w in w_f32)
    any_spec = pl.BlockSpec(memory_space=pl.ANY)

    nb = NB_PROMPT
    assert batch % nb == 0
    kv_shape = jax.ShapeDtypeStruct((batch, 1, N_HEADS, HEAD_DIM, seq), F32)
    kv_spec = pl.BlockSpec((nb, 1, N_HEADS, HEAD_DIM, seq), lambda i: (i, 0, 0, 0, 0))
    y_prompt, new_kt, new_vt, wie, woe, wio, woo = pl.pallas_call(
        functools.partial(_prompt_body, nb=nb, t=seq),
        out_shape=(jax.ShapeDtypeStruct(x_prompt.shape, F32), kv_shape, kv_shape)
                  + tuple(jax.ShapeDtypeStruct(w.shape[1:], BF16) for w in w_f32),
        grid=(batch // nb,),
        in_specs=[pl.BlockSpec((nb, seq, d), lambda i: (i, 0, 0)), m_spec] + small_specs
                 + [any_spec] * len(w_f32),
        out_specs=(pl.BlockSpec((nb, seq, d), lambda i: (i, 0, 0)), kv_spec, kv_spec)
                  + (any_spec,) * len(w_f32),
        scratch_shapes=_stream_scratch(nb, seq, True)
                       + [pltpu.VMEM(w.shape[1:], BF16) for w in w_f32] + [
            pltpu.VMEM((W_HALF, d), BF16),
            pltpu.VMEM((W_HALF, d), BF16),
            pltpu.VMEM((STAGE_SLOTS, STAGE_ROWS, max(w.shape[2] for w in w_f32)), F32),
            pltpu.SemaphoreType.DMA((STAGE_SLOTS,)),
            pltpu.SemaphoreType.DMA((len(w_f32),)),
        ],
        compiler_params=pltpu.CompilerParams(dimension_semantics=("arbitrary",),
                                             vmem_limit_bytes=VMEM_LIMIT),
        name="prompt",
    )(x_prompt, m, *small, *w_f32)
    ng, fg, wp, ps, cc, cdw, cdb, lng, lnb = small
    w_args = [ng, fg, wie, wp, ps, woe, wio, cc, cdw, cdb, lng, lnb, woo]
    late_w = (woe, wio, woo)
    w_specs = [any_spec if any(a is w for w in late_w) else _const_spec(a.shape) for a in w_args]

    past = cache_k.shape[3]
    cache_spec = pl.BlockSpec((1, 1, N_HEADS, HEAD_DIM, past), lambda i: (i, 0, 0, 0, 0))
    rpb2 = rpb[0].reshape(N_HEADS * (2 * WIN_H - 1), 2 * WIN_W - 1)
    y_sample = pl.pallas_call(
        functools.partial(_sample_body, t=dec_seq),
        out_shape=jax.ShapeDtypeStruct(x_sample.shape, F32),
        grid=(dec_batch,),
        in_specs=[pl.BlockSpec((1, dec_seq, d), lambda i: (i, 0, 0), pipeline_mode=pl.Buffered(1)),
                  m_spec] + w_specs
                 + [cache_spec, cache_spec, _const_spec(rpb2.shape)],
        out_specs=pl.BlockSpec((1, dec_seq, d), lambda i: (i, 0, 0)),
        scratch_shapes=_stream_scratch(1, dec_seq, False) + [
            pltpu.VMEM(rpb2.shape[:1] + (LANES,), F32),
            pltpu.VMEM((N_HEADS * PAIR_TILES, 2 * GRID_W, LANES), F32),
            pltpu.VMEM((2, LANES, past), BF16),
        ] + [pltpu.VMEM(w.shape, BF16) for w in late_w] + [pltpu.SemaphoreType.DMA((len(late_w),))],
        compiler_params=pltpu.CompilerParams(dimension_semantics=("arbitrary",),
                                             vmem_limit_bytes=VMEM_LIMIT),
        name="sample",
    )(x_sample, m, *w_args, jnp.swapaxes(cache_k, 3, 4), jnp.swapaxes(cache_v, 3, 4), rpb2)

    return (y_prompt, y_sample, jnp.swapaxes(new_kt, 3, 4), jnp.swapaxes(new_vt, 3, 4))
```

```python
import functools

import jax
import jax.numpy as jnp
from jax import lax
from jax.experimental import pallas as pl
from jax.experimental.pallas import tpu as pltpu

F32 = jnp.float32
BF16 = jnp.bfloat16

D_MODEL = 1024
W_HALF = 512
N_POOL_GROUPS = 4
POOL_HALF = (1, 2, 4, 8)
N_HEADS = 8
HEAD_DIM = 64
GRID_W = 64
WIN_H = 8
WIN_W = 16
CONV_C = 3
CONV_D = 31
EPS = 1e-6
MASKED = -1e30
LOG2_E = 1.4426950408889634
Q_SCALE = HEAD_DIM ** -0.5 * LOG2_E

LANES = 128
SUBLANES = 8
PAD = 16
ROW_CHUNK = 512
NORM_ROWS = 32
POOL_ROWS = 256
CONV_ROWS = 128
Q_ROWS = 128
NB_PROMPT = 2
NA_GROUP = 8
MOD_ROWS = 512
STAGE_ROWS = 128
STAGE_SLOTS = 4
VMEM_LIMIT = 58 * 1024 * 1024

assert PAD >= CONV_D // 2 + 1 and PAD % SUBLANES == 0 and PAD >= 2 * SUBLANES
assert max(POOL_HALF) <= SUBLANES


def _sigmoid(x):
    return 1.0 / (1.0 + jnp.exp(-x))


def _silu(x):
    return x * _sigmoid(x)


def _dot(a, b):
    return jnp.dot(a, b, preferred_element_type=F32)


def _dot_nt(a, b):
    return lax.dot_general(a, b, (((1,), (1,)), ((), ())), preferred_element_type=F32)


def _lanes(j):
    return slice(j * LANES, (j + 1) * LANES)


def _group(g):
    return slice(g * W_HALF, (g + 1) * W_HALF)


def _rows(start, size, align):
    if isinstance(start, int):
        return slice(start, start + size)
    return pl.ds(pl.multiple_of(start, align), size)


def _mod_body(cctx_ref, c_ref, w_ref, b_ref, o_ref, act_ref):
    layer, kb = pl.program_id(0), pl.program_id(1)
    rows, d = act_ref.shape

    @pl.when(kb == 0)
    def _():
        r = lax.broadcasted_iota(jnp.int32, (rows, d), 0)
        cond = jnp.where(r == 0, cctx_ref[...], 0.0)
        for i in range(c_ref.shape[0]):
            cond = jnp.where(r == i + 1, c_ref[i:i + 1, :], cond)
        act_ref[...] = _silu(cond).astype(BF16)
        o_ref[0] = jnp.broadcast_to(jnp.where(layer == 0, b_ref[0:1, :], b_ref[1:2, :]), o_ref.shape[1:])

    act = act_ref[:, pl.ds(pl.multiple_of(kb * MOD_ROWS, MOD_ROWS), MOD_ROWS)]
    o_ref[0] += _dot(act, w_ref[0].astype(BF16))


def _modulation(c_ctx, c, w_mod, b_mod, rows):
    depth, d, n = w_mod.shape
    assert depth == 2 and 1 + c.shape[0] <= rows and d % MOD_ROWS == 0
    return pl.pallas_call(
        _mod_body,
        out_shape=jax.ShapeDtypeStruct((depth, rows, n), F32),
        grid=(depth, d // MOD_ROWS),
        in_specs=[
            pl.BlockSpec((1, d), lambda l, k: (0, 0)),
            pl.BlockSpec(c.shape, lambda l, k: (0, 0)),
            pl.BlockSpec((1, MOD_ROWS, n), lambda l, k: (l, k, 0)),
            pl.BlockSpec((depth, n), lambda l, k: (0, 0)),
        ],
        out_specs=pl.BlockSpec((1, rows, n), lambda l, k: (l, 0, 0)),
        scratch_shapes=[pltpu.VMEM((rows, d), BF16)],
        compiler_params=pltpu.CompilerParams(dimension_semantics=("arbitrary", "arbitrary")),
        name="mod",
    )(c_ctx.reshape(1, d), c, w_mod, b_mod)


def _cond_row(m_ref, layer, row):
    if isinstance(row, int):
        return m_ref[layer, row:row + 1, :]
    m = m_ref[layer]
    keep = lax.broadcasted_iota(jnp.int32, m.shape, 0) == row
    return jnp.sum(jnp.where(keep, m, 0.0), axis=0, keepdims=True)


def _pieces(c, nb, t):
    if t >= ROW_CHUNK:
        per_seq = t // ROW_CHUNK
        s = 0 if nb == 1 else c // per_seq
        return [(s, (c - s * per_seq) * ROW_CHUNK, ROW_CHUNK, 0)]
    per_chunk = ROW_CHUNK // t
    return [(c * per_chunk + i, 0, t, i * t) for i in range(per_chunk)]


def _for_chunks(n, body, unrolled=False):
    if unrolled or n == 1:
        for c in range(n):
            body(c)
    else:
        lax.fori_loop(0, n, lambda c, carry: (body(c), carry)[1], 0)


def _pad_row(s, off, t):
    return s * (t + 2 * PAD) + PAD + off


def _store_padded(pad_ref, val, pieces, t):
    for s, off, n, o in pieces:
        pad_ref[_rows(_pad_row(s, off, t), n, SUBLANES), :] = val[o:o + n]


def _scale_padded(pad_ref, val, pieces, t):
    for s, off, n, o in pieces:
        rows = _rows(_pad_row(s, off, t), n, SUBLANES)
        pad_ref[rows, :] = pad_ref[rows, :] * val[o:o + n]


def _modnorm_chunk(src_ref, h_ref, c, nb, t, gain, shift):
    for s, off, n, o in _pieces(c, nb, t):
        for i in range(0, n, NORM_ROWS):
            x = src_ref[s, _rows(off + i, NORM_ROWS, NORM_ROWS), :]
            ms = jnp.mean(x * x, axis=-1, keepdims=True)
            h_ref[_rows(c * ROW_CHUNK + o + i, NORM_ROWS, NORM_ROWS), :] = (
                x * lax.rsqrt(ms + EPS) * gain + shift).astype(BF16)


def _zero_pads(pad_ref, nb, t):
    z = jnp.zeros((PAD, W_HALF), F32)
    for s in range(nb):
        pad_ref[_pad_row(s, 0, t) - PAD:_pad_row(s, 0, t), :] = z
        pad_ref[_pad_row(s, t, t):_pad_row(s, t, t) + PAD, :] = z


def _pool_phase(pad_ref, ga_ref, wp_ref, ps_ref, ab_ref, nb, t):
    n_rows = POOL_ROWS
    per_seq = t // n_rows

    def step(i, carry):
        s = i // per_seq
        r0 = (i - s * per_seq) * n_rows
        prow = _pad_row(s, r0, t)
        rows = _rows(i * n_rows, n_rows, n_rows)
        pos = r0 + lax.broadcasted_iota(jnp.int32, (n_rows, LANES), 0)
        before = jnp.minimum(pos, SUBLANES)
        after = jnp.minimum(t - pos, SUBLANES)
        for g in range(N_POOL_GROUPS):
            hw = POOL_HALF[g]
            ln = _lanes(g)
            halo = n_rows + 2 * SUBLANES
            blk = pad_ref[_rows(prow - SUBLANES, halo, SUBLANES), ln]
            run, n = blk, 1
            while n < 2 * hw:
                run = run + pltpu.roll(run, halo - n, 0)
                n *= 2
            if hw < SUBLANES:
                run = pltpu.roll(run, halo - (SUBLANES - hw), 0)
            win = run[:n_rows]
            cnt = (jnp.minimum(before, hw) + jnp.minimum(after, hw)).astype(F32)
            p = (win / cnt - blk[SUBLANES:SUBLANES + n_rows]).astype(BF16)
            y = _dot(p, wp_ref[0, g].astype(BF16)) * ps_ref[:, ln] * ga_ref[rows, ln]
            ab_ref[rows, ln] = y.astype(BF16)
        return carry
    lax.fori_loop(0, nb * per_seq, step, 0)


def _out_proj_chunk(ab_ref, w_ref, x_ref, gate, dst_ref, c, nb, t):
    lhs = ab_ref[_rows(c * ROW_CHUNK, ROW_CHUNK, ROW_CHUNK), :]
    for g in range(D_MODEL // W_HALF):
        y = _dot(lhs, w_ref[:, _group(g)])
        for s, off, n, o in _pieces(c, nb, t):
            rows = _rows(off, n, n)
            dst_ref[s, rows, _group(g)] = x_ref[s, rows, _group(g)] + gate[:, _group(g)] * y[o:o + n]


def _shift_up(x, o, n):
    if o % SUBLANES == 0:
        return x[o:o + n]
    return pltpu.roll(x, x.shape[0] - o, 0)[:n]


def _conv_phase(pad_c, pad_d, bc_ref, ga_ref, gb_ref, cc_ref, cdw_ref, cdb_ref, lng_ref, lnb_ref,
                ab_ref, nb, t):
    n_rows = CONV_ROWS
    per_seq = t // n_rows

    def step(i, carry):
        s = i // per_seq
        r0 = (i - s * per_seq) * n_rows
        prow = _pad_row(s, r0, t)
        rows = _rows(i * n_rows, n_rows, n_rows)
        z = []
        for g in range(W_HALF // LANES):
            ln = _lanes(g)
            blk = pad_c[_rows(prow - SUBLANES, n_rows + 2 * SUBLANES, SUBLANES), ln]
            c3 = None
            for j in range(CONV_C):
                o = SUBLANES + j - CONV_C // 2
                term = _shift_up(blk, o, n_rows) * cc_ref[0, j:j + 1, ln]
                c3 = term if c3 is None else c3 + term
            ab_ref[rows, ln] = (bc_ref[rows, ln] * c3 * ga_ref[rows, ln]).astype(BF16)
            acc = None
            for sft in range(SUBLANES):
                part = None
                for a in range((CONV_D - sft + SUBLANES - 1) // SUBLANES):
                    j = SUBLANES * a + sft
                    src = pad_d[_rows(prow - 2 * SUBLANES + SUBLANES * a, n_rows + SUBLANES,
                                      SUBLANES), ln]
                    term = src * cdw_ref[0, j:j + 1, ln]
                    part = term if part is None else part + term
                o = SUBLANES + sft - (CONV_D // 2 - SUBLANES)
                part = _shift_up(part, o, n_rows)
                acc = part if acc is None else acc + part
            z.append(acc + cdb_ref[:, ln])
        z = jnp.concatenate(z, axis=-1)
        mu = jnp.mean(z, axis=-1, keepdims=True)
        zc = z - mu
        var = jnp.mean(zc * zc, axis=-1, keepdims=True)
        zn = zc * lax.rsqrt(var + EPS) * lng_ref[...] + lnb_ref[...]
        ab_ref[rows, W_HALF:] = (_silu(zn) * gb_ref[rows, :]).astype(BF16)
        return carry
    lax.fori_loop(0, nb * per_seq, step, 0)


def _final_norm_chunk(y_ref, fg, c, nb, t):
    for s, off, n, _ in _pieces(c, nb, t):
        for i in range(0, n, NORM_ROWS):
            rows = _rows(off + i, NORM_ROWS, NORM_ROWS)
            x = y_ref[s, rows, :]
            ms = jnp.mean(x * x, axis=-1, keepdims=True)
            y_ref[s, rows, :] = x * lax.rsqrt(ms + EPS) * fg


def _odd_layer(y_ref, m_row, g_row, fg, wio_ref, cc_ref, cdw_ref, cdb_ref, lng_ref, lnb_ref, woo_ref,
               h_ref, pad_c, pad_d, bc_ref, ga_ref, gb_ref, ab_ref, nb, t, between_phases=(None, None)):
    shift = m_row[:, :D_MODEL]
    gain = g_row * (1.0 + m_row[:, D_MODEL:2 * D_MODEL])
    gate = m_row[:, 2 * D_MODEL:]
    n_chunks = nb * t // ROW_CHUNK

    def in_proj(c):
        _modnorm_chunk(y_ref, h_ref, c, nb, t, gain, shift)
        rows = _rows(c * ROW_CHUNK, ROW_CHUNK, ROW_CHUNK)
        pieces = _pieces(c, nb, t)
        h = h_ref[rows, :]
        bc_ref[rows, :] = _dot(h, wio_ref[:, _group(0)])
        _store_padded(pad_c, _dot(h, wio_ref[:, _group(1)]), pieces, t)
        _scale_padded(pad_c, _dot(h, wio_ref[:, _group(2)]), pieces, t)
        ga_ref[rows, :] = _silu(_dot(h, wio_ref[:, _group(3)]))
        _store_padded(pad_d, _dot(h, wio_ref[:, _group(4)]), pieces, t)
        _scale_padded(pad_d, _sigmoid(_dot(h, wio_ref[:, _group(5)])), pieces, t)
        gb_ref[rows, :] = _silu(_dot(h, wio_ref[:, _group(6)]))
    _for_chunks(n_chunks, in_proj)
    if between_phases[0] is not None:
        between_phases[0]()

    _conv_phase(pad_c, pad_d, bc_ref, ga_ref, gb_ref, cc_ref, cdw_ref, cdb_ref, lng_ref, lnb_ref,
                ab_ref, nb, t)
    if between_phases[1] is not None:
        between_phases[1]()

    def out_proj(c):
        _out_proj_chunk(ab_ref, woo_ref, y_ref, gate, y_ref, c, nb, t)
        _final_norm_chunk(y_ref, fg, c, nb, t)
    _for_chunks(n_chunks, out_proj, unrolled=True)


def _even_in_proj(x_ref, m_row, g_row, w_ref, h_ref, pad_a, ga_ref, gb_ref, q_ref, k_ref, v_ref,
                  kv_t, nb, t):
    shift = m_row[:, :D_MODEL]
    gain = g_row * (1.0 + m_row[:, D_MODEL:2 * D_MODEL])

    def in_proj(c):
        _modnorm_chunk(x_ref, h_ref, c, nb, t, gain, shift)
        rows = _rows(c * ROW_CHUNK, ROW_CHUNK, ROW_CHUNK)
        pieces = _pieces(c, nb, t)
        h = h_ref[rows, :]
        _store_padded(pad_a, _dot(h, w_ref[:, _group(0)]), pieces, t)
        ga_ref[rows, :] = _silu(_dot(h, w_ref[:, _group(1)]))
        q_ref[rows, :] = (_dot(h, w_ref[:, _group(2)]) * Q_SCALE).astype(BF16)
        for i, (dst, g) in enumerate(((k_ref, 3), (v_ref, 4))):
            if kv_t is None:
                dst[rows, :] = _dot(h, w_ref[:, _group(g)]).astype(BF16)
                continue
            acc = _dot_nt(kv_t[2 + i][...], h)
            dst[:, rows] = acc.astype(BF16)
            for s, off, n, o in pieces:
                for hd in range(N_HEADS):
                    kv_t[i][s, 0, hd, :, _rows(off, n, n)] = (
                        acc[hd * HEAD_DIM:(hd + 1) * HEAD_DIM, o:o + n])
        gb_ref[rows, :] = _silu(_dot(h, w_ref[:, _group(5)]))
    _for_chunks(nb * t // ROW_CHUNK, in_proj)


def _even_out_proj(x_ref, y_ref, m_row, w_ref, ab_ref, nb, t):
    gate = m_row[:, 2 * D_MODEL:]
    _for_chunks(nb * t // ROW_CHUNK,
                lambda c: _out_proj_chunk(ab_ref, w_ref, x_ref, gate, y_ref, c, nb, t))


def _split_heads(x):
    lane = lax.broadcasted_iota(jnp.int32, (1, LANES), 1)
    first = jnp.where(lane < HEAD_DIM, 1.0, 0.0).astype(x.dtype)
    return jnp.concatenate([x * first, x * (1 - first)], axis=0)


def _merge_heads(o):
    n = o.shape[0] // 2
    lane = lax.broadcasted_iota(jnp.int32, (n, LANES), 1)
    return jnp.where(lane < HEAD_DIM, o[:n], o[n:])


def _context_attention(q_ref, kt_ref, vt_ref, gb_ref, ab_ref, nb, t):
    for s in range(nb):
        seq = slice(s * t, (s + 1) * t)
        for j in range(N_HEADS // 2):
            ln = _lanes(j)
            kp = kt_ref[ln, seq]
            vp = vt_ref[ln, seq]
            for r0 in range(0, t, Q_ROWS):
                rows = slice(s * t + r0, s * t + r0 + Q_ROWS)
                sc = _dot(_split_heads(q_ref[rows, ln]), kp)
                p = jnp.exp2(sc - jnp.max(sc, axis=-1, keepdims=True))
                o = _dot_nt(p.astype(BF16), vp) / jnp.sum(p, axis=-1, keepdims=True)
                ab_ref[rows, W_HALF + j * LANES:W_HALF + (j + 1) * LANES] = (
                    _merge_heads(o) * gb_ref[rows, ln]).astype(BF16)


def _weight_stager(w_hbm, w_bf, wkt_ref, wvt_ref, stage, sem_in):
    def chunks_of(k):
        return [(k, r0) for r0 in range(0, w_hbm[k].shape[1], STAGE_ROWS)]

    def fetch(k, r0, slot):
        cols = w_hbm[k].shape[2]
        return pltpu.make_async_copy(w_hbm[k].at[0, pl.ds(r0, STAGE_ROWS), :],
                                     stage.at[slot, :, pl.ds(0, cols)], sem_in.at[slot])

    def cast(k, r0, slot):
        cols = w_hbm[k].shape[2]
        rows = slice(r0, r0 + STAGE_ROWS)
        w_bf[k][rows, :] = stage[slot, :, 0:cols].astype(BF16)
        if k == 0:
            wkt_ref[:, rows] = stage[slot, :, _group(3)].T.astype(BF16)
            wvt_ref[:, rows] = stage[slot, :, _group(4)].T.astype(BF16)

    def stream(k):
        chunks = chunks_of(k)
        ahead = STAGE_SLOTS - 1
        for i in range(min(ahead, len(chunks))):
            fetch(*chunks[i], i % STAGE_SLOTS).start()
        for i, c in enumerate(chunks):
            if i + ahead < len(chunks):
                fetch(*chunks[i + ahead], (i + ahead) % STAGE_SLOTS).start()
            fetch(*c, i % STAGE_SLOTS).wait()
            cast(*c, i % STAGE_SLOTS)

    later = [c for k in range(1, len(w_hbm)) for c in chunks_of(k)]
    rounds = [later[i:i + STAGE_SLOTS] for i in range(0, len(later), STAGE_SLOTS)]

    def issue(r):
        for slot, c in enumerate(rounds[r]):
            fetch(*c, slot).start()

    def finish(r):
        for slot, c in enumerate(rounds[r]):
            fetch(*c, slot).wait()
            cast(*c, slot)
        last = {k: max(i for i, rnd in enumerate(rounds) if any(c[0] == k for c in rnd))
                for k in range(1, len(w_hbm))}
        return [k for k, i in last.items() if i == r]

    return stream, issue, finish, rounds


def _prompt_body(x_ref, m_ref, ng_ref, fg_ref, wp_ref, ps_ref, cc_ref, cdw_ref, cdb_ref, lng_ref,
                 lnb_ref, wie_hbm, woe_hbm, wio_hbm, woo_hbm,
                 y_ref, ko_ref, vo_ref, wie_out, woe_out, wio_out, woo_out,
                 h_ref, pad_a, pad_b, ga_ref, gb_ref, bc_ref, q_ref, kt_ref, vt_ref, ab_ref,
                 wie_ref, woe_ref, wio_ref, woo_ref, wkt_ref, wvt_ref, stage, sem_in, sem_out,
                 *, nb, t):
    w_out = (wie_out, woe_out, wio_out, woo_out)
    w_bf = (wie_ref, woe_ref, wio_ref, woo_ref)
    stream, issue, finish, rounds = _weight_stager((wie_hbm, woe_hbm, wio_hbm, woo_hbm), w_bf,
                                                   wkt_ref, wvt_ref, stage, sem_in)
    assert len(rounds) == 6
    assert all(k == 1 for k, _ in rounds[0] + rounds[1])
    assert all(k <= 2 for rnd in rounds[:4] for k, _ in rnd)
    assert all(k == 3 for k, _ in rounds[4] + rounds[5])

    def write_back(k):
        return pltpu.make_async_copy(w_bf[k], w_out[k], sem_out.at[k])

    def at_first_step(fn):
        pl.when(pl.program_id(0) == 0)(fn)

    def turn(r):
        def fn():
            for k in finish(r):
                write_back(k).start()
            if r + 1 < len(rounds):
                issue(r + 1)
        return lambda: at_first_step(fn)

    def first_weight():
        stream(0)
        write_back(0).start()
        issue(0)
    at_first_step(first_weight)

    _zero_pads(pad_a, nb, t)
    _zero_pads(pad_b, nb, t)
    m_even = _cond_row(m_ref, 0, 0)
    _even_in_proj(x_ref, m_even, ng_ref[0:1, :], wie_ref, h_ref, pad_a, ga_ref, gb_ref,
                  q_ref, kt_ref, vt_ref, (ko_ref, vo_ref, wkt_ref, wvt_ref), nb, t)
    turn(0)()
    _pool_phase(pad_a, ga_ref, wp_ref, ps_ref, ab_ref, nb, t)
    turn(1)()
    _context_attention(q_ref, kt_ref, vt_ref, gb_ref, ab_ref, nb, t)
    turn(2)()
    _even_out_proj(x_ref, y_ref, m_even, woe_ref, ab_ref, nb, t)
    turn(3)()
    _odd_layer(y_ref, _cond_row(m_ref, 1, 0), ng_ref[1:2, :], fg_ref[...], wio_ref, cc_ref, cdw_ref,
               cdb_ref, lng_ref, lnb_ref, woo_ref, h_ref, pad_a, pad_b, bc_ref, ga_ref, gb_ref, ab_ref,
               nb, t, between_phases=(turn(4), turn(5)))

    def wait_write_backs():
        for k in range(len(w_bf)):
            write_back(k).wait()
    at_first_step(wait_write_backs)


def _rpb_rows(rpb_ref, e_ref):
    n = rpb_ref.shape[0]
    lane = lax.broadcasted_iota(jnp.int32, (n, LANES), 1)
    i = jnp.where(lane < GRID_W, lane, lane - LANES)
    idx = jnp.clip(i, -(WIN_W - 1), WIN_W - 1) + (WIN_W - 1)
    rp = rpb_ref[...]
    e = jnp.zeros((n, LANES), F32)
    for d in range(2 * WIN_W - 1):
        e = jnp.where(idx == d, rp[:, d:d + 1], e)
    e_ref[...] = e


N_DR = 2 * WIN_H - 1
PAIR_TILES = N_DR // 2


def _bias_tile_index(j, dr_lo):
    if isinstance(dr_lo, int):
        parity, half = dr_lo % 2, dr_lo // 2
    else:
        parity, half = dr_lo & 1, lax.shift_right_logical(dr_lo, 1)
    return (2 * j + parity) * PAIR_TILES + half


def _bias_tables(e_ref, bias_ref):
    q = lax.broadcasted_iota(jnp.int32, (GRID_W, LANES), 0)
    lane = lax.broadcasted_iota(jnp.int32, (GRID_W, LANES), 1)
    kw = jnp.where(lane < GRID_W, lane, lane - GRID_W)
    start = jnp.clip(q - WIN_W // 2, 0, GRID_W - WIN_W)
    col_ok = (kw >= start) & (kw < start + WIN_W)
    for j in range(N_HEADS // 2):
        for dr in range(N_DR - 1):
            for e in range(2):
                r_lo = (2 * j + e) * N_DR + dr
                lo = jnp.broadcast_to(e_ref[r_lo:r_lo + 1, :], (GRID_W, LANES))
                hi = jnp.broadcast_to(e_ref[r_lo + 1:r_lo + 2, :], (GRID_W, LANES))
                lo = pltpu.roll(lo, 0, 1, stride=1, stride_axis=0)
                hi = pltpu.roll(hi, GRID_W, 1, stride=1, stride_axis=0)
                tile = jnp.where(lane < GRID_W, lo, hi)
                bias_ref[_bias_tile_index(j, dr), e * GRID_W:(e + 1) * GRID_W, :] = jnp.where(
                    col_ok, tile * LOG2_E, MASKED)


def _neighbourhood_attention(q_ref, k_ref, v_ref, ck_ref, cv_ref, bias_ref, kvc_ref, gb_ref, ab_ref, t):
    grid_h = t // GRID_W
    band = WIN_H * GRID_W
    for j in range(N_HEADS // 2):
        ln = _lanes(j)
        for i, src in enumerate((ck_ref, cv_ref)):
            kvc_ref[i] = jnp.concatenate([src[0, 0, 2 * j], src[0, 0, 2 * j + 1]],
                                         axis=0).astype(BF16)

        def per_group(g, carry, ln=ln, j=j):
            scored = []
            for u in range(NA_GROUP):
                r = g * NA_GROUP + u
                start = jnp.clip(r - WIN_H // 2, 0, grid_h - WIN_H)
                rows = _rows(r * GRID_W, GRID_W, GRID_W)
                keys = _rows(start * GRID_W, band, GRID_W)
                q2 = _split_heads(q_ref[rows, ln])
                dr0 = (WIN_H - 1) - (r - start)
                bias = jnp.concatenate([bias_ref[_bias_tile_index(j, dr0 + 2 * i)]
                                        for i in range(WIN_H // 2)], axis=-1)
                scored.append((rows, keys, _dot_nt(q2, k_ref[keys, ln]) + bias, _dot(q2, kvc_ref[0])))
            weighted = []
            for rows, keys, s_loc, s_ctx in scored:
                mx = jnp.maximum(jnp.max(s_loc, axis=-1, keepdims=True),
                                 jnp.max(s_ctx, axis=-1, keepdims=True))
                p_loc = jnp.exp2(s_loc - mx)
                p_ctx = jnp.exp2(s_ctx - mx)
                den = (jnp.sum(p_loc, axis=-1, keepdims=True)
                       + jnp.sum(p_ctx, axis=-1, keepdims=True))
                weighted.append((rows, keys, p_loc.astype(BF16), p_ctx.astype(BF16), den))
            for rows, keys, p_loc, p_ctx, den in weighted:
                o = (_dot(p_loc, v_ref[keys, ln]) + _dot_nt(p_ctx, kvc_ref[1])) / den
                ab_ref[rows, W_HALF + j * LANES:W_HALF + (j + 1) * LANES] = (
                    _merge_heads(o) * gb_ref[rows, ln]).astype(BF16)
            return carry
        lax.fori_loop(0, grid_h // NA_GROUP, per_group, 0)


def _sample_body(x_ref, m_ref, ng_ref, fg_ref, wie_ref, wp_ref, ps_ref, woe_hbm, wio_hbm, cc_ref,
                 cdw_ref, cdb_ref, lng_ref, lnb_ref, woo_hbm, ck_ref, cv_ref, rpb_ref,
                 y_ref,
                 h_ref, pad_a, pad_b, ga_ref, gb_ref, bc_ref, q_ref, k_ref, v_ref, ab_ref,
                 e_ref, bias_ref, kvc_ref, woe_ref, wio_ref, woo_ref, sem_w, *, t):
    _zero_pads(pad_a, 1, t)
    _zero_pads(pad_b, 1, t)
    late = ((woe_hbm, woe_ref), (wio_hbm, wio_ref), (woo_hbm, woo_ref))

    def late_copy(i):
        return pltpu.make_async_copy(late[i][0], late[i][1], sem_w.at[i])

    first_step = pl.program_id(0) == 0

    @pl.when(first_step)
    def _():
        for i in range(len(late)):
            late_copy(i).start()
        _rpb_rows(rpb_ref, e_ref)
        _bias_tables(e_ref, bias_ref)

    cond = pl.program_id(0) + 1
    m_even = _cond_row(m_ref, 0, cond)
    _even_in_proj(x_ref, m_even, ng_ref[0:1, :], wie_ref, h_ref, pad_a, ga_ref, gb_ref,
                  q_ref, k_ref, v_ref, None, 1, t)
    _pool_phase(pad_a, ga_ref, wp_ref, ps_ref, ab_ref, 1, t)
    _neighbourhood_attention(q_ref, k_ref, v_ref, ck_ref, cv_ref, bias_ref, kvc_ref, gb_ref, ab_ref, t)
    pl.when(first_step)(lambda: late_copy(0).wait())
    _even_out_proj(x_ref, y_ref, m_even, woe_ref, ab_ref, 1, t)

    @pl.when(first_step)
    def _():
        late_copy(1).wait()
        late_copy(2).wait()

    _odd_layer(y_ref, _cond_row(m_ref, 1, cond), ng_ref[1:2, :], fg_ref[...], wio_ref, cc_ref, cdw_ref,
               cdb_ref, lng_ref, lnb_ref, woo_ref, h_ref, pad_a, pad_b, bc_ref, ga_ref, gb_ref, ab_ref,
               1, t)


def _const_spec(shape):
    zeros = (0,) * len(shape)
    return pl.BlockSpec(shape, lambda i: zeros, pipeline_mode=pl.Buffered(1))


def _stream_scratch(nb, t, kv_transposed):
    r = nb * t
    padded = nb * (t + 2 * PAD)
    kv = (W_HALF, r) if kv_transposed else (r, W_HALF)
    return [
        pltpu.VMEM((r, D_MODEL), BF16),
        pltpu.VMEM((padded, W_HALF), F32),
        pltpu.VMEM((padded, W_HALF), F32),
        pltpu.VMEM((r, W_HALF), F32),
        pltpu.VMEM((r, W_HALF), F32),
        pltpu.VMEM((r, W_HALF), F32),
        pltpu.VMEM((r, W_HALF), BF16),
        pltpu.VMEM(kv, BF16),
        pltpu.VMEM(kv, BF16),
        pltpu.VMEM((r, D_MODEL), BF16),
    ]


def _small_params(norm_g, final_g, w_pool, pool_scale, conv_c, conv_d, conv_d_b, ln_g, ln_b):
    return [norm_g, final_g.reshape(1, D_MODEL), w_pool, pool_scale, conv_c, conv_d, conv_d_b, ln_g,
            ln_b]


def kernel(x_prompt, x_sample, cache_k, cache_v, c, c_ctx, norm_g, w_mod, b_mod, w_in_even, w_pool,
           pool_scale, rpb, w_out_even, w_in_odd, conv_c, conv_d, conv_d_b, ln_g, ln_b, w_out_odd,
           final_g):
    batch, seq, d = x_prompt.shape
    dec_batch, dec_seq, _ = x_sample.shape
    assert d == D_MODEL and w_mod.shape[0] == 2 and w_in_even.shape[0] == 1 and w_in_odd.shape[0] == 1
    assert (NB_PROMPT * seq) % ROW_CHUNK == 0 and ROW_CHUNK % seq == 0 and seq % Q_ROWS == 0
    assert dec_seq % ROW_CHUNK == 0 and dec_seq // GRID_W >= WIN_H
    assert seq % POOL_ROWS == 0 and seq % CONV_ROWS == 0
    assert dec_seq % POOL_ROWS == 0 and dec_seq % CONV_ROWS == 0
    assert (dec_seq // GRID_W) % NA_GROUP == 0

    cond_rows = SUBLANES * ((1 + dec_batch + SUBLANES - 1) // SUBLANES)
    m = _modulation(c_ctx, c, w_mod, b_mod, cond_rows)
    m_spec = _const_spec(m.shape)

    small = _small_params(norm_g, final_g, w_pool, pool_scale, conv_c, conv_d, conv_d_b, ln_g, ln_b)
    small_specs = [_const_spec(a.shape) for a in small]
    w_f32 = (w_in_even, w_out_even, w_in_odd, w_out_odd)
    assert all(w.shape[0] == 1 and w.shape[1] % STAGE_ROWS == 0 for w in w_f32)
    any_spec = pl.BlockSpec(memory_space=pl.ANY)

    nb = NB_PROMPT
    assert batch % nb == 0
    kv_shape = jax.ShapeDtypeStruct((batch, 1, N_HEADS, HEAD_DIM, seq), F32)
    kv_spec = pl.BlockSpec((nb, 1, N_HEADS, HEAD_DIM, seq), lambda i: (i, 0, 0, 0, 0))
    y_prompt, new_kt, new_vt, wie, woe, wio, woo = pl.pallas_call(
        functools.partial(_prompt_body, nb=nb, t=seq),
        out_shape=(jax.ShapeDtypeStruct(x_prompt.shape, F32), kv_shape, kv_shape)
                  + tuple(jax.ShapeDtypeStruct(w.shape[1:], BF16) for w in w_f32),
        grid=(batch // nb,),
        in_specs=[pl.BlockSpec((nb, seq, d), lambda i: (i, 0, 0)), m_spec] + small_specs
                 + [any_spec] * len(w_f32),
        out_specs=(pl.BlockSpec((nb, seq, d), lambda i: (i, 0, 0)), kv_spec, kv_spec)
                  + (any_spec,) * len(w_f32),
        scratch_shapes=_stream_scratch(nb, seq, True)
                       + [pltpu.VMEM(w.shape[1:], BF16) for w in w_f32] + [
            pltpu.VMEM((W_HALF, d), BF16),
            pltpu.VMEM((W_HALF, d), BF16),
            pltpu.VMEM((STAGE_SLOTS, STAGE_ROWS, max(w.shape[2] for w in w_f32)), F32),
            pltpu.SemaphoreType.DMA((STAGE_SLOTS,)),
            pltpu.SemaphoreType.DMA((len(w_f32),)),
        ],
        compiler_params=pltpu.CompilerParams(dimension_semantics=("arbitrary",),
                                             vmem_limit_bytes=VMEM_LIMIT),
        name="prompt",
    )(x_prompt, m, *small, *w_f32)
    ng, fg, wp, ps, cc, cdw, cdb, lng, lnb = small
    w_args = [ng, fg, wie, wp, ps, woe, wio, cc, cdw, cdb, lng, lnb, woo]
    late_w = (woe, wio, woo)
    w_specs = [any_spec if any(a is w for w in late_w) else _const_spec(a.shape) for a in w_args]

    past = cache_k.shape[3]
    cache_spec = pl.BlockSpec((1, 1, N_HEADS, HEAD_DIM, past), lambda i: (i, 0, 0, 0, 0))
    rpb2 = rpb[0].reshape(N_HEADS * (2 * WIN_H - 1), 2 * WIN_W - 1)
    y_sample = pl.pallas_call(
        functools.partial(_sample_body, t=dec_seq),
        out_shape=jax.ShapeDtypeStruct(x_sample.shape, F32),
        grid=(dec_batch,),
        in_specs=[pl.BlockSpec((1, dec_seq, d), lambda i: (i, 0, 0), pipeline_mode=pl.Buffered(1)),
                  m_spec] + w_specs
                 + [cache_spec, cache_spec, _const_spec(rpb2.shape)],
        out_specs=pl.BlockSpec((1, dec_seq, d), lambda i: (i, 0, 0)),
        scratch_shapes=_stream_scratch(1, dec_seq, False) + [
            pltpu.VMEM(rpb2.shape[:1] + (LANES,), F32),
            pltpu.VMEM((N_HEADS * PAIR_TILES, 2 * GRID_W, LANES), F32),
            pltpu.VMEM((2, LANES, past), BF16),
        ] + [pltpu.VMEM(w.shape, BF16) for w in late_w] + [pltpu.SemaphoreType.DMA((len(late_w),))],
        compiler_params=pltpu.CompilerParams(dimension_semantics=("arbitrary",),
                                             vmem_limit_bytes=VMEM_LIMIT),
        name="sample",
    )(x_sample, m, *w_args, jnp.swapaxes(cache_k, 3, 4), jnp.swapaxes(cache_v, 3, 4), rpb2)

    return (y_prompt, y_sample, jnp.swapaxes(new_kt, 3, 4), jnp.swapaxes(new_vt, 3, 4))
```

```python
import functools

import jax
import jax.numpy as jnp
from jax import lax
from jax.experimental import pallas as pl
from jax.experimental.pallas import tpu as pltpu

F32 = jnp.float32
BF16 = jnp.bfloat16

D_MODEL = 1024
W_HALF = 512
N_POOL_GROUPS = 4
POOL_HALF = (1, 2, 4, 8)
N_HEADS = 8
HEAD_DIM = 64
GRID_W = 64
WIN_H = 8
WIN_W = 16
CONV_C = 3
CONV_D = 31
EPS = 1e-6
MASKED = -1e30
LOG2_E = 1.4426950408889634
Q_SCALE = HEAD_DIM ** -0.5 * LOG2_E

LANES = 128
SUBLANES = 8
PAD = 16
ROW_CHUNK = 512
NORM_ROWS = 32
POOL_ROWS = 256
CONV_ROWS = 128
Q_ROWS = 128
NB_PROMPT = 2
NA_GROUP = 8
MOD_ROWS = 512
STAGE_ROWS = 128
STAGE_SLOTS = 4
VMEM_LIMIT = 58 * 1024 * 1024

assert PAD >= CONV_D // 2 + 1 and PAD % SUBLANES == 0 and PAD >= 2 * SUBLANES
assert max(POOL_HALF) <= SUBLANES


def _sigmoid(x):
    return 1.0 / (1.0 + jnp.exp(-x))


def _silu(x):
    return x * _sigmoid(x)


def _dot(a, b):
    return jnp.dot(a, b, preferred_element_type=F32)


def _dot_nt(a, b):
    return lax.dot_general(a, b, (((1,), (1,)), ((), ())), preferred_element_type=F32)


def _lanes(j):
    return slice(j * LANES, (j + 1) * LANES)


def _group(g):
    return slice(g * W_HALF, (g + 1) * W_HALF)


def _rows(start, size, align):
    if isinstance(start, int):
        return slice(start, start + size)
    return pl.ds(pl.multiple_of(start, align), size)


def _mod_body(cctx_ref, c_ref, w_ref, b_ref, o_ref, act_ref):
    layer, kb = pl.program_id(0), pl.program_id(1)
    rows, d = act_ref.shape

    @pl.when(kb == 0)
    def _():
        r = lax.broadcasted_iota(jnp.int32, (rows, d), 0)
        cond = jnp.where(r == 0, cctx_ref[...], 0.0)
        for i in range(c_ref.shape[0]):
            cond = jnp.where(r == i + 1, c_ref[i:i + 1, :], cond)
        act_ref[...] = _silu(cond).astype(BF16)
        o_ref[0] = jnp.broadcast_to(jnp.where(layer == 0, b_ref[0:1, :], b_ref[1:2, :]), o_ref.shape[1:])

    act = act_ref[:, pl.ds(pl.multiple_of(kb * MOD_ROWS, MOD_ROWS), MOD_ROWS)]
    o_ref[0] += _dot(act, w_ref[0].astype(BF16))


def _modulation(c_ctx, c, w_mod, b_mod, rows):
    depth, d, n = w_mod.shape
    assert depth == 2 and 1 + c.shape[0] <= rows and d % MOD_ROWS == 0
    return pl.pallas_call(
        _mod_body,
        out_shape=jax.ShapeDtypeStruct((depth, rows, n), F32),
        grid=(depth, d // MOD_ROWS),
        in_specs=[
            pl.BlockSpec((1, d), lambda l, k: (0, 0)),
            pl.BlockSpec(c.shape, lambda l, k: (0, 0)),
            pl.BlockSpec((1, MOD_ROWS, n), lambda l, k: (l, k, 0)),
            pl.BlockSpec((depth, n), lambda l, k: (0, 0)),
        ],
        out_specs=pl.BlockSpec((1, rows, n), lambda l, k: (l, 0, 0)),
        scratch_shapes=[pltpu.VMEM((rows, d), BF16)],
        compiler_params=pltpu.CompilerParams(dimension_semantics=("arbitrary", "arbitrary")),
        name="mod",
    )(c_ctx.reshape(1, d), c, w_mod, b_mod)


def _cond_row(m_ref, layer, row):
    if isinstance(row, int):
        return m_ref[layer, row:row + 1, :]
    m = m_ref[layer]
    keep = lax.broadcasted_iota(jnp.int32, m.shape, 0) == row
    return jnp.sum(jnp.where(keep, m, 0.0), axis=0, keepdims=True)


def _pieces(c, nb, t):
    if t >= ROW_CHUNK:
        per_seq = t // ROW_CHUNK
        s = 0 if nb == 1 else c // per_seq
        return [(s, (c - s * per_seq) * ROW_CHUNK, ROW_CHUNK, 0)]
    per_chunk = ROW_CHUNK // t
    return [(c * per_chunk + i, 0, t, i * t) for i in range(per_chunk)]


def _for_chunks(n, body, unrolled=False):
    if unrolled or n == 1:
        for c in range(n):
            body(c)
    else:
        lax.fori_loop(0, n, lambda c, carry: (body(c), carry)[1], 0)


def _pad_row(s, off, t):
    return s * (t + 2 * PAD) + PAD + off


def _store_padded(pad_ref, val, pieces, t):
    for s, off, n, o in pieces:
        pad_ref[_rows(_pad_row(s, off, t), n, SUBLANES), :] = val[o:o + n]


def _scale_padded(pad_ref, val, pieces, t):
    for s, off, n, o in pieces:
        rows = _rows(_pad_row(s, off, t), n, SUBLANES)
        pad_ref[rows, :] = pad_ref[rows, :] * val[o:o + n]


def _modnorm_chunk(src_ref, h_ref, c, nb, t, gain, shift):
    for s, off, n, o in _pieces(c, nb, t):
        for i in range(0, n, NORM_ROWS):
            x = src_ref[s, _rows(off + i, NORM_ROWS, NORM_ROWS), :]
            ms = jnp.mean(x * x, axis=-1, keepdims=True)
            h_ref[_rows(c * ROW_CHUNK + o + i, NORM_ROWS, NORM_ROWS), :] = (
                x * lax.rsqrt(ms + EPS) * gain + shift).astype(BF16)


def _zero_pads(pad_ref, nb, t):
    z = jnp.zeros((PAD, W_HALF), F32)
    for s in range(nb):
        pad_ref[_pad_row(s, 0, t) - PAD:_pad_row(s, 0, t), :] = z
        pad_ref[_pad_row(s, t, t):_pad_row(s, t, t) + PAD, :] = z


def _pool_phase(pad_ref, ga_ref, wp_ref, ps_ref, ab_ref, nb, t):
    n_rows = POOL_ROWS
    per_seq = t // n_rows

    def step(i, carry):
        s = i // per_seq
        r0 = (i - s * per_seq) * n_rows
        prow = _pad_row(s, r0, t)
        rows = _rows(i * n_rows, n_rows, n_rows)
        pos = r0 + lax.broadcasted_iota(jnp.int32, (n_rows, LANES), 0)
        before = jnp.minimum(pos, SUBLANES)
        after = jnp.minimum(t - pos, SUBLANES)
        for g in range(N_POOL_GROUPS):
            hw = POOL_HALF[g]
            ln = _lanes(g)
            halo = n_rows + 2 * SUBLANES
            blk = pad_ref[_rows(prow - SUBLANES, halo, SUBLANES), ln]
            run, n = blk, 1
            while n < 2 * hw:
                run = run + pltpu.roll(run, halo - n, 0)
                n *= 2
            if hw < SUBLANES:
                run = pltpu.roll(run, halo - (SUBLANES - hw), 0)
            win = run[:n_rows]
            cnt = (jnp.minimum(before, hw) + jnp.minimum(after, hw)).astype(F32)
            p = (win / cnt - blk[SUBLANES:SUBLANES + n_rows]).astype(BF16)
            y = _dot(p, wp_ref[0, g].astype(BF16)) * ps_ref[:, ln] * ga_ref[rows, ln]
            ab_ref[rows, ln] = y.astype(BF16)
        return carry
    lax.fori_loop(0, nb * per_seq, step, 0)


def _out_proj_chunk(ab_ref, w_ref, x_ref, gate, dst_ref, c, nb, t):
    lhs = ab_ref[_rows(c * ROW_CHUNK, ROW_CHUNK, ROW_CHUNK), :]
    for g in range(D_MODEL // W_HALF):
        y = _dot(lhs, w_ref[:, _group(g)])
        for s, off, n, o in _pieces(c, nb, t):
            rows = _rows(off, n, n)
            dst_ref[s, rows, _group(g)] = x_ref[s, rows, _group(g)] + gate[:, _group(g)] * y[o:o + n]


def _shift_up(x, o, n):
    if o % SUBLANES == 0:
        return x[o:o + n]
    return pltpu.roll(x, x.shape[0] - o, 0)[:n]


def _conv_phase(pad_c, pad_d, bc_ref, ga_ref, gb_ref, cc_ref, cdw_ref, cdb_ref, lng_ref, lnb_ref,
                ab_ref, nb, t):
    n_rows = CONV_ROWS
    per_seq = t // n_rows

    def step(i, carry):
        s = i // per_seq
        r0 = (i - s * per_seq) * n_rows
        prow = _pad_row(s, r0, t)
        rows = _rows(i * n_rows, n_rows, n_rows)
        z = []
        for g in range(W_HALF // LANES):
            ln = _lanes(g)
            blk = pad_c[_rows(prow - SUBLANES, n_rows + 2 * SUBLANES, SUBLANES), ln]
            c3 = None
            for j in range(CONV_C):
                o = SUBLANES + j - CONV_C // 2
                term = _shift_up(blk, o, n_rows) * cc_ref[j:j + 1, ln]
                c3 = term if c3 is None else c3 + term
            ab_ref[rows, ln] = (bc_ref[rows, ln] * c3 * ga_ref[rows, ln]).astype(BF16)
            acc = None
            for sft in range(SUBLANES):
                part = None
                for a in range((CONV_D - sft + SUBLANES - 1) // SUBLANES):
                    j = SUBLANES * a + sft
                    src = pad_d[_rows(prow - 2 * SUBLANES + SUBLANES * a, n_rows + SUBLANES,
                                      SUBLANES), ln]
                    term = src * cdw_ref[j:j + 1, ln]
                    part = term if part is None else part + term
                o = SUBLANES + sft - (CONV_D // 2 - SUBLANES)
                part = _shift_up(part, o, n_rows)
                acc = part if acc is None else acc + part
            z.append(acc + cdb_ref[:, ln])
        z = jnp.concatenate(z, axis=-1)
        mu = jnp.mean(z, axis=-1, keepdims=True)
        zc = z - mu
        var = jnp.mean(zc * zc, axis=-1, keepdims=True)
        zn = zc * lax.rsqrt(var + EPS) * lng_ref[...] + lnb_ref[...]
        ab_ref[rows, W_HALF:] = (_silu(zn) * gb_ref[rows, :]).astype(BF16)
        return carry
    lax.fori_loop(0, nb * per_seq, step, 0)


def _final_norm_chunk(y_ref, fg, c, nb, t):
    for s, off, n, _ in _pieces(c, nb, t):
        for i in range(0, n, NORM_ROWS):
            rows = _rows(off + i, NORM_ROWS, NORM_ROWS)
            x = y_ref[s, rows, :]
            ms = jnp.mean(x * x, axis=-1, keepdims=True)
            y_ref[s, rows, :] = x * lax.rsqrt(ms + EPS) * fg


def _odd_layer(y_ref, m_row, g_row, fg, wio_ref, cc_ref, cdw_ref, cdb_ref, lng_ref, lnb_ref, woo_ref,
               h_ref, pad_c, pad_d, bc_ref, ga_ref, gb_ref, ab_ref, nb, t, between_phases=(None, None)):
    shift = m_row[:, :D_MODEL]
    gain = g_row * (1.0 + m_row[:, D_MODEL:2 * D_MODEL])
    gate = m_row[:, 2 * D_MODEL:]
    n_chunks = nb * t // ROW_CHUNK

    def in_proj(c):
        _modnorm_chunk(y_ref, h_ref, c, nb, t, gain, shift)
        rows = _rows(c * ROW_CHUNK, ROW_CHUNK, ROW_CHUNK)
        pieces = _pieces(c, nb, t)
        h = h_ref[rows, :]
        bc_ref[rows, :] = _dot(h, wio_ref[:, _group(0)])
        _store_padded(pad_c, _dot(h, wio_ref[:, _group(1)]), pieces, t)
        _scale_padded(pad_c, _dot(h, wio_ref[:, _group(2)]), pieces, t)
        ga_ref[rows, :] = _silu(_dot(h, wio_ref[:, _group(3)]))
        _store_padded(pad_d, _dot(h, wio_ref[:, _group(4)]), pieces, t)
        _scale_padded(pad_d, _sigmoid(_dot(h, wio_ref[:, _group(5)])), pieces, t)
        gb_ref[rows, :] = _silu(_dot(h, wio_ref[:, _group(6)]))
    _for_chunks(n_chunks, in_proj)
    if between_phases[0] is not None:
        between_phases[0]()

    _conv_phase(pad_c, pad_d, bc_ref, ga_ref, gb_ref, cc_ref, cdw_ref, cdb_ref, lng_ref, lnb_ref,
                ab_ref, nb, t)
    if between_phases[1] is not None:
        between_phases[1]()

    def out_proj(c):
        _out_proj_chunk(ab_ref, woo_ref, y_ref, gate, y_ref, c, nb, t)
        _final_norm_chunk(y_ref, fg, c, nb, t)
    _for_chunks(n_chunks, out_proj, unrolled=True)


def _even_in_proj(x_ref, m_row, g_row, w_ref, h_ref, pad_a, ga_ref, gb_ref, q_ref, k_ref, v_ref,
                  kv_t, nb, t):
    shift = m_row[:, :D_MODEL]
    gain = g_row * (1.0 + m_row[:, D_MODEL:2 * D_MODEL])

    def in_proj(c):
        _modnorm_chunk(x_ref, h_ref, c, nb, t, gain, shift)
        rows = _rows(c * ROW_CHUNK, ROW_CHUNK, ROW_CHUNK)
        pieces = _pieces(c, nb, t)
        h = h_ref[rows, :]
        _store_padded(pad_a, _dot(h, w_ref[:, _group(0)]), pieces, t)
        ga_ref[rows, :] = _silu(_dot(h, w_ref[:, _group(1)]))
        q_ref[rows, :] = (_dot(h, w_ref[:, _group(2)]) * Q_SCALE).astype(BF16)
        for i, (dst, g) in enumerate(((k_ref, 3), (v_ref, 4))):
            if kv_t is None:
                dst[rows, :] = _dot(h, w_ref[:, _group(g)]).astype(BF16)
                continue
            acc = _dot_nt(kv_t[2 + i][...], h)
            dst[:, rows] = acc.astype(BF16)
            for s, off, n, o in pieces:
                for hd in range(N_HEADS):
                    kv_t[i][s, 0, hd, :, _rows(off, n, n)] = (
                        acc[hd * HEAD_DIM:(hd + 1) * HEAD_DIM, o:o + n])
        gb_ref[rows, :] = _silu(_dot(h, w_ref[:, _group(5)]))
    _for_chunks(nb * t // ROW_CHUNK, in_proj)


def _even_out_proj(x_ref, y_ref, m_row, w_ref, ab_ref, nb, t):
    gate = m_row[:, 2 * D_MODEL:]
    _for_chunks(nb * t // ROW_CHUNK,
                lambda c: _out_proj_chunk(ab_ref, w_ref, x_ref, gate, y_ref, c, nb, t))


def _split_heads(x):
    lane = lax.broadcasted_iota(jnp.int32, (1, LANES), 1)
    first = jnp.where(lane < HEAD_DIM, 1.0, 0.0).astype(x.dtype)
    return jnp.concatenate([x * first, x * (1 - first)], axis=0)


def _merge_heads(o):
    n = o.shape[0] // 2
    lane = lax.broadcasted_iota(jnp.int32, (n, LANES), 1)
    return jnp.where(lane < HEAD_DIM, o[:n], o[n:])


def _context_attention(q_ref, kt_ref, vt_ref, gb_ref, ab_ref, nb, t):
    for s in range(nb):
        seq = slice(s * t, (s + 1) * t)
        for j in range(N_HEADS // 2):
            ln = _lanes(j)
            kp = kt_ref[ln, seq]
            vp = vt_ref[ln, seq]
            for r0 in range(0, t, Q_ROWS):
                rows = slice(s * t + r0, s * t + r0 + Q_ROWS)
                sc = _dot(_split_heads(q_ref[rows, ln]), kp)
                p = jnp.exp2(sc - jnp.max(sc, axis=-1, keepdims=True))
                o = _dot_nt(p.astype(BF16), vp) / jnp.sum(p, axis=-1, keepdims=True)
                ab_ref[rows, W_HALF + j * LANES:W_HALF + (j + 1) * LANES] = (
                    _merge_heads(o) * gb_ref[rows, ln]).astype(BF16)


def _weight_stager(w_hbm, w_bf, wkt_ref, wvt_ref, stage, sem_in):
    def chunks_of(k):
        return [(k, r0) for r0 in range(0, w_hbm[k].shape[1], STAGE_ROWS)]

    def fetch(k, r0, slot):
        cols = w_hbm[k].shape[2]
        return pltpu.make_async_copy(w_hbm[k].at[0, pl.ds(r0, STAGE_ROWS), :],
                                     stage.at[slot, :, pl.ds(0, cols)], sem_in.at[slot])

    def cast(k, r0, slot):
        cols = w_hbm[k].shape[2]
        rows = slice(r0, r0 + STAGE_ROWS)
        w_bf[k][rows, :] = stage[slot, :, 0:cols].astype(BF16)
        if k == 0:
            wkt_ref[:, rows] = stage[slot, :, _group(3)].T.astype(BF16)
            wvt_ref[:, rows] = stage[slot, :, _group(4)].T.astype(BF16)

    def stream(k):
        chunks = chunks_of(k)
        ahead = STAGE_SLOTS - 1
        for i in range(min(ahead, len(chunks))):
            fetch(*chunks[i], i % STAGE_SLOTS).start()
        for i, c in enumerate(chunks):
            if i + ahead < len(chunks):
                fetch(*chunks[i + ahead], (i + ahead) % STAGE_SLOTS).start()
            fetch(*c, i % STAGE_SLOTS).wait()
            cast(*c, i % STAGE_SLOTS)

    later = [c for k in range(1, len(w_hbm)) for c in chunks_of(k)]
    rounds = [later[i:i + STAGE_SLOTS] for i in range(0, len(later), STAGE_SLOTS)]

    def issue(r):
        for slot, c in enumerate(rounds[r]):
            fetch(*c, slot).start()

    def finish(r):
        for slot, c in enumerate(rounds[r]):
            fetch(*c, slot).wait()
            cast(*c, slot)
        last = {k: max(i for i, rnd in enumerate(rounds) if any(c[0] == k for c in rnd))
                for k in range(1, len(w_hbm))}
        return [k for k, i in last.items() if i == r]

    return stream, issue, finish, rounds


def _prompt_body(x_ref, m_ref, ng_ref, fg_ref, wp_ref, ps_ref, cc_ref, cdw_ref, cdb_ref, lng_ref,
                 lnb_ref, wie_hbm, woe_hbm, wio_hbm, woo_hbm,
                 y_ref, ko_ref, vo_ref, wie_out, woe_out, wio_out, woo_out,
                 h_ref, pad_a, pad_b, ga_ref, gb_ref, bc_ref, q_ref, kt_ref, vt_ref, ab_ref,
                 wie_ref, woe_ref, wio_ref, woo_ref, wkt_ref, wvt_ref, stage, sem_in, sem_out,
                 *, nb, t):
    w_out = (wie_out, woe_out, wio_out, woo_out)
    w_bf = (wie_ref, woe_ref, wio_ref, woo_ref)
    stream, issue, finish, rounds = _weight_stager((wie_hbm, woe_hbm, wio_hbm, woo_hbm), w_bf,
                                                   wkt_ref, wvt_ref, stage, sem_in)
    assert len(rounds) == 6
    assert all(k == 1 for k, _ in rounds[0] + rounds[1])
    assert all(k <= 2 for rnd in rounds[:4] for k, _ in rnd)
    assert all(k == 3 for k, _ in rounds[4] + rounds[5])

    def write_back(k):
        return pltpu.make_async_copy(w_bf[k], w_out[k], sem_out.at[k])

    def at_first_step(fn):
        pl.when(pl.program_id(0) == 0)(fn)

    def turn(r):
        def fn():
            for k in finish(r):
                write_back(k).start()
            if r + 1 < len(rounds):
                issue(r + 1)
        return lambda: at_first_step(fn)

    def first_weight():
        stream(0)
        write_back(0).start()
        issue(0)
    at_first_step(first_weight)

    _zero_pads(pad_a, nb, t)
    _zero_pads(pad_b, nb, t)
    m_even = _cond_row(m_ref, 0, 0)
    _even_in_proj(x_ref, m_even, ng_ref[0:1, :], wie_ref, h_ref, pad_a, ga_ref, gb_ref,
                  q_ref, kt_ref, vt_ref, (ko_ref, vo_ref, wkt_ref, wvt_ref), nb, t)
    turn(0)()
    _pool_phase(pad_a, ga_ref, wp_ref, ps_ref, ab_ref, nb, t)
    turn(1)()
    _context_attention(q_ref, kt_ref, vt_ref, gb_ref, ab_ref, nb, t)
    turn(2)()
    _even_out_proj(x_ref, y_ref, m_even, woe_ref, ab_ref, nb, t)
    turn(3)()
    _odd_layer(y_ref, _cond_row(m_ref, 1, 0), ng_ref[1:2, :], fg_ref[...], wio_ref, cc_ref, cdw_ref,
               cdb_ref, lng_ref, lnb_ref, woo_ref, h_ref, pad_a, pad_b, bc_ref, ga_ref, gb_ref, ab_ref,
               nb, t, between_phases=(turn(4), turn(5)))

    def wait_write_backs():
        for k in range(len(w_bf)):
            write_back(k).wait()
    at_first_step(wait_write_backs)


def _rpb_rows(rpb_ref, e_ref):
    n = rpb_ref.shape[0] * rpb_ref.shape[1]
    lane = lax.broadcasted_iota(jnp.int32, (n, LANES), 1)
    i = jnp.where(lane < GRID_W, lane, lane - LANES)
    idx = jnp.clip(i, -(WIN_W - 1), WIN_W - 1) + (WIN_W - 1)
    rp = rpb_ref[...].reshape(n, rpb_ref.shape[2])
    e = jnp.zeros((n, LANES), F32)
    for d in range(2 * WIN_W - 1):
        e = jnp.where(idx == d, rp[:, d:d + 1], e)
    e_ref[...] = e


N_DR = 2 * WIN_H - 1
PAIR_TILES = N_DR // 2


def _bias_tile_index(j, dr_lo):
    if isinstance(dr_lo, int):
        parity, half = dr_lo % 2, dr_lo // 2
    else:
        parity, half = dr_lo & 1, lax.shift_right_logical(dr_lo, 1)
    return (2 * j + parity) * PAIR_TILES + half


def _bias_tables(e_ref, bias_ref):
    q = lax.broadcasted_iota(jnp.int32, (GRID_W, LANES), 0)
    lane = lax.broadcasted_iota(jnp.int32, (GRID_W, LANES), 1)
    kw = jnp.where(lane < GRID_W, lane, lane - GRID_W)
    start = jnp.clip(q - WIN_W // 2, 0, GRID_W - WIN_W)
    col_ok = (kw >= start) & (kw < start + WIN_W)
    for j in range(N_HEADS // 2):
        for dr in range(N_DR - 1):
            for e in range(2):
                r_lo = dr * N_HEADS + 2 * j + e
                r_hi = r_lo + N_HEADS
                lo = jnp.broadcast_to(e_ref[r_lo:r_lo + 1, :], (GRID_W, LANES))
                hi = jnp.broadcast_to(e_ref[r_hi:r_hi + 1, :], (GRID_W, LANES))
                lo = pltpu.roll(lo, 0, 1, stride=1, stride_axis=0)
                hi = pltpu.roll(hi, GRID_W, 1, stride=1, stride_axis=0)
                tile = jnp.where(lane < GRID_W, lo, hi)
                bias_ref[_bias_tile_index(j, dr), e * GRID_W:(e + 1) * GRID_W, :] = jnp.where(
                    col_ok, tile * LOG2_E, MASKED)


def _neighbourhood_attention(q_ref, k_ref, v_ref, ck_ref, cv_ref, bias_ref, kvc_ref, gb_ref, ab_ref, t):
    grid_h = t // GRID_W
    band = WIN_H * GRID_W
    for j in range(N_HEADS // 2):
        ln = _lanes(j)
        for i, src in enumerate((ck_ref, cv_ref)):
            kvc_ref[i] = jnp.concatenate([src[0, 0, 2 * j], src[0, 0, 2 * j + 1]],
                                         axis=0).astype(BF16)

        def per_group(g, carry, ln=ln, j=j):
            scored = []
            for u in range(NA_GROUP):
                r = g * NA_GROUP + u
                start = jnp.clip(r - WIN_H // 2, 0, grid_h - WIN_H)
                rows = _rows(r * GRID_W, GRID_W, GRID_W)
                keys = _rows(start * GRID_W, band, GRID_W)
                q2 = _split_heads(q_ref[rows, ln])
                dr0 = (WIN_H - 1) - (r - start)
                bias = jnp.concatenate([bias_ref[_bias_tile_index(j, dr0 + 2 * i)]
                                        for i in range(WIN_H // 2)], axis=-1)
                scored.append((rows, keys, _dot_nt(q2, k_ref[keys, ln]) + bias, _dot(q2, kvc_ref[0])))
            weighted = []
            for rows, keys, s_loc, s_ctx in scored:
                mx = jnp.maximum(jnp.max(s_loc, axis=-1, keepdims=True),
                                 jnp.max(s_ctx, axis=-1, keepdims=True))
                p_loc = jnp.exp2(s_loc - mx)
                p_ctx = jnp.exp2(s_ctx - mx)
                den = (jnp.sum(p_loc, axis=-1, keepdims=True)
                       + jnp.sum(p_ctx, axis=-1, keepdims=True))
                weighted.append((rows, keys, p_loc.astype(BF16), p_ctx.astype(BF16), den))
            for rows, keys, p_loc, p_ctx, den in weighted:
                o = (_dot(p_loc, v_ref[keys, ln]) + _dot_nt(p_ctx, kvc_ref[1])) / den
                ab_ref[rows, W_HALF + j * LANES:W_HALF + (j + 1) * LANES] = (
                    _merge_heads(o) * gb_ref[rows, ln]).astype(BF16)
            return carry
        lax.fori_loop(0, grid_h // NA_GROUP, per_group, 0)


def _sample_body(x_ref, m_ref, ng_ref, fg_ref, wie_ref, wp_ref, ps_ref, woe_hbm, wio_hbm, cc_ref,
                 cdw_ref, cdb_ref, lng_ref, lnb_ref, woo_hbm, ck_ref, cv_ref, rpb_ref,
                 y_ref,
                 h_ref, pad_a, pad_b, ga_ref, gb_ref, bc_ref, q_ref, k_ref, v_ref, ab_ref,
                 e_ref, bias_ref, kvc_ref, woe_ref, wio_ref, woo_ref, sem_w, *, t):
    _zero_pads(pad_a, 1, t)
    _zero_pads(pad_b, 1, t)
    late = ((woe_hbm, woe_ref), (wio_hbm, wio_ref), (woo_hbm, woo_ref))

    def late_copy(i):
        return pltpu.make_async_copy(late[i][0], late[i][1], sem_w.at[i])

    first_step = pl.program_id(0) == 0

    @pl.when(first_step)
    def _():
        for i in range(len(late)):
            late_copy(i).start()
        _rpb_rows(rpb_ref, e_ref)
        _bias_tables(e_ref, bias_ref)

    cond = pl.program_id(0) + 1
    m_even = _cond_row(m_ref, 0, cond)
    _even_in_proj(x_ref, m_even, ng_ref[0:1, :], wie_ref, h_ref, pad_a, ga_ref, gb_ref,
                  q_ref, k_ref, v_ref, None, 1, t)
    _pool_phase(pad_a, ga_ref, wp_ref, ps_ref, ab_ref, 1, t)
    _neighbourhood_attention(q_ref, k_ref, v_ref, ck_ref, cv_ref, bias_ref, kvc_ref, gb_ref, ab_ref, t)
    pl.when(first_step)(lambda: late_copy(0).wait())
    _even_out_proj(x_ref, y_ref, m_even, woe_ref, ab_ref, 1, t)

    @pl.when(first_step)
    def _():
        late_copy(1).wait()
        late_copy(2).wait()

    _odd_layer(y_ref, _cond_row(m_ref, 1, cond), ng_ref[1:2, :], fg_ref[...], wio_ref, cc_ref, cdw_ref,
               cdb_ref, lng_ref, lnb_ref, woo_ref, h_ref, pad_a, pad_b, bc_ref, ga_ref, gb_ref, ab_ref,
               1, t)


def _const_spec(shape):
    zeros = (0,) * len(shape)
    return pl.BlockSpec(shape, lambda i: zeros, pipeline_mode=pl.Buffered(1))


def _stream_scratch(nb, t, kv_transposed):
    r = nb * t
    padded = nb * (t + 2 * PAD)
    kv = (W_HALF, r) if kv_transposed else (r, W_HALF)
    return [
        pltpu.VMEM((r, D_MODEL), BF16),
        pltpu.VMEM((padded, W_HALF), F32),
        pltpu.VMEM((padded, W_HALF), F32),
        pltpu.VMEM((r, W_HALF), F32),
        pltpu.VMEM((r, W_HALF), F32),
        pltpu.VMEM((r, W_HALF), F32),
        pltpu.VMEM((r, W_HALF), BF16),
        pltpu.VMEM(kv, BF16),
        pltpu.VMEM(kv, BF16),
        pltpu.VMEM((r, D_MODEL), BF16),
    ]


def _small_params(norm_g, final_g, w_pool, pool_scale, conv_c, conv_d, conv_d_b, ln_g, ln_b):
    return [norm_g, final_g.reshape(1, D_MODEL), w_pool, pool_scale, conv_c[0], conv_d[0], conv_d_b,
            ln_g, ln_b]


def kernel(x_prompt, x_sample, cache_k, cache_v, c, c_ctx, norm_g, w_mod, b_mod, w_in_even, w_pool,
           pool_scale, rpb, w_out_even, w_in_odd, conv_c, conv_d, conv_d_b, ln_g, ln_b, w_out_odd,
           final_g):
    batch, seq, d = x_prompt.shape
    dec_batch, dec_seq, _ = x_sample.shape
    assert d == D_MODEL and w_mod.shape[0] == 2 and w_in_even.shape[0] == 1 and w_in_odd.shape[0] == 1
    assert (NB_PROMPT * seq) % ROW_CHUNK == 0 and ROW_CHUNK % seq == 0 and seq % Q_ROWS == 0
    assert dec_seq % ROW_CHUNK == 0 and dec_seq // GRID_W >= WIN_H
    assert seq % POOL_ROWS == 0 and seq % CONV_ROWS == 0
    assert dec_seq % POOL_ROWS == 0 and dec_seq % CONV_ROWS == 0
    assert (dec_seq // GRID_W) % NA_GROUP == 0

    cond_rows = SUBLANES * ((1 + dec_batch + SUBLANES - 1) // SUBLANES)
    m = _modulation(c_ctx, c, w_mod, b_mod, cond_rows)
    m_spec = _const_spec(m.shape)

    small = _small_params(norm_g, final_g, w_pool, pool_scale, conv_c, conv_d, conv_d_b, ln_g, ln_b)
    small_specs = [_const_spec(a.shape) for a in small]
    w_f32 = (w_in_even, w_out_even, w_in_odd, w_out_odd)
    assert all(w.shape[0] == 1 and w.shape[1] % STAGE_ROWS == 0 for w in w_f32)
    any_spec = pl.BlockSpec(memory_space=pl.ANY)

    nb = NB_PROMPT
    assert batch % nb == 0
    kv_shape = jax.ShapeDtypeStruct((batch, 1, N_HEADS, HEAD_DIM, seq), F32)
    kv_spec = pl.BlockSpec((nb, 1, N_HEADS, HEAD_DIM, seq), lambda i: (i, 0, 0, 0, 0))
    y_prompt, new_kt, new_vt, wie, woe, wio, woo = pl.pallas_call(
        functools.partial(_prompt_body, nb=nb, t=seq),
        out_shape=(jax.ShapeDtypeStruct(x_prompt.shape, F32), kv_shape, kv_shape)
                  + tuple(jax.ShapeDtypeStruct(w.shape[1:], BF16) for w in w_f32),
        grid=(batch // nb,),
        in_specs=[pl.BlockSpec((nb, seq, d), lambda i: (i, 0, 0)), m_spec] + small_specs
                 + [any_spec] * len(w_f32),
        out_specs=(pl.BlockSpec((nb, seq, d), lambda i: (i, 0, 0)), kv_spec, kv_spec)
                  + (any_spec,) * len(w_f32),
        scratch_shapes=_stream_scratch(nb, seq, True)
                       + [pltpu.VMEM(w.shape[1:], BF16) for w in w_f32] + [
            pltpu.VMEM((W_HALF, d), BF16),
            pltpu.VMEM((W_HALF, d), BF16),
            pltpu.VMEM((STAGE_SLOTS, STAGE_ROWS, max(w.shape[2] for w in w_f32)), F32),
            pltpu.SemaphoreType.DMA((STAGE_SLOTS,)),
            pltpu.SemaphoreType.DMA((len(w_f32),)),
        ],
        compiler_params=pltpu.CompilerParams(dimension_semantics=("arbitrary",),
                                             vmem_limit_bytes=VMEM_LIMIT),
        name="prompt",
    )(x_prompt, m, *small, *w_f32)
    ng, fg, wp, ps, cc, cdw, cdb, lng, lnb = small
    w_args = [ng, fg, wie, wp, ps, woe, wio, cc, cdw, cdb, lng, lnb, woo]
    late_w = (woe, wio, woo)
    w_specs = [any_spec if any(a is w for w in late_w) else _const_spec(a.shape) for a in w_args]

    past = cache_k.shape[3]
    cache_spec = pl.BlockSpec((1, 1, N_HEADS, HEAD_DIM, past), lambda i: (i, 0, 0, 0, 0))
    rpb_t = jnp.swapaxes(rpb[0], 0, 1)
    y_sample = pl.pallas_call(
        functools.partial(_sample_body, t=dec_seq),
        out_shape=jax.ShapeDtypeStruct(x_sample.shape, F32),
        grid=(dec_batch,),
        in_specs=[pl.BlockSpec((1, dec_seq, d), lambda i: (i, 0, 0), pipeline_mode=pl.Buffered(1)),
                  m_spec] + w_specs
                 + [cache_spec, cache_spec, _const_spec(rpb_t.shape)],
        out_specs=pl.BlockSpec((1, dec_seq, d), lambda i: (i, 0, 0)),
        scratch_shapes=_stream_scratch(1, dec_seq, False) + [
            pltpu.VMEM((N_DR * N_HEADS, LANES), F32),
            pltpu.VMEM((N_HEADS * PAIR_TILES, 2 * GRID_W, LANES), F32),
            pltpu.VMEM((2, LANES, past), BF16),
        ] + [pltpu.VMEM(w.shape, BF16) for w in late_w] + [pltpu.SemaphoreType.DMA((len(late_w),))],
        compiler_params=pltpu.CompilerParams(dimension_semantics=("arbitrary",),
                                             vmem_limit_bytes=VMEM_LIMIT),
        name="sample",
    )(x_sample, m, *w_args, jnp.swapaxes(cache_k, 3, 4), jnp.swapaxes(cache_v, 3, 4), rpb_t)

    return (y_prompt, y_sample, jnp.swapaxes(new_kt, 3, 4), jnp.swapaxes(new_vt, 3, 4))
```

```python
import functools

import jax
import jax.numpy as jnp
from jax import lax
from jax.experimental import pallas as pl
from jax.experimental.pallas import tpu as pltpu

F32 = jnp.float32
BF16 = jnp.bfloat16

D_MODEL = 1024
W_HALF = 512
N_POOL_GROUPS = 4
POOL_HALF = (1, 2, 4, 8)
N_HEADS = 8
HEAD_DIM = 64
GRID_W = 64
WIN_H = 8
WIN_W = 16
CONV_C = 3
CONV_D = 31
EPS = 1e-6
MASKED = -1e30
LOG2_E = 1.4426950408889634
Q_SCALE = HEAD_DIM ** -0.5 * LOG2_E

LANES = 128
SUBLANES = 8
PAD = 16
ROW_CHUNK = 512
NORM_ROWS = 32
POOL_ROWS = 256
CONV_ROWS = 128
Q_ROWS = 128
NB_PROMPT = 2
NA_GROUP = 8
MOD_ROWS = 512
STAGE_ROWS = 128
STAGE_SLOTS = 4
VMEM_LIMIT = 58 * 1024 * 1024

assert PAD >= CONV_D // 2 + 1 and PAD % SUBLANES == 0 and PAD >= 2 * SUBLANES
assert max(POOL_HALF) <= SUBLANES


def _sigmoid(x):
    return 1.0 / (1.0 + jnp.exp(-x))


def _silu(x):
    return x * _sigmoid(x)


def _dot(a, b):
    return jnp.dot(a, b, preferred_element_type=F32)


def _dot_nt(a, b):
    return lax.dot_general(a, b, (((1,), (1,)), ((), ())), preferred_element_type=F32)


def _lanes(j):
    return slice(j * LANES, (j + 1) * LANES)


def _group(g):
    return slice(g * W_HALF, (g + 1) * W_HALF)


def _rows(start, size, align):
    if isinstance(start, int):
        return slice(start, start + size)
    return pl.ds(pl.multiple_of(start, align), size)


def _mod_body(cctx_ref, c_ref, w_ref, b_ref, o_ref, act_ref):
    layer, kb = pl.program_id(0), pl.program_id(1)
    rows, d = act_ref.shape

    @pl.when(kb == 0)
    def _():
        r = lax.broadcasted_iota(jnp.int32, (rows, d), 0)
        cond = jnp.where(r == 0, cctx_ref[...], 0.0)
        for i in range(c_ref.shape[0]):
            cond = jnp.where(r == i + 1, c_ref[i:i + 1, :], cond)
        act_ref[...] = _silu(cond).astype(BF16)
        o_ref[0] = jnp.broadcast_to(jnp.where(layer == 0, b_ref[0:1, :], b_ref[1:2, :]), o_ref.shape[1:])

    act = act_ref[:, pl.ds(pl.multiple_of(kb * MOD_ROWS, MOD_ROWS), MOD_ROWS)]
    o_ref[0] += _dot(act, w_ref[0].astype(BF16))


def _modulation(c_ctx, c, w_mod, b_mod, rows):
    depth, d, n = w_mod.shape
    assert depth == 2 and 1 + c.shape[0] <= rows and d % MOD_ROWS == 0
    return pl.pallas_call(
        _mod_body,
        out_shape=jax.ShapeDtypeStruct((depth, rows, n), F32),
        grid=(depth, d // MOD_ROWS),
        in_specs=[
            pl.BlockSpec((1, d), lambda l, k: (0, 0)),
            pl.BlockSpec(c.shape, lambda l, k: (0, 0)),
            pl.BlockSpec((1, MOD_ROWS, n), lambda l, k: (l, k, 0)),
            pl.BlockSpec((depth, n), lambda l, k: (0, 0)),
        ],
        out_specs=pl.BlockSpec((1, rows, n), lambda l, k: (l, 0, 0)),
        scratch_shapes=[pltpu.VMEM((rows, d), BF16)],
        compiler_params=pltpu.CompilerParams(dimension_semantics=("arbitrary", "arbitrary")),
        name="mod",
    )(c_ctx.reshape(1, d), c, w_mod, b_mod)


def _cond_row(m_ref, layer, row):
    if isinstance(row, int):
        return m_ref[layer, row:row + 1, :]
    m = m_ref[layer]
    keep = lax.broadcasted_iota(jnp.int32, m.shape, 0) == row
    return jnp.sum(jnp.where(keep, m, 0.0), axis=0, keepdims=True)


def _pieces(c, nb, t):
    if t >= ROW_CHUNK:
        per_seq = t // ROW_CHUNK
        s = 0 if nb == 1 else c // per_seq
        return [(s, (c - s * per_seq) * ROW_CHUNK, ROW_CHUNK, 0)]
    per_chunk = ROW_CHUNK // t
    return [(c * per_chunk + i, 0, t, i * t) for i in range(per_chunk)]


def _for_chunks(n, body, unrolled=False):
    if unrolled or n == 1:
        for c in range(n):
            body(c)
    else:
        lax.fori_loop(0, n, lambda c, carry: (body(c), carry)[1], 0)


def _pad_row(s, off, t):
    return s * (t + 2 * PAD) + PAD + off


def _store_padded(pad_ref, val, pieces, t):
    for s, off, n, o in pieces:
        pad_ref[_rows(_pad_row(s, off, t), n, SUBLANES), :] = val[o:o + n]


def _scale_padded(pad_ref, val, pieces, t):
    for s, off, n, o in pieces:
        rows = _rows(_pad_row(s, off, t), n, SUBLANES)
        pad_ref[rows, :] = pad_ref[rows, :] * val[o:o + n]


def _modnorm_chunk(src_ref, h_ref, c, nb, t, gain, shift):
    for s, off, n, o in _pieces(c, nb, t):
        for i in range(0, n, NORM_ROWS):
            x = src_ref[s, _rows(off + i, NORM_ROWS, NORM_ROWS), :]
            ms = jnp.mean(x * x, axis=-1, keepdims=True)
            h_ref[_rows(c * ROW_CHUNK + o + i, NORM_ROWS, NORM_ROWS), :] = (
                x * lax.rsqrt(ms + EPS) * gain + shift).astype(BF16)


def _zero_pads(pad_ref, nb, t):
    z = jnp.zeros((PAD, W_HALF), F32)
    for s in range(nb):
        pad_ref[_pad_row(s, 0, t) - PAD:_pad_row(s, 0, t), :] = z
        pad_ref[_pad_row(s, t, t):_pad_row(s, t, t) + PAD, :] = z


def _pool_phase(pad_ref, ga_ref, wp_ref, ps_ref, ab_ref, nb, t):
    n_rows = POOL_ROWS
    per_seq = t // n_rows

    def step(i, carry):
        s = i // per_seq
        r0 = (i - s * per_seq) * n_rows
        prow = _pad_row(s, r0, t)
        rows = _rows(i * n_rows, n_rows, n_rows)
        pos = r0 + lax.broadcasted_iota(jnp.int32, (n_rows, LANES), 0)
        before = jnp.minimum(pos, SUBLANES)
        after = jnp.minimum(t - pos, SUBLANES)
        for g in range(N_POOL_GROUPS):
            hw = POOL_HALF[g]
            ln = _lanes(g)
            halo = n_rows + 2 * SUBLANES
            blk = pad_ref[_rows(prow - SUBLANES, halo, SUBLANES), ln]
            run, n = blk, 1
            while n < 2 * hw:
                run = run + pltpu.roll(run, halo - n, 0)
                n *= 2
            if hw < SUBLANES:
                run = pltpu.roll(run, halo - (SUBLANES - hw), 0)
            win = run[:n_rows]
            cnt = (jnp.minimum(before, hw) + jnp.minimum(after, hw)).astype(F32)
            p = (win / cnt - blk[SUBLANES:SUBLANES + n_rows]).astype(BF16)
            y = _dot(p, wp_ref[0, g].astype(BF16)) * ps_ref[:, ln] * ga_ref[rows, ln]
            ab_ref[rows, ln] = y.astype(BF16)
        return carry
    lax.fori_loop(0, nb * per_seq, step, 0)


def _out_proj_chunk(ab_ref, w_ref, x_ref, gate, dst_ref, c, nb, t):
    lhs = ab_ref[_rows(c * ROW_CHUNK, ROW_CHUNK, ROW_CHUNK), :]
    for g in range(D_MODEL // W_HALF):
        y = _dot(lhs, w_ref[:, _group(g)])
        for s, off, n, o in _pieces(c, nb, t):
            rows = _rows(off, n, n)
            dst_ref[s, rows, _group(g)] = x_ref[s, rows, _group(g)] + gate[:, _group(g)] * y[o:o + n]


def _shift_up(x, o, n):
    if o % SUBLANES == 0:
        return x[o:o + n]
    return pltpu.roll(x, x.shape[0] - o, 0)[:n]


def _conv_phase(pad_c, pad_d, bc_ref, ga_ref, gb_ref, cc_ref, cdw_ref, cdb_ref, lng_ref, lnb_ref,
                ab_ref, nb, t):
    n_rows = CONV_ROWS
    per_seq = t // n_rows

    def step(i, carry):
        s = i // per_seq
        r0 = (i - s * per_seq) * n_rows
        prow = _pad_row(s, r0, t)
        rows = _rows(i * n_rows, n_rows, n_rows)
        z = []
        for g in range(W_HALF // LANES):
            ln = _lanes(g)
            blk = pad_c[_rows(prow - SUBLANES, n_rows + 2 * SUBLANES, SUBLANES), ln]
            c3 = None
            for j in range(CONV_C):
                o = SUBLANES + j - CONV_C // 2
                term = _shift_up(blk, o, n_rows) * cc_ref[j, :, ln]
                c3 = term if c3 is None else c3 + term
            ab_ref[rows, ln] = (bc_ref[rows, ln] * c3 * ga_ref[rows, ln]).astype(BF16)
            acc = None
            for sft in range(SUBLANES):
                part = None
                for a in range((CONV_D - sft + SUBLANES - 1) // SUBLANES):
                    j = SUBLANES * a + sft
                    src = pad_d[_rows(prow - 2 * SUBLANES + SUBLANES * a, n_rows + SUBLANES,
                                      SUBLANES), ln]
                    term = src * cdw_ref[j, :, ln]
                    part = term if part is None else part + term
                o = SUBLANES + sft - (CONV_D // 2 - SUBLANES)
                part = _shift_up(part, o, n_rows)
                acc = part if acc is None else acc + part
            z.append(acc + cdb_ref[:, ln])
        z = jnp.concatenate(z, axis=-1)
        mu = jnp.mean(z, axis=-1, keepdims=True)
        zc = z - mu
        var = jnp.mean(zc * zc, axis=-1, keepdims=True)
        zn = zc * lax.rsqrt(var + EPS) * lng_ref[...] + lnb_ref[...]
        ab_ref[rows, W_HALF:] = (_silu(zn) * gb_ref[rows, :]).astype(BF16)
        return carry
    lax.fori_loop(0, nb * per_seq, step, 0)


def _final_norm_chunk(y_ref, fg, c, nb, t):
    for s, off, n, _ in _pieces(c, nb, t):
        for i in range(0, n, NORM_ROWS):
            rows = _rows(off + i, NORM_ROWS, NORM_ROWS)
            x = y_ref[s, rows, :]
            ms = jnp.mean(x * x, axis=-1, keepdims=True)
            y_ref[s, rows, :] = x * lax.rsqrt(ms + EPS) * fg


def _odd_layer(y_ref, m_row, g_row, fg, wio_ref, cc_ref, cdw_ref, cdb_ref, lng_ref, lnb_ref, woo_ref,
               h_ref, pad_c, pad_d, bc_ref, ga_ref, gb_ref, ab_ref, nb, t, between_phases=(None, None)):
    shift = m_row[:, :D_MODEL]
    gain = g_row * (1.0 + m_row[:, D_MODEL:2 * D_MODEL])
    gate = m_row[:, 2 * D_MODEL:]
    n_chunks = nb * t // ROW_CHUNK

    def in_proj(c):
        _modnorm_chunk(y_ref, h_ref, c, nb, t, gain, shift)
        rows = _rows(c * ROW_CHUNK, ROW_CHUNK, ROW_CHUNK)
        pieces = _pieces(c, nb, t)
        h = h_ref[rows, :]
        bc_ref[rows, :] = _dot(h, wio_ref[:, _group(0)])
        _store_padded(pad_c, _dot(h, wio_ref[:, _group(1)]), pieces, t)
        _scale_padded(pad_c, _dot(h, wio_ref[:, _group(2)]), pieces, t)
        ga_ref[rows, :] = _silu(_dot(h, wio_ref[:, _group(3)]))
        _store_padded(pad_d, _dot(h, wio_ref[:, _group(4)]), pieces, t)
        _scale_padded(pad_d, _sigmoid(_dot(h, wio_ref[:, _group(5)])), pieces, t)
        gb_ref[rows, :] = _silu(_dot(h, wio_ref[:, _group(6)]))
    _for_chunks(n_chunks, in_proj)
    if between_phases[0] is not None:
        between_phases[0]()

    _conv_phase(pad_c, pad_d, bc_ref, ga_ref, gb_ref, cc_ref, cdw_ref, cdb_ref, lng_ref, lnb_ref,
                ab_ref, nb, t)
    if between_phases[1] is not None:
        between_phases[1]()

    def out_proj(c):
        _out_proj_chunk(ab_ref, woo_ref, y_ref, gate, y_ref, c, nb, t)
        _final_norm_chunk(y_ref, fg, c, nb, t)
    _for_chunks(n_chunks, out_proj, unrolled=True)


def _even_in_proj(x_ref, m_row, g_row, w_ref, h_ref, pad_a, ga_ref, gb_ref, q_ref, k_ref, v_ref,
                  kv_t, nb, t):
    shift = m_row[:, :D_MODEL]
    gain = g_row * (1.0 + m_row[:, D_MODEL:2 * D_MODEL])

    def in_proj(c):
        _modnorm_chunk(x_ref, h_ref, c, nb, t, gain, shift)
        rows = _rows(c * ROW_CHUNK, ROW_CHUNK, ROW_CHUNK)
        pieces = _pieces(c, nb, t)
        h = h_ref[rows, :]
        _store_padded(pad_a, _dot(h, w_ref[:, _group(0)]), pieces, t)
        ga_ref[rows, :] = _silu(_dot(h, w_ref[:, _group(1)]))
        q_ref[rows, :] = (_dot(h, w_ref[:, _group(2)]) * Q_SCALE).astype(BF16)
        for i, (dst, g) in enumerate(((k_ref, 3), (v_ref, 4))):
            if kv_t is None:
                dst[rows, :] = _dot(h, w_ref[:, _group(g)]).astype(BF16)
                continue
            acc = _dot_nt(kv_t[2 + i][...], h)
            dst[:, rows] = acc.astype(BF16)
            for s, off, n, o in pieces:
                for hd in range(N_HEADS):
                    kv_t[i][s, 0, hd, :, _rows(off, n, n)] = (
                        acc[hd * HEAD_DIM:(hd + 1) * HEAD_DIM, o:o + n])
        gb_ref[rows, :] = _silu(_dot(h, w_ref[:, _group(5)]))
    _for_chunks(nb * t // ROW_CHUNK, in_proj)


def _even_out_proj(x_ref, y_ref, m_row, w_ref, ab_ref, nb, t):
    gate = m_row[:, 2 * D_MODEL:]
    _for_chunks(nb * t // ROW_CHUNK,
                lambda c: _out_proj_chunk(ab_ref, w_ref, x_ref, gate, y_ref, c, nb, t))


def _split_heads(x):
    lane = lax.broadcasted_iota(jnp.int32, (1, LANES), 1)
    first = jnp.where(lane < HEAD_DIM, 1.0, 0.0).astype(x.dtype)
    return jnp.concatenate([x * first, x * (1 - first)], axis=0)


def _merge_heads(o):
    n = o.shape[0] // 2
    lane = lax.broadcasted_iota(jnp.int32, (n, LANES), 1)
    return jnp.where(lane < HEAD_DIM, o[:n], o[n:])


def _context_attention(q_ref, kt_ref, vt_ref, gb_ref, ab_ref, nb, t):
    for s in range(nb):
        seq = slice(s * t, (s + 1) * t)
        for j in range(N_HEADS // 2):
            ln = _lanes(j)
            kp = kt_ref[ln, seq]
            vp = vt_ref[ln, seq]
            for r0 in range(0, t, Q_ROWS):
                rows = slice(s * t + r0, s * t + r0 + Q_ROWS)
                sc = _dot(_split_heads(q_ref[rows, ln]), kp)
                p = jnp.exp2(sc - jnp.max(sc, axis=-1, keepdims=True))
                o = _dot_nt(p.astype(BF16), vp) / jnp.sum(p, axis=-1, keepdims=True)
                ab_ref[rows, W_HALF + j * LANES:W_HALF + (j + 1) * LANES] = (
                    _merge_heads(o) * gb_ref[rows, ln]).astype(BF16)


def _weight_stager(w_hbm, w_bf, wkt_ref, wvt_ref, stage, sem_in):
    def chunks_of(k):
        return [(k, r0) for r0 in range(0, w_hbm[k].shape[1], STAGE_ROWS)]

    def fetch(k, r0, slot):
        cols = w_hbm[k].shape[2]
        return pltpu.make_async_copy(w_hbm[k].at[0, pl.ds(r0, STAGE_ROWS), :],
                                     stage.at[slot, :, pl.ds(0, cols)], sem_in.at[slot])

    def cast(k, r0, slot):
        cols = w_hbm[k].shape[2]
        rows = slice(r0, r0 + STAGE_ROWS)
        w_bf[k][rows, :] = stage[slot, :, 0:cols].astype(BF16)
        if k == 0:
            wkt_ref[:, rows] = stage[slot, :, _group(3)].T.astype(BF16)
            wvt_ref[:, rows] = stage[slot, :, _group(4)].T.astype(BF16)

    def stream(k):
        chunks = chunks_of(k)
        ahead = STAGE_SLOTS - 1
        for i in range(min(ahead, len(chunks))):
            fetch(*chunks[i], i % STAGE_SLOTS).start()
        for i, c in enumerate(chunks):
            if i + ahead < len(chunks):
                fetch(*chunks[i + ahead], (i + ahead) % STAGE_SLOTS).start()
            fetch(*c, i % STAGE_SLOTS).wait()
            cast(*c, i % STAGE_SLOTS)

    later = [c for k in range(1, len(w_hbm)) for c in chunks_of(k)]
    rounds = [later[i:i + STAGE_SLOTS] for i in range(0, len(later), STAGE_SLOTS)]

    def issue(r):
        for slot, c in enumerate(rounds[r]):
            fetch(*c, slot).start()

    def finish(r):
        for slot, c in enumerate(rounds[r]):
            fetch(*c, slot).wait()
            cast(*c, slot)
        last = {k: max(i for i, rnd in enumerate(rounds) if any(c[0] == k for c in rnd))
                for k in range(1, len(w_hbm))}
        return [k for k, i in last.items() if i == r]

    return stream, issue, finish, rounds


def _prompt_body(x_ref, m_ref, ng_ref, fg_ref, wp_ref, ps_ref, cc_ref, cdw_ref, cdb_ref, lng_ref,
                 lnb_ref, wie_hbm, woe_hbm, wio_hbm, woo_hbm,
                 y_ref, ko_ref, vo_ref, wie_out, woe_out, wio_out, woo_out,
                 h_ref, pad_a, pad_b, ga_ref, gb_ref, bc_ref, q_ref, kt_ref, vt_ref, ab_ref,
                 wie_ref, woe_ref, wio_ref, woo_ref, wkt_ref, wvt_ref, stage, sem_in, sem_out,
                 *, nb, t):
    w_out = (wie_out, woe_out, wio_out, woo_out)
    w_bf = (wie_ref, woe_ref, wio_ref, woo_ref)
    stream, issue, finish, rounds = _weight_stager((wie_hbm, woe_hbm, wio_hbm, woo_hbm), w_bf,
                                                   wkt_ref, wvt_ref, stage, sem_in)
    assert len(rounds) == 6
    assert all(k == 1 for k, _ in rounds[0] + rounds[1])
    assert all(k == 2 for k, _ in rounds[2] + rounds[3])
    assert all(k == 3 for k, _ in rounds[4] + rounds[5])

    def write_back(k):
        return pltpu.make_async_copy(w_bf[k], w_out[k], sem_out.at[k])

    def at_first_step(fn):
        pl.when(pl.program_id(0) == 0)(fn)

    def turn(r):
        def fn():
            for k in finish(r):
                write_back(k).start()
            if r + 1 < len(rounds):
                issue(r + 1)
        return lambda: at_first_step(fn)

    def first_weight():
        stream(0)
        write_back(0).start()
        issue(0)
    at_first_step(first_weight)

    _zero_pads(pad_a, nb, t)
    _zero_pads(pad_b, nb, t)
    m_even = _cond_row(m_ref, 0, 0)
    _even_in_proj(x_ref, m_even, ng_ref[0:1, :], wie_ref, h_ref, pad_a, ga_ref, gb_ref,
                  q_ref, kt_ref, vt_ref, (ko_ref, vo_ref, wkt_ref, wvt_ref), nb, t)
    turn(0)()
    _pool_phase(pad_a, ga_ref, wp_ref, ps_ref, ab_ref, nb, t)
    turn(1)()
    _context_attention(q_ref, kt_ref, vt_ref, gb_ref, ab_ref, nb, t)
    turn(2)()
    _even_out_proj(x_ref, y_ref, m_even, woe_ref, ab_ref, nb, t)
    turn(3)()
    _odd_layer(y_ref, _cond_row(m_ref, 1, 0), ng_ref[1:2, :], fg_ref[...], wio_ref, cc_ref, cdw_ref,
               cdb_ref, lng_ref, lnb_ref, woo_ref, h_ref, pad_a, pad_b, bc_ref, ga_ref, gb_ref, ab_ref,
               nb, t, between_phases=(turn(4), turn(5)))

    def wait_write_backs():
        for k in range(len(w_bf)):
            write_back(k).wait()
    at_first_step(wait_write_backs)


def _rpb_rows(rpb_ref, e_ref):
    n = rpb_ref.shape[0] * rpb_ref.shape[1]
    lane = lax.broadcasted_iota(jnp.int32, (n, LANES), 1)
    i = jnp.where(lane < GRID_W, lane, lane - LANES)
    idx = jnp.clip(i, -(WIN_W - 1), WIN_W - 1) + (WIN_W - 1)
    rp = rpb_ref[...].reshape(n, rpb_ref.shape[2])
    e = jnp.zeros((n, LANES), F32)
    for d in range(2 * WIN_W - 1):
        e = jnp.where(idx == d, rp[:, d:d + 1], e)
    e_ref[...] = e


N_DR = 2 * WIN_H - 1
PAIR_TILES = N_DR // 2


def _bias_tile_index(j, dr_lo):
    if isinstance(dr_lo, int):
        parity, half = dr_lo % 2, dr_lo // 2
    else:
        parity, half = dr_lo & 1, lax.shift_right_logical(dr_lo, 1)
    return (2 * j + parity) * PAIR_TILES + half


def _bias_tables(e_ref, bias_ref):
    q = lax.broadcasted_iota(jnp.int32, (GRID_W, LANES), 0)
    lane = lax.broadcasted_iota(jnp.int32, (GRID_W, LANES), 1)
    kw = jnp.where(lane < GRID_W, lane, lane - GRID_W)
    start = jnp.clip(q - WIN_W // 2, 0, GRID_W - WIN_W)
    col_ok = (kw >= start) & (kw < start + WIN_W)
    for j in range(N_HEADS // 2):
        for dr in range(N_DR - 1):
            for e in range(2):
                r_lo = dr * N_HEADS + 2 * j + e
                r_hi = r_lo + N_HEADS
                lo = jnp.broadcast_to(e_ref[r_lo:r_lo + 1, :], (GRID_W, LANES))
                hi = jnp.broadcast_to(e_ref[r_hi:r_hi + 1, :], (GRID_W, LANES))
                lo = pltpu.roll(lo, 0, 1, stride=1, stride_axis=0)
                hi = pltpu.roll(hi, GRID_W, 1, stride=1, stride_axis=0)
                tile = jnp.where(lane < GRID_W, lo, hi)
                bias_ref[_bias_tile_index(j, dr), e * GRID_W:(e + 1) * GRID_W, :] = jnp.where(
                    col_ok, tile * LOG2_E, MASKED)


def _neighbourhood_attention(q_ref, k_ref, v_ref, ck_ref, cv_ref, bias_ref, kvc_ref, gb_ref, ab_ref, t):
    grid_h = t // GRID_W
    band = WIN_H * GRID_W
    for j in range(N_HEADS // 2):
        ln = _lanes(j)
        for i, src in enumerate((ck_ref, cv_ref)):
            kvc_ref[i] = jnp.concatenate([src[0, 0, 2 * j], src[0, 0, 2 * j + 1]],
                                         axis=0).astype(BF16)

        def per_group(g, carry, ln=ln, j=j):
            scored = []
            for u in range(NA_GROUP):
                r = g * NA_GROUP + u
                start = jnp.clip(r - WIN_H // 2, 0, grid_h - WIN_H)
                rows = _rows(r * GRID_W, GRID_W, GRID_W)
                keys = _rows(start * GRID_W, band, GRID_W)
                q2 = _split_heads(q_ref[rows, ln])
                dr0 = (WIN_H - 1) - (r - start)
                bias = jnp.concatenate([bias_ref[_bias_tile_index(j, dr0 + 2 * i)]
                                        for i in range(WIN_H // 2)], axis=-1)
                scored.append((rows, keys, _dot_nt(q2, k_ref[keys, ln]) + bias, _dot(q2, kvc_ref[0])))
            weighted = []
            for rows, keys, s_loc, s_ctx in scored:
                mx = jnp.maximum(jnp.max(s_loc, axis=-1, keepdims=True),
                                 jnp.max(s_ctx, axis=-1, keepdims=True))
                p_loc = jnp.exp2(s_loc - mx)
                p_ctx = jnp.exp2(s_ctx - mx)
                den = (jnp.sum(p_loc, axis=-1, keepdims=True)
                       + jnp.sum(p_ctx, axis=-1, keepdims=True))
                weighted.append((rows, keys, p_loc.astype(BF16), p_ctx.astype(BF16), den))
            for rows, keys, p_loc, p_ctx, den in weighted:
                o = (_dot(p_loc, v_ref[keys, ln]) + _dot_nt(p_ctx, kvc_ref[1])) / den
                ab_ref[rows, W_HALF + j * LANES:W_HALF + (j + 1) * LANES] = (
                    _merge_heads(o) * gb_ref[rows, ln]).astype(BF16)
            return carry
        lax.fori_loop(0, grid_h // NA_GROUP, per_group, 0)


def _sample_body(x_ref, m_ref, ng_ref, fg_ref, wie_ref, wp_ref, ps_ref, woe_hbm, wio_hbm, cc_ref,
                 cdw_ref, cdb_ref, lng_ref, lnb_ref, woo_hbm, ck_ref, cv_ref, rpb_ref,
                 y_ref,
                 h_ref, pad_a, pad_b, ga_ref, gb_ref, bc_ref, q_ref, k_ref, v_ref, ab_ref,
                 e_ref, bias_ref, kvc_ref, woe_ref, wio_ref, woo_ref, sem_w, *, t):
    _zero_pads(pad_a, 1, t)
    _zero_pads(pad_b, 1, t)
    late = ((woe_hbm, woe_ref), (wio_hbm, wio_ref), (woo_hbm, woo_ref))

    def late_copy(i):
        return pltpu.make_async_copy(late[i][0], late[i][1], sem_w.at[i])

    first_step = pl.program_id(0) == 0

    @pl.when(first_step)
    def _():
        for i in range(len(late)):
            late_copy(i).start()
        _rpb_rows(rpb_ref, e_ref)
        _bias_tables(e_ref, bias_ref)

    cond = pl.program_id(0) + 1
    m_even = _cond_row(m_ref, 0, cond)
    _even_in_proj(x_ref, m_even, ng_ref[0:1, :], wie_ref, h_ref, pad_a, ga_ref, gb_ref,
                  q_ref, k_ref, v_ref, None, 1, t)
    _pool_phase(pad_a, ga_ref, wp_ref, ps_ref, ab_ref, 1, t)
    _neighbourhood_attention(q_ref, k_ref, v_ref, ck_ref, cv_ref, bias_ref, kvc_ref, gb_ref, ab_ref, t)
    pl.when(first_step)(lambda: late_copy(0).wait())
    _even_out_proj(x_ref, y_ref, m_even, woe_ref, ab_ref, 1, t)

    @pl.when(first_step)
    def _():
        late_copy(1).wait()
        late_copy(2).wait()

    _odd_layer(y_ref, _cond_row(m_ref, 1, cond), ng_ref[1:2, :], fg_ref[...], wio_ref, cc_ref, cdw_ref,
               cdb_ref, lng_ref, lnb_ref, woo_ref, h_ref, pad_a, pad_b, bc_ref, ga_ref, gb_ref, ab_ref,
               1, t)


def _const_spec(shape):
    zeros = (0,) * len(shape)
    return pl.BlockSpec(shape, lambda i: zeros, pipeline_mode=pl.Buffered(1))


def _stream_scratch(nb, t, kv_transposed):
    r = nb * t
    padded = nb * (t + 2 * PAD)
    kv = (W_HALF, r) if kv_transposed else (r, W_HALF)
    return [
        pltpu.VMEM((r, D_MODEL), BF16),
        pltpu.VMEM((padded, W_HALF), F32),
        pltpu.VMEM((padded, W_HALF), F32),
        pltpu.VMEM((r, W_HALF), F32),
        pltpu.VMEM((r, W_HALF), F32),
        pltpu.VMEM((r, W_HALF), F32),
        pltpu.VMEM((r, W_HALF), BF16),
        pltpu.VMEM(kv, BF16),
        pltpu.VMEM(kv, BF16),
        pltpu.VMEM((r, D_MODEL), BF16),
    ]


def _small_params(norm_g, final_g, w_pool, pool_scale, conv_c, conv_d, conv_d_b, ln_g, ln_b):
    return [norm_g, final_g.reshape(1, D_MODEL), w_pool, pool_scale, jnp.swapaxes(conv_c, 0, 1),
            jnp.swapaxes(conv_d, 0, 1), conv_d_b, ln_g, ln_b]


def kernel(x_prompt, x_sample, cache_k, cache_v, c, c_ctx, norm_g, w_mod, b_mod, w_in_even, w_pool,
           pool_scale, rpb, w_out_even, w_in_odd, conv_c, conv_d, conv_d_b, ln_g, ln_b, w_out_odd,
           final_g):
    batch, seq, d = x_prompt.shape
    dec_batch, dec_seq, _ = x_sample.shape
    assert d == D_MODEL and w_mod.shape[0] == 2 and w_in_even.shape[0] == 1 and w_in_odd.shape[0] == 1
    assert (NB_PROMPT * seq) % ROW_CHUNK == 0 and ROW_CHUNK % seq == 0 and seq % Q_ROWS == 0
    assert dec_seq % ROW_CHUNK == 0 and dec_seq // GRID_W >= WIN_H
    assert seq % POOL_ROWS == 0 and seq % CONV_ROWS == 0
    assert dec_seq % POOL_ROWS == 0 and dec_seq % CONV_ROWS == 0
    assert (dec_seq // GRID_W) % NA_GROUP == 0

    cond_rows = SUBLANES * ((1 + dec_batch + SUBLANES - 1) // SUBLANES)
    m = _modulation(c_ctx, c, w_mod, b_mod, cond_rows)
    m_spec = _const_spec(m.shape)

    small = _small_params(norm_g, final_g, w_pool, pool_scale, conv_c, conv_d, conv_d_b, ln_g, ln_b)
    small_specs = [_const_spec(a.shape) for a in small]
    w_f32 = (w_in_even, w_out_even, w_in_odd, w_out_odd)
    assert all(w.shape[0] == 1 and w.shape[1] % STAGE_ROWS == 0 for w in w_f32)
    any_spec = pl.BlockSpec(memory_space=pl.ANY)

    nb = NB_PROMPT
    assert batch % nb == 0
    kv_shape = jax.ShapeDtypeStruct((batch, 1, N_HEADS, HEAD_DIM, seq), F32)
    kv_spec = pl.BlockSpec((nb, 1, N_HEADS, HEAD_DIM, seq), lambda i: (i, 0, 0, 0, 0))
    y_prompt, new_kt, new_vt, wie, woe, wio, woo = pl.pallas_call(
        functools.partial(_prompt_body, nb=nb, t=seq),
        out_shape=(jax.ShapeDtypeStruct(x_prompt.shape, F32), kv_shape, kv_shape)
                  + tuple(jax.ShapeDtypeStruct(w.shape[1:], BF16) for w in w_f32),
        grid=(batch // nb,),
        in_specs=[pl.BlockSpec((nb, seq, d), lambda i: (i, 0, 0)), m_spec] + small_specs
                 + [any_spec] * len(w_f32),
        out_specs=(pl.BlockSpec((nb, seq, d), lambda i: (i, 0, 0)), kv_spec, kv_spec)
                  + (any_spec,) * len(w_f32),
        scratch_shapes=_stream_scratch(nb, seq, True)
                       + [pltpu.VMEM(w.shape[1:], BF16) for w in w_f32] + [
            pltpu.VMEM((W_HALF, d), BF16),
            pltpu.VMEM((W_HALF, d), BF16),
            pltpu.VMEM((STAGE_SLOTS, STAGE_ROWS, max(w.shape[2] for w in w_f32)), F32),
            pltpu.SemaphoreType.DMA((STAGE_SLOTS,)),
            pltpu.SemaphoreType.DMA((len(w_f32),)),
        ],
        compiler_params=pltpu.CompilerParams(dimension_semantics=("arbitrary",),
                                             vmem_limit_bytes=VMEM_LIMIT),
        name="prompt",
    )(x_prompt, m, *small, *w_f32)
    ng, fg, wp, ps, cc, cdw, cdb, lng, lnb = small
    w_args = [ng, fg, wie, wp, ps, woe, wio, cc, cdw, cdb, lng, lnb, woo]
    late_w = (woe, wio, woo)
    w_specs = [any_spec if any(a is w for w in late_w) else _const_spec(a.shape) for a in w_args]

    past = cache_k.shape[3]
    cache_spec = pl.BlockSpec((1, 1, N_HEADS, HEAD_DIM, past), lambda i: (i, 0, 0, 0, 0))
    rpb_t = jnp.swapaxes(rpb[0], 0, 1)
    y_sample = pl.pallas_call(
        functools.partial(_sample_body, t=dec_seq),
        out_shape=jax.ShapeDtypeStruct(x_sample.shape, F32),
        grid=(dec_batch,),
        in_specs=[pl.BlockSpec((1, dec_seq, d), lambda i: (i, 0, 0), pipeline_mode=pl.Buffered(1)),
                  m_spec] + w_specs
                 + [cache_spec, cache_spec, _const_spec(rpb_t.shape)],
        out_specs=pl.BlockSpec((1, dec_seq, d), lambda i: (i, 0, 0)),
        scratch_shapes=_stream_scratch(1, dec_seq, False) + [
            pltpu.VMEM((N_DR * N_HEADS, LANES), F32),
            pltpu.VMEM((N_HEADS * PAIR_TILES, 2 * GRID_W, LANES), F32),
            pltpu.VMEM((2, LANES, past), BF16),
        ] + [pltpu.VMEM(w.shape, BF16) for w in late_w] + [pltpu.SemaphoreType.DMA((len(late_w),))],
        compiler_params=pltpu.CompilerParams(dimension_semantics=("arbitrary",),
                                             vmem_limit_bytes=VMEM_LIMIT),
        name="sample",
    )(x_sample, m, *w_args, jnp.swapaxes(cache_k, 3, 4), jnp.swapaxes(cache_v, 3, 4), rpb_t)

    return (y_prompt, y_sample, jnp.swapaxes(new_kt, 3, 4), jnp.swapaxes(new_vt, 3, 4))
```

```python
import functools

import jax
import jax.numpy as jnp
from jax import lax
from jax.experimental import pallas as pl
from jax.experimental.pallas import tpu as pltpu

F32 = jnp.float32
BF16 = jnp.bfloat16

D_MODEL = 1024
W_HALF = 512
N_POOL_GROUPS = 4
POOL_HALF = (1, 2, 4, 8)
N_HEADS = 8
HEAD_DIM = 64
GRID_W = 64
WIN_H = 8
WIN_W = 16
CONV_C = 3
CONV_D = 31
EPS = 1e-6
MASKED = -1e30
LOG2_E = 1.4426950408889634
Q_SCALE = HEAD_DIM ** -0.5 * LOG2_E

LANES = 128
SUBLANES = 8
PAD = 16
ROW_CHUNK = 512
NORM_ROWS = 32
POOL_ROWS = 256
CONV_ROWS = 128
Q_ROWS = 128
NB_PROMPT = 2
NA_GROUP = 8
MOD_ROWS = 512
STAGE_ROWS = 128
STAGE_SLOTS = 4
VMEM_LIMIT = 58 * 1024 * 1024

assert PAD >= CONV_D // 2 + 1 and PAD % SUBLANES == 0 and PAD >= 2 * SUBLANES
assert max(POOL_HALF) <= SUBLANES


def _sigmoid(x):
    return 1.0 / (1.0 + jnp.exp(-x))


def _silu(x):
    return x * _sigmoid(x)


def _dot(a, b):
    return jnp.dot(a, b, preferred_element_type=F32)


def _dot_nt(a, b):
    return lax.dot_general(a, b, (((1,), (1,)), ((), ())), preferred_element_type=F32)


def _lanes(j):
    return slice(j * LANES, (j + 1) * LANES)


def _group(g):
    return slice(g * W_HALF, (g + 1) * W_HALF)


def _rows(start, size, align):
    if isinstance(start, int):
        return slice(start, start + size)
    return pl.ds(pl.multiple_of(start, align), size)


def _mod_body(cctx_ref, c_ref, w_ref, b_ref, o_ref, act_ref):
    layer, kb = pl.program_id(0), pl.program_id(1)
    rows, d = act_ref.shape

    @pl.when(kb == 0)
    def _():
        r = lax.broadcasted_iota(jnp.int32, (rows, d), 0)
        cond = jnp.where(r == 0, cctx_ref[...], 0.0)
        for i in range(c_ref.shape[0]):
            cond = jnp.where(r == i + 1, c_ref[i:i + 1, :], cond)
        act_ref[...] = _silu(cond).astype(BF16)
        o_ref[0] = jnp.broadcast_to(jnp.where(layer == 0, b_ref[0:1, :], b_ref[1:2, :]), o_ref.shape[1:])

    act = act_ref[:, pl.ds(pl.multiple_of(kb * MOD_ROWS, MOD_ROWS), MOD_ROWS)]
    o_ref[0] += _dot(act, w_ref[0].astype(BF16))


def _modulation(c_ctx, c, w_mod, b_mod, rows):
    depth, d, n = w_mod.shape
    assert depth == 2 and 1 + c.shape[0] <= rows and d % MOD_ROWS == 0
    return pl.pallas_call(
        _mod_body,
        out_shape=jax.ShapeDtypeStruct((depth, rows, n), F32),
        grid=(depth, d // MOD_ROWS),
        in_specs=[
            pl.BlockSpec((1, d), lambda l, k: (0, 0)),
            pl.BlockSpec(c.shape, lambda l, k: (0, 0)),
            pl.BlockSpec((1, MOD_ROWS, n), lambda l, k: (l, k, 0)),
            pl.BlockSpec((depth, n), lambda l, k: (0, 0)),
        ],
        out_specs=pl.BlockSpec((1, rows, n), lambda l, k: (l, 0, 0)),
        scratch_shapes=[pltpu.VMEM((rows, d), BF16)],
        compiler_params=pltpu.CompilerParams(dimension_semantics=("arbitrary", "arbitrary")),
        name="mod",
    )(c_ctx.reshape(1, d), c, w_mod, b_mod)


def _cond_row(m_ref, layer, row):
    if isinstance(row, int):
        return m_ref[layer, row:row + 1, :]
    m = m_ref[layer]
    keep = lax.broadcasted_iota(jnp.int32, m.shape, 0) == row
    return jnp.sum(jnp.where(keep, m, 0.0), axis=0, keepdims=True)


def _pieces(c, nb, t):
    if t >= ROW_CHUNK:
        per_seq = t // ROW_CHUNK
        s = 0 if nb == 1 else c // per_seq
        return [(s, (c - s * per_seq) * ROW_CHUNK, ROW_CHUNK, 0)]
    per_chunk = ROW_CHUNK // t
    return [(c * per_chunk + i, 0, t, i * t) for i in range(per_chunk)]


def _for_chunks(n, body, unrolled=False):
    if unrolled or n == 1:
        for c in range(n):
            body(c)
    else:
        lax.fori_loop(0, n, lambda c, carry: (body(c), carry)[1], 0)


def _pad_row(s, off, t):
    return s * (t + 2 * PAD) + PAD + off


def _store_padded(pad_ref, val, pieces, t):
    for s, off, n, o in pieces:
        pad_ref[_rows(_pad_row(s, off, t), n, SUBLANES), :] = val[o:o + n]


def _scale_padded(pad_ref, val, pieces, t):
    for s, off, n, o in pieces:
        rows = _rows(_pad_row(s, off, t), n, SUBLANES)
        pad_ref[rows, :] = pad_ref[rows, :] * val[o:o + n]


def _modnorm_chunk(src_ref, h_ref, c, nb, t, gain, shift):
    for s, off, n, o in _pieces(c, nb, t):
        for i in range(0, n, NORM_ROWS):
            x = src_ref[s, _rows(off + i, NORM_ROWS, NORM_ROWS), :]
            ms = jnp.mean(x * x, axis=-1, keepdims=True)
            h_ref[_rows(c * ROW_CHUNK + o + i, NORM_ROWS, NORM_ROWS), :] = (
                x * lax.rsqrt(ms + EPS) * gain + shift).astype(BF16)


def _zero_pads(pad_ref, nb, t):
    z = jnp.zeros((PAD, W_HALF), F32)
    for s in range(nb):
        pad_ref[_pad_row(s, 0, t) - PAD:_pad_row(s, 0, t), :] = z
        pad_ref[_pad_row(s, t, t):_pad_row(s, t, t) + PAD, :] = z


def _pool_phase(pad_ref, ga_ref, wp_ref, ps_ref, ab_ref, nb, t):
    n_rows = POOL_ROWS
    per_seq = t // n_rows

    def step(i, carry):
        s = i // per_seq
        r0 = (i - s * per_seq) * n_rows
        prow = _pad_row(s, r0, t)
        rows = _rows(i * n_rows, n_rows, n_rows)
        pos = r0 + lax.broadcasted_iota(jnp.int32, (n_rows, LANES), 0)
        before = jnp.minimum(pos, SUBLANES)
        after = jnp.minimum(t - pos, SUBLANES)
        for g in range(N_POOL_GROUPS):
            hw = POOL_HALF[g]
            ln = _lanes(g)
            halo = n_rows + 2 * SUBLANES
            blk = pad_ref[_rows(prow - SUBLANES, halo, SUBLANES), ln]
            run, n = blk, 1
            while n < 2 * hw:
                run = run + pltpu.roll(run, halo - n, 0)
                n *= 2
            if hw < SUBLANES:
                run = pltpu.roll(run, halo - (SUBLANES - hw), 0)
            win = run[:n_rows]
            cnt = (jnp.minimum(before, hw) + jnp.minimum(after, hw)).astype(F32)
            p = (win / cnt - blk[SUBLANES:SUBLANES + n_rows]).astype(BF16)
            y = _dot(p, wp_ref[0, g].astype(BF16)) * ps_ref[:, ln] * ga_ref[rows, ln]
            ab_ref[rows, ln] = y.astype(BF16)
        return carry
    lax.fori_loop(0, nb * per_seq, step, 0)


def _out_proj_chunk(ab_ref, w_ref, x_ref, gate, dst_ref, c, nb, t):
    lhs = ab_ref[_rows(c * ROW_CHUNK, ROW_CHUNK, ROW_CHUNK), :]
    for g in range(D_MODEL // W_HALF):
        y = _dot(lhs, w_ref[:, _group(g)])
        for s, off, n, o in _pieces(c, nb, t):
            rows = _rows(off, n, n)
            dst_ref[s, rows, _group(g)] = x_ref[s, rows, _group(g)] + gate[:, _group(g)] * y[o:o + n]


def _shift_up(x, o, n):
    if o % SUBLANES == 0:
        return x[o:o + n]
    return pltpu.roll(x, x.shape[0] - o, 0)[:n]


def _conv_phase(pad_c, pad_d, bc_ref, ga_ref, gb_ref, cc_ref, cdw_ref, cdb_ref, lng_ref, lnb_ref,
                ab_ref, nb, t):
    n_rows = CONV_ROWS
    per_seq = t // n_rows

    def step(i, carry):
        s = i // per_seq
        r0 = (i - s * per_seq) * n_rows
        prow = _pad_row(s, r0, t)
        rows = _rows(i * n_rows, n_rows, n_rows)
        z = []
        for g in range(W_HALF // LANES):
            ln = _lanes(g)
            blk = pad_c[_rows(prow - SUBLANES, n_rows + 2 * SUBLANES, SUBLANES), ln]
            c3 = None
            for j in range(CONV_C):
                o = SUBLANES + j - CONV_C // 2
                term = _shift_up(blk, o, n_rows) * cc_ref[j, :, ln]
                c3 = term if c3 is None else c3 + term
            ab_ref[rows, ln] = (bc_ref[rows, ln] * c3 * ga_ref[rows, ln]).astype(BF16)
            acc = None
            for sft in range(SUBLANES):
                part = None
                for a in range((CONV_D - sft + SUBLANES - 1) // SUBLANES):
                    j = SUBLANES * a + sft
                    src = pad_d[_rows(prow - 2 * SUBLANES + SUBLANES * a, n_rows + SUBLANES,
                                      SUBLANES), ln]
                    term = src * cdw_ref[j, :, ln]
                    part = term if part is None else part + term
                o = SUBLANES + sft - (CONV_D // 2 - SUBLANES)
                part = _shift_up(part, o, n_rows)
                acc = part if acc is None else acc + part
            z.append(acc + cdb_ref[:, ln])
        z = jnp.concatenate(z, axis=-1)
        mu = jnp.mean(z, axis=-1, keepdims=True)
        zc = z - mu
        var = jnp.mean(zc * zc, axis=-1, keepdims=True)
        zn = zc * lax.rsqrt(var + EPS) * lng_ref[...] + lnb_ref[...]
        ab_ref[rows, W_HALF:] = (_silu(zn) * gb_ref[rows, :]).astype(BF16)
        return carry
    lax.fori_loop(0, nb * per_seq, step, 0)


def _final_norm_chunk(y_ref, fg, c, nb, t):
    for s, off, n, _ in _pieces(c, nb, t):
        for i in range(0, n, NORM_ROWS):
            rows = _rows(off + i, NORM_ROWS, NORM_ROWS)
            x = y_ref[s, rows, :]
            ms = jnp.mean(x * x, axis=-1, keepdims=True)
            y_ref[s, rows, :] = x * lax.rsqrt(ms + EPS) * fg


def _odd_layer(y_ref, m_row, g_row, fg, wio_ref, cc_ref, cdw_ref, cdb_ref, lng_ref, lnb_ref, woo_ref,
               h_ref, pad_c, pad_d, bc_ref, ga_ref, gb_ref, ab_ref, nb, t):
    shift = m_row[:, :D_MODEL]
    gain = g_row * (1.0 + m_row[:, D_MODEL:2 * D_MODEL])
    gate = m_row[:, 2 * D_MODEL:]
    n_chunks = nb * t // ROW_CHUNK

    def in_proj(c):
        _modnorm_chunk(y_ref, h_ref, c, nb, t, gain, shift)
        rows = _rows(c * ROW_CHUNK, ROW_CHUNK, ROW_CHUNK)
        pieces = _pieces(c, nb, t)
        h = h_ref[rows, :]
        bc_ref[rows, :] = _dot(h, wio_ref[:, _group(0)])
        _store_padded(pad_c, _dot(h, wio_ref[:, _group(1)]), pieces, t)
        _scale_padded(pad_c, _dot(h, wio_ref[:, _group(2)]), pieces, t)
        ga_ref[rows, :] = _silu(_dot(h, wio_ref[:, _group(3)]))
        _store_padded(pad_d, _dot(h, wio_ref[:, _group(4)]), pieces, t)
        _scale_padded(pad_d, _sigmoid(_dot(h, wio_ref[:, _group(5)])), pieces, t)
        gb_ref[rows, :] = _silu(_dot(h, wio_ref[:, _group(6)]))
    _for_chunks(n_chunks, in_proj)

    _conv_phase(pad_c, pad_d, bc_ref, ga_ref, gb_ref, cc_ref, cdw_ref, cdb_ref, lng_ref, lnb_ref,
                ab_ref, nb, t)

    def out_proj(c):
        _out_proj_chunk(ab_ref, woo_ref, y_ref, gate, y_ref, c, nb, t)
        _final_norm_chunk(y_ref, fg, c, nb, t)
    _for_chunks(n_chunks, out_proj, unrolled=True)


def _even_in_proj(x_ref, m_row, g_row, w_ref, h_ref, pad_a, ga_ref, gb_ref, q_ref, k_ref, v_ref,
                  kv_t, nb, t):
    shift = m_row[:, :D_MODEL]
    gain = g_row * (1.0 + m_row[:, D_MODEL:2 * D_MODEL])

    def in_proj(c):
        _modnorm_chunk(x_ref, h_ref, c, nb, t, gain, shift)
        rows = _rows(c * ROW_CHUNK, ROW_CHUNK, ROW_CHUNK)
        pieces = _pieces(c, nb, t)
        h = h_ref[rows, :]
        _store_padded(pad_a, _dot(h, w_ref[:, _group(0)]), pieces, t)
        ga_ref[rows, :] = _silu(_dot(h, w_ref[:, _group(1)]))
        q_ref[rows, :] = (_dot(h, w_ref[:, _group(2)]) * Q_SCALE).astype(BF16)
        for i, (dst, g) in enumerate(((k_ref, 3), (v_ref, 4))):
            if kv_t is None:
                dst[rows, :] = _dot(h, w_ref[:, _group(g)]).astype(BF16)
                continue
            acc = _dot_nt(kv_t[2 + i][...], h)
            dst[:, rows] = acc.astype(BF16)
            for s, off, n, o in pieces:
                for hd in range(N_HEADS):
                    kv_t[i][s, 0, hd, :, _rows(off, n, n)] = (
                        acc[hd * HEAD_DIM:(hd + 1) * HEAD_DIM, o:o + n])
        gb_ref[rows, :] = _silu(_dot(h, w_ref[:, _group(5)]))
    _for_chunks(nb * t // ROW_CHUNK, in_proj)


def _even_out_proj(x_ref, y_ref, m_row, w_ref, ab_ref, nb, t):
    gate = m_row[:, 2 * D_MODEL:]
    _for_chunks(nb * t // ROW_CHUNK,
                lambda c: _out_proj_chunk(ab_ref, w_ref, x_ref, gate, y_ref, c, nb, t))


def _split_heads(x):
    lane = lax.broadcasted_iota(jnp.int32, (1, LANES), 1)
    first = jnp.where(lane < HEAD_DIM, 1.0, 0.0).astype(x.dtype)
    return jnp.concatenate([x * first, x * (1 - first)], axis=0)


def _merge_heads(o):
    n = o.shape[0] // 2
    lane = lax.broadcasted_iota(jnp.int32, (n, LANES), 1)
    return jnp.where(lane < HEAD_DIM, o[:n], o[n:])


def _context_attention(q_ref, kt_ref, vt_ref, gb_ref, ab_ref, nb, t):
    for s in range(nb):
        seq = slice(s * t, (s + 1) * t)
        for j in range(N_HEADS // 2):
            ln = _lanes(j)
            kp = kt_ref[ln, seq]
            vp = vt_ref[ln, seq]
            for r0 in range(0, t, Q_ROWS):
                rows = slice(s * t + r0, s * t + r0 + Q_ROWS)
                sc = _dot(_split_heads(q_ref[rows, ln]), kp)
                p = jnp.exp2(sc - jnp.max(sc, axis=-1, keepdims=True))
                o = _dot_nt(p.astype(BF16), vp) / jnp.sum(p, axis=-1, keepdims=True)
                ab_ref[rows, W_HALF + j * LANES:W_HALF + (j + 1) * LANES] = (
                    _merge_heads(o) * gb_ref[rows, ln]).astype(BF16)


def _stage_weights(w_hbm, w_bf, wkt_ref, wvt_ref, stage, sem_in):
    chunks = [(k, r0) for k in range(len(w_hbm)) for r0 in range(0, w_hbm[k].shape[1], STAGE_ROWS)]

    def fetch(i):
        k, r0 = chunks[i]
        cols = w_hbm[k].shape[2]
        slot = i % STAGE_SLOTS
        return pltpu.make_async_copy(w_hbm[k].at[0, pl.ds(r0, STAGE_ROWS), :],
                                     stage.at[slot, :, pl.ds(0, cols)], sem_in.at[slot])

    for i in range(min(STAGE_SLOTS - 1, len(chunks))):
        fetch(i).start()
    for i, (k, r0) in enumerate(chunks):
        if i + STAGE_SLOTS - 1 < len(chunks):
            fetch(i + STAGE_SLOTS - 1).start()
        fetch(i).wait()
        cols = w_hbm[k].shape[2]
        rows = slice(r0, r0 + STAGE_ROWS)
        slot = i % STAGE_SLOTS
        w_bf[k][rows, :] = stage[slot, :, 0:cols].astype(BF16)
        if k == 0:
            wkt_ref[:, rows] = stage[slot, :, _group(3)].T.astype(BF16)
            wvt_ref[:, rows] = stage[slot, :, _group(4)].T.astype(BF16)


def _prompt_body(x_ref, m_ref, ng_ref, fg_ref, wp_ref, ps_ref, cc_ref, cdw_ref, cdb_ref, lng_ref,
                 lnb_ref, wie_hbm, woe_hbm, wio_hbm, woo_hbm,
                 y_ref, ko_ref, vo_ref, wie_out, woe_out, wio_out, woo_out,
                 h_ref, pad_a, pad_b, ga_ref, gb_ref, bc_ref, q_ref, kt_ref, vt_ref, ab_ref,
                 wie_ref, woe_ref, wio_ref, woo_ref, wkt_ref, wvt_ref, stage, sem_in, sem_out,
                 *, nb, t):
    w_out = (wie_out, woe_out, wio_out, woo_out)
    w_bf = (wie_ref, woe_ref, wio_ref, woo_ref)

    def write_back(k):
        return pltpu.make_async_copy(w_bf[k], w_out[k], sem_out.at[k])

    @pl.when(pl.program_id(0) == 0)
    def _():
        _stage_weights((wie_hbm, woe_hbm, wio_hbm, woo_hbm), w_bf, wkt_ref, wvt_ref, stage, sem_in)
        for k in range(len(w_bf)):
            write_back(k).start()

    _zero_pads(pad_a, nb, t)
    _zero_pads(pad_b, nb, t)
    m_even = _cond_row(m_ref, 0, 0)
    _even_in_proj(x_ref, m_even, ng_ref[0:1, :], wie_ref, h_ref, pad_a, ga_ref, gb_ref,
                  q_ref, kt_ref, vt_ref, (ko_ref, vo_ref, wkt_ref, wvt_ref), nb, t)
    _pool_phase(pad_a, ga_ref, wp_ref, ps_ref, ab_ref, nb, t)
    _context_attention(q_ref, kt_ref, vt_ref, gb_ref, ab_ref, nb, t)
    _even_out_proj(x_ref, y_ref, m_even, woe_ref, ab_ref, nb, t)
    _odd_layer(y_ref, _cond_row(m_ref, 1, 0), ng_ref[1:2, :], fg_ref[...], wio_ref, cc_ref, cdw_ref,
               cdb_ref, lng_ref, lnb_ref, woo_ref, h_ref, pad_a, pad_b, bc_ref, ga_ref, gb_ref, ab_ref,
               nb, t)

    @pl.when(pl.program_id(0) == 0)
    def _():
        for k in range(len(w_bf)):
            write_back(k).wait()


def _rpb_rows(rpb_ref, e_ref):
    n = rpb_ref.shape[0] * rpb_ref.shape[1]
    lane = lax.broadcasted_iota(jnp.int32, (n, LANES), 1)
    i = jnp.where(lane < GRID_W, lane, lane - LANES)
    idx = jnp.clip(i, -(WIN_W - 1), WIN_W - 1) + (WIN_W - 1)
    rp = rpb_ref[...].reshape(n, rpb_ref.shape[2])
    e = jnp.zeros((n, LANES), F32)
    for d in range(2 * WIN_W - 1):
        e = jnp.where(idx == d, rp[:, d:d + 1], e)
    e_ref[...] = e


N_DR = 2 * WIN_H - 1
PAIR_TILES = N_DR // 2


def _bias_tile_index(j, dr_lo):
    if isinstance(dr_lo, int):
        parity, half = dr_lo % 2, dr_lo // 2
    else:
        parity, half = dr_lo & 1, lax.shift_right_logical(dr_lo, 1)
    return (2 * j + parity) * PAIR_TILES + half


def _bias_tables(e_ref, bias_ref):
    q = lax.broadcasted_iota(jnp.int32, (GRID_W, LANES), 0)
    lane = lax.broadcasted_iota(jnp.int32, (GRID_W, LANES), 1)
    kw = jnp.where(lane < GRID_W, lane, lane - GRID_W)
    start = jnp.clip(q - WIN_W // 2, 0, GRID_W - WIN_W)
    col_ok = (kw >= start) & (kw < start + WIN_W)
    for j in range(N_HEADS // 2):
        for dr in range(N_DR - 1):
            for e in range(2):
                r_lo = dr * N_HEADS + 2 * j + e
                r_hi = r_lo + N_HEADS
                lo = jnp.broadcast_to(e_ref[r_lo:r_lo + 1, :], (GRID_W, LANES))
                hi = jnp.broadcast_to(e_ref[r_hi:r_hi + 1, :], (GRID_W, LANES))
                lo = pltpu.roll(lo, 0, 1, stride=1, stride_axis=0)
                hi = pltpu.roll(hi, GRID_W, 1, stride=1, stride_axis=0)
                tile = jnp.where(lane < GRID_W, lo, hi)
                bias_ref[_bias_tile_index(j, dr), e * GRID_W:(e + 1) * GRID_W, :] = jnp.where(
                    col_ok, tile * LOG2_E, MASKED)


def _neighbourhood_attention(q_ref, k_ref, v_ref, ck_ref, cv_ref, bias_ref, kvc_ref, gb_ref, ab_ref, t):
    grid_h = t // GRID_W
    band = WIN_H * GRID_W
    for j in range(N_HEADS // 2):
        ln = _lanes(j)
        for i, src in enumerate((ck_ref, cv_ref)):
            kvc_ref[i] = jnp.concatenate([src[0, 0, 2 * j], src[0, 0, 2 * j + 1]],
                                         axis=0).astype(BF16)

        def per_group(g, carry, ln=ln, j=j):
            scored = []
            for u in range(NA_GROUP):
                r = g * NA_GROUP + u
                start = jnp.clip(r - WIN_H // 2, 0, grid_h - WIN_H)
                rows = _rows(r * GRID_W, GRID_W, GRID_W)
                keys = _rows(start * GRID_W, band, GRID_W)
                q2 = _split_heads(q_ref[rows, ln])
                dr0 = (WIN_H - 1) - (r - start)
                bias = jnp.concatenate([bias_ref[_bias_tile_index(j, dr0 + 2 * i)]
                                        for i in range(WIN_H // 2)], axis=-1)
                scored.append((rows, keys, _dot_nt(q2, k_ref[keys, ln]) + bias, _dot(q2, kvc_ref[0])))
            weighted = []
            for rows, keys, s_loc, s_ctx in scored:
                mx = jnp.maximum(jnp.max(s_loc, axis=-1, keepdims=True),
                                 jnp.max(s_ctx, axis=-1, keepdims=True))
                p_loc = jnp.exp2(s_loc - mx)
                p_ctx = jnp.exp2(s_ctx - mx)
                den = (jnp.sum(p_loc, axis=-1, keepdims=True)
                       + jnp.sum(p_ctx, axis=-1, keepdims=True))
                weighted.append((rows, keys, p_loc.astype(BF16), p_ctx.astype(BF16), den))
            for rows, keys, p_loc, p_ctx, den in weighted:
                o = (_dot(p_loc, v_ref[keys, ln]) + _dot_nt(p_ctx, kvc_ref[1])) / den
                ab_ref[rows, W_HALF + j * LANES:W_HALF + (j + 1) * LANES] = (
                    _merge_heads(o) * gb_ref[rows, ln]).astype(BF16)
            return carry
        lax.fori_loop(0, grid_h // NA_GROUP, per_group, 0)


def _sample_body(x_ref, m_ref, ng_ref, fg_ref, wie_ref, wp_ref, ps_ref, woe_hbm, wio_hbm, cc_ref,
                 cdw_ref, cdb_ref, lng_ref, lnb_ref, woo_hbm, ck_ref, cv_ref, rpb_ref,
                 y_ref,
                 h_ref, pad_a, pad_b, ga_ref, gb_ref, bc_ref, q_ref, k_ref, v_ref, ab_ref,
                 e_ref, bias_ref, kvc_ref, woe_ref, wio_ref, woo_ref, sem_w, *, t):
    _zero_pads(pad_a, 1, t)
    _zero_pads(pad_b, 1, t)
    late = ((woe_hbm, woe_ref), (wio_hbm, wio_ref), (woo_hbm, woo_ref))

    def late_copy(i):
        return pltpu.make_async_copy(late[i][0], late[i][1], sem_w.at[i])

    first_step = pl.program_id(0) == 0

    @pl.when(first_step)
    def _():
        for i in range(len(late)):
            late_copy(i).start()
        _rpb_rows(rpb_ref, e_ref)
        _bias_tables(e_ref, bias_ref)

    cond = pl.program_id(0) + 1
    m_even = _cond_row(m_ref, 0, cond)
    _even_in_proj(x_ref, m_even, ng_ref[0:1, :], wie_ref, h_ref, pad_a, ga_ref, gb_ref,
                  q_ref, k_ref, v_ref, None, 1, t)
    _pool_phase(pad_a, ga_ref, wp_ref, ps_ref, ab_ref, 1, t)
    _neighbourhood_attention(q_ref, k_ref, v_ref, ck_ref, cv_ref, bias_ref, kvc_ref, gb_ref, ab_ref, t)
    pl.when(first_step)(lambda: late_copy(0).wait())
    _even_out_proj(x_ref, y_ref, m_even, woe_ref, ab_ref, 1, t)

    @pl.when(first_step)
    def _():
        late_copy(1).wait()
        late_copy(2).wait()

    _odd_layer(y_ref, _cond_row(m_ref, 1, cond), ng_ref[1:2, :], fg_ref[...], wio_ref, cc_ref, cdw_ref,
               cdb_ref, lng_ref, lnb_ref, woo_ref, h_ref, pad_a, pad_b, bc_ref, ga_ref, gb_ref, ab_ref,
               1, t)


def _const_spec(shape):
    zeros = (0,) * len(shape)
    return pl.BlockSpec(shape, lambda i: zeros, pipeline_mode=pl.Buffered(1))


def _stream_scratch(nb, t, kv_transposed):
    r = nb * t
    padded = nb * (t + 2 * PAD)
    kv = (W_HALF, r) if kv_transposed else (r, W_HALF)
    return [
        pltpu.VMEM((r, D_MODEL), BF16),
        pltpu.VMEM((padded, W_HALF), F32),
        pltpu.VMEM((padded, W_HALF), F32),
        pltpu.VMEM((r, W_HALF), F32),
        pltpu.VMEM((r, W_HALF), F32),
        pltpu.VMEM((r, W_HALF), F32),
        pltpu.VMEM((r, W_HALF), BF16),
        pltpu.VMEM(kv, BF16),
        pltpu.VMEM(kv, BF16),
        pltpu.VMEM((r, D_MODEL), BF16),
    ]


def _small_params(norm_g, final_g, w_pool, pool_scale, conv_c, conv_d, conv_d_b, ln_g, ln_b):
    return [norm_g, final_g.reshape(1, D_MODEL), w_pool, pool_scale, jnp.swapaxes(conv_c, 0, 1),
            jnp.swapaxes(conv_d, 0, 1), conv_d_b, ln_g, ln_b]


def kernel(x_prompt, x_sample, cache_k, cache_v, c, c_ctx, norm_g, w_mod, b_mod, w_in_even, w_pool,
           pool_scale, rpb, w_out_even, w_in_odd, conv_c, conv_d, conv_d_b, ln_g, ln_b, w_out_odd,
           final_g):
    batch, seq, d = x_prompt.shape
    dec_batch, dec_seq, _ = x_sample.shape
    assert d == D_MODEL and w_mod.shape[0] == 2 and w_in_even.shape[0] == 1 and w_in_odd.shape[0] == 1
    assert (NB_PROMPT * seq) % ROW_CHUNK == 0 and ROW_CHUNK % seq == 0 and seq % Q_ROWS == 0
    assert dec_seq % ROW_CHUNK == 0 and dec_seq // GRID_W >= WIN_H
    assert seq % POOL_ROWS == 0 and seq % CONV_ROWS == 0
    assert dec_seq % POOL_ROWS == 0 and dec_seq % CONV_ROWS == 0
    assert (dec_seq // GRID_W) % NA_GROUP == 0

    cond_rows = SUBLANES * ((1 + dec_batch + SUBLANES - 1) // SUBLANES)
    m = _modulation(c_ctx, c, w_mod, b_mod, cond_rows)
    m_spec = _const_spec(m.shape)

    small = _small_params(norm_g, final_g, w_pool, pool_scale, conv_c, conv_d, conv_d_b, ln_g, ln_b)
    small_specs = [_const_spec(a.shape) for a in small]
    w_f32 = (w_in_even, w_out_even, w_in_odd, w_out_odd)
    assert all(w.shape[0] == 1 and w.shape[1] % STAGE_ROWS == 0 for w in w_f32)
    any_spec = pl.BlockSpec(memory_space=pl.ANY)

    nb = NB_PROMPT
    assert batch % nb == 0
    kv_shape = jax.ShapeDtypeStruct((batch, 1, N_HEADS, HEAD_DIM, seq), F32)
    kv_spec = pl.BlockSpec((nb, 1, N_HEADS, HEAD_DIM, seq), lambda i: (i, 0, 0, 0, 0))
    y_prompt, new_kt, new_vt, wie, woe, wio, woo = pl.pallas_call(
        functools.partial(_prompt_body, nb=nb, t=seq),
        out_shape=(jax.ShapeDtypeStruct(x_prompt.shape, F32), kv_shape, kv_shape)
                  + tuple(jax.ShapeDtypeStruct(w.shape[1:], BF16) for w in w_f32),
        grid=(batch // nb,),
        in_specs=[pl.BlockSpec((nb, seq, d), lambda i: (i, 0, 0)), m_spec] + small_specs
                 + [any_spec] * len(w_f32),
        out_specs=(pl.BlockSpec((nb, seq, d), lambda i: (i, 0, 0)), kv_spec, kv_spec)
                  + (any_spec,) * len(w_f32),
        scratch_shapes=_stream_scratch(nb, seq, True)
                       + [pltpu.VMEM(w.shape[1:], BF16) for w in w_f32] + [
            pltpu.VMEM((W_HALF, d), BF16),
            pltpu.VMEM((W_HALF, d), BF16),
            pltpu.VMEM((STAGE_SLOTS, STAGE_ROWS, max(w.shape[2] for w in w_f32)), F32),
            pltpu.SemaphoreType.DMA((STAGE_SLOTS,)),
            pltpu.SemaphoreType.DMA((len(w_f32),)),
        ],
        compiler_params=pltpu.CompilerParams(dimension_semantics=("arbitrary",),
                                             vmem_limit_bytes=VMEM_LIMIT),
        name="prompt",
    )(x_prompt, m, *small, *w_f32)
    ng, fg, wp, ps, cc, cdw, cdb, lng, lnb = small
    w_args = [ng, fg, wie, wp, ps, woe, wio, cc, cdw, cdb, lng, lnb, woo]
    late_w = (woe, wio, woo)
    w_specs = [any_spec if any(a is w for w in late_w) else _const_spec(a.shape) for a in w_args]

    past = cache_k.shape[3]
    cache_spec = pl.BlockSpec((1, 1, N_HEADS, HEAD_DIM, past), lambda i: (i, 0, 0, 0, 0))
    rpb_t = jnp.swapaxes(rpb[0], 0, 1)
    y_sample = pl.pallas_call(
        functools.partial(_sample_body, t=dec_seq),
        out_shape=jax.ShapeDtypeStruct(x_sample.shape, F32),
        grid=(dec_batch,),
        in_specs=[pl.BlockSpec((1, dec_seq, d), lambda i: (i, 0, 0), pipeline_mode=pl.Buffered(1)),
                  m_spec] + w_specs
                 + [cache_spec, cache_spec, _const_spec(rpb_t.shape)],
        out_specs=pl.BlockSpec((1, dec_seq, d), lambda i: (i, 0, 0)),
        scratch_shapes=_stream_scratch(1, dec_seq, False) + [
            pltpu.VMEM((N_DR * N_HEADS, LANES), F32),
            pltpu.VMEM((N_HEADS * PAIR_TILES, 2 * GRID_W, LANES), F32),
            pltpu.VMEM((2, LANES, past), BF16),
        ] + [pltpu.VMEM(w.shape, BF16) for w in late_w] + [pltpu.SemaphoreType.DMA((len(late_w),))],
        compiler_params=pltpu.CompilerParams(dimension_semantics=("arbitrary",),
                                             vmem_limit_bytes=VMEM_LIMIT),
        name="sample",
    )(x_sample, m, *w_args, jnp.swapaxes(cache_k, 3, 4), jnp.swapaxes(cache_v, 3, 4), rpb_t)

    return (y_prompt, y_sample, jnp.swapaxes(new_kt, 3, 4), jnp.swapaxes(new_vt, 3, 4))
```

```python
import functools

import jax
import jax.numpy as jnp
from jax import lax
from jax.experimental import pallas as pl
from jax.experimental.pallas import tpu as pltpu

F32 = jnp.float32
BF16 = jnp.bfloat16

D_MODEL = 1024
W_HALF = 512
N_POOL_GROUPS = 4
POOL_HALF = (1, 2, 4, 8)
N_HEADS = 8
HEAD_DIM = 64
GRID_W = 64
WIN_H = 8
WIN_W = 16
CONV_C = 3
CONV_D = 31
EPS = 1e-6
MASKED = -1e30
LOG2_E = 1.4426950408889634
Q_SCALE = HEAD_DIM ** -0.5 * LOG2_E

LANES = 128
SUBLANES = 8
PAD = 16
ROW_CHUNK = 512
NORM_ROWS = 32
POOL_ROWS = 256
CONV_ROWS = 128
Q_ROWS = 128
NB_PROMPT = 2
NA_GROUP = 8
MOD_ROWS = 512
STAGE_ROWS = 128
STAGE_SLOTS = 4
VMEM_LIMIT = 58 * 1024 * 1024

assert PAD >= CONV_D // 2 + 1 and PAD % SUBLANES == 0 and PAD >= 2 * SUBLANES
assert max(POOL_HALF) <= SUBLANES


def _sigmoid(x):
    return 1.0 / (1.0 + jnp.exp(-x))


def _silu(x):
    return x * _sigmoid(x)


def _dot(a, b):
    return jnp.dot(a, b, preferred_element_type=F32)


def _dot_nt(a, b):
    return lax.dot_general(a, b, (((1,), (1,)), ((), ())), preferred_element_type=F32)


def _lanes(j):
    return slice(j * LANES, (j + 1) * LANES)


def _group(g):
    return slice(g * W_HALF, (g + 1) * W_HALF)


def _rows(start, size, align):
    if isinstance(start, int):
        return slice(start, start + size)
    return pl.ds(pl.multiple_of(start, align), size)


def _mod_body(cctx_ref, c_ref, w_ref, b_ref, o_ref, act_ref):
    layer, kb = pl.program_id(0), pl.program_id(1)
    rows, d = act_ref.shape

    @pl.when(kb == 0)
    def _():
        r = lax.broadcasted_iota(jnp.int32, (rows, d), 0)
        cond = jnp.where(r == 0, cctx_ref[...], 0.0)
        for i in range(c_ref.shape[0]):
            cond = jnp.where(r == i + 1, c_ref[i:i + 1, :], cond)
        act_ref[...] = _silu(cond).astype(BF16)
        o_ref[0] = jnp.broadcast_to(jnp.where(layer == 0, b_ref[0:1, :], b_ref[1:2, :]), o_ref.shape[1:])

    act = act_ref[:, pl.ds(pl.multiple_of(kb * MOD_ROWS, MOD_ROWS), MOD_ROWS)]
    o_ref[0] += _dot(act, w_ref[0].astype(BF16))


def _modulation(c_ctx, c, w_mod, b_mod, rows):
    depth, d, n = w_mod.shape
    assert depth == 2 and 1 + c.shape[0] <= rows and d % MOD_ROWS == 0
    return pl.pallas_call(
        _mod_body,
        out_shape=jax.ShapeDtypeStruct((depth, rows, n), F32),
        grid=(depth, d // MOD_ROWS),
        in_specs=[
            pl.BlockSpec((1, d), lambda l, k: (0, 0)),
            pl.BlockSpec(c.shape, lambda l, k: (0, 0)),
            pl.BlockSpec((1, MOD_ROWS, n), lambda l, k: (l, k, 0)),
            pl.BlockSpec((depth, n), lambda l, k: (0, 0)),
        ],
        out_specs=pl.BlockSpec((1, rows, n), lambda l, k: (l, 0, 0)),
        scratch_shapes=[pltpu.VMEM((rows, d), BF16)],
        compiler_params=pltpu.CompilerParams(dimension_semantics=("arbitrary", "arbitrary")),
        name="mod",
    )(c_ctx.reshape(1, d), c, w_mod, b_mod)


def _cond_row(m_ref, layer, row):
    if isinstance(row, int):
        return m_ref[layer, row:row + 1, :]
    m = m_ref[layer]
    keep = lax.broadcasted_iota(jnp.int32, m.shape, 0) == row
    return jnp.sum(jnp.where(keep, m, 0.0), axis=0, keepdims=True)


def _pieces(c, nb, t):
    if t >= ROW_CHUNK:
        per_seq = t // ROW_CHUNK
        s = 0 if nb == 1 else c // per_seq
        return [(s, (c - s * per_seq) * ROW_CHUNK, ROW_CHUNK, 0)]
    per_chunk = ROW_CHUNK // t
    return [(c * per_chunk + i, 0, t, i * t) for i in range(per_chunk)]


def _for_chunks(n, body):
    for c in range(n):
        body(c)


def _pad_row(s, off, t):
    return s * (t + 2 * PAD) + PAD + off


def _store_padded(pad_ref, val, pieces, t):
    for s, off, n, o in pieces:
        pad_ref[_rows(_pad_row(s, off, t), n, SUBLANES), :] = val[o:o + n]


def _scale_padded(pad_ref, val, pieces, t):
    for s, off, n, o in pieces:
        rows = _rows(_pad_row(s, off, t), n, SUBLANES)
        pad_ref[rows, :] = pad_ref[rows, :] * val[o:o + n]


def _modnorm_chunk(src_ref, h_ref, c, nb, t, gain, shift):
    for s, off, n, o in _pieces(c, nb, t):
        for i in range(0, n, NORM_ROWS):
            x = src_ref[s, _rows(off + i, NORM_ROWS, NORM_ROWS), :]
            ms = jnp.mean(x * x, axis=-1, keepdims=True)
            h_ref[_rows(c * ROW_CHUNK + o + i, NORM_ROWS, NORM_ROWS), :] = (
                x * lax.rsqrt(ms + EPS) * gain + shift).astype(BF16)


def _zero_pads(pad_ref, nb, t):
    z = jnp.zeros((PAD, W_HALF), F32)
    for s in range(nb):
        pad_ref[_pad_row(s, 0, t) - PAD:_pad_row(s, 0, t), :] = z
        pad_ref[_pad_row(s, t, t):_pad_row(s, t, t) + PAD, :] = z


def _pool_phase(pad_ref, ga_ref, wp_ref, ps_ref, ab_ref, nb, t):
    n_rows = POOL_ROWS
    per_seq = t // n_rows

    def step(i, carry):
        s = i // per_seq
        r0 = (i - s * per_seq) * n_rows
        prow = _pad_row(s, r0, t)
        rows = _rows(i * n_rows, n_rows, n_rows)
        pos = r0 + lax.broadcasted_iota(jnp.int32, (n_rows, LANES), 0)
        before = jnp.minimum(pos, SUBLANES)
        after = jnp.minimum(t - pos, SUBLANES)
        for g in range(N_POOL_GROUPS):
            hw = POOL_HALF[g]
            ln = _lanes(g)
            halo = n_rows + 2 * SUBLANES
            blk = pad_ref[_rows(prow - SUBLANES, halo, SUBLANES), ln]
            run, n = blk, 1
            while n < 2 * hw:
                run = run + pltpu.roll(run, halo - n, 0)
                n *= 2
            if hw < SUBLANES:
                run = pltpu.roll(run, halo - (SUBLANES - hw), 0)
            win = run[:n_rows]
            cnt = (jnp.minimum(before, hw) + jnp.minimum(after, hw)).astype(F32)
            p = (win / cnt - blk[SUBLANES:SUBLANES + n_rows]).astype(BF16)
            y = _dot(p, wp_ref[0, g].astype(BF16)) * ps_ref[:, ln] * ga_ref[rows, ln]
            ab_ref[rows, ln] = y.astype(BF16)
        return carry
    lax.fori_loop(0, nb * per_seq, step, 0)


def _out_proj_chunk(ab_ref, w_ref, x_ref, gate, dst_ref, c, nb, t):
    lhs = ab_ref[_rows(c * ROW_CHUNK, ROW_CHUNK, ROW_CHUNK), :]
    for g in range(D_MODEL // W_HALF):
        y = _dot(lhs, w_ref[:, _group(g)])
        for s, off, n, o in _pieces(c, nb, t):
            rows = _rows(off, n, n)
            dst_ref[s, rows, _group(g)] = x_ref[s, rows, _group(g)] + gate[:, _group(g)] * y[o:o + n]


def _shift_up(x, o, n):
    if o % SUBLANES == 0:
        return x[o:o + n]
    return pltpu.roll(x, x.shape[0] - o, 0)[:n]


def _conv_phase(pad_c, pad_d, bc_ref, ga_ref, gb_ref, cc_ref, cdw_ref, cdb_ref, lng_ref, lnb_ref,
                ab_ref, nb, t):
    n_rows = CONV_ROWS
    per_seq = t // n_rows

    def step(i, carry):
        s = i // per_seq
        r0 = (i - s * per_seq) * n_rows
        prow = _pad_row(s, r0, t)
        rows = _rows(i * n_rows, n_rows, n_rows)
        z = []
        for g in range(W_HALF // LANES):
            ln = _lanes(g)
            blk = pad_c[_rows(prow - SUBLANES, n_rows + 2 * SUBLANES, SUBLANES), ln]
            c3 = None
            for j in range(CONV_C):
                o = SUBLANES + j - CONV_C // 2
                term = _shift_up(blk, o, n_rows) * cc_ref[j, :, ln]
                c3 = term if c3 is None else c3 + term
            ab_ref[rows, ln] = (bc_ref[rows, ln] * c3 * ga_ref[rows, ln]).astype(BF16)
            acc = None
            for sft in range(SUBLANES):
                part = None
                for a in range((CONV_D - sft + SUBLANES - 1) // SUBLANES):
                    j = SUBLANES * a + sft
                    src = pad_d[_rows(prow - 2 * SUBLANES + SUBLANES * a, n_rows + SUBLANES,
                                      SUBLANES), ln]
                    term = src * cdw_ref[j, :, ln]
                    part = term if part is None else part + term
                o = SUBLANES + sft - (CONV_D // 2 - SUBLANES)
                part = _shift_up(part, o, n_rows)
                acc = part if acc is None else acc + part
            z.append(acc + cdb_ref[:, ln])
        z = jnp.concatenate(z, axis=-1)
        mu = jnp.mean(z, axis=-1, keepdims=True)
        zc = z - mu
        var = jnp.mean(zc * zc, axis=-1, keepdims=True)
        zn = zc * lax.rsqrt(var + EPS) * lng_ref[...] + lnb_ref[...]
        ab_ref[rows, W_HALF:] = (_silu(zn) * gb_ref[rows, :]).astype(BF16)
        return carry
    lax.fori_loop(0, nb * per_seq, step, 0)


def _final_norm_chunk(y_ref, fg, c, nb, t):
    for s, off, n, _ in _pieces(c, nb, t):
        for i in range(0, n, NORM_ROWS):
            rows = _rows(off + i, NORM_ROWS, NORM_ROWS)
            x = y_ref[s, rows, :]
            ms = jnp.mean(x * x, axis=-1, keepdims=True)
            y_ref[s, rows, :] = x * lax.rsqrt(ms + EPS) * fg


def _odd_layer(y_ref, m_row, g_row, fg, wio_ref, cc_ref, cdw_ref, cdb_ref, lng_ref, lnb_ref, woo_ref,
               h_ref, pad_c, pad_d, bc_ref, ga_ref, gb_ref, ab_ref, nb, t):
    shift = m_row[:, :D_MODEL]
    gain = g_row * (1.0 + m_row[:, D_MODEL:2 * D_MODEL])
    gate = m_row[:, 2 * D_MODEL:]
    n_chunks = nb * t // ROW_CHUNK

    def in_proj(c):
        _modnorm_chunk(y_ref, h_ref, c, nb, t, gain, shift)
        rows = _rows(c * ROW_CHUNK, ROW_CHUNK, ROW_CHUNK)
        pieces = _pieces(c, nb, t)
        h = h_ref[rows, :]
        bc_ref[rows, :] = _dot(h, wio_ref[:, _group(0)])
        _store_padded(pad_c, _dot(h, wio_ref[:, _group(1)]), pieces, t)
        _scale_padded(pad_c, _dot(h, wio_ref[:, _group(2)]), pieces, t)
        ga_ref[rows, :] = _silu(_dot(h, wio_ref[:, _group(3)]))
        _store_padded(pad_d, _dot(h, wio_ref[:, _group(4)]), pieces, t)
        _scale_padded(pad_d, _sigmoid(_dot(h, wio_ref[:, _group(5)])), pieces, t)
        gb_ref[rows, :] = _silu(_dot(h, wio_ref[:, _group(6)]))
    _for_chunks(n_chunks, in_proj)

    _conv_phase(pad_c, pad_d, bc_ref, ga_ref, gb_ref, cc_ref, cdw_ref, cdb_ref, lng_ref, lnb_ref,
                ab_ref, nb, t)

    def out_proj(c):
        _out_proj_chunk(ab_ref, woo_ref, y_ref, gate, y_ref, c, nb, t)
        _final_norm_chunk(y_ref, fg, c, nb, t)
    _for_chunks(n_chunks, out_proj)


def _even_in_proj(x_ref, m_row, g_row, w_ref, h_ref, pad_a, ga_ref, gb_ref, q_ref, k_ref, v_ref,
                  kv_t, nb, t):
    shift = m_row[:, :D_MODEL]
    gain = g_row * (1.0 + m_row[:, D_MODEL:2 * D_MODEL])

    def in_proj(c):
        _modnorm_chunk(x_ref, h_ref, c, nb, t, gain, shift)
        rows = _rows(c * ROW_CHUNK, ROW_CHUNK, ROW_CHUNK)
        pieces = _pieces(c, nb, t)
        h = h_ref[rows, :]
        _store_padded(pad_a, _dot(h, w_ref[:, _group(0)]), pieces, t)
        ga_ref[rows, :] = _silu(_dot(h, w_ref[:, _group(1)]))
        q_ref[rows, :] = (_dot(h, w_ref[:, _group(2)]) * Q_SCALE).astype(BF16)
        for i, (dst, g) in enumerate(((k_ref, 3), (v_ref, 4))):
            if kv_t is None:
                dst[rows, :] = _dot(h, w_ref[:, _group(g)]).astype(BF16)
                continue
            acc = _dot_nt(kv_t[2 + i][...], h)
            dst[:, rows] = acc.astype(BF16)
            for s, off, n, o in pieces:
                for hd in range(N_HEADS):
                    kv_t[i][s, 0, hd, :, _rows(off, n, n)] = (
                        acc[hd * HEAD_DIM:(hd + 1) * HEAD_DIM, o:o + n])
        gb_ref[rows, :] = _silu(_dot(h, w_ref[:, _group(5)]))
    _for_chunks(nb * t // ROW_CHUNK, in_proj)


def _even_out_proj(x_ref, y_ref, m_row, w_ref, ab_ref, nb, t):
    gate = m_row[:, 2 * D_MODEL:]
    _for_chunks(nb * t // ROW_CHUNK,
                lambda c: _out_proj_chunk(ab_ref, w_ref, x_ref, gate, y_ref, c, nb, t))


def _split_heads(x):
    lane = lax.broadcasted_iota(jnp.int32, (1, LANES), 1)
    first = jnp.where(lane < HEAD_DIM, 1.0, 0.0).astype(x.dtype)
    return jnp.concatenate([x * first, x * (1 - first)], axis=0)


def _merge_heads(o):
    n = o.shape[0] // 2
    lane = lax.broadcasted_iota(jnp.int32, (n, LANES), 1)
    return jnp.where(lane < HEAD_DIM, o[:n], o[n:])


def _context_attention(q_ref, kt_ref, vt_ref, gb_ref, ab_ref, nb, t):
    for s in range(nb):
        seq = slice(s * t, (s + 1) * t)
        for j in range(N_HEADS // 2):
            ln = _lanes(j)
            kp = kt_ref[ln, seq]
            vp = vt_ref[ln, seq]
            for r0 in range(0, t, Q_ROWS):
                rows = slice(s * t + r0, s * t + r0 + Q_ROWS)
                sc = _dot(_split_heads(q_ref[rows, ln]), kp)
                p = jnp.exp2(sc - jnp.max(sc, axis=-1, keepdims=True))
                o = _dot_nt(p.astype(BF16), vp) / jnp.sum(p, axis=-1, keepdims=True)
                ab_ref[rows, W_HALF + j * LANES:W_HALF + (j + 1) * LANES] = (
                    _merge_heads(o) * gb_ref[rows, ln]).astype(BF16)


def _stage_weights(w_hbm, w_bf, wkt_ref, wvt_ref, stage, sem_in):
    chunks = [(k, r0) for k in range(len(w_hbm)) for r0 in range(0, w_hbm[k].shape[1], STAGE_ROWS)]

    def fetch(i):
        k, r0 = chunks[i]
        cols = w_hbm[k].shape[2]
        slot = i % STAGE_SLOTS
        return pltpu.make_async_copy(w_hbm[k].at[0, pl.ds(r0, STAGE_ROWS), :],
                                     stage.at[slot, :, pl.ds(0, cols)], sem_in.at[slot])

    for i in range(min(STAGE_SLOTS - 1, len(chunks))):
        fetch(i).start()
    for i, (k, r0) in enumerate(chunks):
        if i + STAGE_SLOTS - 1 < len(chunks):
            fetch(i + STAGE_SLOTS - 1).start()
        fetch(i).wait()
        cols = w_hbm[k].shape[2]
        rows = slice(r0, r0 + STAGE_ROWS)
        slot = i % STAGE_SLOTS
        w_bf[k][rows, :] = stage[slot, :, 0:cols].astype(BF16)
        if k == 0:
            wkt_ref[:, rows] = stage[slot, :, _group(3)].T.astype(BF16)
            wvt_ref[:, rows] = stage[slot, :, _group(4)].T.astype(BF16)


def _prompt_body(x_ref, m_ref, ng_ref, fg_ref, wp_ref, ps_ref, cc_ref, cdw_ref, cdb_ref, lng_ref,
                 lnb_ref, wie_hbm, woe_hbm, wio_hbm, woo_hbm,
                 y_ref, ko_ref, vo_ref, wie_out, woe_out, wio_out, woo_out,
                 h_ref, pad_a, pad_b, ga_ref, gb_ref, bc_ref, q_ref, kt_ref, vt_ref, ab_ref,
                 wie_ref, woe_ref, wio_ref, woo_ref, wkt_ref, wvt_ref, stage, sem_in, sem_out,
                 *, nb, t):
    w_out = (wie_out, woe_out, wio_out, woo_out)
    w_bf = (wie_ref, woe_ref, wio_ref, woo_ref)

    def write_back(k):
        return pltpu.make_async_copy(w_bf[k], w_out[k], sem_out.at[k])

    @pl.when(pl.program_id(0) == 0)
    def _():
        _stage_weights((wie_hbm, woe_hbm, wio_hbm, woo_hbm), w_bf, wkt_ref, wvt_ref, stage, sem_in)
        for k in range(len(w_bf)):
            write_back(k).start()

    _zero_pads(pad_a, nb, t)
    _zero_pads(pad_b, nb, t)
    m_even = _cond_row(m_ref, 0, 0)
    _even_in_proj(x_ref, m_even, ng_ref[0:1, :], wie_ref, h_ref, pad_a, ga_ref, gb_ref,
                  q_ref, kt_ref, vt_ref, (ko_ref, vo_ref, wkt_ref, wvt_ref), nb, t)
    _pool_phase(pad_a, ga_ref, wp_ref, ps_ref, ab_ref, nb, t)
    _context_attention(q_ref, kt_ref, vt_ref, gb_ref, ab_ref, nb, t)
    _even_out_proj(x_ref, y_ref, m_even, woe_ref, ab_ref, nb, t)
    _odd_layer(y_ref, _cond_row(m_ref, 1, 0), ng_ref[1:2, :], fg_ref[...], wio_ref, cc_ref, cdw_ref,
               cdb_ref, lng_ref, lnb_ref, woo_ref, h_ref, pad_a, pad_b, bc_ref, ga_ref, gb_ref, ab_ref,
               nb, t)

    @pl.when(pl.program_id(0) == 0)
    def _():
        for k in range(len(w_bf)):
            write_back(k).wait()


def _rpb_rows(rpb_ref, e_ref):
    n = rpb_ref.shape[0] * rpb_ref.shape[1]
    lane = lax.broadcasted_iota(jnp.int32, (n, LANES), 1)
    i = jnp.where(lane < GRID_W, lane, lane - LANES)
    idx = jnp.clip(i, -(WIN_W - 1), WIN_W - 1) + (WIN_W - 1)
    rp = rpb_ref[...].reshape(n, rpb_ref.shape[2])
    e = jnp.zeros((n, LANES), F32)
    for d in range(2 * WIN_W - 1):
        e = jnp.where(idx == d, rp[:, d:d + 1], e)
    e_ref[...] = e


N_DR = 2 * WIN_H - 1
PAIR_TILES = N_DR // 2


def _bias_tile_index(j, dr_lo):
    if isinstance(dr_lo, int):
        parity, half = dr_lo % 2, dr_lo // 2
    else:
        parity, half = dr_lo & 1, lax.shift_right_logical(dr_lo, 1)
    return (2 * j + parity) * PAIR_TILES + half


def _bias_tables(e_ref, bias_ref):
    q = lax.broadcasted_iota(jnp.int32, (GRID_W, LANES), 0)
    lane = lax.broadcasted_iota(jnp.int32, (GRID_W, LANES), 1)
    kw = jnp.where(lane < GRID_W, lane, lane - GRID_W)
    start = jnp.clip(q - WIN_W // 2, 0, GRID_W - WIN_W)
    col_ok = (kw >= start) & (kw < start + WIN_W)
    for j in range(N_HEADS // 2):
        for dr in range(N_DR - 1):
            for e in range(2):
                r_lo = dr * N_HEADS + 2 * j + e
                r_hi = r_lo + N_HEADS
                lo = jnp.broadcast_to(e_ref[r_lo:r_lo + 1, :], (GRID_W, LANES))
                hi = jnp.broadcast_to(e_ref[r_hi:r_hi + 1, :], (GRID_W, LANES))
                lo = pltpu.roll(lo, 0, 1, stride=1, stride_axis=0)
                hi = pltpu.roll(hi, GRID_W, 1, stride=1, stride_axis=0)
                tile = jnp.where(lane < GRID_W, lo, hi)
                bias_ref[_bias_tile_index(j, dr), e * GRID_W:(e + 1) * GRID_W, :] = jnp.where(
                    col_ok, tile * LOG2_E, MASKED)


def _neighbourhood_attention(q_ref, k_ref, v_ref, ck_ref, cv_ref, bias_ref, kvc_ref, gb_ref, ab_ref, t):
    grid_h = t // GRID_W
    band = WIN_H * GRID_W
    for j in range(N_HEADS // 2):
        ln = _lanes(j)
        for i, src in enumerate((ck_ref, cv_ref)):
            kvc_ref[i] = jnp.concatenate([src[0, 0, 2 * j], src[0, 0, 2 * j + 1]],
                                         axis=0).astype(BF16)

        def per_group(g, carry, ln=ln, j=j):
            scored = []
            for u in range(NA_GROUP):
                r = g * NA_GROUP + u
                start = jnp.clip(r - WIN_H // 2, 0, grid_h - WIN_H)
                rows = _rows(r * GRID_W, GRID_W, GRID_W)
                keys = _rows(start * GRID_W, band, GRID_W)
                q2 = _split_heads(q_ref[rows, ln])
                dr0 = (WIN_H - 1) - (r - start)
                bias = jnp.concatenate([bias_ref[_bias_tile_index(j, dr0 + 2 * i)]
                                        for i in range(WIN_H // 2)], axis=-1)
                scored.append((rows, keys, _dot_nt(q2, k_ref[keys, ln]) + bias, _dot(q2, kvc_ref[0])))
            weighted = []
            for rows, keys, s_loc, s_ctx in scored:
                mx = jnp.maximum(jnp.max(s_loc, axis=-1, keepdims=True),
                                 jnp.max(s_ctx, axis=-1, keepdims=True))
                p_loc = jnp.exp2(s_loc - mx)
                p_ctx = jnp.exp2(s_ctx - mx)
                den = (jnp.sum(p_loc, axis=-1, keepdims=True)
                       + jnp.sum(p_ctx, axis=-1, keepdims=True))
                weighted.append((rows, keys, p_loc.astype(BF16), p_ctx.astype(BF16), den))
            for rows, keys, p_loc, p_ctx, den in weighted:
                o = (_dot(p_loc, v_ref[keys, ln]) + _dot_nt(p_ctx, kvc_ref[1])) / den
                ab_ref[rows, W_HALF + j * LANES:W_HALF + (j + 1) * LANES] = (
                    _merge_heads(o) * gb_ref[rows, ln]).astype(BF16)
            return carry
        lax.fori_loop(0, grid_h // NA_GROUP, per_group, 0)


def _sample_body(x_ref, m_ref, ng_ref, fg_ref, wie_ref, wp_ref, ps_ref, woe_hbm, wio_hbm, cc_ref,
                 cdw_ref, cdb_ref, lng_ref, lnb_ref, woo_hbm, ck_ref, cv_ref, rpb_ref,
                 y_ref,
                 h_ref, pad_a, pad_b, ga_ref, gb_ref, bc_ref, q_ref, k_ref, v_ref, ab_ref,
                 e_ref, bias_ref, kvc_ref, woe_ref, wio_ref, woo_ref, sem_w, *, t):
    _zero_pads(pad_a, 1, t)
    _zero_pads(pad_b, 1, t)
    late = ((woe_hbm, woe_ref), (wio_hbm, wio_ref), (woo_hbm, woo_ref))

    def late_copy(i):
        return pltpu.make_async_copy(late[i][0], late[i][1], sem_w.at[i])

    first_step = pl.program_id(0) == 0

    @pl.when(first_step)
    def _():
        for i in range(len(late)):
            late_copy(i).start()
        _rpb_rows(rpb_ref, e_ref)
        _bias_tables(e_ref, bias_ref)

    cond = pl.program_id(0) + 1
    m_even = _cond_row(m_ref, 0, cond)
    _even_in_proj(x_ref, m_even, ng_ref[0:1, :], wie_ref, h_ref, pad_a, ga_ref, gb_ref,
                  q_ref, k_ref, v_ref, None, 1, t)
    _pool_phase(pad_a, ga_ref, wp_ref, ps_ref, ab_ref, 1, t)
    _neighbourhood_attention(q_ref, k_ref, v_ref, ck_ref, cv_ref, bias_ref, kvc_ref, gb_ref, ab_ref, t)
    pl.when(first_step)(lambda: late_copy(0).wait())
    _even_out_proj(x_ref, y_ref, m_even, woe_ref, ab_ref, 1, t)

    @pl.when(first_step)
    def _():
        late_copy(1).wait()
        late_copy(2).wait()

    _odd_layer(y_ref, _cond_row(m_ref, 1, cond), ng_ref[1:2, :], fg_ref[...], wio_ref, cc_ref, cdw_ref,
               cdb_ref, lng_ref, lnb_ref, woo_ref, h_ref, pad_a, pad_b, bc_ref, ga_ref, gb_ref, ab_ref,
               1, t)


def _const_spec(shape):
    zeros = (0,) * len(shape)
    return pl.BlockSpec(shape, lambda i: zeros, pipeline_mode=pl.Buffered(1))


def _stream_scratch(nb, t, kv_transposed):
    r = nb * t
    padded = nb * (t + 2 * PAD)
    kv = (W_HALF, r) if kv_transposed else (r, W_HALF)
    return [
        pltpu.VMEM((r, D_MODEL), BF16),
        pltpu.VMEM((padded, W_HALF), F32),
        pltpu.VMEM((padded, W_HALF), F32),
        pltpu.VMEM((r, W_HALF), F32),
        pltpu.VMEM((r, W_HALF), F32),
        pltpu.VMEM((r, W_HALF), F32),
        pltpu.VMEM((r, W_HALF), BF16),
        pltpu.VMEM(kv, BF16),
        pltpu.VMEM(kv, BF16),
        pltpu.VMEM((r, D_MODEL), BF16),
    ]


def _small_params(norm_g, final_g, w_pool, pool_scale, conv_c, conv_d, conv_d_b, ln_g, ln_b):
    return [norm_g, final_g.reshape(1, D_MODEL), w_pool, pool_scale, jnp.swapaxes(conv_c, 0, 1),
            jnp.swapaxes(conv_d, 0, 1), conv_d_b, ln_g, ln_b]


def kernel(x_prompt, x_sample, cache_k, cache_v, c, c_ctx, norm_g, w_mod, b_mod, w_in_even, w_pool,
           pool_scale, rpb, w_out_even, w_in_odd, conv_c, conv_d, conv_d_b, ln_g, ln_b, w_out_odd,
           final_g):
    batch, seq, d = x_prompt.shape
    dec_batch, dec_seq, _ = x_sample.shape
    assert d == D_MODEL and w_mod.shape[0] == 2 and w_in_even.shape[0] == 1 and w_in_odd.shape[0] == 1
    assert (NB_PROMPT * seq) % ROW_CHUNK == 0 and ROW_CHUNK % seq == 0 and seq % Q_ROWS == 0
    assert dec_seq % ROW_CHUNK == 0 and dec_seq // GRID_W >= WIN_H
    assert seq % POOL_ROWS == 0 and seq % CONV_ROWS == 0
    assert dec_seq % POOL_ROWS == 0 and dec_seq % CONV_ROWS == 0
    assert (dec_seq // GRID_W) % NA_GROUP == 0

    cond_rows = SUBLANES * ((1 + dec_batch + SUBLANES - 1) // SUBLANES)
    m = _modulation(c_ctx, c, w_mod, b_mod, cond_rows)
    m_spec = _const_spec(m.shape)

    small = _small_params(norm_g, final_g, w_pool, pool_scale, conv_c, conv_d, conv_d_b, ln_g, ln_b)
    small_specs = [_const_spec(a.shape) for a in small]
    w_f32 = (w_in_even, w_out_even, w_in_odd, w_out_odd)
    assert all(w.shape[0] == 1 and w.shape[1] % STAGE_ROWS == 0 for w in w_f32)
    any_spec = pl.BlockSpec(memory_space=pl.ANY)

    nb = NB_PROMPT
    assert batch % nb == 0
    kv_shape = jax.ShapeDtypeStruct((batch, 1, N_HEADS, HEAD_DIM, seq), F32)
    kv_spec = pl.BlockSpec((nb, 1, N_HEADS, HEAD_DIM, seq), lambda i: (i, 0, 0, 0, 0))
    y_prompt, new_kt, new_vt, wie, woe, wio, woo = pl.pallas_call(
        functools.partial(_prompt_body, nb=nb, t=seq),
        out_shape=(jax.ShapeDtypeStruct(x_prompt.shape, F32), kv_shape, kv_shape)
                  + tuple(jax.ShapeDtypeStruct(w.shape[1:], BF16) for w in w_f32),
        grid=(batch // nb,),
        in_specs=[pl.BlockSpec((nb, seq, d), lambda i: (i, 0, 0)), m_spec] + small_specs
                 + [any_spec] * len(w_f32),
        out_specs=(pl.BlockSpec((nb, seq, d), lambda i: (i, 0, 0)), kv_spec, kv_spec)
                  + (any_spec,) * len(w_f32),
        scratch_shapes=_stream_scratch(nb, seq, True)
                       + [pltpu.VMEM(w.shape[1:], BF16) for w in w_f32] + [
            pltpu.VMEM((W_HALF, d), BF16),
            pltpu.VMEM((W_HALF, d), BF16),
            pltpu.VMEM((STAGE_SLOTS, STAGE_ROWS, max(w.shape[2] for w in w_f32)), F32),
            pltpu.SemaphoreType.DMA((STAGE_SLOTS,)),
            pltpu.SemaphoreType.DMA((len(w_f32),)),
        ],
        compiler_params=pltpu.CompilerParams(dimension_semantics=("arbitrary",),
                                             vmem_limit_bytes=VMEM_LIMIT),
        name="prompt",
    )(x_prompt, m, *small, *w_f32)
    ng, fg, wp, ps, cc, cdw, cdb, lng, lnb = small
    w_args = [ng, fg, wie, wp, ps, woe, wio, cc, cdw, cdb, lng, lnb, woo]
    late_w = (woe, wio, woo)
    w_specs = [any_spec if any(a is w for w in late_w) else _const_spec(a.shape) for a in w_args]

    past = cache_k.shape[3]
    cache_spec = pl.BlockSpec((1, 1, N_HEADS, HEAD_DIM, past), lambda i: (i, 0, 0, 0, 0))
    rpb_t = jnp.swapaxes(rpb[0], 0, 1)
    y_sample = pl.pallas_call(
        functools.partial(_sample_body, t=dec_seq),
        out_shape=jax.ShapeDtypeStruct(x_sample.shape, F32),
        grid=(dec_batch,),
        in_specs=[pl.BlockSpec((1, dec_seq, d), lambda i: (i, 0, 0), pipeline_mode=pl.Buffered(1)),
                  m_spec] + w_specs
                 + [cache_spec, cache_spec, _const_spec(rpb_t.shape)],
        out_specs=pl.BlockSpec((1, dec_seq, d), lambda i: (i, 0, 0)),
        scratch_shapes=_stream_scratch(1, dec_seq, False) + [
            pltpu.VMEM((N_DR * N_HEADS, LANES), F32),
            pltpu.VMEM((N_HEADS * PAIR_TILES, 2 * GRID_W, LANES), F32),
            pltpu.VMEM((2, LANES, past), BF16),
        ] + [pltpu.VMEM(w.shape, BF16) for w in late_w] + [pltpu.SemaphoreType.DMA((len(late_w),))],
        compiler_params=pltpu.CompilerParams(dimension_semantics=("arbitrary",),
                                             vmem_limit_bytes=VMEM_LIMIT),
        name="sample",
    )(x_sample, m, *w_args, jnp.swapaxes(cache_k, 3, 4), jnp.swapaxes(cache_v, 3, 4), rpb_t)

    return (y_prompt, y_sample, jnp.swapaxes(new_kt, 3, 4), jnp.swapaxes(new_vt, 3, 4))
```

```python
import functools

import jax
import jax.numpy as jnp
from jax import lax
from jax.experimental import pallas as pl
from jax.experimental.pallas import tpu as pltpu

F32 = jnp.float32
BF16 = jnp.bfloat16

D_MODEL = 1024
W_HALF = 512
N_POOL_GROUPS = 4
POOL_HALF = (1, 2, 4, 8)
N_HEADS = 8
HEAD_DIM = 64
GRID_W = 64
WIN_H = 8
WIN_W = 16
CONV_C = 3
CONV_D = 31
EPS = 1e-6
MASKED = -1e30
LOG2_E = 1.4426950408889634
Q_SCALE = HEAD_DIM ** -0.5 * LOG2_E

LANES = 128
SUBLANES = 8
PAD = 16
ROW_CHUNK = 512
NORM_ROWS = 32
POOL_ROWS = 256
CONV_ROWS = 128
Q_ROWS = 128
NB_PROMPT = 2
NA_GROUP = 8
MOD_ROWS = 512
STAGE_ROWS = 128
STAGE_SLOTS = 4
VMEM_LIMIT = 58 * 1024 * 1024

assert PAD >= CONV_D // 2 + 1 and PAD % SUBLANES == 0 and PAD >= 2 * SUBLANES
assert max(POOL_HALF) <= SUBLANES


def _sigmoid(x):
    return 1.0 / (1.0 + jnp.exp(-x))


def _silu(x):
    return x * _sigmoid(x)


def _dot(a, b):
    return jnp.dot(a, b, preferred_element_type=F32)


def _dot_nt(a, b):
    return lax.dot_general(a, b, (((1,), (1,)), ((), ())), preferred_element_type=F32)


def _lanes(j):
    return slice(j * LANES, (j + 1) * LANES)


def _group(g):
    return slice(g * W_HALF, (g + 1) * W_HALF)


def _rows(start, size, align):
    if isinstance(start, int):
        return slice(start, start + size)
    return pl.ds(pl.multiple_of(start, align), size)


def _mod_body(cctx_ref, c_ref, w_ref, b_ref, o_ref, act_ref):
    layer, kb = pl.program_id(0), pl.program_id(1)
    rows, d = act_ref.shape

    @pl.when(kb == 0)
    def _():
        r = lax.broadcasted_iota(jnp.int32, (rows, d), 0)
        cond = jnp.where(r == 0, cctx_ref[...], 0.0)
        for i in range(c_ref.shape[0]):
            cond = jnp.where(r == i + 1, c_ref[i:i + 1, :], cond)
        act_ref[...] = _silu(cond).astype(BF16)
        o_ref[0] = jnp.broadcast_to(jnp.where(layer == 0, b_ref[0:1, :], b_ref[1:2, :]), o_ref.shape[1:])

    act = act_ref[:, pl.ds(pl.multiple_of(kb * MOD_ROWS, MOD_ROWS), MOD_ROWS)]
    o_ref[0] += _dot(act, w_ref[0].astype(BF16))


def _modulation(c_ctx, c, w_mod, b_mod, rows):
    depth, d, n = w_mod.shape
    assert depth == 2 and 1 + c.shape[0] <= rows and d % MOD_ROWS == 0
    return pl.pallas_call(
        _mod_body,
        out_shape=jax.ShapeDtypeStruct((depth, rows, n), F32),
        grid=(depth, d // MOD_ROWS),
        in_specs=[
            pl.BlockSpec((1, d), lambda l, k: (0, 0)),
            pl.BlockSpec(c.shape, lambda l, k: (0, 0)),
            pl.BlockSpec((1, MOD_ROWS, n), lambda l, k: (l, k, 0)),
            pl.BlockSpec((depth, n), lambda l, k: (0, 0)),
        ],
        out_specs=pl.BlockSpec((1, rows, n), lambda l, k: (l, 0, 0)),
        scratch_shapes=[pltpu.VMEM((rows, d), BF16)],
        compiler_params=pltpu.CompilerParams(dimension_semantics=("arbitrary", "arbitrary")),
        name="mod",
    )(c_ctx.reshape(1, d), c, w_mod, b_mod)


def _cond_row(m_ref, layer, row):
    if isinstance(row, int):
        return m_ref[layer, row:row + 1, :]
    m = m_ref[layer]
    keep = lax.broadcasted_iota(jnp.int32, m.shape, 0) == row
    return jnp.sum(jnp.where(keep, m, 0.0), axis=0, keepdims=True)


def _pieces(c, nb, t):
    if t >= ROW_CHUNK:
        per_seq = t // ROW_CHUNK
        s = 0 if nb == 1 else c // per_seq
        return [(s, (c - s * per_seq) * ROW_CHUNK, ROW_CHUNK, 0)]
    per_chunk = ROW_CHUNK // t
    return [(c * per_chunk + i, 0, t, i * t) for i in range(per_chunk)]


def _for_chunks(n, body):
    for c in range(n):
        body(c)


def _pad_row(s, off, t):
    return s * (t + 2 * PAD) + PAD + off


def _store_padded(pad_ref, val, pieces, t):
    for s, off, n, o in pieces:
        pad_ref[_rows(_pad_row(s, off, t), n, SUBLANES), :] = val[o:o + n]


def _scale_padded(pad_ref, val, pieces, t):
    for s, off, n, o in pieces:
        rows = _rows(_pad_row(s, off, t), n, SUBLANES)
        pad_ref[rows, :] = pad_ref[rows, :] * val[o:o + n]


def _modnorm_chunk(src_ref, h_ref, c, nb, t, gain, shift):
    for s, off, n, o in _pieces(c, nb, t):
        for i in range(0, n, NORM_ROWS):
            x = src_ref[s, _rows(off + i, NORM_ROWS, NORM_ROWS), :]
            ms = jnp.mean(x * x, axis=-1, keepdims=True)
            h_ref[_rows(c * ROW_CHUNK + o + i, NORM_ROWS, NORM_ROWS), :] = (
                x * lax.rsqrt(ms + EPS) * gain + shift).astype(BF16)


def _zero_pads(pad_ref, nb, t):
    z = jnp.zeros((PAD, W_HALF), F32)
    for s in range(nb):
        pad_ref[_pad_row(s, 0, t) - PAD:_pad_row(s, 0, t), :] = z
        pad_ref[_pad_row(s, t, t):_pad_row(s, t, t) + PAD, :] = z


def _pool_phase(pad_ref, ga_ref, wp_ref, ps_ref, ab_ref, nb, t):
    n_rows = POOL_ROWS
    per_seq = t // n_rows

    def step(i, carry):
        s = i // per_seq
        r0 = (i - s * per_seq) * n_rows
        prow = _pad_row(s, r0, t)
        rows = _rows(i * n_rows, n_rows, n_rows)
        pos = r0 + lax.broadcasted_iota(jnp.int32, (n_rows, LANES), 0)
        before = jnp.minimum(pos, SUBLANES)
        after = jnp.minimum(t - pos, SUBLANES)
        for g in range(N_POOL_GROUPS):
            hw = POOL_HALF[g]
            ln = _lanes(g)
            halo = n_rows + 2 * SUBLANES
            blk = pad_ref[_rows(prow - SUBLANES, halo, SUBLANES), ln]
            run, n = blk, 1
            while n < 2 * hw:
                run = run + pltpu.roll(run, halo - n, 0)
                n *= 2
            if hw < SUBLANES:
                run = pltpu.roll(run, halo - (SUBLANES - hw), 0)
            win = run[:n_rows]
            cnt = (jnp.minimum(before, hw) + jnp.minimum(after, hw)).astype(F32)
            p = (win / cnt - blk[SUBLANES:SUBLANES + n_rows]).astype(BF16)
            y = _dot(p, wp_ref[0, g].astype(BF16)) * ps_ref[:, ln] * ga_ref[rows, ln]
            ab_ref[rows, ln] = y.astype(BF16)
        return carry
    lax.fori_loop(0, nb * per_seq, step, 0)


def _out_proj_chunk(ab_ref, w_ref, x_ref, gate, dst_ref, c, nb, t):
    lhs = ab_ref[_rows(c * ROW_CHUNK, ROW_CHUNK, ROW_CHUNK), :]
    for g in range(D_MODEL // W_HALF):
        y = _dot(lhs, w_ref[:, _group(g)])
        for s, off, n, o in _pieces(c, nb, t):
            rows = _rows(off, n, n)
            dst_ref[s, rows, _group(g)] = x_ref[s, rows, _group(g)] + gate[:, _group(g)] * y[o:o + n]


def _shift_up(x, o, n):
    if o % SUBLANES == 0:
        return x[o:o + n]
    return pltpu.roll(x, x.shape[0] - o, 0)[:n]


def _conv_phase(pad_c, pad_d, bc_ref, ga_ref, gb_ref, cc_ref, cdw_ref, cdb_ref, lng_ref, lnb_ref,
                ab_ref, nb, t):
    n_rows = CONV_ROWS
    per_seq = t // n_rows

    def step(i, carry):
        s = i // per_seq
        r0 = (i - s * per_seq) * n_rows
        prow = _pad_row(s, r0, t)
        rows = _rows(i * n_rows, n_rows, n_rows)
        z = []
        for g in range(W_HALF // LANES):
            ln = _lanes(g)
            blk = pad_c[_rows(prow - SUBLANES, n_rows + 2 * SUBLANES, SUBLANES), ln]
            c3 = None
            for j in range(CONV_C):
                o = SUBLANES + j - CONV_C // 2
                term = _shift_up(blk, o, n_rows) * cc_ref[j, :, ln]
                c3 = term if c3 is None else c3 + term
            ab_ref[rows, ln] = (bc_ref[rows, ln] * c3 * ga_ref[rows, ln]).astype(BF16)
            acc = None
            for sft in range(SUBLANES):
                part = None
                for a in range((CONV_D - sft + SUBLANES - 1) // SUBLANES):
                    j = SUBLANES * a + sft
                    src = pad_d[_rows(prow - 2 * SUBLANES + SUBLANES * a, n_rows + SUBLANES,
                                      SUBLANES), ln]
                    term = src * cdw_ref[j, :, ln]
                    part = term if part is None else part + term
                o = SUBLANES + sft - (CONV_D // 2 - SUBLANES)
                part = _shift_up(part, o, n_rows)
                acc = part if acc is None else acc + part
            z.append(acc + cdb_ref[:, ln])
        z = jnp.concatenate(z, axis=-1)
        mu = jnp.mean(z, axis=-1, keepdims=True)
        zc = z - mu
        var = jnp.mean(zc * zc, axis=-1, keepdims=True)
        zn = zc * lax.rsqrt(var + EPS) * lng_ref[...] + lnb_ref[...]
        ab_ref[rows, W_HALF:] = (_silu(zn) * gb_ref[rows, :]).astype(BF16)
        return carry
    lax.fori_loop(0, nb * per_seq, step, 0)


def _final_norm_chunk(y_ref, fg, c, nb, t):
    for s, off, n, _ in _pieces(c, nb, t):
        for i in range(0, n, NORM_ROWS):
            rows = _rows(off + i, NORM_ROWS, NORM_ROWS)
            x = y_ref[s, rows, :]
            ms = jnp.mean(x * x, axis=-1, keepdims=True)
            y_ref[s, rows, :] = x * lax.rsqrt(ms + EPS) * fg


def _odd_layer(y_ref, m_row, g_row, fg, wio_ref, cc_ref, cdw_ref, cdb_ref, lng_ref, lnb_ref, woo_ref,
               h_ref, pad_c, pad_d, bc_ref, ga_ref, gb_ref, ab_ref, nb, t):
    shift = m_row[:, :D_MODEL]
    gain = g_row * (1.0 + m_row[:, D_MODEL:2 * D_MODEL])
    gate = m_row[:, 2 * D_MODEL:]
    n_chunks = nb * t // ROW_CHUNK

    def in_proj(c):
        _modnorm_chunk(y_ref, h_ref, c, nb, t, gain, shift)
        rows = _rows(c * ROW_CHUNK, ROW_CHUNK, ROW_CHUNK)
        pieces = _pieces(c, nb, t)
        h = h_ref[rows, :]
        bc_ref[rows, :] = _dot(h, wio_ref[:, _group(0)])
        _store_padded(pad_c, _dot(h, wio_ref[:, _group(1)]), pieces, t)
        _scale_padded(pad_c, _dot(h, wio_ref[:, _group(2)]), pieces, t)
        ga_ref[rows, :] = _silu(_dot(h, wio_ref[:, _group(3)]))
        _store_padded(pad_d, _dot(h, wio_ref[:, _group(4)]), pieces, t)
        _scale_padded(pad_d, _sigmoid(_dot(h, wio_ref[:, _group(5)])), pieces, t)
        gb_ref[rows, :] = _silu(_dot(h, wio_ref[:, _group(6)]))
    _for_chunks(n_chunks, in_proj)

    _conv_phase(pad_c, pad_d, bc_ref, ga_ref, gb_ref, cc_ref, cdw_ref, cdb_ref, lng_ref, lnb_ref,
                ab_ref, nb, t)

    def out_proj(c):
        _out_proj_chunk(ab_ref, woo_ref, y_ref, gate, y_ref, c, nb, t)
        _final_norm_chunk(y_ref, fg, c, nb, t)
    _for_chunks(n_chunks, out_proj)


def _even_in_proj(x_ref, m_row, g_row, w_ref, h_ref, pad_a, ga_ref, gb_ref, q_ref, k_ref, v_ref,
                  kv_t, nb, t):
    shift = m_row[:, :D_MODEL]
    gain = g_row * (1.0 + m_row[:, D_MODEL:2 * D_MODEL])

    def in_proj(c):
        _modnorm_chunk(x_ref, h_ref, c, nb, t, gain, shift)
        rows = _rows(c * ROW_CHUNK, ROW_CHUNK, ROW_CHUNK)
        pieces = _pieces(c, nb, t)
        h = h_ref[rows, :]
        _store_padded(pad_a, _dot(h, w_ref[:, _group(0)]), pieces, t)
        ga_ref[rows, :] = _silu(_dot(h, w_ref[:, _group(1)]))
        q_ref[rows, :] = (_dot(h, w_ref[:, _group(2)]) * Q_SCALE).astype(BF16)
        for i, (dst, g) in enumerate(((k_ref, 3), (v_ref, 4))):
            if kv_t is None:
                dst[rows, :] = _dot(h, w_ref[:, _group(g)]).astype(BF16)
                continue
            acc = _dot_nt(kv_t[2 + i][...], h)
            dst[:, rows] = acc.astype(BF16)
            for s, off, n, o in pieces:
                for hd in range(N_HEADS):
                    kv_t[i][s, 0, hd, :, _rows(off, n, n)] = (
                        acc[hd * HEAD_DIM:(hd + 1) * HEAD_DIM, o:o + n])
        gb_ref[rows, :] = _silu(_dot(h, w_ref[:, _group(5)]))
    _for_chunks(nb * t // ROW_CHUNK, in_proj)


def _even_out_proj(x_ref, y_ref, m_row, w_ref, ab_ref, nb, t):
    gate = m_row[:, 2 * D_MODEL:]
    n_chunks = nb * t // ROW_CHUNK

    def out_proj(c, carry=0):
        _out_proj_chunk(ab_ref, w_ref, x_ref, gate, y_ref, c, nb, t)
        return carry
    if n_chunks == 1:
        out_proj(0)
    else:
        lax.fori_loop(0, n_chunks, out_proj, 0)


def _split_heads(x):
    lane = lax.broadcasted_iota(jnp.int32, (1, LANES), 1)
    first = jnp.where(lane < HEAD_DIM, 1.0, 0.0).astype(x.dtype)
    return jnp.concatenate([x * first, x * (1 - first)], axis=0)


def _merge_heads(o):
    n = o.shape[0] // 2
    lane = lax.broadcasted_iota(jnp.int32, (n, LANES), 1)
    return jnp.where(lane < HEAD_DIM, o[:n], o[n:])


def _context_attention(q_ref, kt_ref, vt_ref, gb_ref, ab_ref, nb, t):
    for s in range(nb):
        seq = slice(s * t, (s + 1) * t)
        for j in range(N_HEADS // 2):
            ln = _lanes(j)
            kp = kt_ref[ln, seq]
            vp = vt_ref[ln, seq]
            for r0 in range(0, t, Q_ROWS):
                rows = slice(s * t + r0, s * t + r0 + Q_ROWS)
                sc = _dot(_split_heads(q_ref[rows, ln]), kp)
                p = jnp.exp2(sc - jnp.max(sc, axis=-1, keepdims=True))
                o = _dot_nt(p.astype(BF16), vp) / jnp.sum(p, axis=-1, keepdims=True)
                ab_ref[rows, W_HALF + j * LANES:W_HALF + (j + 1) * LANES] = (
                    _merge_heads(o) * gb_ref[rows, ln]).astype(BF16)


def _stage_weights(w_hbm, w_bf, wkt_ref, wvt_ref, stage, sem_in):
    chunks = [(k, r0) for k in range(len(w_hbm)) for r0 in range(0, w_hbm[k].shape[1], STAGE_ROWS)]

    def fetch(i):
        k, r0 = chunks[i]
        cols = w_hbm[k].shape[2]
        slot = i % STAGE_SLOTS
        return pltpu.make_async_copy(w_hbm[k].at[0, pl.ds(r0, STAGE_ROWS), :],
                                     stage.at[slot, :, pl.ds(0, cols)], sem_in.at[slot])

    for i in range(min(STAGE_SLOTS - 1, len(chunks))):
        fetch(i).start()
    for i, (k, r0) in enumerate(chunks):
        if i + STAGE_SLOTS - 1 < len(chunks):
            fetch(i + STAGE_SLOTS - 1).start()
        fetch(i).wait()
        cols = w_hbm[k].shape[2]
        rows = slice(r0, r0 + STAGE_ROWS)
        slot = i % STAGE_SLOTS
        w_bf[k][rows, :] = stage[slot, :, 0:cols].astype(BF16)
        if k == 0:
            wkt_ref[:, rows] = stage[slot, :, _group(3)].T.astype(BF16)
            wvt_ref[:, rows] = stage[slot, :, _group(4)].T.astype(BF16)


def _prompt_body(x_ref, m_ref, ng_ref, fg_ref, wp_ref, ps_ref, cc_ref, cdw_ref, cdb_ref, lng_ref,
                 lnb_ref, wie_hbm, woe_hbm, wio_hbm, woo_hbm,
                 y_ref, ko_ref, vo_ref, wie_out, woe_out, wio_out, woo_out,
                 h_ref, pad_a, pad_b, ga_ref, gb_ref, bc_ref, q_ref, kt_ref, vt_ref, ab_ref,
                 wie_ref, woe_ref, wio_ref, woo_ref, wkt_ref, wvt_ref, stage, sem_in, sem_out,
                 *, nb, t):
    w_out = (wie_out, woe_out, wio_out, woo_out)
    w_bf = (wie_ref, woe_ref, wio_ref, woo_ref)

    def write_back(k):
        return pltpu.make_async_copy(w_bf[k], w_out[k], sem_out.at[k])

    @pl.when(pl.program_id(0) == 0)
    def _():
        _stage_weights((wie_hbm, woe_hbm, wio_hbm, woo_hbm), w_bf, wkt_ref, wvt_ref, stage, sem_in)
        for k in range(len(w_bf)):
            write_back(k).start()

    _zero_pads(pad_a, nb, t)
    _zero_pads(pad_b, nb, t)
    m_even = _cond_row(m_ref, 0, 0)
    _even_in_proj(x_ref, m_even, ng_ref[0:1, :], wie_ref, h_ref, pad_a, ga_ref, gb_ref,
                  q_ref, kt_ref, vt_ref, (ko_ref, vo_ref, wkt_ref, wvt_ref), nb, t)
    _pool_phase(pad_a, ga_ref, wp_ref, ps_ref, ab_ref, nb, t)
    _context_attention(q_ref, kt_ref, vt_ref, gb_ref, ab_ref, nb, t)
    _even_out_proj(x_ref, y_ref, m_even, woe_ref, ab_ref, nb, t)
    _odd_layer(y_ref, _cond_row(m_ref, 1, 0), ng_ref[1:2, :], fg_ref[...], wio_ref, cc_ref, cdw_ref,
               cdb_ref, lng_ref, lnb_ref, woo_ref, h_ref, pad_a, pad_b, bc_ref, ga_ref, gb_ref, ab_ref,
               nb, t)

    @pl.when(pl.program_id(0) == 0)
    def _():
        for k in range(len(w_bf)):
            write_back(k).wait()


def _rpb_rows(rpb_ref, e_ref):
    n = rpb_ref.shape[0] * rpb_ref.shape[1]
    lane = lax.broadcasted_iota(jnp.int32, (n, LANES), 1)
    i = jnp.where(lane < GRID_W, lane, lane - LANES)
    idx = jnp.clip(i, -(WIN_W - 1), WIN_W - 1) + (WIN_W - 1)
    rp = rpb_ref[...].reshape(n, rpb_ref.shape[2])
    e = jnp.zeros((n, LANES), F32)
    for d in range(2 * WIN_W - 1):
        e = jnp.where(idx == d, rp[:, d:d + 1], e)
    e_ref[...] = e


N_DR = 2 * WIN_H - 1
PAIR_TILES = N_DR // 2


def _bias_tile_index(j, dr_lo):
    if isinstance(dr_lo, int):
        parity, half = dr_lo % 2, dr_lo // 2
    else:
        parity, half = dr_lo & 1, lax.shift_right_logical(dr_lo, 1)
    return (2 * j + parity) * PAIR_TILES + half


def _bias_tables(e_ref, bias_ref):
    q = lax.broadcasted_iota(jnp.int32, (GRID_W, LANES), 0)
    lane = lax.broadcasted_iota(jnp.int32, (GRID_W, LANES), 1)
    kw = jnp.where(lane < GRID_W, lane, lane - GRID_W)
    start = jnp.clip(q - WIN_W // 2, 0, GRID_W - WIN_W)
    col_ok = (kw >= start) & (kw < start + WIN_W)
    for j in range(N_HEADS // 2):
        for dr in range(N_DR - 1):
            for e in range(2):
                r_lo = dr * N_HEADS + 2 * j + e
                r_hi = r_lo + N_HEADS
                lo = jnp.broadcast_to(e_ref[r_lo:r_lo + 1, :], (GRID_W, LANES))
                hi = jnp.broadcast_to(e_ref[r_hi:r_hi + 1, :], (GRID_W, LANES))
                lo = pltpu.roll(lo, 0, 1, stride=1, stride_axis=0)
                hi = pltpu.roll(hi, GRID_W, 1, stride=1, stride_axis=0)
                tile = jnp.where(lane < GRID_W, lo, hi)
                bias_ref[_bias_tile_index(j, dr), e * GRID_W:(e + 1) * GRID_W, :] = jnp.where(
                    col_ok, tile * LOG2_E, MASKED)


def _neighbourhood_attention(q_ref, k_ref, v_ref, ck_ref, cv_ref, bias_ref, kvc_ref, gb_ref, ab_ref, t):
    grid_h = t // GRID_W
    band = WIN_H * GRID_W
    def per_pair(j, carry_j):
        ln = pl.ds(pl.multiple_of(j * LANES, LANES), LANES)
        out_ln = pl.ds(pl.multiple_of(W_HALF + j * LANES, LANES), LANES)
        for i, src in enumerate((ck_ref, cv_ref)):
            kvc_ref[i] = jnp.concatenate([src[0, 0, 2 * j], src[0, 0, 2 * j + 1]],
                                         axis=0).astype(BF16)

        def per_group(g, carry):
            scored = []
            for u in range(NA_GROUP):
                r = g * NA_GROUP + u
                start = jnp.clip(r - WIN_H // 2, 0, grid_h - WIN_H)
                rows = _rows(r * GRID_W, GRID_W, GRID_W)
                keys = _rows(start * GRID_W, band, GRID_W)
                q2 = _split_heads(q_ref[rows, ln])
                dr0 = (WIN_H - 1) - (r - start)
                bias = jnp.concatenate([bias_ref[_bias_tile_index(j, dr0 + 2 * i)]
                                        for i in range(WIN_H // 2)], axis=-1)
                scored.append((rows, keys, _dot_nt(q2, k_ref[keys, ln]) + bias, _dot(q2, kvc_ref[0])))
            weighted = []
            for rows, keys, s_loc, s_ctx in scored:
                mx = jnp.maximum(jnp.max(s_loc, axis=-1, keepdims=True),
                                 jnp.max(s_ctx, axis=-1, keepdims=True))
                p_loc = jnp.exp2(s_loc - mx)
                p_ctx = jnp.exp2(s_ctx - mx)
                den = (jnp.sum(p_loc, axis=-1, keepdims=True)
                       + jnp.sum(p_ctx, axis=-1, keepdims=True))
                weighted.append((rows, keys, p_loc.astype(BF16), p_ctx.astype(BF16), den))
            for rows, keys, p_loc, p_ctx, den in weighted:
                o = (_dot(p_loc, v_ref[keys, ln]) + _dot_nt(p_ctx, kvc_ref[1])) / den
                ab_ref[rows, out_ln] = (_merge_heads(o) * gb_ref[rows, ln]).astype(BF16)
            return carry
        lax.fori_loop(0, grid_h // NA_GROUP, per_group, 0)
        return carry_j
    lax.fori_loop(0, N_HEADS // 2, per_pair, 0)


def _sample_body(x_ref, m_ref, ng_ref, fg_ref, wie_ref, wp_ref, ps_ref, woe_hbm, wio_hbm, cc_ref,
                 cdw_ref, cdb_ref, lng_ref, lnb_ref, woo_hbm, ck_ref, cv_ref, rpb_ref,
                 y_ref,
                 h_ref, pad_a, pad_b, ga_ref, gb_ref, bc_ref, q_ref, k_ref, v_ref, ab_ref,
                 e_ref, bias_ref, kvc_ref, woe_ref, wio_ref, woo_ref, sem_w, *, t):
    _zero_pads(pad_a, 1, t)
    _zero_pads(pad_b, 1, t)
    late = ((woe_hbm, woe_ref), (wio_hbm, wio_ref), (woo_hbm, woo_ref))

    def late_copy(i):
        return pltpu.make_async_copy(late[i][0], late[i][1], sem_w.at[i])

    first_step = pl.program_id(0) == 0

    @pl.when(first_step)
    def _():
        for i in range(len(late)):
            late_copy(i).start()
        _rpb_rows(rpb_ref, e_ref)
        _bias_tables(e_ref, bias_ref)

    cond = pl.program_id(0) + 1
    m_even = _cond_row(m_ref, 0, cond)
    _even_in_proj(x_ref, m_even, ng_ref[0:1, :], wie_ref, h_ref, pad_a, ga_ref, gb_ref,
                  q_ref, k_ref, v_ref, None, 1, t)
    _pool_phase(pad_a, ga_ref, wp_ref, ps_ref, ab_ref, 1, t)
    _neighbourhood_attention(q_ref, k_ref, v_ref, ck_ref, cv_ref, bias_ref, kvc_ref, gb_ref, ab_ref, t)
    pl.when(first_step)(lambda: late_copy(0).wait())
    _even_out_proj(x_ref, y_ref, m_even, woe_ref, ab_ref, 1, t)

    @pl.when(first_step)
    def _():
        late_copy(1).wait()
        late_copy(2).wait()

    _odd_layer(y_ref, _cond_row(m_ref, 1, cond), ng_ref[1:2, :], fg_ref[...], wio_ref, cc_ref, cdw_ref,
               cdb_ref, lng_ref, lnb_ref, woo_ref, h_ref, pad_a, pad_b, bc_ref, ga_ref, gb_ref, ab_ref,
               1, t)


def _const_spec(shape):
    zeros = (0,) * len(shape)
    return pl.BlockSpec(shape, lambda i: zeros, pipeline_mode=pl.Buffered(1))


def _stream_scratch(nb, t, kv_transposed):
    r = nb * t
    padded = nb * (t + 2 * PAD)
    kv = (W_HALF, r) if kv_transposed else (r, W_HALF)
    return [
        pltpu.VMEM((r, D_MODEL), BF16),
        pltpu.VMEM((padded, W_HALF), F32),
        pltpu.VMEM((padded, W_HALF), F32),
        pltpu.VMEM((r, W_HALF), F32),
        pltpu.VMEM((r, W_HALF), F32),
        pltpu.VMEM((r, W_HALF), F32),
        pltpu.VMEM((r, W_HALF), BF16),
        pltpu.VMEM(kv, BF16),
        pltpu.VMEM(kv, BF16),
        pltpu.VMEM((r, D_MODEL), BF16),
    ]


def _small_params(norm_g, final_g, w_pool, pool_scale, conv_c, conv_d, conv_d_b, ln_g, ln_b):
    return [norm_g, final_g.reshape(1, D_MODEL), w_pool, pool_scale, jnp.swapaxes(conv_c, 0, 1),
            jnp.swapaxes(conv_d, 0, 1), conv_d_b, ln_g, ln_b]


def kernel(x_prompt, x_sample, cache_k, cache_v, c, c_ctx, norm_g, w_mod, b_mod, w_in_even, w_pool,
           pool_scale, rpb, w_out_even, w_in_odd, conv_c, conv_d, conv_d_b, ln_g, ln_b, w_out_odd,
           final_g):
    batch, seq, d = x_prompt.shape
    dec_batch, dec_seq, _ = x_sample.shape
    assert d == D_MODEL and w_mod.shape[0] == 2 and w_in_even.shape[0] == 1 and w_in_odd.shape[0] == 1
    assert (NB_PROMPT * seq) % ROW_CHUNK == 0 and ROW_CHUNK % seq == 0 and seq % Q_ROWS == 0
    assert dec_seq % ROW_CHUNK == 0 and dec_seq // GRID_W >= WIN_H
    assert seq % POOL_ROWS == 0 and seq % CONV_ROWS == 0
    assert dec_seq % POOL_ROWS == 0 and dec_seq % CONV_ROWS == 0
    assert (dec_seq // GRID_W) % NA_GROUP == 0

    cond_rows = SUBLANES * ((1 + dec_batch + SUBLANES - 1) // SUBLANES)
    m = _modulation(c_ctx, c, w_mod, b_mod, cond_rows)
    m_spec = _const_spec(m.shape)

    small = _small_params(norm_g, final_g, w_pool, pool_scale, conv_c, conv_d, conv_d_b, ln_g, ln_b)
    small_specs = [_const_spec(a.shape) for a in small]
    w_f32 = (w_in_even, w_out_even, w_in_odd, w_out_odd)
    assert all(w.shape[0] == 1 and w.shape[1] % STAGE_ROWS == 0 for w in w_f32)
    any_spec = pl.BlockSpec(memory_space=pl.ANY)

    nb = NB_PROMPT
    assert batch % nb == 0
    kv_shape = jax.ShapeDtypeStruct((batch, 1, N_HEADS, HEAD_DIM, seq), F32)
    kv_spec = pl.BlockSpec((nb, 1, N_HEADS, HEAD_DIM, seq), lambda i: (i, 0, 0, 0, 0))
    y_prompt, new_kt, new_vt, wie, woe, wio, woo = pl.pallas_call(
        functools.partial(_prompt_body, nb=nb, t=seq),
        out_shape=(jax.ShapeDtypeStruct(x_prompt.shape, F32), kv_shape, kv_shape)
                  + tuple(jax.ShapeDtypeStruct(w.shape[1:], BF16) for w in w_f32),
        grid=(batch // nb,),
        in_specs=[pl.BlockSpec((nb, seq, d), lambda i: (i, 0, 0)), m_spec] + small_specs
                 + [any_spec] * len(w_f32),
        out_specs=(pl.BlockSpec((nb, seq, d), lambda i: (i, 0, 0)), kv_spec, kv_spec)
                  + (any_spec,) * len(w_f32),
        scratch_shapes=_stream_scratch(nb, seq, True)
                       + [pltpu.VMEM(w.shape[1:], BF16) for w in w_f32] + [
            pltpu.VMEM((W_HALF, d), BF16),
            pltpu.VMEM((W_HALF, d), BF16),
            pltpu.VMEM((STAGE_SLOTS, STAGE_ROWS, max(w.shape[2] for w in w_f32)), F32),
            pltpu.SemaphoreType.DMA((STAGE_SLOTS,)),
            pltpu.SemaphoreType.DMA((len(w_f32),)),
        ],
        compiler_params=pltpu.CompilerParams(dimension_semantics=("arbitrary",),
                                             vmem_limit_bytes=VMEM_LIMIT),
        name="prompt",
    )(x_prompt, m, *small, *w_f32)
    ng, fg, wp, ps, cc, cdw, cdb, lng, lnb = small
    w_args = [ng, fg, wie, wp, ps, woe, wio, cc, cdw, cdb, lng, lnb, woo]
    late_w = (woe, wio, woo)
    w_specs = [any_spec if any(a is w for w in late_w) else _const_spec(a.shape) for a in w_args]

    past = cache_k.shape[3]
    cache_spec = pl.BlockSpec((1, 1, N_HEADS, HEAD_DIM, past), lambda i: (i, 0, 0, 0, 0))
    rpb_t = jnp.swapaxes(rpb[0], 0, 1)
    y_sample = pl.pallas_call(
        functools.partial(_sample_body, t=dec_seq),
        out_shape=jax.ShapeDtypeStruct(x_sample.shape, F32),
        grid=(dec_batch,),
        in_specs=[pl.BlockSpec((1, dec_seq, d), lambda i: (i, 0, 0), pipeline_mode=pl.Buffered(1)),
                  m_spec] + w_specs
                 + [cache_spec, cache_spec, _const_spec(rpb_t.shape)],
        out_specs=pl.BlockSpec((1, dec_seq, d), lambda i: (i, 0, 0)),
        scratch_shapes=_stream_scratch(1, dec_seq, False) + [
            pltpu.VMEM((N_DR * N_HEADS, LANES), F32),
            pltpu.VMEM((N_HEADS * PAIR_TILES, 2 * GRID_W, LANES), F32),
            pltpu.VMEM((2, LANES, past), BF16),
        ] + [pltpu.VMEM(w.shape, BF16) for w in late_w] + [pltpu.SemaphoreType.DMA((len(late_w),))],
        compiler_params=pltpu.CompilerParams(dimension_semantics=("arbitrary",),
                                             vmem_limit_bytes=VMEM_LIMIT),
        name="sample",
    )(x_sample, m, *w_args, jnp.swapaxes(cache_k, 3, 4), jnp.swapaxes(cache_v, 3, 4), rpb_t)

    return (y_prompt, y_sample, jnp.swapaxes(new_kt, 3, 4), jnp.swapaxes(new_vt, 3, 4))
```

```python
import functools

import jax
import jax.numpy as jnp
from jax import lax
from jax.experimental import pallas as pl
from jax.experimental.pallas import tpu as pltpu

F32 = jnp.float32
BF16 = jnp.bfloat16

D_MODEL = 1024
W_HALF = 512
N_POOL_GROUPS = 4
POOL_HALF = (1, 2, 4, 8)
N_HEADS = 8
HEAD_DIM = 64
GRID_W = 64
WIN_H = 8
WIN_W = 16
CONV_C = 3
CONV_D = 31
EPS = 1e-6
MASKED = -1e30
LOG2_E = 1.4426950408889634
Q_SCALE = HEAD_DIM ** -0.5 * LOG2_E

LANES = 128
SUBLANES = 8
PAD = 16
ROW_CHUNK = 512
NORM_ROWS = 32
POOL_ROWS = 256
CONV_ROWS = 128
Q_ROWS = 128
NB_PROMPT = 2
NA_GROUP = 8
MOD_ROWS = 512
STAGE_ROWS = 128
STAGE_SLOTS = 4
VMEM_LIMIT = 58 * 1024 * 1024

assert PAD >= CONV_D // 2 + 1 and PAD % SUBLANES == 0 and PAD >= 2 * SUBLANES
assert max(POOL_HALF) <= SUBLANES


def _sigmoid(x):
    return 1.0 / (1.0 + jnp.exp(-x))


def _silu(x):
    return x * _sigmoid(x)


def _dot(a, b):
    return jnp.dot(a, b, preferred_element_type=F32)


def _dot_nt(a, b):
    return lax.dot_general(a, b, (((1,), (1,)), ((), ())), preferred_element_type=F32)


def _lanes(j):
    return slice(j * LANES, (j + 1) * LANES)


def _group(g):
    return slice(g * W_HALF, (g + 1) * W_HALF)


def _rows(start, size, align):
    if isinstance(start, int):
        return slice(start, start + size)
    return pl.ds(pl.multiple_of(start, align), size)


def _mod_body(cctx_ref, c_ref, w_ref, b_ref, o_ref, act_ref):
    layer, kb = pl.program_id(0), pl.program_id(1)
    rows, d = act_ref.shape

    @pl.when(kb == 0)
    def _():
        r = lax.broadcasted_iota(jnp.int32, (rows, d), 0)
        cond = jnp.where(r == 0, cctx_ref[...], 0.0)
        for i in range(c_ref.shape[0]):
            cond = jnp.where(r == i + 1, c_ref[i:i + 1, :], cond)
        act_ref[...] = _silu(cond).astype(BF16)
        o_ref[0] = jnp.broadcast_to(jnp.where(layer == 0, b_ref[0:1, :], b_ref[1:2, :]), o_ref.shape[1:])

    act = act_ref[:, pl.ds(pl.multiple_of(kb * MOD_ROWS, MOD_ROWS), MOD_ROWS)]
    o_ref[0] += _dot(act, w_ref[0].astype(BF16))


def _modulation(c_ctx, c, w_mod, b_mod, rows):
    depth, d, n = w_mod.shape
    assert depth == 2 and 1 + c.shape[0] <= rows and d % MOD_ROWS == 0
    return pl.pallas_call(
        _mod_body,
        out_shape=jax.ShapeDtypeStruct((depth, rows, n), F32),
        grid=(depth, d // MOD_ROWS),
        in_specs=[
            pl.BlockSpec((1, d), lambda l, k: (0, 0)),
            pl.BlockSpec(c.shape, lambda l, k: (0, 0)),
            pl.BlockSpec((1, MOD_ROWS, n), lambda l, k: (l, k, 0)),
            pl.BlockSpec((depth, n), lambda l, k: (0, 0)),
        ],
        out_specs=pl.BlockSpec((1, rows, n), lambda l, k: (l, 0, 0)),
        scratch_shapes=[pltpu.VMEM((rows, d), BF16)],
        compiler_params=pltpu.CompilerParams(dimension_semantics=("arbitrary", "arbitrary")),
        name="mod",
    )(c_ctx.reshape(1, d), c, w_mod, b_mod)


def _cond_row(m_ref, layer, row):
    if isinstance(row, int):
        return m_ref[layer, row:row + 1, :]
    m = m_ref[layer]
    keep = lax.broadcasted_iota(jnp.int32, m.shape, 0) == row
    return jnp.sum(jnp.where(keep, m, 0.0), axis=0, keepdims=True)


def _pieces(c, nb, t):
    if t >= ROW_CHUNK:
        per_seq = t // ROW_CHUNK
        s = 0 if nb == 1 else c // per_seq
        return [(s, (c - s * per_seq) * ROW_CHUNK, ROW_CHUNK, 0)]
    per_chunk = ROW_CHUNK // t
    return [(c * per_chunk + i, 0, t, i * t) for i in range(per_chunk)]


def _for_chunks(n, body):
    for c in range(n):
        body(c)


def _pad_row(s, off, t):
    return s * (t + 2 * PAD) + PAD + off


def _store_padded(pad_ref, val, pieces, t):
    for s, off, n, o in pieces:
        pad_ref[_rows(_pad_row(s, off, t), n, SUBLANES), :] = val[o:o + n]


def _scale_padded(pad_ref, val, pieces, t):
    for s, off, n, o in pieces:
        rows = _rows(_pad_row(s, off, t), n, SUBLANES)
        pad_ref[rows, :] = pad_ref[rows, :] * val[o:o + n]


def _modnorm_chunk(src_ref, h_ref, c, nb, t, gain, shift):
    for s, off, n, o in _pieces(c, nb, t):
        for i in range(0, n, NORM_ROWS):
            x = src_ref[s, _rows(off + i, NORM_ROWS, NORM_ROWS), :]
            ms = jnp.mean(x * x, axis=-1, keepdims=True)
            h_ref[_rows(c * ROW_CHUNK + o + i, NORM_ROWS, NORM_ROWS), :] = (
                x * lax.rsqrt(ms + EPS) * gain + shift).astype(BF16)


def _zero_pads(pad_ref, nb, t):
    z = jnp.zeros((PAD, W_HALF), F32)
    for s in range(nb):
        pad_ref[_pad_row(s, 0, t) - PAD:_pad_row(s, 0, t), :] = z
        pad_ref[_pad_row(s, t, t):_pad_row(s, t, t) + PAD, :] = z


def _pool_phase(pad_ref, ga_ref, wp_ref, ps_ref, ab_ref, nb, t):
    n_rows = POOL_ROWS
    per_seq = t // n_rows

    def step(i, carry):
        s = i // per_seq
        r0 = (i - s * per_seq) * n_rows
        prow = _pad_row(s, r0, t)
        rows = _rows(i * n_rows, n_rows, n_rows)
        pos = r0 + lax.broadcasted_iota(jnp.int32, (n_rows, LANES), 0)
        before = jnp.minimum(pos, SUBLANES)
        after = jnp.minimum(t - pos, SUBLANES)
        for g in range(N_POOL_GROUPS):
            hw = POOL_HALF[g]
            ln = _lanes(g)
            halo = n_rows + 2 * SUBLANES
            blk = pad_ref[_rows(prow - SUBLANES, halo, SUBLANES), ln]
            run, n = blk, 1
            while n < 2 * hw:
                run = run + pltpu.roll(run, halo - n, 0)
                n *= 2
            if hw < SUBLANES:
                run = pltpu.roll(run, halo - (SUBLANES - hw), 0)
            win = run[:n_rows]
            cnt = (jnp.minimum(before, hw) + jnp.minimum(after, hw)).astype(F32)
            p = (win / cnt - blk[SUBLANES:SUBLANES + n_rows]).astype(BF16)
            y = _dot(p, wp_ref[0, g].astype(BF16)) * ps_ref[:, ln] * ga_ref[rows, ln]
            ab_ref[rows, ln] = y.astype(BF16)
        return carry
    lax.fori_loop(0, nb * per_seq, step, 0)


def _out_proj_chunk(ab_ref, w_ref, x_ref, gate, dst_ref, c, nb, t):
    lhs = ab_ref[_rows(c * ROW_CHUNK, ROW_CHUNK, ROW_CHUNK), :]
    for g in range(D_MODEL // W_HALF):
        y = _dot(lhs, w_ref[:, _group(g)])
        for s, off, n, o in _pieces(c, nb, t):
            rows = _rows(off, n, n)
            dst_ref[s, rows, _group(g)] = x_ref[s, rows, _group(g)] + gate[:, _group(g)] * y[o:o + n]


def _shift_up(x, o, n):
    if o % SUBLANES == 0:
        return x[o:o + n]
    return pltpu.roll(x, x.shape[0] - o, 0)[:n]


def _conv_phase(pad_c, pad_d, bc_ref, ga_ref, gb_ref, cc_ref, cdw_ref, cdb_ref, lng_ref, lnb_ref,
                ab_ref, nb, t):
    n_rows = CONV_ROWS
    per_seq = t // n_rows

    def step(i, carry):
        s = i // per_seq
        r0 = (i - s * per_seq) * n_rows
        prow = _pad_row(s, r0, t)
        rows = _rows(i * n_rows, n_rows, n_rows)
        z = []
        for g in range(W_HALF // LANES):
            ln = _lanes(g)
            blk = pad_c[_rows(prow - SUBLANES, n_rows + 2 * SUBLANES, SUBLANES), ln]
            c3 = None
            for j in range(CONV_C):
                o = SUBLANES + j - CONV_C // 2
                term = _shift_up(blk, o, n_rows) * cc_ref[j, :, ln]
                c3 = term if c3 is None else c3 + term
            ab_ref[rows, ln] = (bc_ref[rows, ln] * c3 * ga_ref[rows, ln]).astype(BF16)
            acc = None
            for sft in range(SUBLANES):
                part = None
                for a in range((CONV_D - sft + SUBLANES - 1) // SUBLANES):
                    j = SUBLANES * a + sft
                    src = pad_d[_rows(prow - 2 * SUBLANES + SUBLANES * a, n_rows + SUBLANES,
                                      SUBLANES), ln]
                    term = src * cdw_ref[j, :, ln]
                    part = term if part is None else part + term
                o = SUBLANES + sft - (CONV_D // 2 - SUBLANES)
                part = _shift_up(part, o, n_rows)
                acc = part if acc is None else acc + part
            z.append(acc + cdb_ref[:, ln])
        z = jnp.concatenate(z, axis=-1)
        mu = jnp.mean(z, axis=-1, keepdims=True)
        zc = z - mu
        var = jnp.mean(zc * zc, axis=-1, keepdims=True)
        zn = zc * lax.rsqrt(var + EPS) * lng_ref[...] + lnb_ref[...]
        ab_ref[rows, W_HALF:] = (_silu(zn) * gb_ref[rows, :]).astype(BF16)
        return carry
    lax.fori_loop(0, nb * per_seq, step, 0)


def _final_norm_chunk(y_ref, fg, c, nb, t):
    for s, off, n, _ in _pieces(c, nb, t):
        for i in range(0, n, NORM_ROWS):
            rows = _rows(off + i, NORM_ROWS, NORM_ROWS)
            x = y_ref[s, rows, :]
            ms = jnp.mean(x * x, axis=-1, keepdims=True)
            y_ref[s, rows, :] = x * lax.rsqrt(ms + EPS) * fg


def _odd_layer(y_ref, m_row, g_row, fg, wio_ref, cc_ref, cdw_ref, cdb_ref, lng_ref, lnb_ref, woo_ref,
               h_ref, pad_c, pad_d, bc_ref, ga_ref, gb_ref, ab_ref, nb, t):
    shift = m_row[:, :D_MODEL]
    gain = g_row * (1.0 + m_row[:, D_MODEL:2 * D_MODEL])
    gate = m_row[:, 2 * D_MODEL:]
    n_chunks = nb * t // ROW_CHUNK

    def in_proj(c):
        _modnorm_chunk(y_ref, h_ref, c, nb, t, gain, shift)
        rows = _rows(c * ROW_CHUNK, ROW_CHUNK, ROW_CHUNK)
        pieces = _pieces(c, nb, t)
        h = h_ref[rows, :]
        bc_ref[rows, :] = _dot(h, wio_ref[:, _group(0)])
        _store_padded(pad_c, _dot(h, wio_ref[:, _group(1)]), pieces, t)
        _scale_padded(pad_c, _dot(h, wio_ref[:, _group(2)]), pieces, t)
        ga_ref[rows, :] = _silu(_dot(h, wio_ref[:, _group(3)]))
        _store_padded(pad_d, _dot(h, wio_ref[:, _group(4)]), pieces, t)
        _scale_padded(pad_d, _sigmoid(_dot(h, wio_ref[:, _group(5)])), pieces, t)
        gb_ref[rows, :] = _silu(_dot(h, wio_ref[:, _group(6)]))
    _for_chunks(n_chunks, in_proj)

    _conv_phase(pad_c, pad_d, bc_ref, ga_ref, gb_ref, cc_ref, cdw_ref, cdb_ref, lng_ref, lnb_ref,
                ab_ref, nb, t)

    def out_proj(c):
        _out_proj_chunk(ab_ref, woo_ref, y_ref, gate, y_ref, c, nb, t)
        _final_norm_chunk(y_ref, fg, c, nb, t)
    _for_chunks(n_chunks, out_proj)


def _even_in_proj(x_ref, m_row, g_row, w_ref, h_ref, pad_a, ga_ref, gb_ref, q_ref, k_ref, v_ref,
                  kv_t, nb, t):
    shift = m_row[:, :D_MODEL]
    gain = g_row * (1.0 + m_row[:, D_MODEL:2 * D_MODEL])

    def in_proj(c):
        _modnorm_chunk(x_ref, h_ref, c, nb, t, gain, shift)
        rows = _rows(c * ROW_CHUNK, ROW_CHUNK, ROW_CHUNK)
        pieces = _pieces(c, nb, t)
        h = h_ref[rows, :]
        _store_padded(pad_a, _dot(h, w_ref[:, _group(0)]), pieces, t)
        ga_ref[rows, :] = _silu(_dot(h, w_ref[:, _group(1)]))
        q_ref[rows, :] = (_dot(h, w_ref[:, _group(2)]) * Q_SCALE).astype(BF16)
        for i, (dst, g) in enumerate(((k_ref, 3), (v_ref, 4))):
            if kv_t is None:
                dst[rows, :] = _dot(h, w_ref[:, _group(g)]).astype(BF16)
                continue
            acc = _dot_nt(kv_t[2 + i][...], h)
            dst[:, rows] = acc.astype(BF16)
            for s, off, n, o in pieces:
                for hd in range(N_HEADS):
                    kv_t[i][s, 0, hd, :, _rows(off, n, n)] = (
                        acc[hd * HEAD_DIM:(hd + 1) * HEAD_DIM, o:o + n])
        gb_ref[rows, :] = _silu(_dot(h, w_ref[:, _group(5)]))
    _for_chunks(nb * t // ROW_CHUNK, in_proj)


def _even_out_proj(x_ref, y_ref, m_row, w_ref, ab_ref, nb, t):
    gate = m_row[:, 2 * D_MODEL:]
    n_chunks = nb * t // ROW_CHUNK

    def out_proj(c, carry=0):
        _out_proj_chunk(ab_ref, w_ref, x_ref, gate, y_ref, c, nb, t)
        return carry
    if n_chunks == 1:
        out_proj(0)
    else:
        lax.fori_loop(0, n_chunks, out_proj, 0)


def _split_heads(x):
    lane = lax.broadcasted_iota(jnp.int32, (1, LANES), 1)
    first = jnp.where(lane < HEAD_DIM, 1.0, 0.0).astype(x.dtype)
    return jnp.concatenate([x * first, x * (1 - first)], axis=0)


def _merge_heads(o):
    n = o.shape[0] // 2
    lane = lax.broadcasted_iota(jnp.int32, (n, LANES), 1)
    return jnp.where(lane < HEAD_DIM, o[:n], o[n:])


def _context_attention(q_ref, kt_ref, vt_ref, gb_ref, ab_ref, nb, t):
    for s in range(nb):
        seq = slice(s * t, (s + 1) * t)
        for j in range(N_HEADS // 2):
            ln = _lanes(j)
            kp = kt_ref[ln, seq]
            vp = vt_ref[ln, seq]
            for r0 in range(0, t, Q_ROWS):
                rows = slice(s * t + r0, s * t + r0 + Q_ROWS)
                sc = _dot(_split_heads(q_ref[rows, ln]), kp)
                p = jnp.exp2(sc - jnp.max(sc, axis=-1, keepdims=True))
                o = _dot_nt(p.astype(BF16), vp) / jnp.sum(p, axis=-1, keepdims=True)
                ab_ref[rows, W_HALF + j * LANES:W_HALF + (j + 1) * LANES] = (
                    _merge_heads(o) * gb_ref[rows, ln]).astype(BF16)


def _stage_weights(w_hbm, w_bf, wkt_ref, wvt_ref, stage, sem_in):
    chunks = [(k, r0) for k in range(len(w_hbm)) for r0 in range(0, w_hbm[k].shape[1], STAGE_ROWS)]

    def fetch(i):
        k, r0 = chunks[i]
        cols = w_hbm[k].shape[2]
        slot = i % STAGE_SLOTS
        return pltpu.make_async_copy(w_hbm[k].at[0, pl.ds(r0, STAGE_ROWS), :],
                                     stage.at[slot, :, pl.ds(0, cols)], sem_in.at[slot])

    for i in range(min(STAGE_SLOTS - 1, len(chunks))):
        fetch(i).start(priority=i % 2)
    for i, (k, r0) in enumerate(chunks):
        if i + STAGE_SLOTS - 1 < len(chunks):
            fetch(i + STAGE_SLOTS - 1).start(priority=(i + STAGE_SLOTS - 1) % 2)
        fetch(i).wait()
        cols = w_hbm[k].shape[2]
        rows = slice(r0, r0 + STAGE_ROWS)
        slot = i % STAGE_SLOTS
        w_bf[k][rows, :] = stage[slot, :, 0:cols].astype(BF16)
        if k == 0:
            wkt_ref[:, rows] = stage[slot, :, _group(3)].T.astype(BF16)
            wvt_ref[:, rows] = stage[slot, :, _group(4)].T.astype(BF16)


def _prompt_body(x_ref, m_ref, ng_ref, fg_ref, wp_ref, ps_ref, cc_ref, cdw_ref, cdb_ref, lng_ref,
                 lnb_ref, wie_hbm, woe_hbm, wio_hbm, woo_hbm,
                 y_ref, ko_ref, vo_ref, wie_out, woe_out, wio_out, woo_out,
                 h_ref, pad_a, pad_b, ga_ref, gb_ref, bc_ref, q_ref, kt_ref, vt_ref, ab_ref,
                 wie_ref, woe_ref, wio_ref, woo_ref, wkt_ref, wvt_ref, stage, sem_in, sem_out,
                 *, nb, t):
    w_out = (wie_out, woe_out, wio_out, woo_out)
    w_bf = (wie_ref, woe_ref, wio_ref, woo_ref)

    def write_back(k):
        return pltpu.make_async_copy(w_bf[k], w_out[k], sem_out.at[k])

    @pl.when(pl.program_id(0) == 0)
    def _():
        _stage_weights((wie_hbm, woe_hbm, wio_hbm, woo_hbm), w_bf, wkt_ref, wvt_ref, stage, sem_in)
        for k in range(len(w_bf)):
            write_back(k).start()

    _zero_pads(pad_a, nb, t)
    _zero_pads(pad_b, nb, t)
    m_even = _cond_row(m_ref, 0, 0)
    _even_in_proj(x_ref, m_even, ng_ref[0:1, :], wie_ref, h_ref, pad_a, ga_ref, gb_ref,
                  q_ref, kt_ref, vt_ref, (ko_ref, vo_ref, wkt_ref, wvt_ref), nb, t)
    _pool_phase(pad_a, ga_ref, wp_ref, ps_ref, ab_ref, nb, t)
    _context_attention(q_ref, kt_ref, vt_ref, gb_ref, ab_ref, nb, t)
    _even_out_proj(x_ref, y_ref, m_even, woe_ref, ab_ref, nb, t)
    _odd_layer(y_ref, _cond_row(m_ref, 1, 0), ng_ref[1:2, :], fg_ref[...], wio_ref, cc_ref, cdw_ref,
               cdb_ref, lng_ref, lnb_ref, woo_ref, h_ref, pad_a, pad_b, bc_ref, ga_ref, gb_ref, ab_ref,
               nb, t)

    @pl.when(pl.program_id(0) == 0)
    def _():
        for k in range(len(w_bf)):
            write_back(k).wait()


def _rpb_rows(rpb_ref, e_ref):
    n = rpb_ref.shape[0] * rpb_ref.shape[1]
    lane = lax.broadcasted_iota(jnp.int32, (n, LANES), 1)
    i = jnp.where(lane < GRID_W, lane, lane - LANES)
    idx = jnp.clip(i, -(WIN_W - 1), WIN_W - 1) + (WIN_W - 1)
    rp = rpb_ref[...].reshape(n, rpb_ref.shape[2])
    e = jnp.zeros((n, LANES), F32)
    for d in range(2 * WIN_W - 1):
        e = jnp.where(idx == d, rp[:, d:d + 1], e)
    e_ref[...] = e


N_DR = 2 * WIN_H - 1
PAIR_TILES = N_DR // 2


def _bias_tile_index(j, dr_lo):
    if isinstance(dr_lo, int):
        parity, half = dr_lo % 2, dr_lo // 2
    else:
        parity, half = dr_lo & 1, lax.shift_right_logical(dr_lo, 1)
    return (2 * j + parity) * PAIR_TILES + half


def _bias_tables(e_ref, bias_ref):
    q = lax.broadcasted_iota(jnp.int32, (GRID_W, LANES), 0)
    lane = lax.broadcasted_iota(jnp.int32, (GRID_W, LANES), 1)
    kw = jnp.where(lane < GRID_W, lane, lane - GRID_W)
    start = jnp.clip(q - WIN_W // 2, 0, GRID_W - WIN_W)
    col_ok = (kw >= start) & (kw < start + WIN_W)
    for j in range(N_HEADS // 2):
        for dr in range(N_DR - 1):
            for e in range(2):
                r_lo = dr * N_HEADS + 2 * j + e
                r_hi = r_lo + N_HEADS
                lo = jnp.broadcast_to(e_ref[r_lo:r_lo + 1, :], (GRID_W, LANES))
                hi = jnp.broadcast_to(e_ref[r_hi:r_hi + 1, :], (GRID_W, LANES))
                lo = pltpu.roll(lo, 0, 1, stride=1, stride_axis=0)
                hi = pltpu.roll(hi, GRID_W, 1, stride=1, stride_axis=0)
                tile = jnp.where(lane < GRID_W, lo, hi)
                bias_ref[_bias_tile_index(j, dr), e * GRID_W:(e + 1) * GRID_W, :] = jnp.where(
                    col_ok, tile * LOG2_E, MASKED)


def _neighbourhood_attention(q_ref, k_ref, v_ref, ck_ref, cv_ref, bias_ref, kvc_ref, gb_ref, ab_ref, t):
    grid_h = t // GRID_W
    band = WIN_H * GRID_W
    def per_pair(j, carry_j):
        ln = pl.ds(pl.multiple_of(j * LANES, LANES), LANES)
        out_ln = pl.ds(pl.multiple_of(W_HALF + j * LANES, LANES), LANES)
        for i, src in enumerate((ck_ref, cv_ref)):
            kvc_ref[i] = jnp.concatenate([src[0, 0, 2 * j], src[0, 0, 2 * j + 1]],
                                         axis=0).astype(BF16)

        def per_group(g, carry):
            scored = []
            for u in range(NA_GROUP):
                r = g * NA_GROUP + u
                start = jnp.clip(r - WIN_H // 2, 0, grid_h - WIN_H)
                rows = _rows(r * GRID_W, GRID_W, GRID_W)
                keys = _rows(start * GRID_W, band, GRID_W)
                q2 = _split_heads(q_ref[rows, ln])
                dr0 = (WIN_H - 1) - (r - start)
                bias = jnp.concatenate([bias_ref[_bias_tile_index(j, dr0 + 2 * i)]
                                        for i in range(WIN_H // 2)], axis=-1)
                scored.append((rows, keys, _dot_nt(q2, k_ref[keys, ln]) + bias, _dot(q2, kvc_ref[0])))
            weighted = []
            for rows, keys, s_loc, s_ctx in scored:
                mx = jnp.maximum(jnp.max(s_loc, axis=-1, keepdims=True),
                                 jnp.max(s_ctx, axis=-1, keepdims=True))
                p_loc = jnp.exp2(s_loc - mx)
                p_ctx = jnp.exp2(s_ctx - mx)
                den = (jnp.sum(p_loc, axis=-1, keepdims=True)
                       + jnp.sum(p_ctx, axis=-1, keepdims=True))
                weighted.append((rows, keys, p_loc.astype(BF16), p_ctx.astype(BF16), den))
            for rows, keys, p_loc, p_ctx, den in weighted:
                o = (_dot(p_loc, v_ref[keys, ln]) + _dot_nt(p_ctx, kvc_ref[1])) / den
                ab_ref[rows, out_ln] = (_merge_heads(o) * gb_ref[rows, ln]).astype(BF16)
            return carry
        lax.fori_loop(0, grid_h // NA_GROUP, per_group, 0)
        return carry_j
    lax.fori_loop(0, N_HEADS // 2, per_pair, 0)


def _sample_body(x_ref, m_ref, ng_ref, fg_ref, wie_ref, wp_ref, ps_ref, woe_hbm, wio_hbm, cc_ref,
                 cdw_ref, cdb_ref, lng_ref, lnb_ref, woo_hbm, ck_ref, cv_ref, rpb_ref,
                 y_ref,
                 h_ref, pad_a, pad_b, ga_ref, gb_ref, bc_ref, q_ref, k_ref, v_ref, ab_ref,
                 e_ref, bias_ref, kvc_ref, woe_ref, wio_ref, woo_ref, sem_w, *, t):
    _zero_pads(pad_a, 1, t)
    _zero_pads(pad_b, 1, t)
    late = ((woe_hbm, woe_ref), (wio_hbm, wio_ref), (woo_hbm, woo_ref))

    def late_copy(i):
        return pltpu.make_async_copy(late[i][0], late[i][1], sem_w.at[i])

    first_step = pl.program_id(0) == 0

    @pl.when(first_step)
    def _():
        for i in range(len(late)):
            late_copy(i).start()
        _rpb_rows(rpb_ref, e_ref)
        _bias_tables(e_ref, bias_ref)

    cond = pl.program_id(0) + 1
    m_even = _cond_row(m_ref, 0, cond)
    _even_in_proj(x_ref, m_even, ng_ref[0:1, :], wie_ref, h_ref, pad_a, ga_ref, gb_ref,
                  q_ref, k_ref, v_ref, None, 1, t)
    _pool_phase(pad_a, ga_ref, wp_ref, ps_ref, ab_ref, 1, t)
    _neighbourhood_attention(q_ref, k_ref, v_ref, ck_ref, cv_ref, bias_ref, kvc_ref, gb_ref, ab_ref, t)
    pl.when(first_step)(lambda: late_copy(0).wait())
    _even_out_proj(x_ref, y_ref, m_even, woe_ref, ab_ref, 1, t)

    @pl.when(first_step)
    def _():
        late_copy(1).wait()
        late_copy(2).wait()

    _odd_layer(y_ref, _cond_row(m_ref, 1, cond), ng_ref[1:2, :], fg_ref[...], wio_ref, cc_ref, cdw_ref,
               cdb_ref, lng_ref, lnb_ref, woo_ref, h_ref, pad_a, pad_b, bc_ref, ga_ref, gb_ref, ab_ref,
               1, t)


def _const_spec(shape):
    zeros = (0,) * len(shape)
    return pl.BlockSpec(shape, lambda i: zeros, pipeline_mode=pl.Buffered(1))


def _stream_scratch(nb, t, kv_transposed):
    r = nb * t
    padded = nb * (t + 2 * PAD)
    kv = (W_HALF, r) if kv_transposed else (r, W_HALF)
    return [
        pltpu.VMEM((r, D_MODEL), BF16),
        pltpu.VMEM((padded, W_HALF), F32),
        pltpu.VMEM((padded, W_HALF), F32),
        pltpu.VMEM((r, W_HALF), F32),
        pltpu.VMEM((r, W_HALF), F32),
        pltpu.VMEM((r, W_HALF), F32),
        pltpu.VMEM((r, W_HALF), BF16),
        pltpu.VMEM(kv, BF16),
        pltpu.VMEM(kv, BF16),
        pltpu.VMEM((r, D_MODEL), BF16),
    ]


def _small_params(norm_g, final_g, w_pool, pool_scale, conv_c, conv_d, conv_d_b, ln_g, ln_b):
    return [norm_g, final_g.reshape(1, D_MODEL), w_pool, pool_scale, jnp.swapaxes(conv_c, 0, 1),
            jnp.swapaxes(conv_d, 0, 1), conv_d_b, ln_g, ln_b]


def kernel(x_prompt, x_sample, cache_k, cache_v, c, c_ctx, norm_g, w_mod, b_mod, w_in_even, w_pool,
           pool_scale, rpb, w_out_even, w_in_odd, conv_c, conv_d, conv_d_b, ln_g, ln_b, w_out_odd,
           final_g):
    batch, seq, d = x_prompt.shape
    dec_batch, dec_seq, _ = x_sample.shape
    assert d == D_MODEL and w_mod.shape[0] == 2 and w_in_even.shape[0] == 1 and w_in_odd.shape[0] == 1
    assert (NB_PROMPT * seq) % ROW_CHUNK == 0 and ROW_CHUNK % seq == 0 and seq % Q_ROWS == 0
    assert dec_seq % ROW_CHUNK == 0 and dec_seq // GRID_W >= WIN_H
    assert seq % POOL_ROWS == 0 and seq % CONV_ROWS == 0
    assert dec_seq % POOL_ROWS == 0 and dec_seq % CONV_ROWS == 0
    assert (dec_seq // GRID_W) % NA_GROUP == 0

    cond_rows = SUBLANES * ((1 + dec_batch + SUBLANES - 1) // SUBLANES)
    m = _modulation(c_ctx, c, w_mod, b_mod, cond_rows)
    m_spec = _const_spec(m.shape)

    small = _small_params(norm_g, final_g, w_pool, pool_scale, conv_c, conv_d, conv_d_b, ln_g, ln_b)
    small_specs = [_const_spec(a.shape) for a in small]
    w_f32 = (w_in_even, w_out_even, w_in_odd, w_out_odd)
    assert all(w.shape[0] == 1 and w.shape[1] % STAGE_ROWS == 0 for w in w_f32)
    any_spec = pl.BlockSpec(memory_space=pl.ANY)

    nb = NB_PROMPT
    assert batch % nb == 0
    kv_shape = jax.ShapeDtypeStruct((batch, 1, N_HEADS, HEAD_DIM, seq), F32)
    kv_spec = pl.BlockSpec((nb, 1, N_HEADS, HEAD_DIM, seq), lambda i: (i, 0, 0, 0, 0))
    y_prompt, new_kt, new_vt, wie, woe, wio, woo = pl.pallas_call(
        functools.partial(_prompt_body, nb=nb, t=seq),
        out_shape=(jax.ShapeDtypeStruct(x_prompt.shape, F32), kv_shape, kv_shape)
                  + tuple(jax.ShapeDtypeStruct(w.shape[1:], BF16) for w in w_f32),
        grid=(batch // nb,),
        in_specs=[pl.BlockSpec((nb, seq, d), lambda i: (i, 0, 0)), m_spec] + small_specs
                 + [any_spec] * len(w_f32),
        out_specs=(pl.BlockSpec((nb, seq, d), lambda i: (i, 0, 0)), kv_spec, kv_spec)
                  + (any_spec,) * len(w_f32),
        scratch_shapes=_stream_scratch(nb, seq, True)
                       + [pltpu.VMEM(w.shape[1:], BF16) for w in w_f32] + [
            pltpu.VMEM((W_HALF, d), BF16),
            pltpu.VMEM((W_HALF, d), BF16),
            pltpu.VMEM((STAGE_SLOTS, STAGE_ROWS, max(w.shape[2] for w in w_f32)), F32),
            pltpu.SemaphoreType.DMA((STAGE_SLOTS,)),
            pltpu.SemaphoreType.DMA((len(w_f32),)),
        ],
        compiler_params=pltpu.CompilerParams(dimension_semantics=("arbitrary",),
                                             vmem_limit_bytes=VMEM_LIMIT),
        name="prompt",
    )(x_prompt, m, *small, *w_f32)
    ng, fg, wp, ps, cc, cdw, cdb, lng, lnb = small
    w_args = [ng, fg, wie, wp, ps, woe, wio, cc, cdw, cdb, lng, lnb, woo]
    late_w = (woe, wio, woo)
    w_specs = [any_spec if any(a is w for w in late_w) else _const_spec(a.shape) for a in w_args]

    past = cache_k.shape[3]
    cache_spec = pl.BlockSpec((1, 1, N_HEADS, HEAD_DIM, past), lambda i: (i, 0, 0, 0, 0))
    rpb_t = jnp.swapaxes(rpb[0], 0, 1)
    y_sample = pl.pallas_call(
        functools.partial(_sample_body, t=dec_seq),
        out_shape=jax.ShapeDtypeStruct(x_sample.shape, F32),
        grid=(dec_batch,),
        in_specs=[pl.BlockSpec((1, dec_seq, d), lambda i: (i, 0, 0), pipeline_mode=pl.Buffered(1)),
                  m_spec] + w_specs
                 + [cache_spec, cache_spec, _const_spec(rpb_t.shape)],
        out_specs=pl.BlockSpec((1, dec_seq, d), lambda i: (i, 0, 0)),
        scratch_shapes=_stream_scratch(1, dec_seq, False) + [
            pltpu.VMEM((N_DR * N_HEADS, LANES), F32),
            pltpu.VMEM((N_HEADS * PAIR_TILES, 2 * GRID_W, LANES), F32),
            pltpu.VMEM((2, LANES, past), BF16),
        ] + [pltpu.VMEM(w.shape, BF16) for w in late_w] + [pltpu.SemaphoreType.DMA((len(late_w),))],
        compiler_params=pltpu.CompilerParams(dimension_semantics=("arbitrary",),
                                             vmem_limit_bytes=VMEM_LIMIT),
        name="sample",
    )(x_sample, m, *w_args, jnp.swapaxes(cache_k, 3, 4), jnp.swapaxes(cache_v, 3, 4), rpb_t)

    return (y_prompt, y_sample, jnp.swapaxes(new_kt, 3, 4), jnp.swapaxes(new_vt, 3, 4))
```

```python
import functools

import jax
import jax.numpy as jnp
from jax import lax
from jax.experimental import pallas as pl
from jax.experimental.pallas import tpu as pltpu

F32 = jnp.float32
BF16 = jnp.bfloat16

D_MODEL = 1024
W_HALF = 512
N_POOL_GROUPS = 4
POOL_HALF = (1, 2, 4, 8)
N_HEADS = 8
HEAD_DIM = 64
GRID_W = 64
WIN_H = 8
WIN_W = 16
CONV_C = 3
CONV_D = 31
EPS = 1e-6
MASKED = -1e30
LOG2_E = 1.4426950408889634
Q_SCALE = HEAD_DIM ** -0.5 * LOG2_E

LANES = 128
SUBLANES = 8
PAD = 16
ROW_CHUNK = 512
NORM_ROWS = 32
POOL_ROWS = 256
CONV_ROWS = 128
Q_ROWS = 128
NB_PROMPT = 2
NA_GROUP = 8
MOD_ROWS = 512
STAGE_ROWS = 128
STAGE_SLOTS = 4
VMEM_LIMIT = 58 * 1024 * 1024

assert PAD >= CONV_D // 2 + 1 and PAD % SUBLANES == 0 and PAD >= 2 * SUBLANES
assert max(POOL_HALF) <= SUBLANES


def _sigmoid(x):
    return 1.0 / (1.0 + jnp.exp(-x))


def _silu(x):
    return x * _sigmoid(x)


def _dot(a, b):
    return jnp.dot(a, b, preferred_element_type=F32)


def _dot_nt(a, b):
    return lax.dot_general(a, b, (((1,), (1,)), ((), ())), preferred_element_type=F32)


def _lanes(j):
    return slice(j * LANES, (j + 1) * LANES)


def _group(g):
    return slice(g * W_HALF, (g + 1) * W_HALF)


def _rows(start, size, align):
    if isinstance(start, int):
        return slice(start, start + size)
    return pl.ds(pl.multiple_of(start, align), size)


def _mod_body(cctx_ref, c_ref, w_ref, b_ref, o_ref, act_ref):
    layer, kb = pl.program_id(0), pl.program_id(1)
    rows, d = act_ref.shape

    @pl.when(kb == 0)
    def _():
        r = lax.broadcasted_iota(jnp.int32, (rows, d), 0)
        cond = jnp.where(r == 0, cctx_ref[...], 0.0)
        for i in range(c_ref.shape[0]):
            cond = jnp.where(r == i + 1, c_ref[i:i + 1, :], cond)
        act_ref[...] = _silu(cond).astype(BF16)
        o_ref[0] = jnp.broadcast_to(jnp.where(layer == 0, b_ref[0:1, :], b_ref[1:2, :]), o_ref.shape[1:])

    act = act_ref[:, pl.ds(pl.multiple_of(kb * MOD_ROWS, MOD_ROWS), MOD_ROWS)]
    o_ref[0] += _dot(act, w_ref[0].astype(BF16))


def _modulation(c_ctx, c, w_mod, b_mod, rows):
    depth, d, n = w_mod.shape
    assert depth == 2 and 1 + c.shape[0] <= rows and d % MOD_ROWS == 0
    return pl.pallas_call(
        _mod_body,
        out_shape=jax.ShapeDtypeStruct((depth, rows, n), F32),
        grid=(depth, d // MOD_ROWS),
        in_specs=[
            pl.BlockSpec((1, d), lambda l, k: (0, 0)),
            pl.BlockSpec(c.shape, lambda l, k: (0, 0)),
            pl.BlockSpec((1, MOD_ROWS, n), lambda l, k: (l, k, 0)),
            pl.BlockSpec((depth, n), lambda l, k: (0, 0)),
        ],
        out_specs=pl.BlockSpec((1, rows, n), lambda l, k: (l, 0, 0)),
        scratch_shapes=[pltpu.VMEM((rows, d), BF16)],
        compiler_params=pltpu.CompilerParams(dimension_semantics=("arbitrary", "arbitrary")),
        name="mod",
    )(c_ctx.reshape(1, d), c, w_mod, b_mod)


def _cond_row(m_ref, layer, row):
    if isinstance(row, int):
        return m_ref[layer, row:row + 1, :]
    m = m_ref[layer]
    keep = lax.broadcasted_iota(jnp.int32, m.shape, 0) == row
    return jnp.sum(jnp.where(keep, m, 0.0), axis=0, keepdims=True)


def _pieces(c, nb, t):
    if t >= ROW_CHUNK:
        per_seq = t // ROW_CHUNK
        s = 0 if nb == 1 else c // per_seq
        return [(s, (c - s * per_seq) * ROW_CHUNK, ROW_CHUNK, 0)]
    per_chunk = ROW_CHUNK // t
    return [(c * per_chunk + i, 0, t, i * t) for i in range(per_chunk)]


def _for_chunks(n, body):
    for c in range(n):
        body(c)


def _pad_row(s, off, t):
    return s * (t + 2 * PAD) + PAD + off


def _store_padded(pad_ref, val, pieces, t):
    for s, off, n, o in pieces:
        pad_ref[_rows(_pad_row(s, off, t), n, SUBLANES), :] = val[o:o + n]


def _scale_padded(pad_ref, val, pieces, t):
    for s, off, n, o in pieces:
        rows = _rows(_pad_row(s, off, t), n, SUBLANES)
        pad_ref[rows, :] = pad_ref[rows, :] * val[o:o + n]


def _modnorm_chunk(src_ref, h_ref, c, nb, t, gain, shift):
    for s, off, n, o in _pieces(c, nb, t):
        for i in range(0, n, NORM_ROWS):
            x = src_ref[s, _rows(off + i, NORM_ROWS, NORM_ROWS), :]
            ms = jnp.mean(x * x, axis=-1, keepdims=True)
            h_ref[_rows(c * ROW_CHUNK + o + i, NORM_ROWS, NORM_ROWS), :] = (
                x * lax.rsqrt(ms + EPS) * gain + shift).astype(BF16)


def _zero_pads(pad_ref, nb, t):
    z = jnp.zeros((PAD, W_HALF), F32)
    for s in range(nb):
        pad_ref[_pad_row(s, 0, t) - PAD:_pad_row(s, 0, t), :] = z
        pad_ref[_pad_row(s, t, t):_pad_row(s, t, t) + PAD, :] = z


def _pool_phase(pad_ref, ga_ref, wp_ref, ps_ref, ab_ref, nb, t):
    n_rows = POOL_ROWS
    per_seq = t // n_rows

    def step(i, carry):
        s = i // per_seq
        r0 = (i - s * per_seq) * n_rows
        prow = _pad_row(s, r0, t)
        rows = _rows(i * n_rows, n_rows, n_rows)
        pos = r0 + lax.broadcasted_iota(jnp.int32, (n_rows, LANES), 0)
        before = jnp.minimum(pos, SUBLANES)
        after = jnp.minimum(t - pos, SUBLANES)
        for g in range(N_POOL_GROUPS):
            hw = POOL_HALF[g]
            ln = _lanes(g)
            halo = n_rows + 2 * SUBLANES
            blk = pad_ref[_rows(prow - SUBLANES, halo, SUBLANES), ln]
            run, n = blk, 1
            while n < 2 * hw:
                run = run + pltpu.roll(run, halo - n, 0)
                n *= 2
            if hw < SUBLANES:
                run = pltpu.roll(run, halo - (SUBLANES - hw), 0)
            win = run[:n_rows]
            cnt = (jnp.minimum(before, hw) + jnp.minimum(after, hw)).astype(F32)
            p = (win / cnt - blk[SUBLANES:SUBLANES + n_rows]).astype(BF16)
            y = _dot(p, wp_ref[0, g].astype(BF16)) * ps_ref[:, ln] * ga_ref[rows, ln]
            ab_ref[rows, ln] = y.astype(BF16)
        return carry
    lax.fori_loop(0, nb * per_seq, step, 0)


def _out_proj_chunk(ab_ref, w_ref, x_ref, gate, dst_ref, c, nb, t):
    lhs = ab_ref[_rows(c * ROW_CHUNK, ROW_CHUNK, ROW_CHUNK), :]
    for g in range(D_MODEL // W_HALF):
        y = _dot(lhs, w_ref[:, _group(g)])
        for s, off, n, o in _pieces(c, nb, t):
            rows = _rows(off, n, n)
            dst_ref[s, rows, _group(g)] = x_ref[s, rows, _group(g)] + gate[:, _group(g)] * y[o:o + n]


def _shift_up(x, o, n):
    if o % SUBLANES == 0:
        return x[o:o + n]
    return pltpu.roll(x, x.shape[0] - o, 0)[:n]


def _conv_phase(pad_c, pad_d, bc_ref, ga_ref, gb_ref, cc_ref, cdw_ref, cdb_ref, lng_ref, lnb_ref,
                ab_ref, nb, t, only_seq=None):
    n_rows = CONV_ROWS
    per_seq = t // n_rows

    def step(i, carry):
        s = i // per_seq
        r0 = (i - s * per_seq) * n_rows
        prow = _pad_row(s, r0, t)
        rows = _rows(i * n_rows, n_rows, n_rows)
        z = []
        for g in range(W_HALF // LANES):
            ln = _lanes(g)
            blk = pad_c[_rows(prow - SUBLANES, n_rows + 2 * SUBLANES, SUBLANES), ln]
            c3 = None
            for j in range(CONV_C):
                o = SUBLANES + j - CONV_C // 2
                term = _shift_up(blk, o, n_rows) * cc_ref[j, :, ln]
                c3 = term if c3 is None else c3 + term
            ab_ref[rows, ln] = (bc_ref[rows, ln] * c3 * ga_ref[rows, ln]).astype(BF16)
            acc = None
            for sft in range(SUBLANES):
                part = None
                for a in range((CONV_D - sft + SUBLANES - 1) // SUBLANES):
                    j = SUBLANES * a + sft
                    src = pad_d[_rows(prow - 2 * SUBLANES + SUBLANES * a, n_rows + SUBLANES,
                                      SUBLANES), ln]
                    term = src * cdw_ref[j, :, ln]
                    part = term if part is None else part + term
                o = SUBLANES + sft - (CONV_D // 2 - SUBLANES)
                part = _shift_up(part, o, n_rows)
                acc = part if acc is None else acc + part
            z.append(acc + cdb_ref[:, ln])
        z = jnp.concatenate(z, axis=-1)
        mu = jnp.mean(z, axis=-1, keepdims=True)
        zc = z - mu
        var = jnp.mean(zc * zc, axis=-1, keepdims=True)
        zn = zc * lax.rsqrt(var + EPS) * lng_ref[...] + lnb_ref[...]
        ab_ref[rows, W_HALF:] = (_silu(zn) * gb_ref[rows, :]).astype(BF16)
        return carry
    if only_seq is None:
        lax.fori_loop(0, nb * per_seq, step, 0)
    else:
        for i in range(only_seq * per_seq, (only_seq + 1) * per_seq):
            step(i, 0)


def _final_norm_chunk(y_ref, fg, c, nb, t):
    for s, off, n, _ in _pieces(c, nb, t):
        for i in range(0, n, NORM_ROWS):
            rows = _rows(off + i, NORM_ROWS, NORM_ROWS)
            x = y_ref[s, rows, :]
            ms = jnp.mean(x * x, axis=-1, keepdims=True)
            y_ref[s, rows, :] = x * lax.rsqrt(ms + EPS) * fg


def _out_proj_sequence(ab_ref, w_ref, y_ref, gate, fg, sq, t):
    lhs = ab_ref[sq * t:(sq + 1) * t, :]
    for g in range(D_MODEL // W_HALF):
        y = _dot(lhs, w_ref[:, _group(g)])
        y_ref[sq, :, _group(g)] = y_ref[sq, :, _group(g)] + gate[:, _group(g)] * y
    for i in range(0, t, NORM_ROWS):
        x = y_ref[sq, i:i + NORM_ROWS, :]
        ms = jnp.mean(x * x, axis=-1, keepdims=True)
        y_ref[sq, i:i + NORM_ROWS, :] = x * lax.rsqrt(ms + EPS) * fg


def _odd_layer(y_ref, m_row, g_row, fg, wio_ref, cc_ref, cdw_ref, cdb_ref, lng_ref, lnb_ref, woo_ref,
               h_ref, pad_c, pad_d, bc_ref, ga_ref, gb_ref, ab_ref, nb, t):
    shift = m_row[:, :D_MODEL]
    gain = g_row * (1.0 + m_row[:, D_MODEL:2 * D_MODEL])
    gate = m_row[:, 2 * D_MODEL:]
    n_chunks = nb * t // ROW_CHUNK

    def in_proj(c):
        _modnorm_chunk(y_ref, h_ref, c, nb, t, gain, shift)
        rows = _rows(c * ROW_CHUNK, ROW_CHUNK, ROW_CHUNK)
        pieces = _pieces(c, nb, t)
        h = h_ref[rows, :]
        bc_ref[rows, :] = _dot(h, wio_ref[:, _group(0)])
        _store_padded(pad_c, _dot(h, wio_ref[:, _group(1)]), pieces, t)
        _scale_padded(pad_c, _dot(h, wio_ref[:, _group(2)]), pieces, t)
        ga_ref[rows, :] = _silu(_dot(h, wio_ref[:, _group(3)]))
        _store_padded(pad_d, _dot(h, wio_ref[:, _group(4)]), pieces, t)
        _scale_padded(pad_d, _sigmoid(_dot(h, wio_ref[:, _group(5)])), pieces, t)
        gb_ref[rows, :] = _silu(_dot(h, wio_ref[:, _group(6)]))
    _for_chunks(n_chunks, in_proj)

    conv_refs = (pad_c, pad_d, bc_ref, ga_ref, gb_ref, cc_ref, cdw_ref, cdb_ref, lng_ref, lnb_ref,
                 ab_ref)
    if t < ROW_CHUNK:
        for sq in range(nb):
            _conv_phase(*conv_refs, nb, t, only_seq=sq)
            _out_proj_sequence(ab_ref, woo_ref, y_ref, gate, fg, sq, t)
    else:
        _conv_phase(*conv_refs, nb, t)

        def out_proj(c):
            _out_proj_chunk(ab_ref, woo_ref, y_ref, gate, y_ref, c, nb, t)
            _final_norm_chunk(y_ref, fg, c, nb, t)
        _for_chunks(n_chunks, out_proj)


def _even_in_proj(x_ref, m_row, g_row, w_ref, h_ref, pad_a, ga_ref, gb_ref, q_ref, k_ref, v_ref,
                  kv_t, nb, t):
    shift = m_row[:, :D_MODEL]
    gain = g_row * (1.0 + m_row[:, D_MODEL:2 * D_MODEL])

    def in_proj(c):
        _modnorm_chunk(x_ref, h_ref, c, nb, t, gain, shift)
        rows = _rows(c * ROW_CHUNK, ROW_CHUNK, ROW_CHUNK)
        pieces = _pieces(c, nb, t)
        h = h_ref[rows, :]
        _store_padded(pad_a, _dot(h, w_ref[:, _group(0)]), pieces, t)
        ga_ref[rows, :] = _silu(_dot(h, w_ref[:, _group(1)]))
        q_ref[rows, :] = (_dot(h, w_ref[:, _group(2)]) * Q_SCALE).astype(BF16)
        for i, (dst, g) in enumerate(((k_ref, 3), (v_ref, 4))):
            if kv_t is None:
                dst[rows, :] = _dot(h, w_ref[:, _group(g)]).astype(BF16)
                continue
            acc = _dot_nt(kv_t[2 + i][...], h)
            dst[:, rows] = acc.astype(BF16)
            for s, off, n, o in pieces:
                for hd in range(N_HEADS):
                    kv_t[i][s, 0, hd, :, _rows(off, n, n)] = (
                        acc[hd * HEAD_DIM:(hd + 1) * HEAD_DIM, o:o + n])
        gb_ref[rows, :] = _silu(_dot(h, w_ref[:, _group(5)]))
    _for_chunks(nb * t // ROW_CHUNK, in_proj)


def _even_out_proj(x_ref, y_ref, m_row, w_ref, ab_ref, nb, t):
    gate = m_row[:, 2 * D_MODEL:]
    n_chunks = nb * t // ROW_CHUNK

    def out_proj(c, carry=0):
        _out_proj_chunk(ab_ref, w_ref, x_ref, gate, y_ref, c, nb, t)
        return carry
    if n_chunks == 1:
        out_proj(0)
    else:
        lax.fori_loop(0, n_chunks, out_proj, 0)


def _split_heads(x):
    lane = lax.broadcasted_iota(jnp.int32, (1, LANES), 1)
    first = jnp.where(lane < HEAD_DIM, 1.0, 0.0).astype(x.dtype)
    return jnp.concatenate([x * first, x * (1 - first)], axis=0)


def _merge_heads(o):
    n = o.shape[0] // 2
    lane = lax.broadcasted_iota(jnp.int32, (n, LANES), 1)
    return jnp.where(lane < HEAD_DIM, o[:n], o[n:])


def _context_attention(q_ref, kt_ref, vt_ref, gb_ref, ab_ref, nb, t):
    for s in range(nb):
        seq = slice(s * t, (s + 1) * t)
        for j in range(N_HEADS // 2):
            ln = _lanes(j)
            kp = kt_ref[ln, seq]
            vp = vt_ref[ln, seq]
            for r0 in range(0, t, Q_ROWS):
                rows = slice(s * t + r0, s * t + r0 + Q_ROWS)
                sc = _dot(_split_heads(q_ref[rows, ln]), kp)
                p = jnp.exp2(sc - jnp.max(sc, axis=-1, keepdims=True))
                o = _dot_nt(p.astype(BF16), vp) / jnp.sum(p, axis=-1, keepdims=True)
                ab_ref[rows, W_HALF + j * LANES:W_HALF + (j + 1) * LANES] = (
                    _merge_heads(o) * gb_ref[rows, ln]).astype(BF16)


def _stage_weights(w_hbm, w_bf, wkt_ref, wvt_ref, stage, sem_in):
    chunks = [(k, r0) for k in range(len(w_hbm)) for r0 in range(0, w_hbm[k].shape[1], STAGE_ROWS)]

    def fetch(i):
        k, r0 = chunks[i]
        cols = w_hbm[k].shape[2]
        slot = i % STAGE_SLOTS
        return pltpu.make_async_copy(w_hbm[k].at[0, pl.ds(r0, STAGE_ROWS), :],
                                     stage.at[slot, :, pl.ds(0, cols)], sem_in.at[slot])

    for i in range(min(STAGE_SLOTS - 1, len(chunks))):
        fetch(i).start()
    for i, (k, r0) in enumerate(chunks):
        if i + STAGE_SLOTS - 1 < len(chunks):
            fetch(i + STAGE_SLOTS - 1).start()
        fetch(i).wait()
        cols = w_hbm[k].shape[2]
        rows = slice(r0, r0 + STAGE_ROWS)
        slot = i % STAGE_SLOTS
        w_bf[k][rows, :] = stage[slot, :, 0:cols].astype(BF16)
        if k == 0:
            wkt_ref[:, rows] = stage[slot, :, _group(3)].T.astype(BF16)
            wvt_ref[:, rows] = stage[slot, :, _group(4)].T.astype(BF16)


def _prompt_body(x_ref, m_ref, ng_ref, fg_ref, wp_ref, ps_ref, cc_ref, cdw_ref, cdb_ref, lng_ref,
                 lnb_ref, wie_hbm, woe_hbm, wio_hbm, woo_hbm,
                 y_ref, ko_ref, vo_ref, wie_out, woe_out, wio_out, woo_out,
                 h_ref, pad_a, pad_b, ga_ref, gb_ref, bc_ref, q_ref, kt_ref, vt_ref, ab_ref,
                 wie_ref, woe_ref, wio_ref, woo_ref, wkt_ref, wvt_ref, stage, sem_in, sem_out,
                 *, nb, t):
    w_out = (wie_out, woe_out, wio_out, woo_out)
    w_bf = (wie_ref, woe_ref, wio_ref, woo_ref)

    def write_back(k):
        return pltpu.make_async_copy(w_bf[k], w_out[k], sem_out.at[k])

    @pl.when(pl.program_id(0) == 0)
    def _():
        _stage_weights((wie_hbm, woe_hbm, wio_hbm, woo_hbm), w_bf, wkt_ref, wvt_ref, stage, sem_in)
        for k in range(len(w_bf)):
            write_back(k).start()

    _zero_pads(pad_a, nb, t)
    _zero_pads(pad_b, nb, t)
    m_even = _cond_row(m_ref, 0, 0)
    _even_in_proj(x_ref, m_even, ng_ref[0:1, :], wie_ref, h_ref, pad_a, ga_ref, gb_ref,
                  q_ref, kt_ref, vt_ref, (ko_ref, vo_ref, wkt_ref, wvt_ref), nb, t)
    _pool_phase(pad_a, ga_ref, wp_ref, ps_ref, ab_ref, nb, t)
    _context_attention(q_ref, kt_ref, vt_ref, gb_ref, ab_ref, nb, t)
    _even_out_proj(x_ref, y_ref, m_even, woe_ref, ab_ref, nb, t)
    _odd_layer(y_ref, _cond_row(m_ref, 1, 0), ng_ref[1:2, :], fg_ref[...], wio_ref, cc_ref, cdw_ref,
               cdb_ref, lng_ref, lnb_ref, woo_ref, h_ref, pad_a, pad_b, bc_ref, ga_ref, gb_ref, ab_ref,
               nb, t)

    @pl.when(pl.program_id(0) == 0)
    def _():
        for k in range(len(w_bf)):
            write_back(k).wait()


def _rpb_rows(rpb_ref, e_ref):
    n = rpb_ref.shape[0] * rpb_ref.shape[1]
    lane = lax.broadcasted_iota(jnp.int32, (n, LANES), 1)
    i = jnp.where(lane < GRID_W, lane, lane - LANES)
    idx = jnp.clip(i, -(WIN_W - 1), WIN_W - 1) + (WIN_W - 1)
    rp = rpb_ref[...].reshape(n, rpb_ref.shape[2])
    e = jnp.zeros((n, LANES), F32)
    for d in range(2 * WIN_W - 1):
        e = jnp.where(idx == d, rp[:, d:d + 1], e)
    e_ref[...] = e


N_DR = 2 * WIN_H - 1
PAIR_TILES = N_DR // 2


def _bias_tile_index(j, dr_lo):
    if isinstance(dr_lo, int):
        parity, half = dr_lo % 2, dr_lo // 2
    else:
        parity, half = dr_lo & 1, lax.shift_right_logical(dr_lo, 1)
    return (2 * j + parity) * PAIR_TILES + half


def _bias_tables(e_ref, bias_ref):
    q = lax.broadcasted_iota(jnp.int32, (GRID_W, LANES), 0)
    lane = lax.broadcasted_iota(jnp.int32, (GRID_W, LANES), 1)
    kw = jnp.where(lane < GRID_W, lane, lane - GRID_W)
    start = jnp.clip(q - WIN_W // 2, 0, GRID_W - WIN_W)
    col_ok = (kw >= start) & (kw < start + WIN_W)
    for j in range(N_HEADS // 2):
        for dr in range(N_DR - 1):
            for e in range(2):
                r_lo = dr * N_HEADS + 2 * j + e
                r_hi = r_lo + N_HEADS
                lo = jnp.broadcast_to(e_ref[r_lo:r_lo + 1, :], (GRID_W, LANES))
                hi = jnp.broadcast_to(e_ref[r_hi:r_hi + 1, :], (GRID_W, LANES))
                lo = pltpu.roll(lo, 0, 1, stride=1, stride_axis=0)
                hi = pltpu.roll(hi, GRID_W, 1, stride=1, stride_axis=0)
                tile = jnp.where(lane < GRID_W, lo, hi)
                bias_ref[_bias_tile_index(j, dr), e * GRID_W:(e + 1) * GRID_W, :] = jnp.where(
                    col_ok, tile * LOG2_E, MASKED)


def _neighbourhood_attention(q_ref, k_ref, v_ref, ck_ref, cv_ref, bias_ref, kvc_ref, gb_ref, ab_ref, t):
    grid_h = t // GRID_W
    band = WIN_H * GRID_W
    def per_pair(j, carry_j):
        ln = pl.ds(pl.multiple_of(j * LANES, LANES), LANES)
        out_ln = pl.ds(pl.multiple_of(W_HALF + j * LANES, LANES), LANES)
        for i, src in enumerate((ck_ref, cv_ref)):
            kvc_ref[i] = jnp.concatenate([src[0, 0, 2 * j], src[0, 0, 2 * j + 1]],
                                         axis=0).astype(BF16)

        def per_group(g, carry):
            scored = []
            for u in range(NA_GROUP):
                r = g * NA_GROUP + u
                start = jnp.clip(r - WIN_H // 2, 0, grid_h - WIN_H)
                rows = _rows(r * GRID_W, GRID_W, GRID_W)
                keys = _rows(start * GRID_W, band, GRID_W)
                q2 = _split_heads(q_ref[rows, ln])
                dr0 = (WIN_H - 1) - (r - start)
                bias = jnp.concatenate([bias_ref[_bias_tile_index(j, dr0 + 2 * i)]
                                        for i in range(WIN_H // 2)], axis=-1)
                scored.append((rows, keys, _dot_nt(q2, k_ref[keys, ln]) + bias, _dot(q2, kvc_ref[0])))
            weighted = []
            for rows, keys, s_loc, s_ctx in scored:
                mx = jnp.maximum(jnp.max(s_loc, axis=-1, keepdims=True),
                                 jnp.max(s_ctx, axis=-1, keepdims=True))
                p_loc = jnp.exp2(s_loc - mx)
                p_ctx = jnp.exp2(s_ctx - mx)
                den = (jnp.sum(p_loc, axis=-1, keepdims=True)
                       + jnp.sum(p_ctx, axis=-1, keepdims=True))
                weighted.append((rows, keys, p_loc.astype(BF16), p_ctx.astype(BF16), den))
            for rows, keys, p_loc, p_ctx, den in weighted:
                o = (_dot(p_loc, v_ref[keys, ln]) + _dot_nt(p_ctx, kvc_ref[1])) / den
                ab_ref[rows, out_ln] = (_merge_heads(o) * gb_ref[rows, ln]).astype(BF16)
            return carry
        lax.fori_loop(0, grid_h // NA_GROUP, per_group, 0)
        return carry_j
    lax.fori_loop(0, N_HEADS // 2, per_pair, 0)


def _sample_body(x_ref, m_ref, ng_ref, fg_ref, wie_ref, wp_ref, ps_ref, woe_hbm, wio_hbm, cc_ref,
                 cdw_ref, cdb_ref, lng_ref, lnb_ref, woo_hbm, ck_ref, cv_ref, rpb_ref,
                 y_ref,
                 h_ref, pad_a, pad_b, ga_ref, gb_ref, bc_ref, q_ref, k_ref, v_ref, ab_ref,
                 e_ref, bias_ref, kvc_ref, woe_ref, wio_ref, woo_ref, sem_w, *, t):
    _zero_pads(pad_a, 1, t)
    _zero_pads(pad_b, 1, t)
    late = ((woe_hbm, woe_ref), (wio_hbm, wio_ref), (woo_hbm, woo_ref))

    def late_copy(i):
        return pltpu.make_async_copy(late[i][0], late[i][1], sem_w.at[i])

    first_step = pl.program_id(0) == 0

    @pl.when(first_step)
    def _():
        for i in range(len(late)):
            late_copy(i).start()
        _rpb_rows(rpb_ref, e_ref)
        _bias_tables(e_ref, bias_ref)

    cond = pl.program_id(0) + 1
    m_even = _cond_row(m_ref, 0, cond)
    _even_in_proj(x_ref, m_even, ng_ref[0:1, :], wie_ref, h_ref, pad_a, ga_ref, gb_ref,
                  q_ref, k_ref, v_ref, None, 1, t)
    _pool_phase(pad_a, ga_ref, wp_ref, ps_ref, ab_ref, 1, t)
    _neighbourhood_attention(q_ref, k_ref, v_ref, ck_ref, cv_ref, bias_ref, kvc_ref, gb_ref, ab_ref, t)
    pl.when(first_step)(lambda: late_copy(0).wait())
    _even_out_proj(x_ref, y_ref, m_even, woe_ref, ab_ref, 1, t)

    @pl.when(first_step)
    def _():
        late_copy(1).wait()
        late_copy(2).wait()

    _odd_layer(y_ref, _cond_row(m_ref, 1, cond), ng_ref[1:2, :], fg_ref[...], wio_ref, cc_ref, cdw_ref,
               cdb_ref, lng_ref, lnb_ref, woo_ref, h_ref, pad_a, pad_b, bc_ref, ga_ref, gb_ref, ab_ref,
               1, t)


def _const_spec(shape):
    zeros = (0,) * len(shape)
    return pl.BlockSpec(shape, lambda i: zeros, pipeline_mode=pl.Buffered(1))


def _stream_scratch(nb, t, kv_transposed):
    r = nb * t
    padded = nb * (t + 2 * PAD)
    kv = (W_HALF, r) if kv_transposed else (r, W_HALF)
    return [
        pltpu.VMEM((r, D_MODEL), BF16),
        pltpu.VMEM((padded, W_HALF), F32),
        pltpu.VMEM((padded, W_HALF), F32),
        pltpu.VMEM((r, W_HALF), F32),
        pltpu.VMEM((r, W_HALF), F32),
        pltpu.VMEM((r, W_HALF), F32),
        pltpu.VMEM((r, W_HALF), BF16),
        pltpu.VMEM(kv, BF16),
        pltpu.VMEM(kv, BF16),
        pltpu.VMEM((r, D_MODEL), BF16),
    ]


def _small_params(norm_g, final_g, w_pool, pool_scale, conv_c, conv_d, conv_d_b, ln_g, ln_b):
    return [norm_g, final_g.reshape(1, D_MODEL), w_pool, pool_scale, jnp.swapaxes(conv_c, 0, 1),
            jnp.swapaxes(conv_d, 0, 1), conv_d_b, ln_g, ln_b]


def kernel(x_prompt, x_sample, cache_k, cache_v, c, c_ctx, norm_g, w_mod, b_mod, w_in_even, w_pool,
           pool_scale, rpb, w_out_even, w_in_odd, conv_c, conv_d, conv_d_b, ln_g, ln_b, w_out_odd,
           final_g):
    batch, seq, d = x_prompt.shape
    dec_batch, dec_seq, _ = x_sample.shape
    assert d == D_MODEL and w_mod.shape[0] == 2 and w_in_even.shape[0] == 1 and w_in_odd.shape[0] == 1
    assert (NB_PROMPT * seq) % ROW_CHUNK == 0 and ROW_CHUNK % seq == 0 and seq % Q_ROWS == 0
    assert dec_seq % ROW_CHUNK == 0 and dec_seq // GRID_W >= WIN_H
    assert seq % POOL_ROWS == 0 and seq % CONV_ROWS == 0
    assert dec_seq % POOL_ROWS == 0 and dec_seq % CONV_ROWS == 0
    assert (dec_seq // GRID_W) % NA_GROUP == 0

    cond_rows = SUBLANES * ((1 + dec_batch + SUBLANES - 1) // SUBLANES)
    m = _modulation(c_ctx, c, w_mod, b_mod, cond_rows)
    m_spec = _const_spec(m.shape)

    small = _small_params(norm_g, final_g, w_pool, pool_scale, conv_c, conv_d, conv_d_b, ln_g, ln_b)
    small_specs = [_const_spec(a.shape) for a in small]
    w_f32 = (w_in_even, w_out_even, w_in_odd, w_out_odd)
    assert all(w.shape[0] == 1 and w.shape[1] % STAGE_ROWS == 0 for w in w_f32)
    any_spec = pl.BlockSpec(memory_space=pl.ANY)

    nb = NB_PROMPT
    assert batch % nb == 0
    kv_shape = jax.ShapeDtypeStruct((batch, 1, N_HEADS, HEAD_DIM, seq), F32)
    kv_spec = pl.BlockSpec((nb, 1, N_HEADS, HEAD_DIM, seq), lambda i: (i, 0, 0, 0, 0))
    y_prompt, new_kt, new_vt, wie, woe, wio, woo = pl.pallas_call(
        functools.partial(_prompt_body, nb=nb, t=seq),
        out_shape=(jax.ShapeDtypeStruct(x_prompt.shape, F32), kv_shape, kv_shape)
                  + tuple(jax.ShapeDtypeStruct(w.shape[1:], BF16) for w in w_f32),
        grid=(batch // nb,),
        in_specs=[pl.BlockSpec((nb, seq, d), lambda i: (i, 0, 0)), m_spec] + small_specs
                 + [any_spec] * len(w_f32),
        out_specs=(pl.BlockSpec((nb, seq, d), lambda i: (i, 0, 0)), kv_spec, kv_spec)
                  + (any_spec,) * len(w_f32),
        scratch_shapes=_stream_scratch(nb, seq, True)
                       + [pltpu.VMEM(w.shape[1:], BF16) for w in w_f32] + [
            pltpu.VMEM((W_HALF, d), BF16),
            pltpu.VMEM((W_HALF, d), BF16),
            pltpu.VMEM((STAGE_SLOTS, STAGE_ROWS, max(w.shape[2] for w in w_f32)), F32),
            pltpu.SemaphoreType.DMA((STAGE_SLOTS,)),
            pltpu.SemaphoreType.DMA((len(w_f32),)),
        ],
        compiler_params=pltpu.CompilerParams(dimension_semantics=("arbitrary",),
                                             vmem_limit_bytes=VMEM_LIMIT),
        name="prompt",
    )(x_prompt, m, *small, *w_f32)
    ng, fg, wp, ps, cc, cdw, cdb, lng, lnb = small
    w_args = [ng, fg, wie, wp, ps, woe, wio, cc, cdw, cdb, lng, lnb, woo]
    late_w = (woe, wio, woo)
    w_specs = [any_spec if any(a is w for w in late_w) else _const_spec(a.shape) for a in w_args]

    past = cache_k.shape[3]
    cache_spec = pl.BlockSpec((1, 1, N_HEADS, HEAD_DIM, past), lambda i: (i, 0, 0, 0, 0))
    rpb_t = jnp.swapaxes(rpb[0], 0, 1)
    y_sample = pl.pallas_call(
        functools.partial(_sample_body, t=dec_seq),
        out_shape=jax.ShapeDtypeStruct(x_sample.shape, F32),
        grid=(dec_batch,),
        in_specs=[pl.BlockSpec((1, dec_seq, d), lambda i: (i, 0, 0), pipeline_mode=pl.Buffered(1)),
                  m_spec] + w_specs
                 + [cache_spec, cache_spec, _const_spec(rpb_t.shape)],
        out_specs=pl.BlockSpec((1, dec_seq, d), lambda i: (i, 0, 0)),
        scratch_shapes=_stream_scratch(1, dec_seq, False) + [
            pltpu.VMEM((N_DR * N_HEADS, LANES), F32),
            pltpu.VMEM((N_HEADS * PAIR_TILES, 2 * GRID_W, LANES), F32),
            pltpu.VMEM((2, LANES, past), BF16),
        ] + [pltpu.VMEM(w.shape, BF16) for w in late_w] + [pltpu.SemaphoreType.DMA((len(late_w),))],
        compiler_params=pltpu.CompilerParams(dimension_semantics=("arbitrary",),
                                             vmem_limit_bytes=VMEM_LIMIT),
        name="sample",
    )(x_sample, m, *w_args, jnp.swapaxes(cache_k, 3, 4), jnp.swapaxes(cache_v, 3, 4), rpb_t)

    return (y_prompt, y_sample, jnp.swapaxes(new_kt, 3, 4), jnp.swapaxes(new_vt, 3, 4))
```

```python
import functools

import jax
import jax.numpy as jnp
from jax import lax
from jax.experimental import pallas as pl
from jax.experimental.pallas import tpu as pltpu

F32 = jnp.float32
BF16 = jnp.bfloat16

D_MODEL = 1024
W_HALF = 512
N_POOL_GROUPS = 4
POOL_HALF = (1, 2, 4, 8)
N_HEADS = 8
HEAD_DIM = 64
GRID_W = 64
WIN_H = 8
WIN_W = 16
CONV_C = 3
CONV_D = 31
EPS = 1e-6
MASKED = -1e30
LOG2_E = 1.4426950408889634
Q_SCALE = HEAD_DIM ** -0.5 * LOG2_E

LANES = 128
SUBLANES = 8
PAD = 16
ROW_CHUNK = 512
NORM_ROWS = 32
POOL_ROWS = 256
CONV_ROWS = 128
Q_ROWS = 128
NB_PROMPT = 2
NA_GROUP = 8
MOD_ROWS = 512
STAGE_ROWS = 128
STAGE_SLOTS = 4
VMEM_LIMIT = 58 * 1024 * 1024

assert PAD >= CONV_D // 2 + 1 and PAD % SUBLANES == 0 and PAD >= 2 * SUBLANES
assert max(POOL_HALF) <= SUBLANES


def _sigmoid(x):
    return 1.0 / (1.0 + jnp.exp(-x))


def _silu(x):
    return x * _sigmoid(x)


def _dot(a, b):
    return jnp.dot(a, b, preferred_element_type=F32)


def _dot_nt(a, b):
    return lax.dot_general(a, b, (((1,), (1,)), ((), ())), preferred_element_type=F32)


def _lanes(j):
    return slice(j * LANES, (j + 1) * LANES)


def _group(g):
    return slice(g * W_HALF, (g + 1) * W_HALF)


def _rows(start, size, align):
    if isinstance(start, int):
        return slice(start, start + size)
    return pl.ds(pl.multiple_of(start, align), size)


def _mod_body(cctx_ref, c_ref, w_ref, b_ref, o_ref, act_ref):
    layer, kb = pl.program_id(0), pl.program_id(1)
    rows, d = act_ref.shape

    @pl.when(kb == 0)
    def _():
        r = lax.broadcasted_iota(jnp.int32, (rows, d), 0)
        cond = jnp.where(r == 0, cctx_ref[...], 0.0)
        for i in range(c_ref.shape[0]):
            cond = jnp.where(r == i + 1, c_ref[i:i + 1, :], cond)
        act_ref[...] = _silu(cond).astype(BF16)
        o_ref[0] = jnp.broadcast_to(jnp.where(layer == 0, b_ref[0:1, :], b_ref[1:2, :]), o_ref.shape[1:])

    act = act_ref[:, pl.ds(pl.multiple_of(kb * MOD_ROWS, MOD_ROWS), MOD_ROWS)]
    o_ref[0] += _dot(act, w_ref[0].astype(BF16))


def _modulation(c_ctx, c, w_mod, b_mod, rows):
    depth, d, n = w_mod.shape
    assert depth == 2 and 1 + c.shape[0] <= rows and d % MOD_ROWS == 0
    return pl.pallas_call(
        _mod_body,
        out_shape=jax.ShapeDtypeStruct((depth, rows, n), F32),
        grid=(depth, d // MOD_ROWS),
        in_specs=[
            pl.BlockSpec((1, d), lambda l, k: (0, 0)),
            pl.BlockSpec(c.shape, lambda l, k: (0, 0)),
            pl.BlockSpec((1, MOD_ROWS, n), lambda l, k: (l, k, 0)),
            pl.BlockSpec((depth, n), lambda l, k: (0, 0)),
        ],
        out_specs=pl.BlockSpec((1, rows, n), lambda l, k: (l, 0, 0)),
        scratch_shapes=[pltpu.VMEM((rows, d), BF16)],
        compiler_params=pltpu.CompilerParams(dimension_semantics=("arbitrary", "arbitrary")),
        name="mod",
    )(c_ctx.reshape(1, d), c, w_mod, b_mod)


def _cond_row(m_ref, layer, row):
    if isinstance(row, int):
        return m_ref[layer, row:row + 1, :]
    m = m_ref[layer]
    keep = lax.broadcasted_iota(jnp.int32, m.shape, 0) == row
    return jnp.sum(jnp.where(keep, m, 0.0), axis=0, keepdims=True)


def _pieces(c, nb, t):
    if t >= ROW_CHUNK:
        per_seq = t // ROW_CHUNK
        s = 0 if nb == 1 else c // per_seq
        return [(s, (c - s * per_seq) * ROW_CHUNK, ROW_CHUNK, 0)]
    per_chunk = ROW_CHUNK // t
    return [(c * per_chunk + i, 0, t, i * t) for i in range(per_chunk)]


def _for_chunks(n, body):
    for c in range(n):
        body(c)


def _pad_row(s, off, t):
    return s * (t + 2 * PAD) + PAD + off


def _store_padded(pad_ref, val, pieces, t):
    for s, off, n, o in pieces:
        pad_ref[_rows(_pad_row(s, off, t), n, SUBLANES), :] = val[o:o + n]


def _scale_padded(pad_ref, val, pieces, t):
    for s, off, n, o in pieces:
        rows = _rows(_pad_row(s, off, t), n, SUBLANES)
        pad_ref[rows, :] = pad_ref[rows, :] * val[o:o + n]


def _modnorm_chunk(src_ref, h_ref, c, nb, t, gain, shift):
    for s, off, n, o in _pieces(c, nb, t):
        for i in range(0, n, NORM_ROWS):
            x = src_ref[s, _rows(off + i, NORM_ROWS, NORM_ROWS), :]
            ms = jnp.mean(x * x, axis=-1, keepdims=True)
            h_ref[_rows(c * ROW_CHUNK + o + i, NORM_ROWS, NORM_ROWS), :] = (
                x * lax.rsqrt(ms + EPS) * gain + shift).astype(BF16)


def _zero_pads(pad_ref, nb, t):
    z = jnp.zeros((PAD, W_HALF), F32)
    for s in range(nb):
        pad_ref[_pad_row(s, 0, t) - PAD:_pad_row(s, 0, t), :] = z
        pad_ref[_pad_row(s, t, t):_pad_row(s, t, t) + PAD, :] = z


def _pool_phase(pad_ref, ga_ref, wp_ref, ps_ref, ab_ref, nb, t):
    n_rows = POOL_ROWS
    per_seq = t // n_rows

    def step(i, carry):
        s = i // per_seq
        r0 = (i - s * per_seq) * n_rows
        prow = _pad_row(s, r0, t)
        rows = _rows(i * n_rows, n_rows, n_rows)
        pos = r0 + lax.broadcasted_iota(jnp.int32, (n_rows, LANES), 0)
        before = jnp.minimum(pos, SUBLANES)
        after = jnp.minimum(t - pos, SUBLANES)
        for g in range(N_POOL_GROUPS):
            hw = POOL_HALF[g]
            ln = _lanes(g)
            halo = n_rows + 2 * SUBLANES
            blk = pad_ref[_rows(prow - SUBLANES, halo, SUBLANES), ln]
            run, n = blk, 1
            while n < 2 * hw:
                run = run + pltpu.roll(run, halo - n, 0)
                n *= 2
            if hw < SUBLANES:
                run = pltpu.roll(run, halo - (SUBLANES - hw), 0)
            win = run[:n_rows]
            cnt = (jnp.minimum(before, hw) + jnp.minimum(after, hw)).astype(F32)
            p = (win / cnt - blk[SUBLANES:SUBLANES + n_rows]).astype(BF16)
            y = _dot(p, wp_ref[0, g].astype(BF16)) * ps_ref[:, ln] * ga_ref[rows, ln]
            ab_ref[rows, ln] = y.astype(BF16)
        return carry
    lax.fori_loop(0, nb * per_seq, step, 0)


def _out_proj_chunk(ab_ref, w_ref, x_ref, gate, dst_ref, c, nb, t):
    lhs = ab_ref[_rows(c * ROW_CHUNK, ROW_CHUNK, ROW_CHUNK), :]
    for g in range(D_MODEL // W_HALF):
        y = _dot(lhs, w_ref[:, _group(g)])
        for s, off, n, o in _pieces(c, nb, t):
            rows = _rows(off, n, n)
            dst_ref[s, rows, _group(g)] = x_ref[s, rows, _group(g)] + gate[:, _group(g)] * y[o:o + n]


def _shift_up(x, o, n):
    if o % SUBLANES == 0:
        return x[o:o + n]
    return pltpu.roll(x, x.shape[0] - o, 0)[:n]


def _conv_phase(pad_c, pad_d, bc_ref, ga_ref, gb_ref, cc_ref, cdw_ref, cdb_ref, lng_ref, lnb_ref,
                ab_ref, nb, t, only_seq=None):
    n_rows = CONV_ROWS
    per_seq = t // n_rows

    def step(i, carry):
        s = i // per_seq
        r0 = (i - s * per_seq) * n_rows
        prow = _pad_row(s, r0, t)
        rows = _rows(i * n_rows, n_rows, n_rows)
        z = []
        for g in range(W_HALF // LANES):
            ln = _lanes(g)
            blk = pad_c[_rows(prow - SUBLANES, n_rows + 2 * SUBLANES, SUBLANES), ln]
            c3 = None
            for j in range(CONV_C):
                o = SUBLANES + j - CONV_C // 2
                term = _shift_up(blk, o, n_rows) * cc_ref[j, :, ln]
                c3 = term if c3 is None else c3 + term
            ab_ref[rows, ln] = (bc_ref[rows, ln] * c3 * ga_ref[rows, ln]).astype(BF16)
            acc = None
            for sft in range(SUBLANES):
                part = None
                for a in range((CONV_D - sft + SUBLANES - 1) // SUBLANES):
                    j = SUBLANES * a + sft
                    src = pad_d[_rows(prow - 2 * SUBLANES + SUBLANES * a, n_rows + SUBLANES,
                                      SUBLANES), ln]
                    term = src * cdw_ref[j, :, ln]
                    part = term if part is None else part + term
                o = SUBLANES + sft - (CONV_D // 2 - SUBLANES)
                part = _shift_up(part, o, n_rows)
                acc = part if acc is None else acc + part
            z.append(acc + cdb_ref[:, ln])
        z = jnp.concatenate(z, axis=-1)
        mu = jnp.mean(z, axis=-1, keepdims=True)
        zc = z - mu
        var = jnp.mean(zc * zc, axis=-1, keepdims=True)
        zn = zc * lax.rsqrt(var + EPS) * lng_ref[...] + lnb_ref[...]
        ab_ref[rows, W_HALF:] = (_silu(zn) * gb_ref[rows, :]).astype(BF16)
        return carry
    if only_seq is None:
        lax.fori_loop(0, nb * per_seq, step, 0)
    else:
        for i in range(only_seq * per_seq, (only_seq + 1) * per_seq):
            step(i, 0)


def _final_norm_chunk(y_ref, fg, c, nb, t):
    for s, off, n, _ in _pieces(c, nb, t):
        for i in range(0, n, NORM_ROWS):
            rows = _rows(off + i, NORM_ROWS, NORM_ROWS)
            x = y_ref[s, rows, :]
            ms = jnp.mean(x * x, axis=-1, keepdims=True)
            y_ref[s, rows, :] = x * lax.rsqrt(ms + EPS) * fg


def _out_proj_sequence(ab_ref, w_ref, y_ref, gate, fg, sq, t):
    lhs = ab_ref[sq * t:(sq + 1) * t, :]
    for g in range(D_MODEL // W_HALF):
        y = _dot(lhs, w_ref[:, _group(g)])
        y_ref[sq, :, _group(g)] = y_ref[sq, :, _group(g)] + gate[:, _group(g)] * y
    for i in range(0, t, NORM_ROWS):
        x = y_ref[sq, i:i + NORM_ROWS, :]
        ms = jnp.mean(x * x, axis=-1, keepdims=True)
        y_ref[sq, i:i + NORM_ROWS, :] = x * lax.rsqrt(ms + EPS) * fg


def _odd_layer(y_ref, m_row, g_row, fg, wio_ref, cc_ref, cdw_ref, cdb_ref, lng_ref, lnb_ref, woo_ref,
               h_ref, pad_c, pad_d, bc_ref, ga_ref, gb_ref, ab_ref, nb, t):
    shift = m_row[:, :D_MODEL]
    gain = g_row * (1.0 + m_row[:, D_MODEL:2 * D_MODEL])
    gate = m_row[:, 2 * D_MODEL:]
    n_chunks = nb * t // ROW_CHUNK

    def in_proj(c):
        _modnorm_chunk(y_ref, h_ref, c, nb, t, gain, shift)
        rows = _rows(c * ROW_CHUNK, ROW_CHUNK, ROW_CHUNK)
        pieces = _pieces(c, nb, t)
        h = h_ref[rows, :]
        bc_ref[rows, :] = _dot(h, wio_ref[:, _group(0)])
        _store_padded(pad_c, _dot(h, wio_ref[:, _group(1)]), pieces, t)
        _scale_padded(pad_c, _dot(h, wio_ref[:, _group(2)]), pieces, t)
        ga_ref[rows, :] = _silu(_dot(h, wio_ref[:, _group(3)]))
        _store_padded(pad_d, _dot(h, wio_ref[:, _group(4)]), pieces, t)
        _scale_padded(pad_d, _sigmoid(_dot(h, wio_ref[:, _group(5)])), pieces, t)
        gb_ref[rows, :] = _silu(_dot(h, wio_ref[:, _group(6)]))
    _for_chunks(n_chunks, in_proj)

    conv_refs = (pad_c, pad_d, bc_ref, ga_ref, gb_ref, cc_ref, cdw_ref, cdb_ref, lng_ref, lnb_ref,
                 ab_ref)
    if t < ROW_CHUNK:
        for sq in range(nb):
            _conv_phase(*conv_refs, nb, t, only_seq=sq)
            _out_proj_sequence(ab_ref, woo_ref, y_ref, gate, fg, sq, t)
    else:
        _conv_phase(*conv_refs, nb, t)

        def out_proj(c):
            _out_proj_chunk(ab_ref, woo_ref, y_ref, gate, y_ref, c, nb, t)
            _final_norm_chunk(y_ref, fg, c, nb, t)
        _for_chunks(n_chunks, out_proj)


def _even_in_proj(x_ref, m_row, g_row, w_ref, h_ref, pad_a, ga_ref, gb_ref, q_ref, k_ref, v_ref,
                  kv_t, nb, t):
    shift = m_row[:, :D_MODEL]
    gain = g_row * (1.0 + m_row[:, D_MODEL:2 * D_MODEL])

    def in_proj(c):
        _modnorm_chunk(x_ref, h_ref, c, nb, t, gain, shift)
        rows = _rows(c * ROW_CHUNK, ROW_CHUNK, ROW_CHUNK)
        pieces = _pieces(c, nb, t)
        h = h_ref[rows, :]
        _store_padded(pad_a, _dot(h, w_ref[:, _group(0)]), pieces, t)
        ga_ref[rows, :] = _silu(_dot(h, w_ref[:, _group(1)]))
        q_ref[rows, :] = (_dot(h, w_ref[:, _group(2)]) * Q_SCALE).astype(BF16)
        for i, (dst, g) in enumerate(((k_ref, 3), (v_ref, 4))):
            if kv_t is None:
                dst[rows, :] = _dot(h, w_ref[:, _group(g)]).astype(BF16)
                continue
            acc = _dot_nt(kv_t[2 + i][...], h)
            dst[:, rows] = acc.astype(BF16)
            for s, off, n, o in pieces:
                for hd in range(N_HEADS):
                    kv_t[i][s, 0, hd, :, _rows(off, n, n)] = (
                        acc[hd * HEAD_DIM:(hd + 1) * HEAD_DIM, o:o + n])
        gb_ref[rows, :] = _silu(_dot(h, w_ref[:, _group(5)]))
    _for_chunks(nb * t // ROW_CHUNK, in_proj)


def _even_out_proj(x_ref, y_ref, m_row, w_ref, ab_ref, nb, t):
    gate = m_row[:, 2 * D_MODEL:]
    n_chunks = nb * t // ROW_CHUNK

    def out_proj(c, carry=0):
        _out_proj_chunk(ab_ref, w_ref, x_ref, gate, y_ref, c, nb, t)
        return carry
    if n_chunks == 1:
        out_proj(0)
    else:
        lax.fori_loop(0, n_chunks, out_proj, 0)


def _split_heads(x):
    lane = lax.broadcasted_iota(jnp.int32, (1, LANES), 1)
    first = jnp.where(lane < HEAD_DIM, 1.0, 0.0).astype(x.dtype)
    return jnp.concatenate([x * first, x * (1 - first)], axis=0)


def _merge_heads(o):
    n = o.shape[0] // 2
    lane = lax.broadcasted_iota(jnp.int32, (n, LANES), 1)
    return jnp.where(lane < HEAD_DIM, o[:n], o[n:])


def _context_attention(q_ref, kt_ref, vt_ref, gb_ref, ab_ref, nb, t):
    for s in range(nb):
        seq = slice(s * t, (s + 1) * t)
        for j in range(N_HEADS // 2):
            ln = _lanes(j)
            kp = kt_ref[ln, seq]
            vp = vt_ref[ln, seq]
            for r0 in range(0, t, Q_ROWS):
                rows = slice(s * t + r0, s * t + r0 + Q_ROWS)
                sc = _dot(_split_heads(q_ref[rows, ln]), kp)
                p = jnp.exp2(sc - jnp.max(sc, axis=-1, keepdims=True))
                o = _dot_nt(p.astype(BF16), vp) / jnp.sum(p, axis=-1, keepdims=True)
                ab_ref[rows, W_HALF + j * LANES:W_HALF + (j + 1) * LANES] = (
                    _merge_heads(o) * gb_ref[rows, ln]).astype(BF16)


def _stage_weights(w_hbm, w_bf, wkt_ref, wvt_ref, stage, sem_in):
    chunks = [(k, r0) for k in range(len(w_hbm)) for r0 in range(0, w_hbm[k].shape[1], STAGE_ROWS)]

    def fetch(i):
        k, r0 = chunks[i]
        cols = w_hbm[k].shape[2]
        slot = i % STAGE_SLOTS
        return pltpu.make_async_copy(w_hbm[k].at[0, pl.ds(r0, STAGE_ROWS), :],
                                     stage.at[slot, :, pl.ds(0, cols)], sem_in.at[slot])

    for i in range(min(STAGE_SLOTS - 1, len(chunks))):
        fetch(i).start()
    for i, (k, r0) in enumerate(chunks):
        if i + STAGE_SLOTS - 1 < len(chunks):
            fetch(i + STAGE_SLOTS - 1).start()
        fetch(i).wait()
        cols = w_hbm[k].shape[2]
        rows = slice(r0, r0 + STAGE_ROWS)
        slot = i % STAGE_SLOTS
        w_bf[k][rows, :] = stage[slot, :, 0:cols].astype(BF16)
        if k == 0:
            wkt_ref[:, rows] = stage[slot, :, _group(3)].T.astype(BF16)
            wvt_ref[:, rows] = stage[slot, :, _group(4)].T.astype(BF16)


def _prompt_body(x_ref, m_ref, ng_ref, fg_ref, wp_ref, ps_ref, cc_ref, cdw_ref, cdb_ref, lng_ref,
                 lnb_ref, wie_hbm, woe_hbm, wio_hbm, woo_hbm,
                 y_ref, ko_ref, vo_ref, wie_out, woe_out, wio_out, woo_out,
                 h_ref, pad_a, pad_b, ga_ref, gb_ref, bc_ref, q_ref, kt_ref, vt_ref, ab_ref,
                 wie_ref, woe_ref, wio_ref, woo_ref, wkt_ref, wvt_ref, stage, sem_in, sem_out,
                 *, nb, t):
    w_out = (wie_out, woe_out, wio_out, woo_out)
    w_bf = (wie_ref, woe_ref, wio_ref, woo_ref)

    def write_back(k):
        return pltpu.make_async_copy(w_bf[k], w_out[k], sem_out.at[k])

    @pl.when(pl.program_id(0) == 0)
    def _():
        _stage_weights((wie_hbm, woe_hbm, wio_hbm, woo_hbm), w_bf, wkt_ref, wvt_ref, stage, sem_in)
        for k in range(len(w_bf)):
            write_back(k).start()

    _zero_pads(pad_a, nb, t)
    _zero_pads(pad_b, nb, t)
    m_even = _cond_row(m_ref, 0, 0)
    _even_in_proj(x_ref, m_even, ng_ref[0:1, :], wie_ref, h_ref, pad_a, ga_ref, gb_ref,
                  q_ref, kt_ref, vt_ref, (ko_ref, vo_ref, wkt_ref, wvt_ref), nb, t)
    _pool_phase(pad_a, ga_ref, wp_ref, ps_ref, ab_ref, nb, t)
    _context_attention(q_ref, kt_ref, vt_ref, gb_ref, ab_ref, nb, t)
    _even_out_proj(x_ref, y_ref, m_even, woe_ref, ab_ref, nb, t)
    _odd_layer(y_ref, _cond_row(m_ref, 1, 0), ng_ref[1:2, :], fg_ref[...], wio_ref, cc_ref, cdw_ref,
               cdb_ref, lng_ref, lnb_ref, woo_ref, h_ref, pad_a, pad_b, bc_ref, ga_ref, gb_ref, ab_ref,
               nb, t)

    @pl.when(pl.program_id(0) == 0)
    def _():
        for k in range(len(w_bf)):
            write_back(k).wait()


def _rpb_rows(rpb_ref, e_ref):
    n = rpb_ref.shape[0] * rpb_ref.shape[1]
    lane = lax.broadcasted_iota(jnp.int32, (n, LANES), 1)
    i = jnp.where(lane < GRID_W, lane, lane - LANES)
    idx = jnp.clip(i, -(WIN_W - 1), WIN_W - 1) + (WIN_W - 1)
    rp = rpb_ref[...].reshape(n, rpb_ref.shape[2])
    e = jnp.zeros((n, LANES), F32)
    for d in range(2 * WIN_W - 1):
        e = jnp.where(idx == d, rp[:, d:d + 1], e)
    for r in range(n):
        e_ref[r] = e[r:r + 1, :]


N_DR = 2 * WIN_H - 1
PAIR_TILES = N_DR // 2


def _bias_tile_index(j, dr_lo):
    if isinstance(dr_lo, int):
        parity, half = dr_lo % 2, dr_lo // 2
    else:
        parity, half = dr_lo & 1, lax.shift_right_logical(dr_lo, 1)
    return (2 * j + parity) * PAIR_TILES + half


def _bias_tables(e_ref, bias_ref):
    q = lax.broadcasted_iota(jnp.int32, (GRID_W, LANES), 0)
    lane = lax.broadcasted_iota(jnp.int32, (GRID_W, LANES), 1)
    kw = jnp.where(lane < GRID_W, lane, lane - GRID_W)
    start = jnp.clip(q - WIN_W // 2, 0, GRID_W - WIN_W)
    col_ok = (kw >= start) & (kw < start + WIN_W)
    def per_tile(j, dr):
        for e in range(2):
            r_lo = dr * N_HEADS + 2 * j + e
            lo = jnp.broadcast_to(e_ref[r_lo], (GRID_W, LANES))
            hi = jnp.broadcast_to(e_ref[r_lo + N_HEADS], (GRID_W, LANES))
            lo = pltpu.roll(lo, 0, 1, stride=1, stride_axis=0)
            hi = pltpu.roll(hi, GRID_W, 1, stride=1, stride_axis=0)
            tile = jnp.where(lane < GRID_W, lo, hi)
            bias_ref[_bias_tile_index(j, dr), e * GRID_W:(e + 1) * GRID_W, :] = jnp.where(
                col_ok, tile * LOG2_E, MASKED)

    def per_pair(j, carry):
        lax.fori_loop(0, N_DR - 1, lambda dr, c: (per_tile(j, dr), c)[1], 0)
        return carry
    lax.fori_loop(0, N_HEADS // 2, per_pair, 0)


def _neighbourhood_attention(q_ref, k_ref, v_ref, ck_ref, cv_ref, bias_ref, kvc_ref, gb_ref, ab_ref, t):
    grid_h = t // GRID_W
    band = WIN_H * GRID_W
    def per_pair(j, carry_j):
        ln = pl.ds(pl.multiple_of(j * LANES, LANES), LANES)
        out_ln = pl.ds(pl.multiple_of(W_HALF + j * LANES, LANES), LANES)
        for i, src in enumerate((ck_ref, cv_ref)):
            kvc_ref[i] = jnp.concatenate([src[0, 0, 2 * j], src[0, 0, 2 * j + 1]],
                                         axis=0).astype(BF16)

        def per_group(g, carry):
            scored = []
            for u in range(NA_GROUP):
                r = g * NA_GROUP + u
                start = jnp.clip(r - WIN_H // 2, 0, grid_h - WIN_H)
                rows = _rows(r * GRID_W, GRID_W, GRID_W)
                keys = _rows(start * GRID_W, band, GRID_W)
                q2 = _split_heads(q_ref[rows, ln])
                dr0 = (WIN_H - 1) - (r - start)
                bias = jnp.concatenate([bias_ref[_bias_tile_index(j, dr0 + 2 * i)]
                                        for i in range(WIN_H // 2)], axis=-1)
                scored.append((rows, keys, _dot_nt(q2, k_ref[keys, ln]) + bias, _dot(q2, kvc_ref[0])))
            weighted = []
            for rows, keys, s_loc, s_ctx in scored:
                mx = jnp.maximum(jnp.max(s_loc, axis=-1, keepdims=True),
                                 jnp.max(s_ctx, axis=-1, keepdims=True))
                p_loc = jnp.exp2(s_loc - mx)
                p_ctx = jnp.exp2(s_ctx - mx)
                den = (jnp.sum(p_loc, axis=-1, keepdims=True)
                       + jnp.sum(p_ctx, axis=-1, keepdims=True))
                weighted.append((rows, keys, p_loc.astype(BF16), p_ctx.astype(BF16), den))
            for rows, keys, p_loc, p_ctx, den in weighted:
                o = (_dot(p_loc, v_ref[keys, ln]) + _dot_nt(p_ctx, kvc_ref[1])) / den
                ab_ref[rows, out_ln] = (_merge_heads(o) * gb_ref[rows, ln]).astype(BF16)
            return carry
        lax.fori_loop(0, grid_h // NA_GROUP, per_group, 0)
        return carry_j
    lax.fori_loop(0, N_HEADS // 2, per_pair, 0)


def _sample_body(x_ref, m_ref, ng_ref, fg_ref, wie_ref, wp_ref, ps_ref, woe_hbm, wio_hbm, cc_ref,
                 cdw_ref, cdb_ref, lng_ref, lnb_ref, woo_hbm, ck_ref, cv_ref, rpb_ref,
                 y_ref,
                 h_ref, pad_a, pad_b, ga_ref, gb_ref, bc_ref, q_ref, k_ref, v_ref, ab_ref,
                 e_ref, bias_ref, kvc_ref, woe_ref, wio_ref, woo_ref, sem_w, *, t):
    _zero_pads(pad_a, 1, t)
    _zero_pads(pad_b, 1, t)
    late = ((woe_hbm, woe_ref), (wio_hbm, wio_ref), (woo_hbm, woo_ref))

    def late_copy(i):
        return pltpu.make_async_copy(late[i][0], late[i][1], sem_w.at[i])

    first_step = pl.program_id(0) == 0

    @pl.when(first_step)
    def _():
        for i in range(len(late)):
            late_copy(i).start()
        _rpb_rows(rpb_ref, e_ref)
        _bias_tables(e_ref, bias_ref)

    cond = pl.program_id(0) + 1
    m_even = _cond_row(m_ref, 0, cond)
    _even_in_proj(x_ref, m_even, ng_ref[0:1, :], wie_ref, h_ref, pad_a, ga_ref, gb_ref,
                  q_ref, k_ref, v_ref, None, 1, t)
    _pool_phase(pad_a, ga_ref, wp_ref, ps_ref, ab_ref, 1, t)
    _neighbourhood_attention(q_ref, k_ref, v_ref, ck_ref, cv_ref, bias_ref, kvc_ref, gb_ref, ab_ref, t)
    pl.when(first_step)(lambda: late_copy(0).wait())
    _even_out_proj(x_ref, y_ref, m_even, woe_ref, ab_ref, 1, t)

    @pl.when(first_step)
    def _():
        late_copy(1).wait()
        late_copy(2).wait()

    _odd_layer(y_ref, _cond_row(m_ref, 1, cond), ng_ref[1:2, :], fg_ref[...], wio_ref, cc_ref, cdw_ref,
               cdb_ref, lng_ref, lnb_ref, woo_ref, h_ref, pad_a, pad_b, bc_ref, ga_ref, gb_ref, ab_ref,
               1, t)


def _const_spec(shape):
    zeros = (0,) * len(shape)
    return pl.BlockSpec(shape, lambda i: zeros, pipeline_mode=pl.Buffered(1))


def _stream_scratch(nb, t, kv_transposed):
    r = nb * t
    padded = nb * (t + 2 * PAD)
    kv = (W_HALF, r) if kv_transposed else (r, W_HALF)
    return [
        pltpu.VMEM((r, D_MODEL), BF16),
        pltpu.VMEM((padded, W_HALF), F32),
        pltpu.VMEM((padded, W_HALF), F32),
        pltpu.VMEM((r, W_HALF), F32),
        pltpu.VMEM((r, W_HALF), F32),
        pltpu.VMEM((r, W_HALF), F32),
        pltpu.VMEM((r, W_HALF), BF16),
        pltpu.VMEM(kv, BF16),
        pltpu.VMEM(kv, BF16),
        pltpu.VMEM((r, D_MODEL), BF16),
    ]


def _small_params(norm_g, final_g, w_pool, pool_scale, conv_c, conv_d, conv_d_b, ln_g, ln_b):
    return [norm_g, final_g.reshape(1, D_MODEL), w_pool, pool_scale, jnp.swapaxes(conv_c, 0, 1),
            jnp.swapaxes(conv_d, 0, 1), conv_d_b, ln_g, ln_b]


def kernel(x_prompt, x_sample, cache_k, cache_v, c, c_ctx, norm_g, w_mod, b_mod, w_in_even, w_pool,
           pool_scale, rpb, w_out_even, w_in_odd, conv_c, conv_d, conv_d_b, ln_g, ln_b, w_out_odd,
           final_g):
    batch, seq, d = x_prompt.shape
    dec_batch, dec_seq, _ = x_sample.shape
    assert d == D_MODEL and w_mod.shape[0] == 2 and w_in_even.shape[0] == 1 and w_in_odd.shape[0] == 1
    assert (NB_PROMPT * seq) % ROW_CHUNK == 0 and ROW_CHUNK % seq == 0 and seq % Q_ROWS == 0
    assert dec_seq % ROW_CHUNK == 0 and dec_seq // GRID_W >= WIN_H
    assert seq % POOL_ROWS == 0 and seq % CONV_ROWS == 0
    assert dec_seq % POOL_ROWS == 0 and dec_seq % CONV_ROWS == 0
    assert (dec_seq // GRID_W) % NA_GROUP == 0

    cond_rows = SUBLANES * ((1 + dec_batch + SUBLANES - 1) // SUBLANES)
    m = _modulation(c_ctx, c, w_mod, b_mod, cond_rows)
    m_spec = _const_spec(m.shape)

    small = _small_params(norm_g, final_g, w_pool, pool_scale, conv_c, conv_d, conv_d_b, ln_g, ln_b)
    small_specs = [_const_spec(a.shape) for a in small]
    w_f32 = (w_in_even, w_out_even, w_in_odd, w_out_odd)
    assert all(w.shape[0] == 1 and w.shape[1] % STAGE_ROWS == 0 for w in w_f32)
    any_spec = pl.BlockSpec(memory_space=pl.ANY)

    nb = NB_PROMPT
    assert batch % nb == 0
    kv_shape = jax.ShapeDtypeStruct((batch, 1, N_HEADS, HEAD_DIM, seq), F32)
    kv_spec = pl.BlockSpec((nb, 1, N_HEADS, HEAD_DIM, seq), lambda i: (i, 0, 0, 0, 0))
    y_prompt, new_kt, new_vt, wie, woe, wio, woo = pl.pallas_call(
        functools.partial(_prompt_body, nb=nb, t=seq),
        out_shape=(jax.ShapeDtypeStruct(x_prompt.shape, F32), kv_shape, kv_shape)
                  + tuple(jax.ShapeDtypeStruct(w.shape[1:], BF16) for w in w_f32),
        grid=(batch // nb,),
        in_specs=[pl.BlockSpec((nb, seq, d), lambda i: (i, 0, 0)), m_spec] + small_specs
                 + [any_spec] * len(w_f32),
        out_specs=(pl.BlockSpec((nb, seq, d), lambda i: (i, 0, 0)), kv_spec, kv_spec)
                  + (any_spec,) * len(w_f32),
        scratch_shapes=_stream_scratch(nb, seq, True)
                       + [pltpu.VMEM(w.shape[1:], BF16) for w in w_f32] + [
            pltpu.VMEM((W_HALF, d), BF16),
            pltpu.VMEM((W_HALF, d), BF16),
            pltpu.VMEM((STAGE_SLOTS, STAGE_ROWS, max(w.shape[2] for w in w_f32)), F32),
            pltpu.SemaphoreType.DMA((STAGE_SLOTS,)),
            pltpu.SemaphoreType.DMA((len(w_f32),)),
        ],
        compiler_params=pltpu.CompilerParams(dimension_semantics=("arbitrary",),
                                             vmem_limit_bytes=VMEM_LIMIT),
        name="prompt",
    )(x_prompt, m, *small, *w_f32)
    ng, fg, wp, ps, cc, cdw, cdb, lng, lnb = small
    w_args = [ng, fg, wie, wp, ps, woe, wio, cc, cdw, cdb, lng, lnb, woo]
    late_w = (woe, wio, woo)
    w_specs = [any_spec if any(a is w for w in late_w) else _const_spec(a.shape) for a in w_args]

    past = cache_k.shape[3]
    cache_spec = pl.BlockSpec((1, 1, N_HEADS, HEAD_DIM, past), lambda i: (i, 0, 0, 0, 0))
    rpb_t = jnp.swapaxes(rpb[0], 0, 1)
    y_sample = pl.pallas_call(
        functools.partial(_sample_body, t=dec_seq),
        out_shape=jax.ShapeDtypeStruct(x_sample.shape, F32),
        grid=(dec_batch,),
        in_specs=[pl.BlockSpec((1, dec_seq, d), lambda i: (i, 0, 0), pipeline_mode=pl.Buffered(1)),
                  m_spec] + w_specs
                 + [cache_spec, cache_spec, _const_spec(rpb_t.shape)],
        out_specs=pl.BlockSpec((1, dec_seq, d), lambda i: (i, 0, 0)),
        scratch_shapes=_stream_scratch(1, dec_seq, False) + [
            pltpu.VMEM((N_DR * N_HEADS, 1, LANES), F32),
            pltpu.VMEM((N_HEADS * PAIR_TILES, 2 * GRID_W, LANES), F32),
            pltpu.VMEM((2, LANES, past), BF16),
        ] + [pltpu.VMEM(w.shape, BF16) for w in late_w] + [pltpu.SemaphoreType.DMA((len(late_w),))],
        compiler_params=pltpu.CompilerParams(dimension_semantics=("arbitrary",),
                                             vmem_limit_bytes=VMEM_LIMIT),
        name="sample",
    )(x_sample, m, *w_args, jnp.swapaxes(cache_k, 3, 4), jnp.swapaxes(cache_v, 3, 4), rpb_t)

    return (y_prompt, y_sample, jnp.swapaxes(new_kt, 3, 4), jnp.swapaxes(new_vt, 3, 4))
```

```python
import functools

import jax
import jax.numpy as jnp
from jax import lax
from jax.experimental import pallas as pl
from jax.experimental.pallas import tpu as pltpu

F32 = jnp.float32
BF16 = jnp.bfloat16

D_MODEL = 1024
W_HALF = 512
N_POOL_GROUPS = 4
POOL_HALF = (1, 2, 4, 8)
N_HEADS = 8
HEAD_DIM = 64
GRID_W = 64
WIN_H = 8
WIN_W = 16
CONV_C = 3
CONV_D = 31
EPS = 1e-6
MASKED = -1e30
LOG2_E = 1.4426950408889634
Q_SCALE = HEAD_DIM ** -0.5 * LOG2_E

LANES = 128
SUBLANES = 8
PAD = 16
ROW_CHUNK = 512
NORM_ROWS = 32
POOL_ROWS = 256
CONV_ROWS = 128
Q_ROWS = 128
NB_PROMPT = 2
NA_GROUP = 8
MOD_ROWS = 512
STAGE_ROWS = 128
STAGE_SLOTS = 4
VMEM_LIMIT = 58 * 1024 * 1024

assert PAD >= CONV_D // 2 + 1 and PAD % SUBLANES == 0 and PAD >= 2 * SUBLANES
assert max(POOL_HALF) <= SUBLANES


def _sigmoid(x):
    return 1.0 / (1.0 + jnp.exp(-x))


def _silu(x):
    return x * _sigmoid(x)


def _dot(a, b):
    return jnp.dot(a, b, preferred_element_type=F32)


def _dot_nt(a, b):
    return lax.dot_general(a, b, (((1,), (1,)), ((), ())), preferred_element_type=F32)


def _lanes(j):
    return slice(j * LANES, (j + 1) * LANES)


def _group(g):
    return slice(g * W_HALF, (g + 1) * W_HALF)


def _rows(start, size, align):
    if isinstance(start, int):
        return slice(start, start + size)
    return pl.ds(pl.multiple_of(start, align), size)


def _mod_body(cctx_ref, c_ref, w_ref, b_ref, o_ref, act_ref):
    layer, kb = pl.program_id(0), pl.program_id(1)
    rows, d = act_ref.shape

    @pl.when(kb == 0)
    def _():
        r = lax.broadcasted_iota(jnp.int32, (rows, d), 0)
        cond = jnp.where(r == 0, cctx_ref[...], 0.0)
        for i in range(c_ref.shape[0]):
            cond = jnp.where(r == i + 1, c_ref[i:i + 1, :], cond)
        act_ref[...] = _silu(cond).astype(BF16)
        o_ref[0] = jnp.broadcast_to(jnp.where(layer == 0, b_ref[0:1, :], b_ref[1:2, :]), o_ref.shape[1:])

    act = act_ref[:, pl.ds(pl.multiple_of(kb * MOD_ROWS, MOD_ROWS), MOD_ROWS)]
    o_ref[0] += _dot(act, w_ref[0].astype(BF16))


def _modulation(c_ctx, c, w_mod, b_mod, rows):
    depth, d, n = w_mod.shape
    assert depth == 2 and 1 + c.shape[0] <= rows and d % MOD_ROWS == 0
    return pl.pallas_call(
        _mod_body,
        out_shape=jax.ShapeDtypeStruct((depth, rows, n), F32),
        grid=(depth, d // MOD_ROWS),
        in_specs=[
            pl.BlockSpec((1, d), lambda l, k: (0, 0)),
            pl.BlockSpec(c.shape, lambda l, k: (0, 0)),
            pl.BlockSpec((1, MOD_ROWS, n), lambda l, k: (l, k, 0)),
            pl.BlockSpec((depth, n), lambda l, k: (0, 0)),
        ],
        out_specs=pl.BlockSpec((1, rows, n), lambda l, k: (l, 0, 0)),
        scratch_shapes=[pltpu.VMEM((rows, d), BF16)],
        compiler_params=pltpu.CompilerParams(dimension_semantics=("arbitrary", "arbitrary")),
        name="mod",
    )(c_ctx.reshape(1, d), c, w_mod, b_mod)


def _cond_row(m_ref, layer, row):
    if isinstance(row, int):
        return m_ref[layer, row:row + 1, :]
    m = m_ref[layer]
    keep = lax.broadcasted_iota(jnp.int32, m.shape, 0) == row
    return jnp.sum(jnp.where(keep, m, 0.0), axis=0, keepdims=True)


def _pieces(c, nb, t):
    if t >= ROW_CHUNK:
        per_seq = t // ROW_CHUNK
        s = 0 if nb == 1 else c // per_seq
        return [(s, (c - s * per_seq) * ROW_CHUNK, ROW_CHUNK, 0)]
    per_chunk = ROW_CHUNK // t
    return [(c * per_chunk + i, 0, t, i * t) for i in range(per_chunk)]


def _for_chunks(n, body):
    for c in range(n):
        body(c)


def _pad_row(s, off, t):
    return s * (t + 2 * PAD) + PAD + off


def _store_padded(pad_ref, val, pieces, t):
    for s, off, n, o in pieces:
        pad_ref[_rows(_pad_row(s, off, t), n, SUBLANES), :] = val[o:o + n]


def _scale_padded(pad_ref, val, pieces, t):
    for s, off, n, o in pieces:
        rows = _rows(_pad_row(s, off, t), n, SUBLANES)
        pad_ref[rows, :] = pad_ref[rows, :] * val[o:o + n]


def _modnorm_chunk(src_ref, h_ref, c, nb, t, gain, shift):
    for s, off, n, o in _pieces(c, nb, t):
        for i in range(0, n, NORM_ROWS):
            x = src_ref[s, _rows(off + i, NORM_ROWS, NORM_ROWS), :]
            ms = jnp.mean(x * x, axis=-1, keepdims=True)
            h_ref[_rows(c * ROW_CHUNK + o + i, NORM_ROWS, NORM_ROWS), :] = (
                x * lax.rsqrt(ms + EPS) * gain + shift).astype(BF16)


def _zero_pads(pad_ref, nb, t):
    z = jnp.zeros((PAD, W_HALF), F32)
    for s in range(nb):
        pad_ref[_pad_row(s, 0, t) - PAD:_pad_row(s, 0, t), :] = z
        pad_ref[_pad_row(s, t, t):_pad_row(s, t, t) + PAD, :] = z


def _pool_phase(pad_ref, ga_ref, wp_ref, ps_ref, ab_ref, nb, t):
    n_rows = POOL_ROWS
    per_seq = t // n_rows

    def step(i, carry):
        s = i // per_seq
        r0 = (i - s * per_seq) * n_rows
        prow = _pad_row(s, r0, t)
        rows = _rows(i * n_rows, n_rows, n_rows)
        pos = r0 + lax.broadcasted_iota(jnp.int32, (n_rows, LANES), 0)
        before = jnp.minimum(pos, SUBLANES)
        after = jnp.minimum(t - pos, SUBLANES)
        for g in range(N_POOL_GROUPS):
            hw = POOL_HALF[g]
            ln = _lanes(g)
            halo = n_rows + 2 * SUBLANES
            blk = pad_ref[_rows(prow - SUBLANES, halo, SUBLANES), ln]
            run, n = blk, 1
            while n < 2 * hw:
                run = run + pltpu.roll(run, halo - n, 0)
                n *= 2
            if hw < SUBLANES:
                run = pltpu.roll(run, halo - (SUBLANES - hw), 0)
            win = run[:n_rows]
            cnt = (jnp.minimum(before, hw) + jnp.minimum(after, hw)).astype(F32)
            p = (win / cnt - blk[SUBLANES:SUBLANES + n_rows]).astype(BF16)
            y = _dot(p, wp_ref[0, g].astype(BF16)) * ps_ref[:, ln] * ga_ref[rows, ln]
            ab_ref[rows, ln] = y.astype(BF16)
        return carry
    lax.fori_loop(0, nb * per_seq, step, 0)


def _out_proj_chunk(ab_ref, w_ref, x_ref, gate, dst_ref, c, nb, t):
    lhs = ab_ref[_rows(c * ROW_CHUNK, ROW_CHUNK, ROW_CHUNK), :]
    for g in range(D_MODEL // W_HALF):
        y = _dot(lhs, w_ref[:, _group(g)])
        for s, off, n, o in _pieces(c, nb, t):
            rows = _rows(off, n, n)
            dst_ref[s, rows, _group(g)] = x_ref[s, rows, _group(g)] + gate[:, _group(g)] * y[o:o + n]


def _shift_up(x, o, n):
    if o % SUBLANES == 0:
        return x[o:o + n]
    return pltpu.roll(x, x.shape[0] - o, 0)[:n]


def _conv_phase(pad_c, pad_d, bc_ref, ga_ref, gb_ref, cc_ref, cdw_ref, cdb_ref, lng_ref, lnb_ref,
                ab_ref, nb, t, only_seq=None):
    n_rows = CONV_ROWS
    per_seq = t // n_rows

    def step(i, carry):
        s = i // per_seq
        r0 = (i - s * per_seq) * n_rows
        prow = _pad_row(s, r0, t)
        rows = _rows(i * n_rows, n_rows, n_rows)
        z = []
        for g in range(W_HALF // LANES):
            ln = _lanes(g)
            blk = pad_c[_rows(prow - SUBLANES, n_rows + 2 * SUBLANES, SUBLANES), ln]
            c3 = None
            for j in range(CONV_C):
                o = SUBLANES + j - CONV_C // 2
                term = _shift_up(blk, o, n_rows) * cc_ref[j, :, ln]
                c3 = term if c3 is None else c3 + term
            ab_ref[rows, ln] = (bc_ref[rows, ln] * c3 * ga_ref[rows, ln]).astype(BF16)
            acc = None
            for sft in range(SUBLANES):
                part = None
                for a in range((CONV_D - sft + SUBLANES - 1) // SUBLANES):
                    j = SUBLANES * a + sft
                    src = pad_d[_rows(prow - 2 * SUBLANES + SUBLANES * a, n_rows + SUBLANES,
                                      SUBLANES), ln]
                    term = src * cdw_ref[j, :, ln]
                    part = term if part is None else part + term
                o = SUBLANES + sft - (CONV_D // 2 - SUBLANES)
                part = _shift_up(part, o, n_rows)
                acc = part if acc is None else acc + part
            z.append(acc + cdb_ref[:, ln])
        z = jnp.concatenate(z, axis=-1)
        mu = jnp.mean(z, axis=-1, keepdims=True)
        zc = z - mu
        var = jnp.mean(zc * zc, axis=-1, keepdims=True)
        zn = zc * lax.rsqrt(var + EPS) * lng_ref[...] + lnb_ref[...]
        ab_ref[rows, W_HALF:] = (_silu(zn) * gb_ref[rows, :]).astype(BF16)
        return carry
    if only_seq is None:
        lax.fori_loop(0, nb * per_seq, step, 0)
    else:
        for i in range(only_seq * per_seq, (only_seq + 1) * per_seq):
            step(i, 0)


def _final_norm_chunk(y_ref, fg, c, nb, t):
    for s, off, n, _ in _pieces(c, nb, t):
        for i in range(0, n, NORM_ROWS):
            rows = _rows(off + i, NORM_ROWS, NORM_ROWS)
            x = y_ref[s, rows, :]
            ms = jnp.mean(x * x, axis=-1, keepdims=True)
            y_ref[s, rows, :] = x * lax.rsqrt(ms + EPS) * fg


def _out_proj_sequence(ab_ref, w_ref, y_ref, gate, fg, sq, t):
    lhs = ab_ref[sq * t:(sq + 1) * t, :]
    for g in range(D_MODEL // W_HALF):
        y = _dot(lhs, w_ref[:, _group(g)])
        y_ref[sq, :, _group(g)] = y_ref[sq, :, _group(g)] + gate[:, _group(g)] * y
    for i in range(0, t, NORM_ROWS):
        x = y_ref[sq, i:i + NORM_ROWS, :]
        ms = jnp.mean(x * x, axis=-1, keepdims=True)
        y_ref[sq, i:i + NORM_ROWS, :] = x * lax.rsqrt(ms + EPS) * fg


def _odd_layer(y_ref, m_row, g_row, fg, wio_ref, cc_ref, cdw_ref, cdb_ref, lng_ref, lnb_ref, woo_ref,
               h_ref, pad_c, pad_d, bc_ref, ga_ref, gb_ref, ab_ref, nb, t):
    shift = m_row[:, :D_MODEL]
    gain = g_row * (1.0 + m_row[:, D_MODEL:2 * D_MODEL])
    gate = m_row[:, 2 * D_MODEL:]
    n_chunks = nb * t // ROW_CHUNK

    def in_proj(c):
        _modnorm_chunk(y_ref, h_ref, c, nb, t, gain, shift)
        rows = _rows(c * ROW_CHUNK, ROW_CHUNK, ROW_CHUNK)
        pieces = _pieces(c, nb, t)
        h = h_ref[rows, :]
        bc_ref[rows, :] = _dot(h, wio_ref[:, _group(0)])
        _store_padded(pad_c, _dot(h, wio_ref[:, _group(1)]), pieces, t)
        _scale_padded(pad_c, _dot(h, wio_ref[:, _group(2)]), pieces, t)
        ga_ref[rows, :] = _silu(_dot(h, wio_ref[:, _group(3)]))
        _store_padded(pad_d, _dot(h, wio_ref[:, _group(4)]), pieces, t)
        _scale_padded(pad_d, _sigmoid(_dot(h, wio_ref[:, _group(5)])), pieces, t)
        gb_ref[rows, :] = _silu(_dot(h, wio_ref[:, _group(6)]))
    _for_chunks(n_chunks, in_proj)

    conv_refs = (pad_c, pad_d, bc_ref, ga_ref, gb_ref, cc_ref, cdw_ref, cdb_ref, lng_ref, lnb_ref,
                 ab_ref)
    if t < ROW_CHUNK:
        for sq in range(nb):
            _conv_phase(*conv_refs, nb, t, only_seq=sq)
            _out_proj_sequence(ab_ref, woo_ref, y_ref, gate, fg, sq, t)
    else:
        _conv_phase(*conv_refs, nb, t)

        def out_proj(c):
            _out_proj_chunk(ab_ref, woo_ref, y_ref, gate, y_ref, c, nb, t)
            _final_norm_chunk(y_ref, fg, c, nb, t)
        _for_chunks(n_chunks, out_proj)


def _even_in_proj(x_ref, m_row, g_row, w_ref, h_ref, pad_a, ga_ref, gb_ref, q_ref, k_ref, v_ref,
                  kv_t, nb, t):
    shift = m_row[:, :D_MODEL]
    gain = g_row * (1.0 + m_row[:, D_MODEL:2 * D_MODEL])

    def in_proj(c):
        _modnorm_chunk(x_ref, h_ref, c, nb, t, gain, shift)
        rows = _rows(c * ROW_CHUNK, ROW_CHUNK, ROW_CHUNK)
        pieces = _pieces(c, nb, t)
        h = h_ref[rows, :]
        _store_padded(pad_a, _dot(h, w_ref[:, _group(0)]), pieces, t)
        ga_ref[rows, :] = _silu(_dot(h, w_ref[:, _group(1)]))
        q_ref[rows, :] = (_dot(h, w_ref[:, _group(2)]) * Q_SCALE).astype(BF16)
        for i, (dst, g) in enumerate(((k_ref, 3), (v_ref, 4))):
            if kv_t is None:
                dst[rows, :] = _dot(h, w_ref[:, _group(g)]).astype(BF16)
                continue
            acc = _dot_nt(kv_t[2 + i][...], h)
            dst[:, rows] = acc.astype(BF16)
            for s, off, n, o in pieces:
                for hd in range(N_HEADS):
                    kv_t[i][s, 0, hd, :, _rows(off, n, n)] = (
                        acc[hd * HEAD_DIM:(hd + 1) * HEAD_DIM, o:o + n])
        gb_ref[rows, :] = _silu(_dot(h, w_ref[:, _group(5)]))
    n_chunks = nb * t // ROW_CHUNK
    if n_chunks == 1:
        in_proj(0)
    else:
        lax.fori_loop(0, n_chunks, lambda c, carry: (in_proj(c), carry)[1], 0)


def _even_out_proj(x_ref, y_ref, m_row, w_ref, ab_ref, nb, t):
    gate = m_row[:, 2 * D_MODEL:]
    n_chunks = nb * t // ROW_CHUNK

    def out_proj(c, carry=0):
        _out_proj_chunk(ab_ref, w_ref, x_ref, gate, y_ref, c, nb, t)
        return carry
    if n_chunks == 1:
        out_proj(0)
    else:
        lax.fori_loop(0, n_chunks, out_proj, 0)


def _split_heads(x):
    lane = lax.broadcasted_iota(jnp.int32, (1, LANES), 1)
    first = jnp.where(lane < HEAD_DIM, 1.0, 0.0).astype(x.dtype)
    return jnp.concatenate([x * first, x * (1 - first)], axis=0)


def _merge_heads(o):
    n = o.shape[0] // 2
    lane = lax.broadcasted_iota(jnp.int32, (n, LANES), 1)
    return jnp.where(lane < HEAD_DIM, o[:n], o[n:])


def _context_attention(q_ref, kt_ref, vt_ref, gb_ref, ab_ref, nb, t):
    for s in range(nb):
        seq = slice(s * t, (s + 1) * t)
        for j in range(N_HEADS // 2):
            ln = _lanes(j)
            kp = kt_ref[ln, seq]
            vp = vt_ref[ln, seq]
            for r0 in range(0, t, Q_ROWS):
                rows = slice(s * t + r0, s * t + r0 + Q_ROWS)
                sc = _dot(_split_heads(q_ref[rows, ln]), kp)
                p = jnp.exp2(sc - jnp.max(sc, axis=-1, keepdims=True))
                o = _dot_nt(p.astype(BF16), vp) / jnp.sum(p, axis=-1, keepdims=True)
                ab_ref[rows, W_HALF + j * LANES:W_HALF + (j + 1) * LANES] = (
                    _merge_heads(o) * gb_ref[rows, ln]).astype(BF16)


def _stage_weights(w_hbm, w_bf, wkt_ref, wvt_ref, stage, sem_in):
    chunks = [(k, r0) for k in range(len(w_hbm)) for r0 in range(0, w_hbm[k].shape[1], STAGE_ROWS)]

    def fetch(i):
        k, r0 = chunks[i]
        cols = w_hbm[k].shape[2]
        slot = i % STAGE_SLOTS
        return pltpu.make_async_copy(w_hbm[k].at[0, pl.ds(r0, STAGE_ROWS), :],
                                     stage.at[slot, :, pl.ds(0, cols)], sem_in.at[slot])

    for i in range(min(STAGE_SLOTS - 1, len(chunks))):
        fetch(i).start()
    for i, (k, r0) in enumerate(chunks):
        if i + STAGE_SLOTS - 1 < len(chunks):
            fetch(i + STAGE_SLOTS - 1).start()
        fetch(i).wait()
        cols = w_hbm[k].shape[2]
        rows = slice(r0, r0 + STAGE_ROWS)
        slot = i % STAGE_SLOTS
        w_bf[k][rows, :] = stage[slot, :, 0:cols].astype(BF16)
        if k == 0:
            wkt_ref[:, rows] = stage[slot, :, _group(3)].T.astype(BF16)
            wvt_ref[:, rows] = stage[slot, :, _group(4)].T.astype(BF16)


def _prompt_body(x_ref, m_ref, ng_ref, fg_ref, wp_ref, ps_ref, cc_ref, cdw_ref, cdb_ref, lng_ref,
                 lnb_ref, wie_hbm, woe_hbm, wio_hbm, woo_hbm,
                 y_ref, ko_ref, vo_ref, wie_out, woe_out, wio_out, woo_out,
                 h_ref, pad_a, pad_b, ga_ref, gb_ref, bc_ref, q_ref, kt_ref, vt_ref, ab_ref,
                 wie_ref, woe_ref, wio_ref, woo_ref, wkt_ref, wvt_ref, stage, sem_in, sem_out,
                 *, nb, t):
    w_out = (wie_out, woe_out, wio_out, woo_out)
    w_bf = (wie_ref, woe_ref, wio_ref, woo_ref)

    def write_back(k):
        return pltpu.make_async_copy(w_bf[k], w_out[k], sem_out.at[k])

    @pl.when(pl.program_id(0) == 0)
    def _():
        _stage_weights((wie_hbm, woe_hbm, wio_hbm, woo_hbm), w_bf, wkt_ref, wvt_ref, stage, sem_in)
        for k in range(len(w_bf)):
            write_back(k).start()

    _zero_pads(pad_a, nb, t)
    _zero_pads(pad_b, nb, t)
    m_even = _cond_row(m_ref, 0, 0)
    _even_in_proj(x_ref, m_even, ng_ref[0:1, :], wie_ref, h_ref, pad_a, ga_ref, gb_ref,
                  q_ref, kt_ref, vt_ref, (ko_ref, vo_ref, wkt_ref, wvt_ref), nb, t)
    _pool_phase(pad_a, ga_ref, wp_ref, ps_ref, ab_ref, nb, t)
    _context_attention(q_ref, kt_ref, vt_ref, gb_ref, ab_ref, nb, t)
    _even_out_proj(x_ref, y_ref, m_even, woe_ref, ab_ref, nb, t)
    _odd_layer(y_ref, _cond_row(m_ref, 1, 0), ng_ref[1:2, :], fg_ref[...], wio_ref, cc_ref, cdw_ref,
               cdb_ref, lng_ref, lnb_ref, woo_ref, h_ref, pad_a, pad_b, bc_ref, ga_ref, gb_ref, ab_ref,
               nb, t)

    @pl.when(pl.program_id(0) == 0)
    def _():
        for k in range(len(w_bf)):
            write_back(k).wait()


def _rpb_rows(rpb_ref, e_ref):
    n = rpb_ref.shape[0] * rpb_ref.shape[1]
    lane = lax.broadcasted_iota(jnp.int32, (n, LANES), 1)
    i = jnp.where(lane < GRID_W, lane, lane - LANES)
    idx = jnp.clip(i, -(WIN_W - 1), WIN_W - 1) + (WIN_W - 1)
    rp = rpb_ref[...].reshape(n, rpb_ref.shape[2])
    e = jnp.zeros((n, LANES), F32)
    for d in range(2 * WIN_W - 1):
        e = jnp.where(idx == d, rp[:, d:d + 1], e)
    for r in range(n):
        e_ref[r] = e[r:r + 1, :]


N_DR = 2 * WIN_H - 1
PAIR_TILES = N_DR // 2


def _bias_tile_index(j, dr_lo):
    if isinstance(dr_lo, int):
        parity, half = dr_lo % 2, dr_lo // 2
    else:
        parity, half = dr_lo & 1, lax.shift_right_logical(dr_lo, 1)
    return (2 * j + parity) * PAIR_TILES + half


def _bias_tables(e_ref, bias_ref):
    q = lax.broadcasted_iota(jnp.int32, (GRID_W, LANES), 0)
    lane = lax.broadcasted_iota(jnp.int32, (GRID_W, LANES), 1)
    kw = jnp.where(lane < GRID_W, lane, lane - GRID_W)
    start = jnp.clip(q - WIN_W // 2, 0, GRID_W - WIN_W)
    col_ok = (kw >= start) & (kw < start + WIN_W)
    def per_tile(j, dr):
        for e in range(2):
            r_lo = dr * N_HEADS + 2 * j + e
            lo = jnp.broadcast_to(e_ref[r_lo], (GRID_W, LANES))
            hi = jnp.broadcast_to(e_ref[r_lo + N_HEADS], (GRID_W, LANES))
            lo = pltpu.roll(lo, 0, 1, stride=1, stride_axis=0)
            hi = pltpu.roll(hi, GRID_W, 1, stride=1, stride_axis=0)
            tile = jnp.where(lane < GRID_W, lo, hi)
            bias_ref[_bias_tile_index(j, dr), e * GRID_W:(e + 1) * GRID_W, :] = jnp.where(
                col_ok, tile * LOG2_E, MASKED)

    def per_pair(j, carry):
        lax.fori_loop(0, N_DR - 1, lambda dr, c: (per_tile(j, dr), c)[1], 0)
        return carry
    lax.fori_loop(0, N_HEADS // 2, per_pair, 0)


def _neighbourhood_attention(q_ref, k_ref, v_ref, ck_ref, cv_ref, bias_ref, kvc_ref, gb_ref, ab_ref, t):
    grid_h = t // GRID_W
    band = WIN_H * GRID_W
    def per_pair(j, carry_j):
        ln = pl.ds(pl.multiple_of(j * LANES, LANES), LANES)
        out_ln = pl.ds(pl.multiple_of(W_HALF + j * LANES, LANES), LANES)
        for i, src in enumerate((ck_ref, cv_ref)):
            kvc_ref[i] = jnp.concatenate([src[0, 0, 2 * j], src[0, 0, 2 * j + 1]],
                                         axis=0).astype(BF16)

        def per_group(g, carry):
            scored = []
            for u in range(NA_GROUP):
                r = g * NA_GROUP + u
                start = jnp.clip(r - WIN_H // 2, 0, grid_h - WIN_H)
                rows = _rows(r * GRID_W, GRID_W, GRID_W)
                keys = _rows(start * GRID_W, band, GRID_W)
                q2 = _split_heads(q_ref[rows, ln])
                dr0 = (WIN_H - 1) - (r - start)
                bias = jnp.concatenate([bias_ref[_bias_tile_index(j, dr0 + 2 * i)]
                                        for i in range(WIN_H // 2)], axis=-1)
                scored.append((rows, keys, _dot_nt(q2, k_ref[keys, ln]) + bias, _dot(q2, kvc_ref[0])))
            weighted = []
            for rows, keys, s_loc, s_ctx in scored:
                mx = jnp.maximum(jnp.max(s_loc, axis=-1, keepdims=True),
                                 jnp.max(s_ctx, axis=-1, keepdims=True))
                p_loc = jnp.exp2(s_loc - mx)
                p_ctx = jnp.exp2(s_ctx - mx)
                den = (jnp.sum(p_loc, axis=-1, keepdims=True)
                       + jnp.sum(p_ctx, axis=-1, keepdims=True))
                weighted.append((rows, keys, p_loc.astype(BF16), p_ctx.astype(BF16), den))
            for rows, keys, p_loc, p_ctx, den in weighted:
                o = (_dot(p_loc, v_ref[keys, ln]) + _dot_nt(p_ctx, kvc_ref[1])) / den
                ab_ref[rows, out_ln] = (_merge_heads(o) * gb_ref[rows, ln]).astype(BF16)
            return carry
        lax.fori_loop(0, grid_h // NA_GROUP, per_group, 0)
        return carry_j
    lax.fori_loop(0, N_HEADS // 2, per_pair, 0)


def _sample_body(x_ref, m_ref, ng_ref, fg_ref, wie_ref, wp_ref, ps_ref, woe_hbm, wio_hbm, cc_ref,
                 cdw_ref, cdb_ref, lng_ref, lnb_ref, woo_hbm, ck_ref, cv_ref, rpb_ref,
                 y_ref,
                 h_ref, pad_a, pad_b, ga_ref, gb_ref, bc_ref, q_ref, k_ref, v_ref, ab_ref,
                 e_ref, bias_ref, kvc_ref, woe_ref, wio_ref, woo_ref, sem_w, *, t):
    _zero_pads(pad_a, 1, t)
    _zero_pads(pad_b, 1, t)
    late = ((woe_hbm, woe_ref), (wio_hbm, wio_ref), (woo_hbm, woo_ref))

    def late_copy(i):
        return pltpu.make_async_copy(late[i][0], late[i][1], sem_w.at[i])

    first_step = pl.program_id(0) == 0

    @pl.when(first_step)
    def _():
        for i in range(len(late)):
            late_copy(i).start()
        _rpb_rows(rpb_ref, e_ref)
        _bias_tables(e_ref, bias_ref)

    cond = pl.program_id(0) + 1
    m_even = _cond_row(m_ref, 0, cond)
    _even_in_proj(x_ref, m_even, ng_ref[0:1, :], wie_ref, h_ref, pad_a, ga_ref, gb_ref,
                  q_ref, k_ref, v_ref, None, 1, t)
    _pool_phase(pad_a, ga_ref, wp_ref, ps_ref, ab_ref, 1, t)
    _neighbourhood_attention(q_ref, k_ref, v_ref, ck_ref, cv_ref, bias_ref, kvc_ref, gb_ref, ab_ref, t)
    pl.when(first_step)(lambda: late_copy(0).wait())
    _even_out_proj(x_ref, y_ref, m_even, woe_ref, ab_ref, 1, t)

    @pl.when(first_step)
    def _():
        late_copy(1).wait()
        late_copy(2).wait()

    _odd_layer(y_ref, _cond_row(m_ref, 1, cond), ng_ref[1:2, :], fg_ref[...], wio_ref, cc_ref, cdw_ref,
               cdb_ref, lng_ref, lnb_ref, woo_ref, h_ref, pad_a, pad_b, bc_ref, ga_ref, gb_ref, ab_ref,
               1, t)


def _const_spec(shape):
    zeros = (0,) * len(shape)
    return pl.BlockSpec(shape, lambda i: zeros, pipeline_mode=pl.Buffered(1))


def _stream_scratch(nb, t, kv_transposed):
    r = nb * t
    padded = nb * (t + 2 * PAD)
    kv = (W_HALF, r) if kv_transposed else (r, W_HALF)
    return [
        pltpu.VMEM((r, D_MODEL), BF16),
        pltpu.VMEM((padded, W_HALF), F32),
        pltpu.VMEM((padded, W_HALF), F32),
        pltpu.VMEM((r, W_HALF), F32),
        pltpu.VMEM((r, W_HALF), F32),
        pltpu.VMEM((r, W_HALF), F32),
        pltpu.VMEM((r, W_HALF), BF16),
        pltpu.VMEM(kv, BF16),
        pltpu.VMEM(kv, BF16),
        pltpu.VMEM((r, D_MODEL), BF16),
    ]


def _small_params(norm_g, final_g, w_pool, pool_scale, conv_c, conv_d, conv_d_b, ln_g, ln_b):
    return [norm_g, final_g.reshape(1, D_MODEL), w_pool, pool_scale, jnp.swapaxes(conv_c, 0, 1),
            jnp.swapaxes(conv_d, 0, 1), conv_d_b, ln_g, ln_b]


def kernel(x_prompt, x_sample, cache_k, cache_v, c, c_ctx, norm_g, w_mod, b_mod, w_in_even, w_pool,
           pool_scale, rpb, w_out_even, w_in_odd, conv_c, conv_d, conv_d_b, ln_g, ln_b, w_out_odd,
           final_g):
    batch, seq, d = x_prompt.shape
    dec_batch, dec_seq, _ = x_sample.shape
    assert d == D_MODEL and w_mod.shape[0] == 2 and w_in_even.shape[0] == 1 and w_in_odd.shape[0] == 1
    assert (NB_PROMPT * seq) % ROW_CHUNK == 0 and ROW_CHUNK % seq == 0 and seq % Q_ROWS == 0
    assert dec_seq % ROW_CHUNK == 0 and dec_seq // GRID_W >= WIN_H
    assert seq % POOL_ROWS == 0 and seq % CONV_ROWS == 0
    assert dec_seq % POOL_ROWS == 0 and dec_seq % CONV_ROWS == 0
    assert (dec_seq // GRID_W) % NA_GROUP == 0

    cond_rows = SUBLANES * ((1 + dec_batch + SUBLANES - 1) // SUBLANES)
    m = _modulation(c_ctx, c, w_mod, b_mod, cond_rows)
    m_spec = _const_spec(m.shape)

    small = _small_params(norm_g, final_g, w_pool, pool_scale, conv_c, conv_d, conv_d_b, ln_g, ln_b)
    small_specs = [_const_spec(a.shape) for a in small]
    w_f32 = (w_in_even, w_out_even, w_in_odd, w_out_odd)
    assert all(w.shape[0] == 1 and w.shape[1] % STAGE_ROWS == 0 for w in w_f32)
    any_spec = pl.BlockSpec(memory_space=pl.ANY)

    nb = NB_PROMPT
    assert batch % nb == 0
    kv_shape = jax.ShapeDtypeStruct((batch, 1, N_HEADS, HEAD_DIM, seq), F32)
    kv_spec = pl.BlockSpec((nb, 1, N_HEADS, HEAD_DIM, seq), lambda i: (i, 0, 0, 0, 0))
    y_prompt, new_kt, new_vt, wie, woe, wio, woo = pl.pallas_call(
        functools.partial(_prompt_body, nb=nb, t=seq),
        out_shape=(jax.ShapeDtypeStruct(x_prompt.shape, F32), kv_shape, kv_shape)
                  + tuple(jax.ShapeDtypeStruct(w.shape[1:], BF16) for w in w_f32),
        grid=(batch // nb,),
        in_specs=[pl.BlockSpec((nb, seq, d), lambda i: (i, 0, 0)), m_spec] + small_specs
                 + [any_spec] * len(w_f32),
        out_specs=(pl.BlockSpec((nb, seq, d), lambda i: (i, 0, 0)), kv_spec, kv_spec)
                  + (any_spec,) * len(w_f32),
        scratch_shapes=_stream_scratch(nb, seq, True)
                       + [pltpu.VMEM(w.shape[1:], BF16) for w in w_f32] + [
            pltpu.VMEM((W_HALF, d), BF16),
            pltpu.VMEM((W_HALF, d), BF16),
            pltpu.VMEM((STAGE_SLOTS, STAGE_ROWS, max(w.shape[2] for w in w_f32)), F32),
            pltpu.SemaphoreType.DMA((STAGE_SLOTS,)),
            pltpu.SemaphoreType.DMA((len(w_f32),)),
        ],
        compiler_params=pltpu.CompilerParams(dimension_semantics=("arbitrary",),
                                             vmem_limit_bytes=VMEM_LIMIT),
        name="prompt",
    )(x_prompt, m, *small, *w_f32)
    ng, fg, wp, ps, cc, cdw, cdb, lng, lnb = small
    w_args = [ng, fg, wie, wp, ps, woe, wio, cc, cdw, cdb, lng, lnb, woo]
    late_w = (woe, wio, woo)
    w_specs = [any_spec if any(a is w for w in late_w) else _const_spec(a.shape) for a in w_args]

    past = cache_k.shape[3]
    cache_spec = pl.BlockSpec((1, 1, N_HEADS, HEAD_DIM, past), lambda i: (i, 0, 0, 0, 0))
    rpb_t = jnp.swapaxes(rpb[0], 0, 1)
    y_sample = pl.pallas_call(
        functools.partial(_sample_body, t=dec_seq),
        out_shape=jax.ShapeDtypeStruct(x_sample.shape, F32),
        grid=(dec_batch,),
        in_specs=[pl.BlockSpec((1, dec_seq, d), lambda i: (i, 0, 0), pipeline_mode=pl.Buffered(1)),
                  m_spec] + w_specs
                 + [cache_spec, cache_spec, _const_spec(rpb_t.shape)],
        out_specs=pl.BlockSpec((1, dec_seq, d), lambda i: (i, 0, 0)),
        scratch_shapes=_stream_scratch(1, dec_seq, False) + [
            pltpu.VMEM((N_DR * N_HEADS, 1, LANES), F32),
            pltpu.VMEM((N_HEADS * PAIR_TILES, 2 * GRID_W, LANES), F32),
            pltpu.VMEM((2, LANES, past), BF16),
        ] + [pltpu.VMEM(w.shape, BF16) for w in late_w] + [pltpu.SemaphoreType.DMA((len(late_w),))],
        compiler_params=pltpu.CompilerParams(dimension_semantics=("arbitrary",),
                                             vmem_limit_bytes=VMEM_LIMIT),
        name="sample",
    )(x_sample, m, *w_args, jnp.swapaxes(cache_k, 3, 4), jnp.swapaxes(cache_v, 3, 4), rpb_t)

    return (y_prompt, y_sample, jnp.swapaxes(new_kt, 3, 4), jnp.swapaxes(new_vt, 3, 4))
```

```python
import functools

import jax
import jax.numpy as jnp
from jax import lax
from jax.experimental import pallas as pl
from jax.experimental.pallas import tpu as pltpu

F32 = jnp.float32
BF16 = jnp.bfloat16

D_MODEL = 1024
W_HALF = 512
N_POOL_GROUPS = 4
POOL_HALF = (1, 2, 4, 8)
N_HEADS = 8
HEAD_DIM = 64
GRID_W = 64
WIN_H = 8
WIN_W = 16
CONV_C = 3
CONV_D = 31
EPS = 1e-6
MASKED = -1e30
LOG2_E = 1.4426950408889634
Q_SCALE = HEAD_DIM ** -0.5 * LOG2_E

LANES = 128
SUBLANES = 8
PAD = 16
ROW_CHUNK = 512
NORM_ROWS = 32
POOL_ROWS = 256
CONV_ROWS = 128
Q_ROWS = 128
NB_PROMPT = 2
NA_GROUP = 8
MOD_ROWS = 512
STAGE_ROWS = 128
STAGE_SLOTS = 4
VMEM_LIMIT = 58 * 1024 * 1024

assert PAD >= CONV_D // 2 + 1 and PAD % SUBLANES == 0 and PAD >= 2 * SUBLANES
assert max(POOL_HALF) <= SUBLANES


def _sigmoid(x):
    return 1.0 / (1.0 + jnp.exp(-x))


def _silu(x):
    return x * _sigmoid(x)


def _dot(a, b):
    return jnp.dot(a, b, preferred_element_type=F32)


def _dot_nt(a, b):
    return lax.dot_general(a, b, (((1,), (1,)), ((), ())), preferred_element_type=F32)


def _lanes(j):
    return slice(j * LANES, (j + 1) * LANES)


def _group(g):
    return slice(g * W_HALF, (g + 1) * W_HALF)


def _rows(start, size, align):
    if isinstance(start, int):
        return slice(start, start + size)
    return pl.ds(pl.multiple_of(start, align), size)


def _mod_body(cctx_ref, c_ref, w_ref, b_ref, o_ref, act_ref):
    layer, kb = pl.program_id(0), pl.program_id(1)
    rows, d = act_ref.shape

    @pl.when(kb == 0)
    def _():
        r = lax.broadcasted_iota(jnp.int32, (rows, d), 0)
        cond = jnp.where(r == 0, cctx_ref[...], 0.0)
        for i in range(c_ref.shape[0]):
            cond = jnp.where(r == i + 1, c_ref[i:i + 1, :], cond)
        act_ref[...] = _silu(cond).astype(BF16)
        o_ref[0] = jnp.broadcast_to(jnp.where(layer == 0, b_ref[0:1, :], b_ref[1:2, :]), o_ref.shape[1:])

    act = act_ref[:, pl.ds(pl.multiple_of(kb * MOD_ROWS, MOD_ROWS), MOD_ROWS)]
    o_ref[0] += _dot(act, w_ref[0].astype(BF16))


def _modulation(c_ctx, c, w_mod, b_mod, rows):
    depth, d, n = w_mod.shape
    assert depth == 2 and 1 + c.shape[0] <= rows and d % MOD_ROWS == 0
    return pl.pallas_call(
        _mod_body,
        out_shape=jax.ShapeDtypeStruct((depth, rows, n), F32),
        grid=(depth, d // MOD_ROWS),
        in_specs=[
            pl.BlockSpec((1, d), lambda l, k: (0, 0)),
            pl.BlockSpec(c.shape, lambda l, k: (0, 0)),
            pl.BlockSpec((1, MOD_ROWS, n), lambda l, k: (l, k, 0)),
            pl.BlockSpec((depth, n), lambda l, k: (0, 0)),
        ],
        out_specs=pl.BlockSpec((1, rows, n), lambda l, k: (l, 0, 0)),
        scratch_shapes=[pltpu.VMEM((rows, d), BF16)],
        compiler_params=pltpu.CompilerParams(dimension_semantics=("arbitrary", "arbitrary")),
        name="mod",
    )(c_ctx.reshape(1, d), c, w_mod, b_mod)


def _cond_row(m_ref, layer, row):
    if isinstance(row, int):
        return m_ref[layer, row:row + 1, :]
    m = m_ref[layer]
    keep = lax.broadcasted_iota(jnp.int32, m.shape, 0) == row
    return jnp.sum(jnp.where(keep, m, 0.0), axis=0, keepdims=True)


def _pieces(c, nb, t):
    if t >= ROW_CHUNK:
        per_seq = t // ROW_CHUNK
        s = 0 if nb == 1 else c // per_seq
        return [(s, (c - s * per_seq) * ROW_CHUNK, ROW_CHUNK, 0)]
    per_chunk = ROW_CHUNK // t
    return [(c * per_chunk + i, 0, t, i * t) for i in range(per_chunk)]


def _for_chunks(n, body):
    for c in range(n):
        body(c)


def _pad_row(s, off, t):
    return s * (t + 2 * PAD) + PAD + off


def _store_padded(pad_ref, val, pieces, t):
    for s, off, n, o in pieces:
        pad_ref[_rows(_pad_row(s, off, t), n, SUBLANES), :] = val[o:o + n]


def _scale_padded(pad_ref, val, pieces, t):
    for s, off, n, o in pieces:
        rows = _rows(_pad_row(s, off, t), n, SUBLANES)
        pad_ref[rows, :] = pad_ref[rows, :] * val[o:o + n]


def _modnorm_chunk(src_ref, h_ref, c, nb, t, gain, shift):
    for s, off, n, o in _pieces(c, nb, t):
        for i in range(0, n, NORM_ROWS):
            x = src_ref[s, _rows(off + i, NORM_ROWS, NORM_ROWS), :]
            ms = jnp.mean(x * x, axis=-1, keepdims=True)
            h_ref[_rows(c * ROW_CHUNK + o + i, NORM_ROWS, NORM_ROWS), :] = (
                x * lax.rsqrt(ms + EPS) * gain + shift).astype(BF16)


def _zero_pads(pad_ref, nb, t):
    z = jnp.zeros((PAD, W_HALF), F32)
    for s in range(nb):
        pad_ref[_pad_row(s, 0, t) - PAD:_pad_row(s, 0, t), :] = z
        pad_ref[_pad_row(s, t, t):_pad_row(s, t, t) + PAD, :] = z


def _pool_phase(pad_ref, ga_ref, wp_ref, ps_ref, ab_ref, nb, t):
    n_rows = POOL_ROWS
    per_seq = t // n_rows

    def step(i, carry):
        s = i // per_seq
        r0 = (i - s * per_seq) * n_rows
        prow = _pad_row(s, r0, t)
        rows = _rows(i * n_rows, n_rows, n_rows)
        pos = r0 + lax.broadcasted_iota(jnp.int32, (n_rows, LANES), 0)
        before = jnp.minimum(pos, SUBLANES)
        after = jnp.minimum(t - pos, SUBLANES)
        for g in range(N_POOL_GROUPS):
            hw = POOL_HALF[g]
            ln = _lanes(g)
            halo = n_rows + 2 * SUBLANES
            blk = pad_ref[_rows(prow - SUBLANES, halo, SUBLANES), ln]
            run, n = blk, 1
            while n < 2 * hw:
                run = run + pltpu.roll(run, halo - n, 0)
                n *= 2
            if hw < SUBLANES:
                run = pltpu.roll(run, halo - (SUBLANES - hw), 0)
            win = run[:n_rows]
            cnt = (jnp.minimum(before, hw) + jnp.minimum(after, hw)).astype(F32)
            p = (win / cnt - blk[SUBLANES:SUBLANES + n_rows]).astype(BF16)
            y = _dot(p, wp_ref[0, g].astype(BF16)) * ps_ref[:, ln] * ga_ref[rows, ln]
            ab_ref[rows, ln] = y.astype(BF16)
        return carry
    lax.fori_loop(0, nb * per_seq, step, 0)


def _out_proj_chunk(ab_ref, w_ref, x_ref, gate, dst_ref, c, nb, t):
    lhs = ab_ref[_rows(c * ROW_CHUNK, ROW_CHUNK, ROW_CHUNK), :]
    for g in range(D_MODEL // W_HALF):
        y = _dot(lhs, w_ref[:, _group(g)])
        for s, off, n, o in _pieces(c, nb, t):
            rows = _rows(off, n, n)
            dst_ref[s, rows, _group(g)] = x_ref[s, rows, _group(g)] + gate[:, _group(g)] * y[o:o + n]


def _shift_up(x, o, n):
    if o % SUBLANES == 0:
        return x[o:o + n]
    return pltpu.roll(x, x.shape[0] - o, 0)[:n]


def _conv_phase(pad_c, pad_d, bc_ref, ga_ref, gb_ref, cc_ref, cdw_ref, cdb_ref, lng_ref, lnb_ref,
                ab_ref, nb, t, only_seq=None):
    n_rows = CONV_ROWS
    per_seq = t // n_rows

    def step(i, carry):
        s = i // per_seq
        r0 = (i - s * per_seq) * n_rows
        prow = _pad_row(s, r0, t)
        rows = _rows(i * n_rows, n_rows, n_rows)
        z = []
        for g in range(W_HALF // LANES):
            ln = _lanes(g)
            blk = pad_c[_rows(prow - SUBLANES, n_rows + 2 * SUBLANES, SUBLANES), ln]
            c3 = None
            for j in range(CONV_C):
                o = SUBLANES + j - CONV_C // 2
                term = _shift_up(blk, o, n_rows) * cc_ref[j, :, ln]
                c3 = term if c3 is None else c3 + term
            ab_ref[rows, ln] = (bc_ref[rows, ln] * c3 * ga_ref[rows, ln]).astype(BF16)
            acc = None
            for sft in range(SUBLANES):
                part = None
                for a in range((CONV_D - sft + SUBLANES - 1) // SUBLANES):
                    j = SUBLANES * a + sft
                    src = pad_d[_rows(prow - 2 * SUBLANES + SUBLANES * a, n_rows + SUBLANES,
                                      SUBLANES), ln]
                    term = src * cdw_ref[j, :, ln]
                    part = term if part is None else part + term
                o = SUBLANES + sft - (CONV_D // 2 - SUBLANES)
                part = _shift_up(part, o, n_rows)
                acc = part if acc is None else acc + part
            z.append(acc + cdb_ref[:, ln])
        z = jnp.concatenate(z, axis=-1)
        mu = jnp.mean(z, axis=-1, keepdims=True)
        zc = z - mu
        var = jnp.mean(zc * zc, axis=-1, keepdims=True)
        zn = zc * lax.rsqrt(var + EPS) * lng_ref[...] + lnb_ref[...]
        ab_ref[rows, W_HALF:] = (_silu(zn) * gb_ref[rows, :]).astype(BF16)
        return carry
    if only_seq is None:
        lax.fori_loop(0, nb * per_seq, step, 0)
    else:
        for i in range(only_seq * per_seq, (only_seq + 1) * per_seq):
            step(i, 0)


def _final_norm_chunk(y_ref, fg, c, nb, t):
    for s, off, n, _ in _pieces(c, nb, t):
        for i in range(0, n, NORM_ROWS):
            rows = _rows(off + i, NORM_ROWS, NORM_ROWS)
            x = y_ref[s, rows, :]
            ms = jnp.mean(x * x, axis=-1, keepdims=True)
            y_ref[s, rows, :] = x * lax.rsqrt(ms + EPS) * fg


def _out_proj_sequence(ab_ref, w_ref, y_ref, gate, fg, sq, t):
    lhs = ab_ref[sq * t:(sq + 1) * t, :]
    for g in range(D_MODEL // W_HALF):
        y = _dot(lhs, w_ref[:, _group(g)])
        y_ref[sq, :, _group(g)] = y_ref[sq, :, _group(g)] + gate[:, _group(g)] * y
    for i in range(0, t, NORM_ROWS):
        x = y_ref[sq, i:i + NORM_ROWS, :]
        ms = jnp.mean(x * x, axis=-1, keepdims=True)
        y_ref[sq, i:i + NORM_ROWS, :] = x * lax.rsqrt(ms + EPS) * fg


def _odd_layer(y_ref, m_row, g_row, fg, wio_ref, cc_ref, cdw_ref, cdb_ref, lng_ref, lnb_ref, woo_ref,
               h_ref, pad_c, pad_d, bc_ref, ga_ref, gb_ref, ab_ref, nb, t):
    shift = m_row[:, :D_MODEL]
    gain = g_row * (1.0 + m_row[:, D_MODEL:2 * D_MODEL])
    gate = m_row[:, 2 * D_MODEL:]
    n_chunks = nb * t // ROW_CHUNK

    def in_proj(c):
        _modnorm_chunk(y_ref, h_ref, c, nb, t, gain, shift)
        rows = _rows(c * ROW_CHUNK, ROW_CHUNK, ROW_CHUNK)
        pieces = _pieces(c, nb, t)
        h = h_ref[rows, :]
        bc_ref[rows, :] = _dot(h, wio_ref[:, _group(0)])
        _store_padded(pad_c, _dot(h, wio_ref[:, _group(1)]), pieces, t)
        _scale_padded(pad_c, _dot(h, wio_ref[:, _group(2)]), pieces, t)
        ga_ref[rows, :] = _silu(_dot(h, wio_ref[:, _group(3)]))
        _store_padded(pad_d, _dot(h, wio_ref[:, _group(4)]), pieces, t)
        _scale_padded(pad_d, _sigmoid(_dot(h, wio_ref[:, _group(5)])), pieces, t)
        gb_ref[rows, :] = _silu(_dot(h, wio_ref[:, _group(6)]))
    _for_chunks(n_chunks, in_proj)

    conv_refs = (pad_c, pad_d, bc_ref, ga_ref, gb_ref, cc_ref, cdw_ref, cdb_ref, lng_ref, lnb_ref,
                 ab_ref)
    if t < ROW_CHUNK:
        for sq in range(nb):
            _conv_phase(*conv_refs, nb, t, only_seq=sq)
            _out_proj_sequence(ab_ref, woo_ref, y_ref, gate, fg, sq, t)
    else:
        _conv_phase(*conv_refs, nb, t)

        def out_proj(c):
            _out_proj_chunk(ab_ref, woo_ref, y_ref, gate, y_ref, c, nb, t)
            _final_norm_chunk(y_ref, fg, c, nb, t)
        _for_chunks(n_chunks, out_proj)


def _even_in_proj(x_ref, m_row, g_row, w_ref, h_ref, pad_a, ga_ref, gb_ref, q_ref, k_ref, v_ref,
                  kv_t, nb, t):
    shift = m_row[:, :D_MODEL]
    gain = g_row * (1.0 + m_row[:, D_MODEL:2 * D_MODEL])

    def in_proj(c):
        _modnorm_chunk(x_ref, h_ref, c, nb, t, gain, shift)
        rows = _rows(c * ROW_CHUNK, ROW_CHUNK, ROW_CHUNK)
        pieces = _pieces(c, nb, t)
        h = h_ref[rows, :]
        _store_padded(pad_a, _dot(h, w_ref[:, _group(0)]), pieces, t)
        ga_ref[rows, :] = _silu(_dot(h, w_ref[:, _group(1)]))
        q_ref[rows, :] = (_dot(h, w_ref[:, _group(2)]) * Q_SCALE).astype(BF16)
        for i, (dst, g) in enumerate(((k_ref, 3), (v_ref, 4))):
            if kv_t is None:
                dst[rows, :] = _dot(h, w_ref[:, _group(g)]).astype(BF16)
                continue
            acc = _dot_nt(kv_t[2 + i][...], h)
            dst[:, rows] = acc.astype(BF16)
            for s, off, n, o in pieces:
                for hd in range(N_HEADS):
                    kv_t[i][s, 0, hd, :, _rows(off, n, n)] = (
                        acc[hd * HEAD_DIM:(hd + 1) * HEAD_DIM, o:o + n])
        gb_ref[rows, :] = _silu(_dot(h, w_ref[:, _group(5)]))
    _for_chunks(nb * t // ROW_CHUNK, in_proj)


def _even_out_proj(x_ref, y_ref, m_row, w_ref, ab_ref, nb, t):
    gate = m_row[:, 2 * D_MODEL:]
    n_chunks = nb * t // ROW_CHUNK

    def out_proj(c, carry=0):
        _out_proj_chunk(ab_ref, w_ref, x_ref, gate, y_ref, c, nb, t)
        return carry
    if n_chunks == 1:
        out_proj(0)
    else:
        lax.fori_loop(0, n_chunks, out_proj, 0)


def _split_heads(x):
    lane = lax.broadcasted_iota(jnp.int32, (1, LANES), 1)
    first = jnp.where(lane < HEAD_DIM, 1.0, 0.0).astype(x.dtype)
    return jnp.concatenate([x * first, x * (1 - first)], axis=0)


def _merge_heads(o):
    n = o.shape[0] // 2
    lane = lax.broadcasted_iota(jnp.int32, (n, LANES), 1)
    return jnp.where(lane < HEAD_DIM, o[:n], o[n:])


def _context_attention(q_ref, kt_ref, vt_ref, gb_ref, ab_ref, nb, t):
    for s in range(nb):
        seq = slice(s * t, (s + 1) * t)
        for j in range(N_HEADS // 2):
            ln = _lanes(j)
            kp = kt_ref[ln, seq]
            vp = vt_ref[ln, seq]
            for r0 in range(0, t, Q_ROWS):
                rows = slice(s * t + r0, s * t + r0 + Q_ROWS)
                sc = _dot(_split_heads(q_ref[rows, ln]), kp)
                p = jnp.exp2(sc - jnp.max(sc, axis=-1, keepdims=True))
                o = _dot_nt(p.astype(BF16), vp) / jnp.sum(p, axis=-1, keepdims=True)
                ab_ref[rows, W_HALF + j * LANES:W_HALF + (j + 1) * LANES] = (
                    _merge_heads(o) * gb_ref[rows, ln]).astype(BF16)


def _stage_weights(w_hbm, w_bf, wkt_ref, wvt_ref, stage, sem_in):
    chunks = [(k, r0) for k in range(len(w_hbm)) for r0 in range(0, w_hbm[k].shape[1], STAGE_ROWS)]

    def fetch(i):
        k, r0 = chunks[i]
        cols = w_hbm[k].shape[2]
        slot = i % STAGE_SLOTS
        return pltpu.make_async_copy(w_hbm[k].at[0, pl.ds(r0, STAGE_ROWS), :],
                                     stage.at[slot, :, pl.ds(0, cols)], sem_in.at[slot])

    for i in range(min(STAGE_SLOTS - 1, len(chunks))):
        fetch(i).start()
    for i, (k, r0) in enumerate(chunks):
        if i + STAGE_SLOTS - 1 < len(chunks):
            fetch(i + STAGE_SLOTS - 1).start()
        fetch(i).wait()
        cols = w_hbm[k].shape[2]
        rows = slice(r0, r0 + STAGE_ROWS)
        slot = i % STAGE_SLOTS
        w_bf[k][rows, :] = stage[slot, :, 0:cols].astype(BF16)
        if k == 0:
            wkt_ref[:, rows] = stage[slot, :, _group(3)].T.astype(BF16)
            wvt_ref[:, rows] = stage[slot, :, _group(4)].T.astype(BF16)


def _prompt_body(x_ref, m_ref, ng_ref, fg_ref, wp_ref, ps_ref, cc_ref, cdw_ref, cdb_ref, lng_ref,
                 lnb_ref, wie_hbm, woe_hbm, wio_hbm, woo_hbm,
                 y_ref, ko_ref, vo_ref, wie_out, woe_out, wio_out, woo_out,
                 h_ref, pad_a, pad_b, ga_ref, gb_ref, bc_ref, q_ref, kt_ref, vt_ref, ab_ref,
                 wie_ref, woe_ref, wio_ref, woo_ref, wkt_ref, wvt_ref, stage, sem_in, sem_out,
                 *, nb, t):
    w_out = (wie_out, woe_out, wio_out, woo_out)
    w_bf = (wie_ref, woe_ref, wio_ref, woo_ref)

    def write_back(k):
        return pltpu.make_async_copy(w_bf[k], w_out[k], sem_out.at[k])

    @pl.when(pl.program_id(0) == 0)
    def _():
        _stage_weights((wie_hbm, woe_hbm, wio_hbm, woo_hbm), w_bf, wkt_ref, wvt_ref, stage, sem_in)
        for k in range(len(w_bf)):
            write_back(k).start()

    _zero_pads(pad_a, nb, t)
    _zero_pads(pad_b, nb, t)
    m_even = _cond_row(m_ref, 0, 0)
    _even_in_proj(x_ref, m_even, ng_ref[0:1, :], wie_ref, h_ref, pad_a, ga_ref, gb_ref,
                  q_ref, kt_ref, vt_ref, (ko_ref, vo_ref, wkt_ref, wvt_ref), nb, t)
    _pool_phase(pad_a, ga_ref, wp_ref, ps_ref, ab_ref, nb, t)
    _context_attention(q_ref, kt_ref, vt_ref, gb_ref, ab_ref, nb, t)
    _even_out_proj(x_ref, y_ref, m_even, woe_ref, ab_ref, nb, t)
    _odd_layer(y_ref, _cond_row(m_ref, 1, 0), ng_ref[1:2, :], fg_ref[...], wio_ref, cc_ref, cdw_ref,
               cdb_ref, lng_ref, lnb_ref, woo_ref, h_ref, pad_a, pad_b, bc_ref, ga_ref, gb_ref, ab_ref,
               nb, t)

    @pl.when(pl.program_id(0) == 0)
    def _():
        for k in range(len(w_bf)):
            write_back(k).wait()


def _rpb_rows(rpb_ref, e_ref):
    n = rpb_ref.shape[0] * rpb_ref.shape[1]
    lane = lax.broadcasted_iota(jnp.int32, (n, LANES), 1)
    i = jnp.where(lane < GRID_W, lane, lane - LANES)
    idx = jnp.clip(i, -(WIN_W - 1), WIN_W - 1) + (WIN_W - 1)
    rp = rpb_ref[...].reshape(n, rpb_ref.shape[2])
    e = jnp.zeros((n, LANES), F32)
    for d in range(2 * WIN_W - 1):
        e = jnp.where(idx == d, rp[:, d:d + 1], e)
    for r in range(n):
        e_ref[r] = e[r:r + 1, :]


N_DR = 2 * WIN_H - 1
PAIR_TILES = N_DR // 2


def _bias_tile_index(j, dr_lo):
    if isinstance(dr_lo, int):
        parity, half = dr_lo % 2, dr_lo // 2
    else:
        parity, half = dr_lo & 1, lax.shift_right_logical(dr_lo, 1)
    return (2 * j + parity) * PAIR_TILES + half


def _bias_tables(e_ref, bias_ref):
    q = lax.broadcasted_iota(jnp.int32, (GRID_W, LANES), 0)
    lane = lax.broadcasted_iota(jnp.int32, (GRID_W, LANES), 1)
    kw = jnp.where(lane < GRID_W, lane, lane - GRID_W)
    start = jnp.clip(q - WIN_W // 2, 0, GRID_W - WIN_W)
    col_ok = (kw >= start) & (kw < start + WIN_W)
    def per_tile(j, dr):
        for e in range(2):
            r_lo = dr * N_HEADS + 2 * j + e
            lo = jnp.broadcast_to(e_ref[r_lo], (GRID_W, LANES))
            hi = jnp.broadcast_to(e_ref[r_lo + N_HEADS], (GRID_W, LANES))
            lo = pltpu.roll(lo, 0, 1, stride=1, stride_axis=0)
            hi = pltpu.roll(hi, GRID_W, 1, stride=1, stride_axis=0)
            tile = jnp.where(lane < GRID_W, lo, hi)
            bias_ref[_bias_tile_index(j, dr), e * GRID_W:(e + 1) * GRID_W, :] = jnp.where(
                col_ok, tile * LOG2_E, MASKED)

    def per_family(f, carry):
        j, parity = lax.shift_right_logical(f, 1), f & 1
        for m in range(PAIR_TILES):
            per_tile(j, 2 * m + parity)
        return carry
    lax.fori_loop(0, N_HEADS, per_family, 0)


def _neighbourhood_attention(q_ref, k_ref, v_ref, ck_ref, cv_ref, bias_ref, kvc_ref, gb_ref, ab_ref, t):
    grid_h = t // GRID_W
    band = WIN_H * GRID_W
    def per_pair(j, carry_j):
        ln = pl.ds(pl.multiple_of(j * LANES, LANES), LANES)
        out_ln = pl.ds(pl.multiple_of(W_HALF + j * LANES, LANES), LANES)
        for i, src in enumerate((ck_ref, cv_ref)):
            kvc_ref[i] = jnp.concatenate([src[0, 0, 2 * j], src[0, 0, 2 * j + 1]],
                                         axis=0).astype(BF16)

        def per_group(g, carry):
            scored = []
            for u in range(NA_GROUP):
                r = g * NA_GROUP + u
                start = jnp.clip(r - WIN_H // 2, 0, grid_h - WIN_H)
                rows = _rows(r * GRID_W, GRID_W, GRID_W)
                keys = _rows(start * GRID_W, band, GRID_W)
                q2 = _split_heads(q_ref[rows, ln])
                dr0 = (WIN_H - 1) - (r - start)
                bias = jnp.concatenate([bias_ref[_bias_tile_index(j, dr0 + 2 * i)]
                                        for i in range(WIN_H // 2)], axis=-1)
                scored.append((rows, keys, _dot_nt(q2, k_ref[keys, ln]) + bias, _dot(q2, kvc_ref[0])))
            weighted = []
            for rows, keys, s_loc, s_ctx in scored:
                mx = jnp.maximum(jnp.max(s_loc, axis=-1, keepdims=True),
                                 jnp.max(s_ctx, axis=-1, keepdims=True))
                p_loc = jnp.exp2(s_loc - mx)
                p_ctx = jnp.exp2(s_ctx - mx)
                den = (jnp.sum(p_loc, axis=-1, keepdims=True)
                       + jnp.sum(p_ctx, axis=-1, keepdims=True))
                weighted.append((rows, keys, p_loc.astype(BF16), p_ctx.astype(BF16), den))
            for rows, keys, p_loc, p_ctx, den in weighted:
                o = (_dot(p_loc, v_ref[keys, ln]) + _dot_nt(p_ctx, kvc_ref[1])) / den
                ab_ref[rows, out_ln] = (_merge_heads(o) * gb_ref[rows, ln]).astype(BF16)
            return carry
        lax.fori_loop(0, grid_h // NA_GROUP, per_group, 0)
        return carry_j
    lax.fori_loop(0, N_HEADS // 2, per_pair, 0)


def _sample_body(x_ref, m_ref, ng_ref, fg_ref, wie_ref, wp_ref, ps_ref, woe_hbm, wio_hbm, cc_ref,
                 cdw_ref, cdb_ref, lng_ref, lnb_ref, woo_hbm, ck_ref, cv_ref, rpb_ref,
                 y_ref,
                 h_ref, pad_a, pad_b, ga_ref, gb_ref, bc_ref, q_ref, k_ref, v_ref, ab_ref,
                 e_ref, bias_ref, kvc_ref, woe_ref, wio_ref, woo_ref, sem_w, *, t):
    _zero_pads(pad_a, 1, t)
    _zero_pads(pad_b, 1, t)
    late = ((woe_hbm, woe_ref), (wio_hbm, wio_ref), (woo_hbm, woo_ref))

    def late_copy(i):
        return pltpu.make_async_copy(late[i][0], late[i][1], sem_w.at[i])

    first_step = pl.program_id(0) == 0

    @pl.when(first_step)
    def _():
        for i in range(len(late)):
            late_copy(i).start()
        _rpb_rows(rpb_ref, e_ref)
        _bias_tables(e_ref, bias_ref)

    cond = pl.program_id(0) + 1
    m_even = _cond_row(m_ref, 0, cond)
    _even_in_proj(x_ref, m_even, ng_ref[0:1, :], wie_ref, h_ref, pad_a, ga_ref, gb_ref,
                  q_ref, k_ref, v_ref, None, 1, t)
    _pool_phase(pad_a, ga_ref, wp_ref, ps_ref, ab_ref, 1, t)
    _neighbourhood_attention(q_ref, k_ref, v_ref, ck_ref, cv_ref, bias_ref, kvc_ref, gb_ref, ab_ref, t)
    pl.when(first_step)(lambda: late_copy(0).wait())
    _even_out_proj(x_ref, y_ref, m_even, woe_ref, ab_ref, 1, t)

    @pl.when(first_step)
    def _():
        late_copy(1).wait()
        late_copy(2).wait()

    _odd_layer(y_ref, _cond_row(m_ref, 1, cond), ng_ref[1:2, :], fg_ref[...], wio_ref, cc_ref, cdw_ref,
               cdb_ref, lng_ref, lnb_ref, woo_ref, h_ref, pad_a, pad_b, bc_ref, ga_ref, gb_ref, ab_ref,
               1, t)


def _const_spec(shape):
    zeros = (0,) * len(shape)
    return pl.BlockSpec(shape, lambda i: zeros, pipeline_mode=pl.Buffered(1))


def _stream_scratch(nb, t, kv_transposed):
    r = nb * t
    padded = nb * (t + 2 * PAD)
    kv = (W_HALF, r) if kv_transposed else (r, W_HALF)
    return [
        pltpu.VMEM((r, D_MODEL), BF16),
        pltpu.VMEM((padded, W_HALF), F32),
        pltpu.VMEM((padded, W_HALF), F32),
        pltpu.VMEM((r, W_HALF), F32),
        pltpu.VMEM((r, W_HALF), F32),
        pltpu.VMEM((r, W_HALF), F32),
        pltpu.VMEM((r, W_HALF), BF16),
        pltpu.VMEM(kv, BF16),
        pltpu.VMEM(kv, BF16),
        pltpu.VMEM((r, D_MODEL), BF16),
    ]


def _small_params(norm_g, final_g, w_pool, pool_scale, conv_c, conv_d, conv_d_b, ln_g, ln_b):
    return [norm_g, final_g.reshape(1, D_MODEL), w_pool, pool_scale, jnp.swapaxes(conv_c, 0, 1),
            jnp.swapaxes(conv_d, 0, 1), conv_d_b, ln_g, ln_b]


def kernel(x_prompt, x_sample, cache_k, cache_v, c, c_ctx, norm_g, w_mod, b_mod, w_in_even, w_pool,
           pool_scale, rpb, w_out_even, w_in_odd, conv_c, conv_d, conv_d_b, ln_g, ln_b, w_out_odd,
           final_g):
    batch, seq, d = x_prompt.shape
    dec_batch, dec_seq, _ = x_sample.shape
    assert d == D_MODEL and w_mod.shape[0] == 2 and w_in_even.shape[0] == 1 and w_in_odd.shape[0] == 1
    assert (NB_PROMPT * seq) % ROW_CHUNK == 0 and ROW_CHUNK % seq == 0 and seq % Q_ROWS == 0
    assert dec_seq % ROW_CHUNK == 0 and dec_seq // GRID_W >= WIN_H
    assert seq % POOL_ROWS == 0 and seq % CONV_ROWS == 0
    assert dec_seq % POOL_ROWS == 0 and dec_seq % CONV_ROWS == 0
    assert (dec_seq // GRID_W) % NA_GROUP == 0

    cond_rows = SUBLANES * ((1 + dec_batch + SUBLANES - 1) // SUBLANES)
    m = _modulation(c_ctx, c, w_mod, b_mod, cond_rows)
    m_spec = _const_spec(m.shape)

    small = _small_params(norm_g, final_g, w_pool, pool_scale, conv_c, conv_d, conv_d_b, ln_g, ln_b)
    small_specs = [_const_spec(a.shape) for a in small]
    w_f32 = (w_in_even, w_out_even, w_in_odd, w_out_odd)
    assert all(w.shape[0] == 1 and w.shape[1] % STAGE_ROWS == 0 for w in w_f32)
    any_spec = pl.BlockSpec(memory_space=pl.ANY)

    nb = NB_PROMPT
    assert batch % nb == 0
    kv_shape = jax.ShapeDtypeStruct((batch, 1, N_HEADS, HEAD_DIM, seq), F32)
    kv_spec = pl.BlockSpec((nb, 1, N_HEADS, HEAD_DIM, seq), lambda i: (i, 0, 0, 0, 0))
    y_prompt, new_kt, new_vt, wie, woe, wio, woo = pl.pallas_call(
        functools.partial(_prompt_body, nb=nb, t=seq),
        out_shape=(jax.ShapeDtypeStruct(x_prompt.shape, F32), kv_shape, kv_shape)
                  + tuple(jax.ShapeDtypeStruct(w.shape[1:], BF16) for w in w_f32),
        grid=(batch // nb,),
        in_specs=[pl.BlockSpec((nb, seq, d), lambda i: (i, 0, 0)), m_spec] + small_specs
                 + [any_spec] * len(w_f32),
        out_specs=(pl.BlockSpec((nb, seq, d), lambda i: (i, 0, 0)), kv_spec, kv_spec)
                  + (any_spec,) * len(w_f32),
        scratch_shapes=_stream_scratch(nb, seq, True)
                       + [pltpu.VMEM(w.shape[1:], BF16) for w in w_f32] + [
            pltpu.VMEM((W_HALF, d), BF16),
            pltpu.VMEM((W_HALF, d), BF16),
            pltpu.VMEM((STAGE_SLOTS, STAGE_ROWS, max(w.shape[2] for w in w_f32)), F32),
            pltpu.SemaphoreType.DMA((STAGE_SLOTS,)),
            pltpu.SemaphoreType.DMA((len(w_f32),)),
        ],
        compiler_params=pltpu.CompilerParams(dimension_semantics=("arbitrary",),
                                             vmem_limit_bytes=VMEM_LIMIT),
        name="prompt",
    )(x_prompt, m, *small, *w_f32)
    ng, fg, wp, ps, cc, cdw, cdb, lng, lnb = small
    w_args = [ng, fg, wie, wp, ps, woe, wio, cc, cdw, cdb, lng, lnb, woo]
    late_w = (woe, wio, woo)
    w_specs = [any_spec if any(a is w for w in late_w) else _const_spec(a.shape) for a in w_args]

    past = cache_k.shape[3]
    cache_spec = pl.BlockSpec((1, 1, N_HEADS, HEAD_DIM, past), lambda i: (i, 0, 0, 0, 0))
    rpb_t = jnp.swapaxes(rpb[0], 0, 1)
    y_sample = pl.pallas_call(
        functools.partial(_sample_body, t=dec_seq),
        out_shape=jax.ShapeDtypeStruct(x_sample.shape, F32),
        grid=(dec_batch,),
        in_specs=[pl.BlockSpec((1, dec_seq, d), lambda i: (i, 0, 0), pipeline_mode=pl.Buffered(1)),
                  m_spec] + w_specs
                 + [cache_spec, cache_spec, _const_spec(rpb_t.shape)],
        out_specs=pl.BlockSpec((1, dec_seq, d), lambda i: (i, 0, 0)),
        scratch_shapes=_stream_scratch(1, dec_seq, False) + [
            pltpu.VMEM((N_DR * N_HEADS, 1, LANES), F32),
            pltpu.VMEM((N_HEADS * PAIR_TILES, 2 * GRID_W, LANES), F32),
            pltpu.VMEM((2, LANES, past), BF16),
        ] + [pltpu.VMEM(w.shape, BF16) for w in late_w] + [pltpu.SemaphoreType.DMA((len(late_w),))],
        compiler_params=pltpu.CompilerParams(dimension_semantics=("arbitrary",),
                                             vmem_limit_bytes=VMEM_LIMIT),
        name="sample",
    )(x_sample, m, *w_args, jnp.swapaxes(cache_k, 3, 4), jnp.swapaxes(cache_v, 3, 4), rpb_t)

    return (y_prompt, y_sample, jnp.swapaxes(new_kt, 3, 4), jnp.swapaxes(new_vt, 3, 4))
```

```python
import functools

import jax
import jax.numpy as jnp
from jax import lax
from jax.experimental import pallas as pl
from jax.experimental.pallas import tpu as pltpu

F32 = jnp.float32
BF16 = jnp.bfloat16

D_MODEL = 1024
W_HALF = 512
N_POOL_GROUPS = 4
POOL_HALF = (1, 2, 4, 8)
N_HEADS = 8
HEAD_DIM = 64
GRID_W = 64
WIN_H = 8
WIN_W = 16
CONV_C = 3
CONV_D = 31
EPS = 1e-6
MASKED = -1e30
LOG2_E = 1.4426950408889634
Q_SCALE = HEAD_DIM ** -0.5 * LOG2_E

LANES = 128
SUBLANES = 8
PAD = 16
ROW_CHUNK = 512
NORM_ROWS = 32
POOL_ROWS = 256
CONV_ROWS = 128
Q_ROWS = 128
NB_PROMPT = 2
NA_GROUP = 8
MOD_ROWS = 512
STAGE_ROWS = 128
STAGE_SLOTS = 4
VMEM_LIMIT = 58 * 1024 * 1024

assert PAD >= CONV_D // 2 + 1 and PAD % SUBLANES == 0 and PAD >= 2 * SUBLANES
assert max(POOL_HALF) <= SUBLANES


def _sigmoid(x):
    return 1.0 / (1.0 + jnp.exp(-x))


def _silu(x):
    return x * _sigmoid(x)


def _dot(a, b):
    return jnp.dot(a, b, preferred_element_type=F32)


def _dot_nt(a, b):
    return lax.dot_general(a, b, (((1,), (1,)), ((), ())), preferred_element_type=F32)


def _lanes(j):
    return slice(j * LANES, (j + 1) * LANES)


def _group(g):
    return slice(g * W_HALF, (g + 1) * W_HALF)


def _rows(start, size, align):
    if isinstance(start, int):
        return slice(start, start + size)
    return pl.ds(pl.multiple_of(start, align), size)


def _mod_body(cctx_ref, c_ref, w_ref, b_ref, o_ref, act_ref):
    layer, kb = pl.program_id(0), pl.program_id(1)
    rows, d = act_ref.shape

    @pl.when(kb == 0)
    def _():
        r = lax.broadcasted_iota(jnp.int32, (rows, d), 0)
        cond = jnp.where(r == 0, cctx_ref[...], 0.0)
        for i in range(c_ref.shape[0]):
            cond = jnp.where(r == i + 1, c_ref[i:i + 1, :], cond)
        act_ref[...] = _silu(cond).astype(BF16)
        o_ref[0] = jnp.broadcast_to(jnp.where(layer == 0, b_ref[0:1, :], b_ref[1:2, :]), o_ref.shape[1:])

    act = act_ref[:, pl.ds(pl.multiple_of(kb * MOD_ROWS, MOD_ROWS), MOD_ROWS)]
    o_ref[0] += _dot(act, w_ref[0].astype(BF16))


def _modulation(c_ctx, c, w_mod, b_mod, rows):
    depth, d, n = w_mod.shape
    assert depth == 2 and 1 + c.shape[0] <= rows and d % MOD_ROWS == 0
    return pl.pallas_call(
        _mod_body,
        out_shape=jax.ShapeDtypeStruct((depth, rows, n), F32),
        grid=(depth, d // MOD_ROWS),
        in_specs=[
            pl.BlockSpec((1, d), lambda l, k: (0, 0)),
            pl.BlockSpec(c.shape, lambda l, k: (0, 0)),
            pl.BlockSpec((1, MOD_ROWS, n), lambda l, k: (l, k, 0)),
            pl.BlockSpec((depth, n), lambda l, k: (0, 0)),
        ],
        out_specs=pl.BlockSpec((1, rows, n), lambda l, k: (l, 0, 0)),
        scratch_shapes=[pltpu.VMEM((rows, d), BF16)],
        compiler_params=pltpu.CompilerParams(dimension_semantics=("arbitrary", "arbitrary")),
        name="mod",
    )(c_ctx.reshape(1, d), c, w_mod, b_mod)


def _cond_row(m_ref, layer, row):
    if isinstance(row, int):
        return m_ref[layer, row:row + 1, :]
    m = m_ref[layer]
    keep = lax.broadcasted_iota(jnp.int32, m.shape, 0) == row
    return jnp.sum(jnp.where(keep, m, 0.0), axis=0, keepdims=True)


def _pieces(c, nb, t):
    if t >= ROW_CHUNK:
        per_seq = t // ROW_CHUNK
        s = 0 if nb == 1 else c // per_seq
        return [(s, (c - s * per_seq) * ROW_CHUNK, ROW_CHUNK, 0)]
    per_chunk = ROW_CHUNK // t
    return [(c * per_chunk + i, 0, t, i * t) for i in range(per_chunk)]


def _for_chunks(n, body):
    for c in range(n):
        body(c)


def _pad_row(s, off, t):
    return s * (t + 2 * PAD) + PAD + off


def _store_padded(pad_ref, val, pieces, t):
    for s, off, n, o in pieces:
        pad_ref[_rows(_pad_row(s, off, t), n, SUBLANES), :] = val[o:o + n]


def _scale_padded(pad_ref, val, pieces, t):
    for s, off, n, o in pieces:
        rows = _rows(_pad_row(s, off, t), n, SUBLANES)
        pad_ref[rows, :] = pad_ref[rows, :] * val[o:o + n]


def _modnorm_chunk(src_ref, h_ref, c, nb, t, gain, shift):
    for s, off, n, o in _pieces(c, nb, t):
        for i in range(0, n, NORM_ROWS):
            x = src_ref[s, _rows(off + i, NORM_ROWS, NORM_ROWS), :]
            ms = jnp.mean(x * x, axis=-1, keepdims=True)
            h_ref[_rows(c * ROW_CHUNK + o + i, NORM_ROWS, NORM_ROWS), :] = (
                x * lax.rsqrt(ms + EPS) * gain + shift).astype(BF16)


def _zero_pads(pad_ref, nb, t):
    z = jnp.zeros((PAD, W_HALF), F32)
    for s in range(nb):
        pad_ref[_pad_row(s, 0, t) - PAD:_pad_row(s, 0, t), :] = z
        pad_ref[_pad_row(s, t, t):_pad_row(s, t, t) + PAD, :] = z


def _pool_phase(pad_ref, ga_ref, wp_ref, ps_ref, ab_ref, nb, t, only_seq=None):
    n_rows = POOL_ROWS
    per_seq = t // n_rows

    def step(i, carry):
        s = i // per_seq
        r0 = (i - s * per_seq) * n_rows
        prow = _pad_row(s, r0, t)
        rows = _rows(i * n_rows, n_rows, n_rows)
        pos = r0 + lax.broadcasted_iota(jnp.int32, (n_rows, LANES), 0)
        before = jnp.minimum(pos, SUBLANES)
        after = jnp.minimum(t - pos, SUBLANES)
        for g in range(N_POOL_GROUPS):
            hw = POOL_HALF[g]
            ln = _lanes(g)
            halo = n_rows + 2 * SUBLANES
            blk = pad_ref[_rows(prow - SUBLANES, halo, SUBLANES), ln]
            run, n = blk, 1
            while n < 2 * hw:
                run = run + pltpu.roll(run, halo - n, 0)
                n *= 2
            if hw < SUBLANES:
                run = pltpu.roll(run, halo - (SUBLANES - hw), 0)
            win = run[:n_rows]
            cnt = (jnp.minimum(before, hw) + jnp.minimum(after, hw)).astype(F32)
            p = (win / cnt - blk[SUBLANES:SUBLANES + n_rows]).astype(BF16)
            y = _dot(p, wp_ref[0, g].astype(BF16)) * ps_ref[:, ln] * ga_ref[rows, ln]
            ab_ref[rows, ln] = y.astype(BF16)
        return carry
    if only_seq is None:
        lax.fori_loop(0, nb * per_seq, step, 0)
    else:
        for i in range(only_seq * per_seq, (only_seq + 1) * per_seq):
            step(i, 0)


def _out_proj_chunk(ab_ref, w_ref, x_ref, gate, dst_ref, c, nb, t):
    lhs = ab_ref[_rows(c * ROW_CHUNK, ROW_CHUNK, ROW_CHUNK), :]
    for g in range(D_MODEL // W_HALF):
        y = _dot(lhs, w_ref[:, _group(g)])
        for s, off, n, o in _pieces(c, nb, t):
            rows = _rows(off, n, n)
            dst_ref[s, rows, _group(g)] = x_ref[s, rows, _group(g)] + gate[:, _group(g)] * y[o:o + n]


def _shift_up(x, o, n):
    if o % SUBLANES == 0:
        return x[o:o + n]
    return pltpu.roll(x, x.shape[0] - o, 0)[:n]


def _conv_phase(pad_c, pad_d, bc_ref, ga_ref, gb_ref, cc_ref, cdw_ref, cdb_ref, lng_ref, lnb_ref,
                ab_ref, nb, t, only_seq=None):
    n_rows = CONV_ROWS
    per_seq = t // n_rows

    def step(i, carry):
        s = i // per_seq
        r0 = (i - s * per_seq) * n_rows
        prow = _pad_row(s, r0, t)
        rows = _rows(i * n_rows, n_rows, n_rows)
        z = []
        for g in range(W_HALF // LANES):
            ln = _lanes(g)
            blk = pad_c[_rows(prow - SUBLANES, n_rows + 2 * SUBLANES, SUBLANES), ln]
            c3 = None
            for j in range(CONV_C):
                o = SUBLANES + j - CONV_C // 2
                term = _shift_up(blk, o, n_rows) * cc_ref[j, :, ln]
                c3 = term if c3 is None else c3 + term
            ab_ref[rows, ln] = (bc_ref[rows, ln] * c3 * ga_ref[rows, ln]).astype(BF16)
            acc = None
            for sft in range(SUBLANES):
                part = None
                for a in range((CONV_D - sft + SUBLANES - 1) // SUBLANES):
                    j = SUBLANES * a + sft
                    src = pad_d[_rows(prow - 2 * SUBLANES + SUBLANES * a, n_rows + SUBLANES,
                                      SUBLANES), ln]
                    term = src * cdw_ref[j, :, ln]
                    part = term if part is None else part + term
                o = SUBLANES + sft - (CONV_D // 2 - SUBLANES)
                part = _shift_up(part, o, n_rows)
                acc = part if acc is None else acc + part
            z.append(acc + cdb_ref[:, ln])
        z = jnp.concatenate(z, axis=-1)
        mu = jnp.mean(z, axis=-1, keepdims=True)
        zc = z - mu
        var = jnp.mean(zc * zc, axis=-1, keepdims=True)
        zn = zc * lax.rsqrt(var + EPS) * lng_ref[...] + lnb_ref[...]
        ab_ref[rows, W_HALF:] = (_silu(zn) * gb_ref[rows, :]).astype(BF16)
        return carry
    if only_seq is None:
        lax.fori_loop(0, nb * per_seq, step, 0)
    else:
        for i in range(only_seq * per_seq, (only_seq + 1) * per_seq):
            step(i, 0)


def _final_norm_chunk(y_ref, fg, c, nb, t):
    for s, off, n, _ in _pieces(c, nb, t):
        for i in range(0, n, NORM_ROWS):
            rows = _rows(off + i, NORM_ROWS, NORM_ROWS)
            x = y_ref[s, rows, :]
            ms = jnp.mean(x * x, axis=-1, keepdims=True)
            y_ref[s, rows, :] = x * lax.rsqrt(ms + EPS) * fg


def _out_proj_sequence(ab_ref, w_ref, y_ref, gate, fg, sq, t):
    lhs = ab_ref[sq * t:(sq + 1) * t, :]
    for g in range(D_MODEL // W_HALF):
        y = _dot(lhs, w_ref[:, _group(g)])
        y_ref[sq, :, _group(g)] = y_ref[sq, :, _group(g)] + gate[:, _group(g)] * y
    for i in range(0, t, NORM_ROWS):
        x = y_ref[sq, i:i + NORM_ROWS, :]
        ms = jnp.mean(x * x, axis=-1, keepdims=True)
        y_ref[sq, i:i + NORM_ROWS, :] = x * lax.rsqrt(ms + EPS) * fg


def _odd_layer(y_ref, m_row, g_row, fg, wio_ref, cc_ref, cdw_ref, cdb_ref, lng_ref, lnb_ref, woo_ref,
               h_ref, pad_c, pad_d, bc_ref, ga_ref, gb_ref, ab_ref, nb, t):
    shift = m_row[:, :D_MODEL]
    gain = g_row * (1.0 + m_row[:, D_MODEL:2 * D_MODEL])
    gate = m_row[:, 2 * D_MODEL:]
    n_chunks = nb * t // ROW_CHUNK

    def in_proj(c):
        _modnorm_chunk(y_ref, h_ref, c, nb, t, gain, shift)
        rows = _rows(c * ROW_CHUNK, ROW_CHUNK, ROW_CHUNK)
        pieces = _pieces(c, nb, t)
        h = h_ref[rows, :]
        bc_ref[rows, :] = _dot(h, wio_ref[:, _group(0)])
        _store_padded(pad_c, _dot(h, wio_ref[:, _group(1)]), pieces, t)
        _scale_padded(pad_c, _dot(h, wio_ref[:, _group(2)]), pieces, t)
        ga_ref[rows, :] = _silu(_dot(h, wio_ref[:, _group(3)]))
        _store_padded(pad_d, _dot(h, wio_ref[:, _group(4)]), pieces, t)
        _scale_padded(pad_d, _sigmoid(_dot(h, wio_ref[:, _group(5)])), pieces, t)
        gb_ref[rows, :] = _silu(_dot(h, wio_ref[:, _group(6)]))
    _for_chunks(n_chunks, in_proj)

    conv_refs = (pad_c, pad_d, bc_ref, ga_ref, gb_ref, cc_ref, cdw_ref, cdb_ref, lng_ref, lnb_ref,
                 ab_ref)
    if t < ROW_CHUNK:
        for sq in range(nb):
            _conv_phase(*conv_refs, nb, t, only_seq=sq)
            _out_proj_sequence(ab_ref, woo_ref, y_ref, gate, fg, sq, t)
    else:
        _conv_phase(*conv_refs, nb, t)

        def out_proj(c):
            _out_proj_chunk(ab_ref, woo_ref, y_ref, gate, y_ref, c, nb, t)
            _final_norm_chunk(y_ref, fg, c, nb, t)
        _for_chunks(n_chunks, out_proj)


def _even_in_proj(x_ref, m_row, g_row, w_ref, h_ref, pad_a, ga_ref, gb_ref, q_ref, k_ref, v_ref,
                  kv_t, nb, t):
    shift = m_row[:, :D_MODEL]
    gain = g_row * (1.0 + m_row[:, D_MODEL:2 * D_MODEL])

    def in_proj(c):
        _modnorm_chunk(x_ref, h_ref, c, nb, t, gain, shift)
        rows = _rows(c * ROW_CHUNK, ROW_CHUNK, ROW_CHUNK)
        pieces = _pieces(c, nb, t)
        h = h_ref[rows, :]
        _store_padded(pad_a, _dot(h, w_ref[:, _group(0)]), pieces, t)
        ga_ref[rows, :] = _silu(_dot(h, w_ref[:, _group(1)]))
        q_ref[rows, :] = (_dot(h, w_ref[:, _group(2)]) * Q_SCALE).astype(BF16)
        for i, (dst, g) in enumerate(((k_ref, 3), (v_ref, 4))):
            if kv_t is None:
                dst[rows, :] = _dot(h, w_ref[:, _group(g)]).astype(BF16)
                continue
            acc = _dot_nt(kv_t[2 + i][...], h)
            dst[:, rows] = acc.astype(BF16)
            for s, off, n, o in pieces:
                for hd in range(N_HEADS):
                    kv_t[i][s, 0, hd, :, _rows(off, n, n)] = (
                        acc[hd * HEAD_DIM:(hd + 1) * HEAD_DIM, o:o + n])
        gb_ref[rows, :] = _silu(_dot(h, w_ref[:, _group(5)]))
    _for_chunks(nb * t // ROW_CHUNK, in_proj)


def _even_out_proj(x_ref, y_ref, m_row, w_ref, ab_ref, nb, t):
    gate = m_row[:, 2 * D_MODEL:]
    n_chunks = nb * t // ROW_CHUNK

    def out_proj(c, carry=0):
        _out_proj_chunk(ab_ref, w_ref, x_ref, gate, y_ref, c, nb, t)
        return carry
    if n_chunks == 1:
        out_proj(0)
    else:
        lax.fori_loop(0, n_chunks, out_proj, 0)


def _split_heads(x):
    lane = lax.broadcasted_iota(jnp.int32, (1, LANES), 1)
    first = jnp.where(lane < HEAD_DIM, 1.0, 0.0).astype(x.dtype)
    return jnp.concatenate([x * first, x * (1 - first)], axis=0)


def _merge_heads(o):
    n = o.shape[0] // 2
    lane = lax.broadcasted_iota(jnp.int32, (n, LANES), 1)
    return jnp.where(lane < HEAD_DIM, o[:n], o[n:])


def _context_attention(q_ref, kt_ref, vt_ref, gb_ref, ab_ref, seqs, t):
    for s in seqs:
        seq = slice(s * t, (s + 1) * t)
        for j in range(N_HEADS // 2):
            ln = _lanes(j)
            kp = kt_ref[ln, seq]
            vp = vt_ref[ln, seq]
            for r0 in range(0, t, Q_ROWS):
                rows = slice(s * t + r0, s * t + r0 + Q_ROWS)
                sc = _dot(_split_heads(q_ref[rows, ln]), kp)
                p = jnp.exp2(sc - jnp.max(sc, axis=-1, keepdims=True))
                o = _dot_nt(p.astype(BF16), vp) / jnp.sum(p, axis=-1, keepdims=True)
                ab_ref[rows, W_HALF + j * LANES:W_HALF + (j + 1) * LANES] = (
                    _merge_heads(o) * gb_ref[rows, ln]).astype(BF16)


def _stage_weights(w_hbm, w_bf, wkt_ref, wvt_ref, stage, sem_in):
    chunks = [(k, r0) for k in range(len(w_hbm)) for r0 in range(0, w_hbm[k].shape[1], STAGE_ROWS)]

    def fetch(i):
        k, r0 = chunks[i]
        cols = w_hbm[k].shape[2]
        slot = i % STAGE_SLOTS
        return pltpu.make_async_copy(w_hbm[k].at[0, pl.ds(r0, STAGE_ROWS), :],
                                     stage.at[slot, :, pl.ds(0, cols)], sem_in.at[slot])

    for i in range(min(STAGE_SLOTS - 1, len(chunks))):
        fetch(i).start()
    for i, (k, r0) in enumerate(chunks):
        if i + STAGE_SLOTS - 1 < len(chunks):
            fetch(i + STAGE_SLOTS - 1).start()
        fetch(i).wait()
        cols = w_hbm[k].shape[2]
        rows = slice(r0, r0 + STAGE_ROWS)
        slot = i % STAGE_SLOTS
        w_bf[k][rows, :] = stage[slot, :, 0:cols].astype(BF16)
        if k == 0:
            wkt_ref[:, rows] = stage[slot, :, _group(3)].T.astype(BF16)
            wvt_ref[:, rows] = stage[slot, :, _group(4)].T.astype(BF16)


def _prompt_body(x_ref, m_ref, ng_ref, fg_ref, wp_ref, ps_ref, cc_ref, cdw_ref, cdb_ref, lng_ref,
                 lnb_ref, wie_hbm, woe_hbm, wio_hbm, woo_hbm,
                 y_ref, ko_ref, vo_ref, wie_out, woe_out, wio_out, woo_out,
                 h_ref, pad_a, pad_b, ga_ref, gb_ref, bc_ref, q_ref, kt_ref, vt_ref, ab_ref,
                 wie_ref, woe_ref, wio_ref, woo_ref, wkt_ref, wvt_ref, stage, sem_in, sem_out,
                 *, nb, t):
    w_out = (wie_out, woe_out, wio_out, woo_out)
    w_bf = (wie_ref, woe_ref, wio_ref, woo_ref)

    def write_back(k):
        return pltpu.make_async_copy(w_bf[k], w_out[k], sem_out.at[k])

    @pl.when(pl.program_id(0) == 0)
    def _():
        _stage_weights((wie_hbm, woe_hbm, wio_hbm, woo_hbm), w_bf, wkt_ref, wvt_ref, stage, sem_in)
        for k in range(len(w_bf)):
            write_back(k).start()

    _zero_pads(pad_a, nb, t)
    _zero_pads(pad_b, nb, t)
    m_even = _cond_row(m_ref, 0, 0)
    _even_in_proj(x_ref, m_even, ng_ref[0:1, :], wie_ref, h_ref, pad_a, ga_ref, gb_ref,
                  q_ref, kt_ref, vt_ref, (ko_ref, vo_ref, wkt_ref, wvt_ref), nb, t)
    gate_even = m_even[:, 2 * D_MODEL:]
    for sq in range(nb):
        _pool_phase(pad_a, ga_ref, wp_ref, ps_ref, ab_ref, nb, t, only_seq=sq)
        _context_attention(q_ref, kt_ref, vt_ref, gb_ref, ab_ref, (sq,), t)
        lhs = ab_ref[sq * t:(sq + 1) * t, :]
        for g in range(D_MODEL // W_HALF):
            y_ref[sq, :, _group(g)] = x_ref[sq, :, _group(g)] + gate_even[:, _group(g)] * _dot(
                lhs, woe_ref[:, _group(g)])
    _odd_layer(y_ref, _cond_row(m_ref, 1, 0), ng_ref[1:2, :], fg_ref[...], wio_ref, cc_ref, cdw_ref,
               cdb_ref, lng_ref, lnb_ref, woo_ref, h_ref, pad_a, pad_b, bc_ref, ga_ref, gb_ref, ab_ref,
               nb, t)

    @pl.when(pl.program_id(0) == 0)
    def _():
        for k in range(len(w_bf)):
            write_back(k).wait()


def _rpb_rows(rpb_ref, e_ref):
    n = rpb_ref.shape[0] * rpb_ref.shape[1]
    lane = lax.broadcasted_iota(jnp.int32, (n, LANES), 1)
    i = jnp.where(lane < GRID_W, lane, lane - LANES)
    idx = jnp.clip(i, -(WIN_W - 1), WIN_W - 1) + (WIN_W - 1)
    rp = rpb_ref[...].reshape(n, rpb_ref.shape[2])
    e = jnp.zeros((n, LANES), F32)
    for d in range(2 * WIN_W - 1):
        e = jnp.where(idx == d, rp[:, d:d + 1], e)
    for r in range(n):
        e_ref[r] = e[r:r + 1, :]


N_DR = 2 * WIN_H - 1
PAIR_TILES = N_DR // 2


def _bias_tile_index(j, dr_lo):
    if isinstance(dr_lo, int):
        parity, half = dr_lo % 2, dr_lo // 2
    else:
        parity, half = dr_lo & 1, lax.shift_right_logical(dr_lo, 1)
    return (2 * j + parity) * PAIR_TILES + half


def _bias_tables(e_ref, bias_ref):
    q = lax.broadcasted_iota(jnp.int32, (GRID_W, LANES), 0)
    lane = lax.broadcasted_iota(jnp.int32, (GRID_W, LANES), 1)
    kw = jnp.where(lane < GRID_W, lane, lane - GRID_W)
    start = jnp.clip(q - WIN_W // 2, 0, GRID_W - WIN_W)
    col_ok = (kw >= start) & (kw < start + WIN_W)
    def per_tile(j, dr):
        for e in range(2):
            r_lo = dr * N_HEADS + 2 * j + e
            lo = jnp.broadcast_to(e_ref[r_lo], (GRID_W, LANES))
            hi = jnp.broadcast_to(e_ref[r_lo + N_HEADS], (GRID_W, LANES))
            lo = pltpu.roll(lo, 0, 1, stride=1, stride_axis=0)
            hi = pltpu.roll(hi, GRID_W, 1, stride=1, stride_axis=0)
            tile = jnp.where(lane < GRID_W, lo, hi)
            bias_ref[_bias_tile_index(j, dr), e * GRID_W:(e + 1) * GRID_W, :] = jnp.where(
                col_ok, tile * LOG2_E, MASKED)

    def per_family(f, carry):
        j, parity = lax.shift_right_logical(f, 1), f & 1
        for m in range(PAIR_TILES):
            per_tile(j, 2 * m + parity)
        return carry
    lax.fori_loop(0, N_HEADS, per_family, 0)


def _neighbourhood_attention(q_ref, k_ref, v_ref, ck_ref, cv_ref, bias_ref, kvc_ref, gb_ref, ab_ref, t):
    grid_h = t // GRID_W
    band = WIN_H * GRID_W
    def per_pair(j, carry_j):
        ln = pl.ds(pl.multiple_of(j * LANES, LANES), LANES)
        out_ln = pl.ds(pl.multiple_of(W_HALF + j * LANES, LANES), LANES)
        for i, src in enumerate((ck_ref, cv_ref)):
            kvc_ref[i] = jnp.concatenate([src[0, 0, 2 * j], src[0, 0, 2 * j + 1]],
                                         axis=0).astype(BF16)

        def per_group(g, carry):
            scored = []
            for u in range(NA_GROUP):
                r = g * NA_GROUP + u
                start = jnp.clip(r - WIN_H // 2, 0, grid_h - WIN_H)
                rows = _rows(r * GRID_W, GRID_W, GRID_W)
                keys = _rows(start * GRID_W, band, GRID_W)
                q2 = _split_heads(q_ref[rows, ln])
                dr0 = (WIN_H - 1) - (r - start)
                bias = jnp.concatenate([bias_ref[_bias_tile_index(j, dr0 + 2 * i)]
                                        for i in range(WIN_H // 2)], axis=-1)
                scored.append((rows, keys, _dot_nt(q2, k_ref[keys, ln]) + bias, _dot(q2, kvc_ref[0])))
            weighted = []
            for rows, keys, s_loc, s_ctx in scored:
                mx = jnp.maximum(jnp.max(s_loc, axis=-1, keepdims=True),
                                 jnp.max(s_ctx, axis=-1, keepdims=True))
                p_loc = jnp.exp2(s_loc - mx)
                p_ctx = jnp.exp2(s_ctx - mx)
                den = (jnp.sum(p_loc, axis=-1, keepdims=True)
                       + jnp.sum(p_ctx, axis=-1, keepdims=True))
                weighted.append((rows, keys, p_loc.astype(BF16), p_ctx.astype(BF16), den))
            for rows, keys, p_loc, p_ctx, den in weighted:
                o = (_dot(p_loc, v_ref[keys, ln]) + _dot_nt(p_ctx, kvc_ref[1])) / den
                ab_ref[rows, out_ln] = (_merge_heads(o) * gb_ref[rows, ln]).astype(BF16)
            return carry
        lax.fori_loop(0, grid_h // NA_GROUP, per_group, 0)
        return carry_j
    lax.fori_loop(0, N_HEADS // 2, per_pair, 0)


def _sample_body(x_ref, m_ref, ng_ref, fg_ref, wie_ref, wp_ref, ps_ref, woe_hbm, wio_hbm, cc_ref,
                 cdw_ref, cdb_ref, lng_ref, lnb_ref, woo_hbm, ck_ref, cv_ref, rpb_ref,
                 y_ref,
                 h_ref, pad_a, pad_b, ga_ref, gb_ref, bc_ref, q_ref, k_ref, v_ref, ab_ref,
                 e_ref, bias_ref, kvc_ref, woe_ref, wio_ref, woo_ref, sem_w, *, t):
    _zero_pads(pad_a, 1, t)
    _zero_pads(pad_b, 1, t)
    late = ((woe_hbm, woe_ref), (wio_hbm, wio_ref), (woo_hbm, woo_ref))

    def late_copy(i):
        return pltpu.make_async_copy(late[i][0], late[i][1], sem_w.at[i])

    first_step = pl.program_id(0) == 0

    @pl.when(first_step)
    def _():
        for i in range(len(late)):
            late_copy(i).start()
        _rpb_rows(rpb_ref, e_ref)
        _bias_tables(e_ref, bias_ref)

    cond = pl.program_id(0) + 1
    m_even = _cond_row(m_ref, 0, cond)
    _even_in_proj(x_ref, m_even, ng_ref[0:1, :], wie_ref, h_ref, pad_a, ga_ref, gb_ref,
                  q_ref, k_ref, v_ref, None, 1, t)
    _pool_phase(pad_a, ga_ref, wp_ref, ps_ref, ab_ref, 1, t)
    _neighbourhood_attention(q_ref, k_ref, v_ref, ck_ref, cv_ref, bias_ref, kvc_ref, gb_ref, ab_ref, t)
    pl.when(first_step)(lambda: late_copy(0).wait())
    _even_out_proj(x_ref, y_ref, m_even, woe_ref, ab_ref, 1, t)

    @pl.when(first_step)
    def _():
        late_copy(1).wait()
        late_copy(2).wait()

    _odd_layer(y_ref, _cond_row(m_ref, 1, cond), ng_ref[1:2, :], fg_ref[...], wio_ref, cc_ref, cdw_ref,
               cdb_ref, lng_ref, lnb_ref, woo_ref, h_ref, pad_a, pad_b, bc_ref, ga_ref, gb_ref, ab_ref,
               1, t)


def _const_spec(shape):
    zeros = (0,) * len(shape)
    return pl.BlockSpec(shape, lambda i: zeros, pipeline_mode=pl.Buffered(1))


def _stream_scratch(nb, t, kv_transposed):
    r = nb * t
    padded = nb * (t + 2 * PAD)
    kv = (W_HALF, r) if kv_transposed else (r, W_HALF)
    return [
        pltpu.VMEM((r, D_MODEL), BF16),
        pltpu.VMEM((padded, W_HALF), F32),
        pltpu.VMEM((padded, W_HALF), F32),
        pltpu.VMEM((r, W_HALF), F32),
        pltpu.VMEM((r, W_HALF), F32),
        pltpu.VMEM((r, W_HALF), F32),
        pltpu.VMEM((r, W_HALF), BF16),
        pltpu.VMEM(kv, BF16),
        pltpu.VMEM(kv, BF16),
        pltpu.VMEM((r, D_MODEL), BF16),
    ]


def _small_params(norm_g, final_g, w_pool, pool_scale, conv_c, conv_d, conv_d_b, ln_g, ln_b):
    return [norm_g, final_g.reshape(1, D_MODEL), w_pool, pool_scale, jnp.swapaxes(conv_c, 0, 1),
            jnp.swapaxes(conv_d, 0, 1), conv_d_b, ln_g, ln_b]


def kernel(x_prompt, x_sample, cache_k, cache_v, c, c_ctx, norm_g, w_mod, b_mod, w_in_even, w_pool,
           pool_scale, rpb, w_out_even, w_in_odd, conv_c, conv_d, conv_d_b, ln_g, ln_b, w_out_odd,
           final_g):
    batch, seq, d = x_prompt.shape
    dec_batch, dec_seq, _ = x_sample.shape
    assert d == D_MODEL and w_mod.shape[0] == 2 and w_in_even.shape[0] == 1 and w_in_odd.shape[0] == 1
    assert (NB_PROMPT * seq) % ROW_CHUNK == 0 and ROW_CHUNK % seq == 0 and seq % Q_ROWS == 0
    assert dec_seq % ROW_CHUNK == 0 and dec_seq // GRID_W >= WIN_H
    assert seq % POOL_ROWS == 0 and seq % CONV_ROWS == 0
    assert dec_seq % POOL_ROWS == 0 and dec_seq % CONV_ROWS == 0
    assert (dec_seq // GRID_W) % NA_GROUP == 0

    cond_rows = SUBLANES * ((1 + dec_batch + SUBLANES - 1) // SUBLANES)
    m = _modulation(c_ctx, c, w_mod, b_mod, cond_rows)
    m_spec = _const_spec(m.shape)

    small = _small_params(norm_g, final_g, w_pool, pool_scale, conv_c, conv_d, conv_d_b, ln_g, ln_b)
    small_specs = [_const_spec(a.shape) for a in small]
    w_f32 = (w_in_even, w_out_even, w_in_odd, w_out_odd)
    assert all(w.shape[0] == 1 and w.shape[1] % STAGE_ROWS == 0 for w in w_f32)
    any_spec = pl.BlockSpec(memory_space=pl.ANY)

    nb = NB_PROMPT
    assert batch % nb == 0
    kv_shape = jax.ShapeDtypeStruct((batch, 1, N_HEADS, HEAD_DIM, seq), F32)
    kv_spec = pl.BlockSpec((nb, 1, N_HEADS, HEAD_DIM, seq), lambda i: (i, 0, 0, 0, 0))
    y_prompt, new_kt, new_vt, wie, woe, wio, woo = pl.pallas_call(
        functools.partial(_prompt_body, nb=nb, t=seq),
        out_shape=(jax.ShapeDtypeStruct(x_prompt.shape, F32), kv_shape, kv_shape)
                  + tuple(jax.ShapeDtypeStruct(w.shape[1:], BF16) for w in w_f32),
        grid=(batch // nb,),
        in_specs=[pl.BlockSpec((nb, seq, d), lambda i: (i, 0, 0)), m_spec] + small_specs
                 + [any_spec] * len(w_f32),
        out_specs=(pl.BlockSpec((nb, seq, d), lambda i: (i, 0, 0)), kv_spec, kv_spec)
                  + (any_spec,) * len(w_f32),
        scratch_shapes=_stream_scratch(nb, seq, True)
                       + [pltpu.VMEM(w.shape[1:], BF16) for w in w_f32] + [
            pltpu.VMEM((W_HALF, d), BF16),
            pltpu.VMEM((W_HALF, d), BF16),
            pltpu.VMEM((STAGE_SLOTS, STAGE_ROWS, max(w.shape[2] for w in w_f32)), F32),
            pltpu.SemaphoreType.DMA((STAGE_SLOTS,)),
            pltpu.SemaphoreType.DMA((len(w_f32),)),
        ],
        compiler_params=pltpu.CompilerParams(dimension_semantics=("arbitrary",),
                                             vmem_limit_bytes=VMEM_LIMIT),
        name="prompt",
    )(x_prompt, m, *small, *w_f32)
    ng, fg, wp, ps, cc, cdw, cdb, lng, lnb = small
    w_args = [ng, fg, wie, wp, ps, woe, wio, cc, cdw, cdb, lng, lnb, woo]
    late_w = (woe, wio, woo)
    w_specs = [any_spec if any(a is w for w in late_w) else _const_spec(a.shape) for a in w_args]

    past = cache_k.shape[3]
    cache_spec = pl.BlockSpec((1, 1, N_HEADS, HEAD_DIM, past), lambda i: (i, 0, 0, 0, 0))
    rpb_t = jnp.swapaxes(rpb[0], 0, 1)
    y_sample = pl.pallas_call(
        functools.partial(_sample_body, t=dec_seq),
        out_shape=jax.ShapeDtypeStruct(x_sample.shape, F32),
        grid=(dec_batch,),
        in_specs=[pl.BlockSpec((1, dec_seq, d), lambda i: (i, 0, 0), pipeline_mode=pl.Buffered(1)),
                  m_spec] + w_specs
                 + [cache_spec, cache_spec, _const_spec(rpb_t.shape)],
        out_specs=pl.BlockSpec((1, dec_seq, d), lambda i: (i, 0, 0)),
        scratch_shapes=_stream_scratch(1, dec_seq, False) + [
            pltpu.VMEM((N_DR * N_HEADS, 1, LANES), F32),
            pltpu.VMEM((N_HEADS * PAIR_TILES, 2 * GRID_W, LANES), F32),
            pltpu.VMEM((2, LANES, past), BF16),
        ] + [pltpu.VMEM(w.shape, BF16) for w in late_w] + [pltpu.SemaphoreType.DMA((len(late_w),))],
        compiler_params=pltpu.CompilerParams(dimension_semantics=("arbitrary",),
                                             vmem_limit_bytes=VMEM_LIMIT),
        name="sample",
    )(x_sample, m, *w_args, jnp.swapaxes(cache_k, 3, 4), jnp.swapaxes(cache_v, 3, 4), rpb_t)

    return (y_prompt, y_sample, jnp.swapaxes(new_kt, 3, 4), jnp.swapaxes(new_vt, 3, 4))
```
